```python
import math
import jax
import jax.numpy as jnp
from jax import lax
import numpy as np

D_MODEL = 1024
BATCH = 2
SEQ = 16384
DEPTH = 4

GRID_W = 64
CTX_LEN = 256

F_GROUPS = 4
F_DIM = 64
F_WIDTH = F_GROUPS * F_DIM
A_HEADS = 4
A_DH = 64
A_DV = 2 * A_DH
A_QK_WIDTH = A_HEADS * 2 * A_DH
A_WIDTH = A_HEADS * A_DV
Q_BLOCK = 128
ROPE_BASE = 10000.0
G_GROUPS = 4
G_DIM = 64
G_WIDTH = G_GROUPS * G_DIM
CHUNK = 128

MIX_WIDTH = F_WIDTH + A_WIDTH + G_WIDTH
IN_COLS = F_WIDTH + 2 * A_QK_WIDTH + A_WIDTH + 2 * G_WIDTH
COL_SPLITS = [F_WIDTH, F_WIDTH + A_QK_WIDTH, F_WIDTH + 2 * A_QK_WIDTH, F_WIDTH + 2 * A_QK_WIDTH + A_WIDTH]

N_EXPERTS = 16
N_EXPERT_GROUPS = 4
EXPERTS_PER_GROUP = N_EXPERTS // N_EXPERT_GROUPS
TOP_K = 2
D_EXPERT = 512

LN_EPS = 1e-5
RMS_EPS = 1e-5
DEEPNORM_ALPHA = (2 * DEPTH) ** 0.25
DEEPNORM_BETA = (8 * DEPTH) ** -0.25

kernel_name = "hybrid_fourier_diffattn_sgu_moe_dit"


def _layernorm(x, g, b):
    xf = x.astype(jnp.float32)
    mu = jnp.mean(xf, axis=-1, keepdims=True)
    var = jnp.mean(jnp.square(xf - mu), axis=-1, keepdims=True)
    return ((xf - mu) * lax.rsqrt(var + LN_EPS) * g + b).astype(x.dtype)


def _modulate(h, shift, scale):
    return h * (1 + scale) + shift


def _axial_rope_tables(n):
    rows = n // GRID_W
    row = jnp.repeat(jnp.arange(rows, dtype=jnp.float32), GRID_W)
    col = jnp.tile(jnp.arange(GRID_W, dtype=jnp.float32), rows)
    half = A_DH // 2
    inv = ROPE_BASE ** (-jnp.arange(0, half, 2, dtype=jnp.float32) / half)
    ang = jnp.stack([row[:, None] * inv, col[:, None] * inv], axis=1)
    return jnp.cos(ang), jnp.sin(ang)


def _apply_axial_rope(t, cos, sin):
    shp = t.shape
    tr = t.reshape(shp[:-1] + (2, 2, A_DH // 4))
    t1, t2 = tr[..., 0, :], tr[..., 1, :]
    cs, sn = cos[:, None, None], sin[:, None, None]
    out = jnp.stack([t1 * cs - t2 * sn, t1 * sn + t2 * cs], axis=-2)
    return out.reshape(shp).astype(t.dtype)


def _split_groups(p):
    b, n, _ = p.shape
    f, q, k, v, g = jnp.split(p, COL_SPLITS, axis=-1)
    q = q.reshape(b, n, A_HEADS, 2, A_DH)
    k = k.reshape(b, n, A_HEADS, 2, A_DH)
    v = v.reshape(b, n, A_HEADS, A_DV)
    return f, q, k, v, g


def _fourier_mix(f, w):
    b, n, _ = f.shape
    fg = f.reshape(b, n, F_GROUPS, F_DIM).astype(jnp.float32)
    spec = jnp.fft.fftn(fg, axes=(1, 3), norm="ortho").real
    y = jnp.einsum('bngc,gcd->bngd', spec, w.astype(jnp.float32))
    return y.reshape(b, n, F_WIDTH).astype(f.dtype)


def _diff_attn_block(q, k, v, lam):
    s = jnp.einsum('bhcqd,bhckd->bhcqk', q, k).astype(jnp.float32) * (A_DH ** -0.5)
    p = jax.nn.softmax(s, axis=-1)
    a = p[:, :, 0] - lam * p[:, :, 1]
    return jnp.einsum('bhqk,bhkd->bhqd', a.astype(v.dtype), v)


def _latent_diff_attn(q, k_all, v_all, lam):
    b, h, _, n, dh = q.shape
    nb = n // Q_BLOCK
    qb = q.reshape(b, h, 2, nb, Q_BLOCK, dh).transpose(3, 0, 1, 2, 4, 5)
    out = lax.map(lambda qi: _diff_attn_block(qi, k_all, v_all, lam), qb)
    return out.transpose(1, 2, 0, 3, 4).reshape(b, h, n, A_DV)


def _diff_head_norm(o, gain, lam_init):
    of = o.astype(jnp.float32)
    of = of * lax.rsqrt(jnp.mean(of * of, axis=-1, keepdims=True) + RMS_EPS) * gain * (1 - lam_init)
    b, h, n, dv = o.shape
    return of.transpose(0, 2, 1, 3).reshape(b, n, h * dv).astype(o.dtype)


def _to_heads(q, k, v):
    return q.transpose(0, 2, 3, 1, 4), k.transpose(0, 2, 3, 1, 4), v.transpose(0, 2, 1, 3)


def _chunk_sgu(g, ln_g, ln_b, w_s, b_s):
    b, n, _ = g.shape
    g = jax.nn.gelu(g)
    u, v = jnp.split(g, 2, axis=-1)
    v = _layernorm(v, ln_g, ln_b)
    vc = v.reshape(b, n // CHUNK, CHUNK, G_GROUPS, G_DIM)
    mixed = jnp.einsum('bcpgd,gqp->bcqgd', vc, w_s) + jnp.swapaxes(b_s, 0, 1)[:, :, None]
    return u * mixed.reshape(b, n, G_WIDTH)


def _route(h, w_router, router_bias):
    b, t, _ = h.shape
    scores = jax.nn.sigmoid(jnp.einsum('btd,de->bte', h, w_router).astype(jnp.float32))
    sel = scores + router_bias.astype(jnp.float32)
    group_score = jnp.sum(lax.top_k(sel.reshape(b, t, N_EXPERT_GROUPS, EXPERTS_PER_GROUP), TOP_K)[0], axis=-1)
    gidx = jnp.argmax(group_score, axis=-1)
    gmask = gidx[..., None] == jnp.arange(N_EXPERT_GROUPS)
    emask = jnp.repeat(gmask, EXPERTS_PER_GROUP, axis=-1)
    _, idx = lax.top_k(jnp.where(emask, sel, -jnp.inf), TOP_K)
    w = jnp.take_along_axis(scores, idx, axis=-1)
    w = w / jnp.sum(w, axis=-1, keepdims=True)
    return jnp.sum(jax.nn.one_hot(idx, N_EXPERTS, dtype=jnp.float32) * w[..., None], axis=-2)


def _moe(h, gates, w1, w3, w2):
    y = jnp.zeros_like(h)
    for e in range(N_EXPERTS):
        a = jax.nn.silu(h @ w1[e]) * (h @ w3[e])
        y = y + gates[..., e:e + 1].astype(h.dtype) * (a @ w2[e])
    return y


def setup_inputs(seed: int = 0) -> dict:
    key = jax.random.key(seed)
    ks = jax.random.split(key, 22)
    D = D_MODEL
    f32 = jnp.float32

    def nrm(k, shape, scale):
        return jax.random.normal(k, shape, f32) * scale

    return {
        "x": nrm(ks[0], (BATCH, SEQ, D), 1.0),
        "c": nrm(ks[1], (BATCH, D), 1.0),
        "ctx": nrm(ks[2], (BATCH, CTX_LEN, D), 1.0),
        "c_ctx": nrm(ks[3], (D,), 1.0),
        "w_ada": nrm(ks[4], (DEPTH, D, 6 * D), 0.5 * D ** -0.5),
        "b_ada": nrm(ks[5], (DEPTH, 6 * D), 0.02),
        "w_in": nrm(ks[6], (DEPTH, D, IN_COLS), D ** -0.5),
        "w_fourier": nrm(ks[7], (DEPTH, F_GROUPS, F_DIM, F_DIM), F_DIM ** -0.5),
        "diff_lambda": nrm(ks[8], (DEPTH, 4, A_DH), 0.1),
        "diff_subln": 1.0 + nrm(ks[9], (DEPTH, A_DV), 0.02),
        "sgu_ln_g": 1.0 + nrm(ks[10], (DEPTH, G_WIDTH), 0.02),
        "sgu_ln_b": nrm(ks[11], (DEPTH, G_WIDTH), 0.02),
        "sgu_w": nrm(ks[12], (DEPTH, G_GROUPS, CHUNK, CHUNK), CHUNK ** -0.5),
        "sgu_b": 1.0 + nrm(ks[13], (DEPTH, G_GROUPS, CHUNK), 0.02),
        "w_out": nrm(ks[14], (DEPTH, MIX_WIDTH, D), DEEPNORM_BETA * MIX_WIDTH ** -0.5),
        "ln_g": 1.0 + nrm(ks[15], (DEPTH, 2, D), 0.02),
        "ln_b": nrm(ks[16], (DEPTH, 2, D), 0.02),
        "w_router": nrm(ks[17], (D, N_EXPERTS), D ** -0.5),
        "router_bias": nrm(ks[18], (N_EXPERTS,), 0.01),
        "moe_w1": nrm(ks[19], (DEPTH, N_EXPERTS, D, D_EXPERT), D ** -0.5),
        "moe_w3": nrm(ks[20], (DEPTH, N_EXPERTS, D, D_EXPERT), D ** -0.5),
        "moe_w2": nrm(ks[21], (DEPTH, N_EXPERTS, D_EXPERT, D), DEEPNORM_BETA * D_EXPERT ** -0.5),
    }


def reference(x, c, ctx, c_ctx, w_ada, b_ada, w_in, w_fourier, diff_lambda, diff_subln,
              sgu_ln_g, sgu_ln_b, sgu_w, sgu_b, w_out, ln_g, ln_b, w_router, router_bias,
              moe_w1, moe_w3, moe_w2):
    n = x.shape[1]
    cos, sin = _axial_rope_tables(n)
    c_act = jax.nn.silu(c)
    cc_act = jax.nn.silu(c_ctx)
    for l in range(DEPTH):
        last = l == DEPTH - 1
        mod_x = (c_act @ w_ada[l] + b_ada[l])[:, None, :]
        mod_c = cc_act @ w_ada[l] + b_ada[l]
        sh1, sc1, g1, sh2, sc2, g2 = jnp.split(mod_x, 6, axis=-1)
        csh1, csc1, cg1, csh2, csc2, cg2 = jnp.split(mod_c, 6, axis=-1)
        lam_init = 0.8 - 0.6 * math.exp(-0.3 * l)
        lp = diff_lambda[l].astype(jnp.float32)
        lam = jnp.exp(jnp.sum(lp[0] * lp[1])) - jnp.exp(jnp.sum(lp[2] * lp[3])) + lam_init

        px = _modulate(x, sh1, sc1) @ w_in[l]
        pc = _modulate(ctx, csh1, csc1) @ w_in[l]
        fx, qx, kx, vx, gx = _split_groups(px)
        fc, qc, kc, vc, gc = _split_groups(pc)
        qx = _apply_axial_rope(qx, cos, sin)
        kx = _apply_axial_rope(kx, cos, sin)
        qx_h, kx_h, vx_h = _to_heads(qx, kx, vx)
        qc_h, kc_h, vc_h = _to_heads(qc, kc, vc)
        k_all = jnp.concatenate([kc_h, kx_h], axis=3)
        v_all = jnp.concatenate([vc_h, vx_h], axis=2)
        att_x = _diff_head_norm(_latent_diff_attn(qx_h, k_all, v_all, lam), diff_subln[l], lam_init)
        mix_x = jnp.concatenate([
            _fourier_mix(fx, w_fourier[l]),
            att_x,
            _chunk_sgu(gx, sgu_ln_g[l], sgu_ln_b[l], sgu_w[l], sgu_b[l]),
        ], axis=-1)
        x = _layernorm(DEEPNORM_ALPHA * x + g1 * (mix_x @ w_out[l]), ln_g[l, 0], ln_b[l, 0])
        if not last:
            att_c = _diff_head_norm(_diff_attn_block(qc_h, kc_h, vc_h, lam), diff_subln[l], lam_init)
            mix_c = jnp.concatenate([
                _fourier_mix(fc, w_fourier[l]),
                att_c,
                _chunk_sgu(gc, sgu_ln_g[l], sgu_ln_b[l], sgu_w[l], sgu_b[l]),
            ], axis=-1)
            ctx = _layernorm(DEEPNORM_ALPHA * ctx + cg1 * (mix_c @ w_out[l]), ln_g[l, 0], ln_b[l, 0])

        hx = _modulate(x, sh2, sc2)
        yx = _moe(hx, _route(hx, w_router, router_bias), moe_w1[l], moe_w3[l], moe_w2[l])
        x = _layernorm(DEEPNORM_ALPHA * x + g2 * yx, ln_g[l, 1], ln_b[l, 1])
        if not last:
            hc = _modulate(ctx, csh2, csc2)
            yc = _moe(hc, _route(hc, w_router, router_bias), moe_w1[l], moe_w3[l], moe_w2[l])
            ctx = _layernorm(DEEPNORM_ALPHA * ctx + cg2 * yc, ln_g[l, 1], ln_b[l, 1])
    return x
```

```python
import functools
import math

import jax
import jax.numpy as jnp
from jax import lax
from jax.experimental import pallas as pl
from jax.experimental.pallas import tpu as pltpu

F32 = jnp.float32
BF16 = jnp.bfloat16
HIGHEST = lax.Precision.HIGHEST

GRID_W = 64
F_GROUPS, F_DIM = 4, 64
F_WIDTH = F_GROUPS * F_DIM
A_HEADS, A_DH = 4, 64
A_DV = 2 * A_DH
A_QK_WIDTH = A_HEADS * 2 * A_DH
A_WIDTH = A_HEADS * A_DV
G_GROUPS, G_DIM = 4, 64
G_WIDTH = G_GROUPS * G_DIM
CHUNK = 128
ROPE_BASE = 10000.0
N_EXPERTS = 16
N_EXPERT_GROUPS = 4
EXPERTS_PER_GROUP = N_EXPERTS // N_EXPERT_GROUPS
LN_EPS = 1e-5
RMS_EPS = 1e-5

COL_FA = 0
COL_FB = COL_FA + F_WIDTH
COL_Q = COL_FB + F_WIDTH
COL_K = COL_Q + A_QK_WIDTH
COL_V = COL_K + A_QK_WIDTH
COL_G = COL_V + A_WIDTH
BIG_COLS = COL_G + 2 * G_WIDTH

LANES = 128
SUBLANES = 8
MOD_ROWS = 8
ROUTER_LANES = 128
DFT_A = 128
VMEM_LIMIT = 56 * 1024 * 1024

TOKEN_TILE = 256
ATTN_TQ = 256
MOE_TILE = 1024


def _cparams(sem):
    return pltpu.CompilerParams(dimension_semantics=sem, vmem_limit_bytes=VMEM_LIMIT)


def _hdot(a, b):
    return jnp.dot(a, b, precision=HIGHEST, preferred_element_type=F32)


def _bdot(a, b):
    return jnp.dot(a, b, preferred_element_type=F32)


def _layernorm(y, g, b):
    mu = jnp.mean(y, axis=-1, keepdims=True)
    d = y - mu
    var = jnp.mean(d * d, axis=-1, keepdims=True)
    return d * lax.rsqrt(var + LN_EPS) * g + b


def _ada_kernel(c_ref, w_ref, b_ref, o_ref):
    c = c_ref[...]
    a = c * jax.nn.sigmoid(c)
    o_ref[...] = _hdot(a, w_ref[...]) + b_ref[...]


def _ada(c_rows, w_ada, b_ada):
    depth, d, cols = w_ada.shape
    tn = 1536
    assert cols % tn == 0
    return pl.pallas_call(
        _ada_kernel,
        out_shape=jax.ShapeDtypeStruct((depth, MOD_ROWS, cols), F32),
        grid=(depth, cols // tn),
        in_specs=[
            pl.BlockSpec((MOD_ROWS, d), lambda l, j: (0, 0)),
            pl.BlockSpec((None, d, tn), lambda l, j: (l, 0, j)),
            pl.BlockSpec((None, 1, tn), lambda l, j: (l, 0, j)),
        ],
        out_specs=pl.BlockSpec((None, MOD_ROWS, tn), lambda l, j: (l, 0, j)),
        compiler_params=_cparams(("parallel", "parallel")),
        name="ada_mod",
    )(c_rows, w_ada, b_ada.reshape(depth, 1, cols))


def _fw_kernel(wf_ref, bdc_ref, bds_ref, bdw_ref, o_ref):
    bdw = bdw_ref[...]
    mc = _hdot(bdc_ref[...], bdw)
    ms = _hdot(bds_ref[...], bdw)
    wf = wf_ref[...]
    o_ref[:, :F_WIDTH] = _hdot(wf, mc)
    o_ref[:, F_WIDTH:] = _hdot(wf, ms)


def _fourier_weights(wf, bdc, bds, bdw):
    depth, d, _ = wf.shape
    return pl.pallas_call(
        _fw_kernel,
        out_shape=jax.ShapeDtypeStruct((depth, d, 2 * F_WIDTH), F32),
        grid=(depth,),
        in_specs=[
            pl.BlockSpec((None, d, F_WIDTH), lambda l: (l, 0, 0)),
            pl.BlockSpec((F_WIDTH, F_WIDTH), lambda l: (0, 0)),
            pl.BlockSpec((F_WIDTH, F_WIDTH), lambda l: (0, 0)),
            pl.BlockSpec((None, F_WIDTH, F_WIDTH), lambda l: (l, 0, 0)),
        ],
        out_specs=pl.BlockSpec((None, d, 2 * F_WIDTH), lambda l: (l, 0, 0)),
        compiler_params=_cparams(("parallel",)),
        name="fourier_weights",
    )(wf, bdc, bds, bdw)


def _gelu_tanh(x):
    c = math.sqrt(2.0 / math.pi)
    return x * (0.5 * (1.0 + jnp.tanh(c * (x + 0.044715 * (x * x * x)))))


def _inproj_kernel(x_ref, mod_ref, w_ref, cos_ref, sin_ref, lng_ref, lnb_ref, ws_ref, bs_ref,
                   fa_ref, fb_ref, q_ref, k_ref, v_ref, sg_ref, *, q_scale):
    x = x_ref[...]
    sh = mod_ref[0:1, :]
    sc = mod_ref[1:2, :]
    h = (x * (1.0 + sc) + sh).astype(BF16)
    p = _bdot(h, w_ref[...])
    fa_ref[...] = p[:, COL_FA:COL_FA + F_WIDTH]
    fb_ref[...] = p[:, COL_FB:COL_FB + F_WIDTH]
    cosf = cos_ref[...]
    sinf = sin_ref[...]
    for hh in range(A_HEADS):
        t = p[:, COL_Q + LANES * hh:COL_Q + LANES * (hh + 1)]
        r = t * cosf + pltpu.roll(t, LANES // 2, 1) * sinf
        q_ref[:, LANES * hh:LANES * (hh + 1)] = (r * q_scale).astype(BF16)
        t = p[:, COL_K + LANES * hh:COL_K + LANES * (hh + 1)]
        r = t * cosf + pltpu.roll(t, LANES // 2, 1) * sinf
        k_ref[:, LANES * hh:LANES * (hh + 1)] = r.astype(BF16)
    v_ref[...] = p[:, COL_V:COL_V + A_WIDTH].astype(BF16)

    g = _gelu_tanh(p[:, COL_G:COL_G + 2 * G_WIDTH])
    u = g[:, :G_WIDTH]
    vn = _layernorm(g[:, G_WIDTH:], lng_ref[...], lnb_ref[...])
    lane_group = lax.broadcasted_iota(jnp.int32, (CHUNK, G_WIDTH), 1) // G_DIM
    tm = x.shape[0]
    for c in range(tm // CHUNK):
        rows = slice(c * CHUNK, (c + 1) * CHUNK)
        vc = vn[rows]
        mixed = bs_ref[...]
        for gi in range(G_GROUPS):
            vm = jnp.where(lane_group == gi, vc, 0.0).astype(BF16)
            mixed = mixed + _bdot(ws_ref[gi], vm)
        sg_ref[rows, :] = (u[rows] * mixed).astype(BF16)


def _inproj(xa, mods_l, w_big, cosf, sinf, lng, lnb, ws, bs_full, n_lat, q_scale):
    b, l, d = xa.shape
    tm = TOKEN_TILE
    nlat = n_lat // tm
    seg = lambda i: jnp.where(i >= nlat, 1, 0)
    row = lambda width: pl.BlockSpec((None, tm, width), lambda bb, i: (bb, i, 0))
    const2 = lambda shape: pl.BlockSpec(shape, lambda bb, i: (0, 0))
    out_shapes = (
        jax.ShapeDtypeStruct((b, l, F_WIDTH), F32),
        jax.ShapeDtypeStruct((b, l, F_WIDTH), F32),
        jax.ShapeDtypeStruct((b, l, A_QK_WIDTH), BF16),
        jax.ShapeDtypeStruct((b, l, A_QK_WIDTH), BF16),
        jax.ShapeDtypeStruct((b, l, A_WIDTH), BF16),
        jax.ShapeDtypeStruct((b, l, G_WIDTH), BF16),
    )
    return pl.pallas_call(
        functools.partial(_inproj_kernel, q_scale=q_scale),
        out_shape=out_shapes,
        grid=(b, l // tm),
        in_specs=[
            row(d),
            pl.BlockSpec((None, None, MOD_ROWS, d), lambda bb, i: (bb, seg(i), 0, 0)),
            const2((d, BIG_COLS)),
            pl.BlockSpec((tm, LANES), lambda bb, i: (i, 0)),
            pl.BlockSpec((tm, LANES), lambda bb, i: (i, 0)),
            const2((1, G_WIDTH)),
            const2((1, G_WIDTH)),
            pl.BlockSpec((G_GROUPS, CHUNK, CHUNK), lambda bb, i: (0, 0, 0)),
            const2((CHUNK, G_WIDTH)),
        ],
        out_specs=(row(F_WIDTH), row(F_WIDTH), row(A_QK_WIDTH), row(A_QK_WIDTH),
                   row(A_WIDTH), row(G_WIDTH)),
        compiler_params=_cparams(("parallel", "parallel")),
        name="inproj",
    )(xa, mods_l, w_big, cosf, sinf, lng, lnb, ws, bs_full)


def _fft_a_kernel(fa_ref, fb_ref, c_ref, s_ref, tc_ref, ts_ref, zr_ref, zi_ref):
    fa = fa_ref[...]
    fb = fb_ref[...]
    cm = c_ref[...]
    sm = s_ref[...]
    zr = _hdot(cm, fa) - _hdot(sm, fb)
    zi = -(_hdot(cm, fb) + _hdot(sm, fa))
    tc = tc_ref[...]
    ts = ts_ref[...]
    zr_ref[...] = zr * tc + zi * ts
    zi_ref[...] = zi * tc - zr * ts


def _fft_a(fa, fb, cm, sm, twc, tws, bn):
    b, l, _ = fa.shape
    cols = bn * F_WIDTH
    fa3 = fa.reshape(b, l // bn, cols)
    fb3 = fb.reshape(b, l // bn, cols)
    tcw = min(cols, 2048)
    assert cols % tcw == 0
    blk = pl.BlockSpec((None, DFT_A, tcw), lambda bb, j: (bb, 0, j))
    tw = pl.BlockSpec((DFT_A, tcw), lambda bb, j: (0, j))
    mat = pl.BlockSpec((DFT_A, DFT_A), lambda bb, j: (0, 0))
    return pl.pallas_call(
        _fft_a_kernel,
        out_shape=(jax.ShapeDtypeStruct((b, DFT_A, cols), F32),) * 2,
        grid=(b, cols // tcw),
        in_specs=[blk, blk, mat, mat, tw, tw],
        out_specs=(blk, blk),
        compiler_params=_cparams(("parallel", "parallel")),
        name="fft_stage_a",
    )(fa3, fb3, cm, sm, twc, tws)


def _fft_b_kernel(zr_ref, zi_ref, c_ref, s_ref, o_ref):
    cb = c_ref[...]
    sb = s_ref[...]
    for j in range(zr_ref.shape[0]):
        y = _hdot(cb, zr_ref[j]) + _hdot(sb, zi_ref[j])
        o_ref[:, F_WIDTH * j:F_WIDTH * (j + 1)] = y.astype(o_ref.dtype)


def _fft_b(zr, zi, cb, sb, bn):
    b = zr.shape[0]
    kb = 8
    zr4 = zr.reshape(b, DFT_A, bn, F_WIDTH)
    zi4 = zi.reshape(b, DFT_A, bn, F_WIDTH)
    blk = pl.BlockSpec((None, kb, bn, F_WIDTH), lambda bb, i: (bb, i, 0, 0))
    mat = pl.BlockSpec((bn, bn), lambda bb, i: (0, 0))
    out = pl.pallas_call(
        _fft_b_kernel,
        out_shape=jax.ShapeDtypeStruct((b, bn, DFT_A * F_WIDTH), BF16),
        grid=(b, DFT_A // kb),
        in_specs=[blk, blk, mat, mat],
        out_specs=pl.BlockSpec((None, bn, kb * F_WIDTH), lambda bb, i: (bb, 0, i)),
        compiler_params=_cparams(("parallel", "parallel")),
        name="fft_stage_b",
    )(zr4, zi4, cb, sb)
    return out.reshape(b, bn * DFT_A, F_WIDTH)


def _dft_ctx_kernel(fa_ref, fb_ref, c_ref, s_ref, o_ref):
    y = _hdot(c_ref[...], fa_ref[...]) - _hdot(s_ref[...], fb_ref[...])
    o_ref[...] = y.astype(o_ref.dtype)


def _dft_ctx(fa, fb, cc, sc, n_lat, n_ctx):
    b = fa.shape[0]
    blk = pl.BlockSpec((None, n_ctx, F_WIDTH), lambda bb: (bb, n_lat // n_ctx, 0))
    mat = pl.BlockSpec((n_ctx, n_ctx), lambda bb: (0, 0))
    return pl.pallas_call(
        _dft_ctx_kernel,
        out_shape=jax.ShapeDtypeStruct((b, n_ctx, F_WIDTH), BF16),
        grid=(b,),
        in_specs=[blk, blk, mat, mat],
        out_specs=pl.BlockSpec((None, n_ctx, F_WIDTH), lambda bb: (bb, 0, 0)),
        compiler_params=_cparams(("parallel",)),
        name="dft_ctx",
    )(fa, fb, cc, sc)


def _attn_kernel(q_ref, kt_ref, v_ref, lam_ref, gain_ref, o_ref, q2_ref, m_ref, acc_ref,
                 *, lam_init, nk):
    j = pl.program_id(3)
    tq = q_ref.shape[0]
    tk = kt_ref.shape[1]

    @pl.when(j == 0)
    def _init():
        q = q_ref[...]
        lane = lax.broadcasted_iota(jnp.int32, q.shape, 1)
        comp0 = (lane % A_DH) < (A_DH // 2)
        zero = jnp.zeros_like(q)
        q2_ref[0:tq, :] = jnp.where(comp0, q, zero)
        q2_ref[tq:2 * tq, :] = jnp.where(comp0, zero, q)
        m_ref[...] = jnp.full(m_ref.shape, -jnp.inf, F32)
        acc_ref[...] = jnp.zeros(acc_ref.shape, F32)

    s = _bdot(q2_ref[...], kt_ref[...])
    m_old = m_ref[...]
    m_new = jnp.maximum(m_old, jnp.max(s, axis=1, keepdims=True))
    alpha = jnp.exp2(m_old - m_new)
    p = jnp.exp2(s - jnp.concatenate([m_new] * (tk // LANES), axis=1)).astype(BF16)
    v_ext = jnp.concatenate([v_ref[...], jnp.ones((tk, LANES), BF16)], axis=1)
    pv = _bdot(p, v_ext)
    acc_ref[...] = acc_ref[...] * jnp.concatenate([alpha, alpha], axis=1) + pv
    m_ref[...] = m_new

    @pl.when(j == nk - 1)
    def _fin():
        acc = acc_ref[...]
        o0 = acc[0:tq, 0:LANES] / acc[0:tq, LANES:]
        o1 = acc[tq:, 0:LANES] / acc[tq:, LANES:]
        lp = lam_ref[...]
        lam = (jnp.exp(jnp.sum(lp[0:1] * lp[1:2], keepdims=True))
               - jnp.exp(jnp.sum(lp[2:3] * lp[3:4], keepdims=True)) + lam_init)
        o = o0 - lam * o1
        ms = jnp.mean(o * o, axis=-1, keepdims=True)
        o = o * lax.rsqrt(ms + RMS_EPS) * gain_ref[...] * (1.0 - lam_init)
        o_ref[...] = o.astype(o_ref.dtype)


def _attention(q, kt, v, lam_p, gain, *, q_block0, n_q, k_block0, n_k, tq, tk, lam_init):
    b = q.shape[0]
    nq = n_q // tq
    nk = n_k // tk
    return pl.pallas_call(
        functools.partial(_attn_kernel, lam_init=lam_init, nk=nk),
        out_shape=jax.ShapeDtypeStruct((b, n_q, A_WIDTH), BF16),
        grid=(b, A_HEADS, nq, nk),
        in_specs=[
            pl.BlockSpec((None, tq, A_DV), lambda bb, h, i, j: (bb, q_block0 + i, h)),
            pl.BlockSpec((None, A_DV, tk), lambda bb, h, i, j: (bb, h, k_block0 + j)),
            pl.BlockSpec((None, tk, A_DV), lambda bb, h, i, j: (bb, k_block0 + j, h)),
            pl.BlockSpec((4, A_DH), lambda bb, h, i, j: (0, 0)),
            pl.BlockSpec((1, A_DV), lambda bb, h, i, j: (0, 0)),
        ],
        out_specs=pl.BlockSpec((None, tq, A_DV), lambda bb, h, i, j: (bb, i, h)),
        scratch_shapes=[
            pltpu.VMEM((2 * tq, A_DV), BF16),
            pltpu.VMEM((2 * tq, LANES), F32),
            pltpu.VMEM((2 * tq, 2 * LANES), F32),
        ],
        compiler_params=_cparams(("parallel", "parallel", "parallel", "arbitrary")),
        name="diff_attn",
    )(q, kt, v, lam_p, gain)


def _router_gates(sc_t, sel_t, gates_ref):
    s_rows = [sc_t[e:e + 1, :] for e in range(N_EXPERTS)]
    v_rows = [sel_t[e:e + 1, :] for e in range(N_EXPERTS)]
    in_top2 = []
    group_score = []
    for g in range(N_EXPERT_GROUPS):
        vs = v_rows[g * EXPERTS_PER_GROUP:(g + 1) * EXPERTS_PER_GROUP]
        tops = []
        for jj in range(EXPERTS_PER_GROUP):
            rank = jnp.zeros_like(vs[jj])
            for ii in range(EXPERTS_PER_GROUP):
                if ii == jj:
                    continue
                beats = (vs[ii] >= vs[jj]) if ii < jj else (vs[ii] > vs[jj])
                rank = rank + jnp.where(beats, 1.0, 0.0)
            tops.append(rank < 2.0)
        in_top2 += tops
        gs = jnp.zeros_like(vs[0])
        for jj in range(EXPERTS_PER_GROUP):
            gs = gs + jnp.where(tops[jj], vs[jj], 0.0)
        group_score.append(gs)
    best = group_score[0]
    gidx = jnp.zeros_like(best)
    for g in range(1, N_EXPERT_GROUPS):
        upd = group_score[g] > best
        best = jnp.where(upd, group_score[g], best)
        gidx = jnp.where(upd, float(g), gidx)
    chosen = [jnp.logical_and(in_top2[e], gidx == float(e // EXPERTS_PER_GROUP))
              for e in range(N_EXPERTS)]
    denom = jnp.zeros_like(best)
    for e in range(N_EXPERTS):
        denom = denom + jnp.where(chosen[e], s_rows[e], 0.0)
    for e in range(N_EXPERTS):
        gates_ref[e:e + 1, :] = jnp.where(chosen[e], s_rows[e] / denom, 0.0)


def _outproj_kernel(yf_ref, att_ref, sg_ref, x_ref, mod_ref, wo_ref, lng_ref, lnb_ref,
                    wr_ref, rb_ref, x1_ref, h_ref, gates_ref, *, alpha):
    mix = _bdot(yf_ref[...], wo_ref[0:F_WIDTH, :])
    mix = mix + _bdot(att_ref[...], wo_ref[F_WIDTH:F_WIDTH + A_WIDTH, :])
    mix = mix + _bdot(sg_ref[...], wo_ref[F_WIDTH + A_WIDTH:, :])
    g1 = mod_ref[2:3, :]
    x1 = _layernorm(alpha * x_ref[...] + g1 * mix, lng_ref[...], lnb_ref[...])
    x1_ref[...] = x1
    h = x1 * (1.0 + mod_ref[4:5, :]) + mod_ref[3:4, :]
    h_ref[...] = h.astype(BF16)
    scores = jax.nn.sigmoid(_hdot(h, wr_ref[...]))
    sel = scores + rb_ref[...]
    _router_gates(scores.T[0:N_EXPERTS, :], sel.T[0:N_EXPERTS, :], gates_ref)


def _outproj(yf, att, sg, xa, mods_l, wo, lng, lnb, wr, rb, n_lat, alpha):
    b, l, d = xa.shape
    tm = TOKEN_TILE
    nlat = n_lat // tm
    seg = lambda i: jnp.where(i >= nlat, 1, 0)
    row = lambda width: pl.BlockSpec((None, tm, width), lambda bb, i: (bb, i, 0))
    const2 = lambda shape: pl.BlockSpec(shape, lambda bb, i: (0, 0))
    return pl.pallas_call(
        functools.partial(_outproj_kernel, alpha=alpha),
        out_shape=(
            jax.ShapeDtypeStruct((b, l, d), F32),
            jax.ShapeDtypeStruct((b, l, d), BF16),
            jax.ShapeDtypeStruct((b, N_EXPERTS, l), F32),
        ),
        grid=(b, l // tm),
        in_specs=[
            row(F_WIDTH), row(A_WIDTH), row(G_WIDTH), row(d),
            pl.BlockSpec((None, None, MOD_ROWS, d), lambda bb, i: (bb, seg(i), 0, 0)),
            const2((d, d)), const2((1, d)), const2((1, d)),
            const2((d, ROUTER_LANES)), const2((1, ROUTER_LANES)),
        ],
        out_specs=(row(d), row(d),
                   pl.BlockSpec((None, N_EXPERTS, tm), lambda bb, i: (bb, 0, i))),
        compiler_params=_cparams(("parallel", "parallel")),
        name="outproj_ln_router",
    )(yf, att, sg, xa, mods_l, wo, lng, lnb, wr, rb)


def _moe_kernel(*refs, alpha, aliased):
    if aliased:
        refs = refs[1:]
    (h_ref, gates_ref, x_ref, mod_ref, w1_ref, w3_ref, w2_ref, lng_ref, lnb_ref,
     o_ref, acc_ref) = refs
    e = pl.program_id(2)

    @pl.when(e == 0)
    def _init():
        acc_ref[...] = jnp.zeros(acc_ref.shape, F32)

    h = h_ref[...]
    a = _bdot(h, w1_ref[...])
    a = (a * jax.nn.sigmoid(a)) * _bdot(h, w3_ref[...])
    y = _bdot(a.astype(BF16), w2_ref[...])
    gates = gates_ref[...]
    lane = lax.broadcasted_iota(jnp.int32, gates.shape, 1)
    ge = jnp.sum(jnp.where(lane == e, gates, 0.0), axis=1, keepdims=True)
    acc_ref[...] += ge * y

    @pl.when(e == N_EXPERTS - 1)
    def _fin():
        g2 = mod_ref[5:6, :]
        o_ref[...] = _layernorm(alpha * x_ref[...] + g2 * acc_ref[...], lng_ref[...], lnb_ref[...])


def _moe(h, gates, x1, mods_l, w1, w3, w2, lng, lnb, *, row_block0, n_rows, tm, seg, out_rows,
         alpha, carry=None):
    b, _, d = x1.shape
    de = w1.shape[-1]
    row = lambda width: pl.BlockSpec((None, tm, width), lambda bb, i, e: (bb, row_block0 + i, 0))
    const2 = lambda shape: pl.BlockSpec(shape, lambda bb, i, e: (0, 0))
    in_specs = [
        row(d), row(N_EXPERTS), row(d),
        pl.BlockSpec((None, None, MOD_ROWS, d), lambda bb, i, e: (bb, seg, 0, 0)),
        pl.BlockSpec((None, d, de), lambda bb, i, e: (e, 0, 0)),
        pl.BlockSpec((None, d, de), lambda bb, i, e: (e, 0, 0)),
        pl.BlockSpec((None, de, d), lambda bb, i, e: (e, 0, 0)),
        const2((1, d)), const2((1, d)),
    ]
    args = [h, gates, x1, mods_l, w1, w3, w2, lng, lnb]
    aliases = {}
    if carry is not None:
        in_specs = [pl.BlockSpec(memory_space=pl.ANY)] + in_specs
        args = [carry] + args
        aliases = {0: 0}
    return pl.pallas_call(
        functools.partial(_moe_kernel, alpha=alpha, aliased=carry is not None),
        out_shape=jax.ShapeDtypeStruct((b, out_rows, d), F32),
        grid=(b, n_rows // tm, N_EXPERTS),
        in_specs=in_specs,
        out_specs=pl.BlockSpec((None, tm, d), lambda bb, i, e: (bb, row_block0 + i, 0)),
        scratch_shapes=[pltpu.VMEM((tm, d), F32)],
        input_output_aliases=aliases,
        compiler_params=_cparams(("parallel", "parallel", "arbitrary")),
        name="moe_ln",
    )(*args)


def _head_lane_fields():
    j = jnp.arange(LANES)
    half = j // (LANES // 2)
    comp = (j % (LANES // 2)) // (A_DH // 2)
    axis = (j % (A_DH // 2)) // (A_DH // 4)
    freq = j % (A_DH // 4)
    return half, comp, axis, freq


def _qk_column_perm():
    half, comp, axis, freq = _head_lane_fields()
    orig = comp * A_DH + axis * (A_DH // 2) + half * (A_DH // 4) + freq
    return (jnp.arange(A_HEADS)[:, None] * LANES + orig[None, :]).reshape(-1)


def _rope_tables(n_lat, n_ctx):
    half, _, axis, freq = _head_lane_fields()
    rows = n_lat // GRID_W
    row = jnp.repeat(jnp.arange(rows, dtype=F32), GRID_W)
    col = jnp.tile(jnp.arange(GRID_W, dtype=F32), rows)
    hd = A_DH // 2
    inv = ROPE_BASE ** (-jnp.arange(0, hd, 2, dtype=F32) / hd)
    pos = jnp.where(axis[None, :] == 0, row[:, None], col[:, None])
    ang = pos * inv[freq][None, :]
    cosf = jnp.cos(ang)
    sinf = jnp.sin(ang) * jnp.where(half == 0, -1.0, 1.0)[None, :]
    cosf = jnp.concatenate([cosf, jnp.ones((n_ctx, LANES), F32)], axis=0)
    sinf = jnp.concatenate([sinf, jnp.zeros((n_ctx, LANES), F32)], axis=0)
    return cosf, sinf


def _dft_mats(n, scale=1.0):
    i = jnp.arange(n, dtype=jnp.int32)
    ang = ((i[:, None] * i[None, :]) % n).astype(F32) * (2.0 * math.pi / n)
    return jnp.cos(ang) * scale, jnp.sin(ang) * scale


def _twiddles(n, bn):
    ka = jnp.arange(DFT_A, dtype=jnp.int32)
    bb = jnp.arange(bn, dtype=jnp.int32)
    ang = ((ka[:, None] * bb[None, :]) % n).astype(F32) * (2.0 * math.pi / n)
    shape = (DFT_A, bn, F_WIDTH)
    twc = jnp.broadcast_to(jnp.cos(ang)[:, :, None], shape).reshape(DFT_A, bn * F_WIDTH)
    tws = jnp.broadcast_to(jnp.sin(ang)[:, :, None], shape).reshape(DFT_A, bn * F_WIDTH)
    return twc, tws


def _block_diag(blocks):
    g, n = blocks.shape[-3], blocks.shape[-1]
    eye = jnp.eye(g, dtype=blocks.dtype)
    out = blocks[..., :, :, None, :] * eye[:, None, :, None]
    return out.reshape(blocks.shape[:-3] + (g * n, g * n))


def kernel(x, c, ctx, c_ctx, w_ada, b_ada, w_in, w_fourier, diff_lambda, diff_subln,
           sgu_ln_g, sgu_ln_b, sgu_w, sgu_b, w_out, ln_g, ln_b, w_router, router_bias,
           moe_w1, moe_w3, moe_w2):
    b, n, d = x.shape
    n_ctx = ctx.shape[1]
    depth = w_ada.shape[0]
    l = n + n_ctx
    alpha = (2 * depth) ** 0.25
    bn = n // DFT_A
    assert n % DFT_A == 0 and bn % SUBLANES == 0 and n % GRID_W == 0
    assert n_ctx % TOKEN_TILE == 0 and n % TOKEN_TILE == 0 and n % n_ctx == 0
    assert b + 1 <= MOD_ROWS

    c_rows = jnp.concatenate([c, c_ctx[None, :], jnp.zeros((MOD_ROWS - b - 1, d), F32)], axis=0)
    mod = _ada(c_rows, w_ada, b_ada).reshape(depth, MOD_ROWS, 6, d)
    lat = mod[:, :b]
    cm = jnp.broadcast_to(mod[:, b:b + 1], lat.shape)
    mods = jnp.stack([lat, cm], axis=2)
    mods = jnp.pad(mods, ((0, 0), (0, 0), (0, 0), (0, MOD_ROWS - 6), (0, 0)))

    perm = _qk_column_perm()
    c64, s64 = _dft_mats(F_DIM)
    eye_g = jnp.eye(F_GROUPS, dtype=F32)
    bdc = jnp.kron(eye_g, c64)
    bds = jnp.kron(eye_g, s64)
    bdw = _block_diag(w_fourier)
    w_fab = _fourier_weights(w_in[:, :, :F_WIDTH], bdc, bds, bdw)
    o_q = F_WIDTH
    o_k = o_q + A_QK_WIDTH
    o_v = o_k + A_QK_WIDTH
    o_g = o_v + A_WIDTH
    w_big = jnp.concatenate([
        w_fab,
        w_in[:, :, o_q:o_k][:, :, perm],
        w_in[:, :, o_k:o_v][:, :, perm],
        w_in[:, :, o_v:],
    ], axis=-1).astype(BF16)
    wo = w_out.astype(BF16)
    w1 = moe_w1.astype(BF16)
    w3 = moe_w3.astype(BF16)
    w2 = moe_w2.astype(BF16)
    ws = sgu_w.astype(BF16)
    bs_full = jnp.repeat(jnp.swapaxes(sgu_b, 1, 2), G_DIM, axis=2)
    wr = jnp.pad(w_router, ((0, 0), (0, ROUTER_LANES - N_EXPERTS)))
    rb = jnp.pad(router_bias, (0, ROUTER_LANES - N_EXPERTS)).reshape(1, ROUTER_LANES)

    cosf, sinf = _rope_tables(n, n_ctx)
    ca, sa = _dft_mats(DFT_A)
    twc, tws = _twiddles(n, bn)
    cb, sb = _dft_mats(bn, scale=(n * F_DIM) ** -0.5)
    cc, sc = _dft_mats(n_ctx, scale=(n_ctx * F_DIM) ** -0.5)
    q_scale = (A_DH ** -0.5) * math.log2(math.e)

    tk = max(t for t in (1280, 1024, 768, 512, 256) if l % t == 0)
    moe_tile = min(MOE_TILE, n)
    assert n % moe_tile == 0

    xa = jnp.concatenate([x, ctx], axis=1)
    for li in range(depth):
        last = li == depth - 1
        lam_init = 0.8 - 0.6 * math.exp(-0.3 * li)
        fa, fb, q, k, v, sg = _inproj(
            xa, mods[li], w_big[li], cosf, sinf, sgu_ln_g[li][None], sgu_ln_b[li][None],
            ws[li], bs_full[li], n, q_scale)
        zr, zi = _fft_a(fa, fb, ca, sa, twc, tws, bn)
        yf = jnp.concatenate([_fft_b(zr, zi, cb, sb, bn), _dft_ctx(fa, fb, cc, sc, n, n_ctx)], axis=1)
        kt = jnp.swapaxes(k, 1, 2)
        gain = diff_subln[li][None]
        att_x = _attention(q, kt, v, diff_lambda[li], gain, q_block0=0, n_q=n, k_block0=0, n_k=l,
                           tq=ATTN_TQ, tk=tk, lam_init=lam_init)
        att_c = _attention(q, kt, v, diff_lambda[li], gain, q_block0=n // n_ctx, n_q=n_ctx,
                           k_block0=n // n_ctx, n_k=n_ctx, tq=n_ctx, tk=n_ctx, lam_init=lam_init)
        att = jnp.concatenate([att_x, att_c], axis=1)
        x1, h, gates_t = _outproj(yf, att, sg, xa, mods[li], wo[li], ln_g[li, 0][None],
                                  ln_b[li, 0][None], wr, rb, n, alpha)
        gates = jnp.swapaxes(gates_t, 1, 2)
        lng2, lnb2 = ln_g[li, 1][None], ln_b[li, 1][None]
        xa = _moe(h, gates, x1, mods[li], w1[li], w3[li], w2[li], lng2, lnb2, row_block0=0,
                  n_rows=n, tm=moe_tile, seg=0, out_rows=n if last else l, alpha=alpha)
        if not last:
            xa = _moe(h, gates, x1, mods[li], w1[li], w3[li], w2[li], lng2, lnb2,
                      row_block0=n // n_ctx, n_rows=n_ctx, tm=n_ctx, seg=1, out_rows=l,
                      alpha=alpha, carry=xa)
    return xa
```

```python
import functools
import math

import jax
import jax.numpy as jnp
from jax import lax
from jax.experimental import pallas as pl
from jax.experimental.pallas import tpu as pltpu

F32 = jnp.float32
BF16 = jnp.bfloat16
HIGHEST = lax.Precision.HIGHEST

GRID_W = 64
F_GROUPS, F_DIM = 4, 64
F_WIDTH = F_GROUPS * F_DIM
A_HEADS, A_DH = 4, 64
A_DV = 2 * A_DH
A_QK_WIDTH = A_HEADS * 2 * A_DH
A_WIDTH = A_HEADS * A_DV
G_GROUPS, G_DIM = 4, 64
G_WIDTH = G_GROUPS * G_DIM
CHUNK = 128
ROPE_BASE = 10000.0
N_EXPERTS = 16
N_EXPERT_GROUPS = 4
EXPERTS_PER_GROUP = N_EXPERTS // N_EXPERT_GROUPS
LN_EPS = 1e-5
RMS_EPS = 1e-5

COL_FA = 0
COL_FB = COL_FA + F_WIDTH
COL_Q = COL_FB + F_WIDTH
COL_K = COL_Q + A_QK_WIDTH
COL_V = COL_K + A_QK_WIDTH
COL_G = COL_V + A_WIDTH
BIG_COLS = COL_G + 2 * G_WIDTH

LANES = 128
SUBLANES = 8
MOD_ROWS = 8
ROUTER_LANES = 128
DFT_A = 128
VMEM_LIMIT = 56 * 1024 * 1024

TOKEN_TILE = 256
ATTN_TQ = 256
MOE_TILE = 1024


def _cparams(sem):
    return pltpu.CompilerParams(dimension_semantics=sem, vmem_limit_bytes=VMEM_LIMIT)


def _hdot(a, b):
    return jnp.dot(a, b, precision=HIGHEST, preferred_element_type=F32)


def _bdot(a, b):
    return jnp.dot(a, b, preferred_element_type=F32)


def _layernorm(y, g, b):
    mu = jnp.mean(y, axis=-1, keepdims=True)
    d = y - mu
    var = jnp.mean(d * d, axis=-1, keepdims=True)
    return d * lax.rsqrt(var + LN_EPS) * g + b


def _ada_kernel(c_ref, w_ref, b_ref, o_ref):
    c = c_ref[...]
    a = c * jax.nn.sigmoid(c)
    o_ref[...] = _hdot(a, w_ref[...]) + b_ref[...]


def _ada(c_rows, w_ada, b_ada):
    depth, d, cols = w_ada.shape
    tn = 1536
    assert cols % tn == 0
    return pl.pallas_call(
        _ada_kernel,
        out_shape=jax.ShapeDtypeStruct((depth, MOD_ROWS, cols), F32),
        grid=(depth, cols // tn),
        in_specs=[
            pl.BlockSpec((MOD_ROWS, d), lambda l, j: (0, 0)),
            pl.BlockSpec((None, d, tn), lambda l, j: (l, 0, j)),
            pl.BlockSpec((None, 1, tn), lambda l, j: (l, 0, j)),
        ],
        out_specs=pl.BlockSpec((None, MOD_ROWS, tn), lambda l, j: (l, 0, j)),
        compiler_params=_cparams(("parallel", "parallel")),
        name="ada_mod",
    )(c_rows, w_ada, b_ada.reshape(depth, 1, cols))


def _fw_kernel(wf_ref, bdc_ref, bds_ref, bdw_ref, o_ref):
    bdw = bdw_ref[...]
    mc = _hdot(bdc_ref[...], bdw)
    ms = _hdot(bds_ref[...], bdw)
    wf = wf_ref[...]
    o_ref[:, :F_WIDTH] = _hdot(wf, mc)
    o_ref[:, F_WIDTH:] = _hdot(wf, ms)


def _fourier_weights(wf, bdc, bds, bdw):
    depth, d, _ = wf.shape
    return pl.pallas_call(
        _fw_kernel,
        out_shape=jax.ShapeDtypeStruct((depth, d, 2 * F_WIDTH), F32),
        grid=(depth,),
        in_specs=[
            pl.BlockSpec((None, d, F_WIDTH), lambda l: (l, 0, 0)),
            pl.BlockSpec((F_WIDTH, F_WIDTH), lambda l: (0, 0)),
            pl.BlockSpec((F_WIDTH, F_WIDTH), lambda l: (0, 0)),
            pl.BlockSpec((None, F_WIDTH, F_WIDTH), lambda l: (l, 0, 0)),
        ],
        out_specs=pl.BlockSpec((None, d, 2 * F_WIDTH), lambda l: (l, 0, 0)),
        compiler_params=_cparams(("parallel",)),
        name="fourier_weights",
    )(wf, bdc, bds, bdw)


def _gelu_tanh(x):
    c = math.sqrt(2.0 / math.pi)
    return x * (0.5 * (1.0 + jnp.tanh(c * (x + 0.044715 * (x * x * x)))))


def _inproj_kernel(x_ref, mod_ref, w_ref, cos_ref, sin_ref, lng_ref, lnb_ref, ws_ref, bs_ref,
                   fa_ref, fb_ref, q_ref, k_ref, v_ref, sg_ref, *, q_scale):
    x = x_ref[...]
    sh = mod_ref[0:1, :]
    sc = mod_ref[1:2, :]
    h = (x * (1.0 + sc) + sh).astype(BF16)
    p = _bdot(h, w_ref[...])
    fa_ref[...] = p[:, COL_FA:COL_FA + F_WIDTH]
    fb_ref[...] = p[:, COL_FB:COL_FB + F_WIDTH]
    cosf = cos_ref[...]
    sinf = sin_ref[...]
    for hh in range(A_HEADS):
        t = p[:, COL_Q + LANES * hh:COL_Q + LANES * (hh + 1)]
        r = t * cosf + pltpu.roll(t, LANES // 2, 1) * sinf
        q_ref[:, LANES * hh:LANES * (hh + 1)] = (r * q_scale).astype(BF16)
        t = p[:, COL_K + LANES * hh:COL_K + LANES * (hh + 1)]
        r = t * cosf + pltpu.roll(t, LANES // 2, 1) * sinf
        k_ref[:, LANES * hh:LANES * (hh + 1)] = r.astype(BF16)
    v_ref[...] = p[:, COL_V:COL_V + A_WIDTH].astype(BF16)

    g = _gelu_tanh(p[:, COL_G:COL_G + 2 * G_WIDTH])
    u = g[:, :G_WIDTH]
    vn = _layernorm(g[:, G_WIDTH:], lng_ref[...], lnb_ref[...])
    lane_group = lax.broadcasted_iota(jnp.int32, (CHUNK, G_WIDTH), 1) // G_DIM
    tm = x.shape[0]
    for c in range(tm // CHUNK):
        rows = slice(c * CHUNK, (c + 1) * CHUNK)
        vc = vn[rows]
        mixed = bs_ref[...]
        for gi in range(G_GROUPS):
            vm = jnp.where(lane_group == gi, vc, 0.0).astype(BF16)
            mixed = mixed + _bdot(ws_ref[gi], vm)
        sg_ref[rows, :] = (u[rows] * mixed).astype(BF16)


def _inproj(xa, mods_l, w_big, cosf, sinf, lng, lnb, ws, bs_full, n_lat, q_scale):
    b, l, d = xa.shape
    tm = TOKEN_TILE
    nlat = n_lat // tm
    seg = lambda i: jnp.where(i >= nlat, 1, 0)
    row = lambda width: pl.BlockSpec((None, tm, width), lambda bb, i: (bb, i, 0))
    const2 = lambda shape: pl.BlockSpec(shape, lambda bb, i: (0, 0))
    out_shapes = (
        jax.ShapeDtypeStruct((b, l, F_WIDTH), F32),
        jax.ShapeDtypeStruct((b, l, F_WIDTH), F32),
        jax.ShapeDtypeStruct((b, l, A_QK_WIDTH), BF16),
        jax.ShapeDtypeStruct((b, l, A_QK_WIDTH), BF16),
        jax.ShapeDtypeStruct((b, l, A_WIDTH), BF16),
        jax.ShapeDtypeStruct((b, l, G_WIDTH), BF16),
    )
    return pl.pallas_call(
        functools.partial(_inproj_kernel, q_scale=q_scale),
        out_shape=out_shapes,
        grid=(b, l // tm),
        in_specs=[
            row(d),
            pl.BlockSpec((None, None, MOD_ROWS, d), lambda bb, i: (bb, seg(i), 0, 0)),
            const2((d, BIG_COLS)),
            pl.BlockSpec((tm, LANES), lambda bb, i: (i, 0)),
            pl.BlockSpec((tm, LANES), lambda bb, i: (i, 0)),
            const2((1, G_WIDTH)),
            const2((1, G_WIDTH)),
            pl.BlockSpec((G_GROUPS, CHUNK, CHUNK), lambda bb, i: (0, 0, 0)),
            const2((CHUNK, G_WIDTH)),
        ],
        out_specs=(row(F_WIDTH), row(F_WIDTH), row(A_QK_WIDTH), row(A_QK_WIDTH),
                   row(A_WIDTH), row(G_WIDTH)),
        compiler_params=_cparams(("parallel", "parallel")),
        name="inproj",
    )(xa, mods_l, w_big, cosf, sinf, lng, lnb, ws, bs_full)


def _fft_a_kernel(fa_ref, fb_ref, c_ref, s_ref, tc_ref, ts_ref, zr_ref, zi_ref):
    fa = fa_ref[...]
    fb = fb_ref[...]
    cm = c_ref[...]
    sm = s_ref[...]
    zr = _hdot(cm, fa) - _hdot(sm, fb)
    zi = -(_hdot(cm, fb) + _hdot(sm, fa))
    tc = tc_ref[...]
    ts = ts_ref[...]
    zr_ref[...] = zr * tc + zi * ts
    zi_ref[...] = zi * tc - zr * ts


def _fft_a(fa, fb, cm, sm, twc, tws, bn):
    b, l, _ = fa.shape
    cols = bn * F_WIDTH
    fa3 = fa.reshape(b, l // bn, cols)
    fb3 = fb.reshape(b, l // bn, cols)
    tcw = min(cols, 2048)
    assert cols % tcw == 0
    blk = pl.BlockSpec((None, DFT_A, tcw), lambda bb, j: (bb, 0, j))
    tw = pl.BlockSpec((DFT_A, tcw), lambda bb, j: (0, j))
    mat = pl.BlockSpec((DFT_A, DFT_A), lambda bb, j: (0, 0))
    return pl.pallas_call(
        _fft_a_kernel,
        out_shape=(jax.ShapeDtypeStruct((b, DFT_A, cols), F32),) * 2,
        grid=(b, cols // tcw),
        in_specs=[blk, blk, mat, mat, tw, tw],
        out_specs=(blk, blk),
        compiler_params=_cparams(("parallel", "parallel")),
        name="fft_stage_a",
    )(fa3, fb3, cm, sm, twc, tws)


def _fft_b_kernel(zr_ref, zi_ref, c_ref, s_ref, o_ref):
    cb = c_ref[...]
    sb = s_ref[...]
    for j in range(zr_ref.shape[0]):
        y = _hdot(cb, zr_ref[j]) + _hdot(sb, zi_ref[j])
        o_ref[:, F_WIDTH * j:F_WIDTH * (j + 1)] = y.astype(o_ref.dtype)


def _fft_b(zr, zi, cb, sb, bn):
    b = zr.shape[0]
    kb = 8
    zr4 = zr.reshape(b, DFT_A, bn, F_WIDTH)
    zi4 = zi.reshape(b, DFT_A, bn, F_WIDTH)
    blk = pl.BlockSpec((None, kb, bn, F_WIDTH), lambda bb, i: (bb, i, 0, 0))
    mat = pl.BlockSpec((bn, bn), lambda bb, i: (0, 0))
    out = pl.pallas_call(
        _fft_b_kernel,
        out_shape=jax.ShapeDtypeStruct((b, bn, DFT_A * F_WIDTH), BF16),
        grid=(b, DFT_A // kb),
        in_specs=[blk, blk, mat, mat],
        out_specs=pl.BlockSpec((None, bn, kb * F_WIDTH), lambda bb, i: (bb, 0, i)),
        compiler_params=_cparams(("parallel", "parallel")),
        name="fft_stage_b",
    )(zr4, zi4, cb, sb)
    return out.reshape(b, bn * DFT_A, F_WIDTH)


def _dft_ctx_kernel(fa_ref, fb_ref, c_ref, s_ref, o_ref):
    y = _hdot(c_ref[...], fa_ref[...]) - _hdot(s_ref[...], fb_ref[...])
    o_ref[...] = y.astype(o_ref.dtype)


def _dft_ctx(fa, fb, cc, sc, n_lat, n_ctx):
    b = fa.shape[0]
    blk = pl.BlockSpec((None, n_ctx, F_WIDTH), lambda bb: (bb, n_lat // n_ctx, 0))
    mat = pl.BlockSpec((n_ctx, n_ctx), lambda bb: (0, 0))
    return pl.pallas_call(
        _dft_ctx_kernel,
        out_shape=jax.ShapeDtypeStruct((b, n_ctx, F_WIDTH), BF16),
        grid=(b,),
        in_specs=[blk, blk, mat, mat],
        out_specs=pl.BlockSpec((None, n_ctx, F_WIDTH), lambda bb: (bb, 0, 0)),
        compiler_params=_cparams(("parallel",)),
        name="dft_ctx",
    )(fa, fb, cc, sc)


def _attn_kernel(q_ref, kt_ref, v_ref, lam_ref, gain_ref, o_ref, q2_ref, s_ref, m_ref, acc_ref,
                 *, lam_init):
    tq = q_ref.shape[0]
    nk, _, tk = kt_ref.shape

    q = q_ref[...]
    lane = lax.broadcasted_iota(jnp.int32, q.shape, 1)
    comp0 = (lane % A_DH) < (A_DH // 2)
    zero = jnp.zeros_like(q)
    q2_ref[0:tq, :] = jnp.where(comp0, q, zero)
    q2_ref[tq:2 * tq, :] = jnp.where(comp0, zero, q)
    m_ref[...] = jnp.full(m_ref.shape, -jnp.inf, F32)
    acc_ref[...] = jnp.zeros(acc_ref.shape, F32)

    def scores(j, slot):
        s_ref[slot] = _bdot(q2_ref[...], kt_ref[j])

    def softmax_pv(j, slot):
        s = s_ref[slot]
        m_old = m_ref[...]
        m_new = jnp.maximum(m_old, jnp.max(s, axis=1, keepdims=True))
        alpha = jnp.exp2(m_old - m_new)
        p = jnp.exp2(s - jnp.concatenate([m_new] * (tk // LANES), axis=1)).astype(BF16)
        pv = _bdot(p, v_ref[j])
        acc_ref[...] = acc_ref[...] * jnp.concatenate([alpha, alpha], axis=1) + pv
        m_ref[...] = m_new

    scores(0, 0)

    def pair(jj, carry):
        j0 = 2 * jj
        scores(j0 + 1, 1)
        softmax_pv(j0, 0)
        scores(j0 + 2, 0)
        softmax_pv(j0 + 1, 1)
        return carry

    n_pairs = (nk - 1) // 2
    lax.fori_loop(0, n_pairs, pair, 0)
    j_tail = 2 * n_pairs
    if nk - j_tail == 2:
        scores(j_tail + 1, 1)
        softmax_pv(j_tail, 0)
        softmax_pv(j_tail + 1, 1)
    else:
        softmax_pv(j_tail, 0)

    acc = acc_ref[...]
    o0 = acc[0:tq, 0:LANES] / acc[0:tq, LANES:]
    o1 = acc[tq:, 0:LANES] / acc[tq:, LANES:]
    lp = lam_ref[...]
    lam = (jnp.exp(jnp.sum(lp[0:1] * lp[1:2], keepdims=True))
           - jnp.exp(jnp.sum(lp[2:3] * lp[3:4], keepdims=True)) + lam_init)
    o = o0 - lam * o1
    ms = jnp.mean(o * o, axis=-1, keepdims=True)
    o = o * lax.rsqrt(ms + RMS_EPS) * gain_ref[...] * (1.0 - lam_init)
    o_ref[...] = o.astype(o_ref.dtype)


def _attention(q, kt, v_ext, lam_p, gain, *, q_block0, n_q, tq, lam_init):
    b = q.shape[0]
    _, _, nk, _, tk = kt.shape
    return pl.pallas_call(
        functools.partial(_attn_kernel, lam_init=lam_init),
        out_shape=jax.ShapeDtypeStruct((b, n_q, A_WIDTH), BF16),
        grid=(b, A_HEADS, n_q // tq),
        in_specs=[
            pl.BlockSpec((None, tq, A_DV), lambda bb, h, i: (bb, q_block0 + i, h)),
            pl.BlockSpec((None, None, nk, A_DV, tk), lambda bb, h, i: (bb, h, 0, 0, 0)),
            pl.BlockSpec((None, None, nk, tk, 2 * LANES), lambda bb, h, i: (bb, h, 0, 0, 0)),
            pl.BlockSpec((4, A_DH), lambda bb, h, i: (0, 0)),
            pl.BlockSpec((1, A_DV), lambda bb, h, i: (0, 0)),
        ],
        out_specs=pl.BlockSpec((None, tq, A_DV), lambda bb, h, i: (bb, i, h)),
        scratch_shapes=[
            pltpu.VMEM((2 * tq, A_DV), BF16),
            pltpu.VMEM((2, 2 * tq, tk), F32),
            pltpu.VMEM((2 * tq, LANES), F32),
            pltpu.VMEM((2 * tq, 2 * LANES), F32),
        ],
        compiler_params=_cparams(("parallel", "parallel", "arbitrary")),
        name="diff_attn",
    )(q, kt, v_ext, lam_p, gain)


def _key_value_chunks(k, v, tk):
    b, nkeys, _ = k.shape
    nk = nkeys // tk
    kt = k.reshape(b, nk, tk, A_HEADS, A_DV).transpose(0, 3, 1, 4, 2)
    vh = v.reshape(b, nk, tk, A_HEADS, A_DV).transpose(0, 3, 1, 2, 4)
    v_ext = jnp.concatenate([vh, jnp.ones_like(vh)], axis=-1)
    return kt, v_ext


def _router_gates(sc_t, sel_t, gates_ref):
    s_rows = [sc_t[e:e + 1, :] for e in range(N_EXPERTS)]
    v_rows = [sel_t[e:e + 1, :] for e in range(N_EXPERTS)]
    in_top2 = []
    group_score = []
    for g in range(N_EXPERT_GROUPS):
        vs = v_rows[g * EXPERTS_PER_GROUP:(g + 1) * EXPERTS_PER_GROUP]
        tops = []
        for jj in range(EXPERTS_PER_GROUP):
            rank = jnp.zeros_like(vs[jj])
            for ii in range(EXPERTS_PER_GROUP):
                if ii == jj:
                    continue
                beats = (vs[ii] >= vs[jj]) if ii < jj else (vs[ii] > vs[jj])
                rank = rank + jnp.where(beats, 1.0, 0.0)
            tops.append(rank < 2.0)
        in_top2 += tops
        gs = jnp.zeros_like(vs[0])
        for jj in range(EXPERTS_PER_GROUP):
            gs = gs + jnp.where(tops[jj], vs[jj], 0.0)
        group_score.append(gs)
    best = group_score[0]
    gidx = jnp.zeros_like(best)
    for g in range(1, N_EXPERT_GROUPS):
        upd = group_score[g] > best
        best = jnp.where(upd, group_score[g], best)
        gidx = jnp.where(upd, float(g), gidx)
    chosen = [jnp.logical_and(in_top2[e], gidx == float(e // EXPERTS_PER_GROUP))
              for e in range(N_EXPERTS)]
    denom = jnp.zeros_like(best)
    for e in range(N_EXPERTS):
        denom = denom + jnp.where(chosen[e], s_rows[e], 0.0)
    for e in range(N_EXPERTS):
        gates_ref[e:e + 1, :] = jnp.where(chosen[e], s_rows[e] / denom, 0.0)


def _outproj_kernel(yf_ref, att_ref, sg_ref, x_ref, mod_ref, wo_ref, lng_ref, lnb_ref,
                    wr_ref, rb_ref, x1_ref, h_ref, gates_ref, *, alpha):
    mix = _bdot(yf_ref[...], wo_ref[0:F_WIDTH, :])
    mix = mix + _bdot(att_ref[...], wo_ref[F_WIDTH:F_WIDTH + A_WIDTH, :])
    mix = mix + _bdot(sg_ref[...], wo_ref[F_WIDTH + A_WIDTH:, :])
    g1 = mod_ref[2:3, :]
    x1 = _layernorm(alpha * x_ref[...] + g1 * mix, lng_ref[...], lnb_ref[...])
    x1_ref[...] = x1
    h = x1 * (1.0 + mod_ref[4:5, :]) + mod_ref[3:4, :]
    h_ref[...] = h.astype(BF16)
    scores = jax.nn.sigmoid(_hdot(h, wr_ref[...]))
    sel = scores + rb_ref[...]
    _router_gates(scores.T[0:N_EXPERTS, :], sel.T[0:N_EXPERTS, :], gates_ref)


def _outproj(yf, att, sg, xa, mods_l, wo, lng, lnb, wr, rb, n_lat, alpha):
    b, l, d = xa.shape
    tm = TOKEN_TILE
    nlat = n_lat // tm
    seg = lambda i: jnp.where(i >= nlat, 1, 0)
    row = lambda width: pl.BlockSpec((None, tm, width), lambda bb, i: (bb, i, 0))
    const2 = lambda shape: pl.BlockSpec(shape, lambda bb, i: (0, 0))
    return pl.pallas_call(
        functools.partial(_outproj_kernel, alpha=alpha),
        out_shape=(
            jax.ShapeDtypeStruct((b, l, d), F32),
            jax.ShapeDtypeStruct((b, l, d), BF16),
            jax.ShapeDtypeStruct((b, N_EXPERTS, l), F32),
        ),
        grid=(b, l // tm),
        in_specs=[
            row(F_WIDTH), row(A_WIDTH), row(G_WIDTH), row(d),
            pl.BlockSpec((None, None, MOD_ROWS, d), lambda bb, i: (bb, seg(i), 0, 0)),
            const2((d, d)), const2((1, d)), const2((1, d)),
            const2((d, ROUTER_LANES)), const2((1, ROUTER_LANES)),
        ],
        out_specs=(row(d), row(d),
                   pl.BlockSpec((None, N_EXPERTS, tm), lambda bb, i: (bb, 0, i))),
        compiler_params=_cparams(("parallel", "parallel")),
        name="outproj_ln_router",
    )(yf, att, sg, xa, mods_l, wo, lng, lnb, wr, rb)


def _moe_kernel(*refs, alpha, aliased):
    if aliased:
        refs = refs[1:]
    (h_ref, gates_ref, x_ref, mod_ref, w1_ref, w3_ref, w2_ref, lng_ref, lnb_ref,
     o_ref, acc_ref) = refs
    e = pl.program_id(2)

    @pl.when(e == 0)
    def _init():
        acc_ref[...] = jnp.zeros(acc_ref.shape, F32)

    h = h_ref[...]
    a = _bdot(h, w1_ref[...])
    a = (a * jax.nn.sigmoid(a)) * _bdot(h, w3_ref[...])
    y = _bdot(a.astype(BF16), w2_ref[...])
    gates = gates_ref[...]
    lane = lax.broadcasted_iota(jnp.int32, gates.shape, 1)
    ge = jnp.sum(jnp.where(lane == e, gates, 0.0), axis=1, keepdims=True)
    acc_ref[...] += ge * y

    @pl.when(e == N_EXPERTS - 1)
    def _fin():
        g2 = mod_ref[5:6, :]
        o_ref[...] = _layernorm(alpha * x_ref[...] + g2 * acc_ref[...], lng_ref[...], lnb_ref[...])


def _moe(h, gates, x1, mods_l, w1, w3, w2, lng, lnb, *, row_block0, n_rows, tm, seg, out_rows,
         alpha, carry=None):
    b, _, d = x1.shape
    de = w1.shape[-1]
    row = lambda width: pl.BlockSpec((None, tm, width), lambda bb, i, e: (bb, row_block0 + i, 0))
    const2 = lambda shape: pl.BlockSpec(shape, lambda bb, i, e: (0, 0))
    in_specs = [
        row(d), row(N_EXPERTS), row(d),
        pl.BlockSpec((None, None, MOD_ROWS, d), lambda bb, i, e: (bb, seg, 0, 0)),
        pl.BlockSpec((None, d, de), lambda bb, i, e: (e, 0, 0)),
        pl.BlockSpec((None, d, de), lambda bb, i, e: (e, 0, 0)),
        pl.BlockSpec((None, de, d), lambda bb, i, e: (e, 0, 0)),
        const2((1, d)), const2((1, d)),
    ]
    args = [h, gates, x1, mods_l, w1, w3, w2, lng, lnb]
    aliases = {}
    if carry is not None:
        in_specs = [pl.BlockSpec(memory_space=pl.ANY)] + in_specs
        args = [carry] + args
        aliases = {0: 0}
    return pl.pallas_call(
        functools.partial(_moe_kernel, alpha=alpha, aliased=carry is not None),
        out_shape=jax.ShapeDtypeStruct((b, out_rows, d), F32),
        grid=(b, n_rows // tm, N_EXPERTS),
        in_specs=in_specs,
        out_specs=pl.BlockSpec((None, tm, d), lambda bb, i, e: (bb, row_block0 + i, 0)),
        scratch_shapes=[pltpu.VMEM((tm, d), F32)],
        input_output_aliases=aliases,
        compiler_params=_cparams(("parallel", "parallel", "arbitrary")),
        name="moe_ln",
    )(*args)


def _head_lane_fields():
    j = jnp.arange(LANES)
    half = j // (LANES // 2)
    comp = (j % (LANES // 2)) // (A_DH // 2)
    axis = (j % (A_DH // 2)) // (A_DH // 4)
    freq = j % (A_DH // 4)
    return half, comp, axis, freq


def _qk_column_perm():
    half, comp, axis, freq = _head_lane_fields()
    orig = comp * A_DH + axis * (A_DH // 2) + half * (A_DH // 4) + freq
    return (jnp.arange(A_HEADS)[:, None] * LANES + orig[None, :]).reshape(-1)


def _rope_tables(n_lat, n_ctx):
    half, _, axis, freq = _head_lane_fields()
    rows = n_lat // GRID_W
    row = jnp.repeat(jnp.arange(rows, dtype=F32), GRID_W)
    col = jnp.tile(jnp.arange(GRID_W, dtype=F32), rows)
    hd = A_DH // 2
    inv = ROPE_BASE ** (-jnp.arange(0, hd, 2, dtype=F32) / hd)
    pos = jnp.where(axis[None, :] == 0, row[:, None], col[:, None])
    ang = pos * inv[freq][None, :]
    cosf = jnp.cos(ang)
    sinf = jnp.sin(ang) * jnp.where(half == 0, -1.0, 1.0)[None, :]
    cosf = jnp.concatenate([cosf, jnp.ones((n_ctx, LANES), F32)], axis=0)
    sinf = jnp.concatenate([sinf, jnp.zeros((n_ctx, LANES), F32)], axis=0)
    return cosf, sinf


def _dft_mats(n, scale=1.0):
    i = jnp.arange(n, dtype=jnp.int32)
    ang = ((i[:, None] * i[None, :]) % n).astype(F32) * (2.0 * math.pi / n)
    return jnp.cos(ang) * scale, jnp.sin(ang) * scale


def _twiddles(n, bn):
    ka = jnp.arange(DFT_A, dtype=jnp.int32)
    bb = jnp.arange(bn, dtype=jnp.int32)
    ang = ((ka[:, None] * bb[None, :]) % n).astype(F32) * (2.0 * math.pi / n)
    shape = (DFT_A, bn, F_WIDTH)
    twc = jnp.broadcast_to(jnp.cos(ang)[:, :, None], shape).reshape(DFT_A, bn * F_WIDTH)
    tws = jnp.broadcast_to(jnp.sin(ang)[:, :, None], shape).reshape(DFT_A, bn * F_WIDTH)
    return twc, tws


def _block_diag(blocks):
    g, n = blocks.shape[-3], blocks.shape[-1]
    eye = jnp.eye(g, dtype=blocks.dtype)
    out = blocks[..., :, :, None, :] * eye[:, None, :, None]
    return out.reshape(blocks.shape[:-3] + (g * n, g * n))


def kernel(x, c, ctx, c_ctx, w_ada, b_ada, w_in, w_fourier, diff_lambda, diff_subln,
           sgu_ln_g, sgu_ln_b, sgu_w, sgu_b, w_out, ln_g, ln_b, w_router, router_bias,
           moe_w1, moe_w3, moe_w2):
    b, n, d = x.shape
    n_ctx = ctx.shape[1]
    depth = w_ada.shape[0]
    l = n + n_ctx
    alpha = (2 * depth) ** 0.25
    bn = n // DFT_A
    assert n % DFT_A == 0 and bn % SUBLANES == 0 and n % GRID_W == 0
    assert n_ctx % TOKEN_TILE == 0 and n % TOKEN_TILE == 0 and n % n_ctx == 0
    assert b + 1 <= MOD_ROWS

    c_rows = jnp.concatenate([c, c_ctx[None, :], jnp.zeros((MOD_ROWS - b - 1, d), F32)], axis=0)
    mod = _ada(c_rows, w_ada, b_ada).reshape(depth, MOD_ROWS, 6, d)
    lat = mod[:, :b]
    cm = jnp.broadcast_to(mod[:, b:b + 1], lat.shape)
    mods = jnp.stack([lat, cm], axis=2)
    mods = jnp.pad(mods, ((0, 0), (0, 0), (0, 0), (0, MOD_ROWS - 6), (0, 0)))

    perm = _qk_column_perm()
    c64, s64 = _dft_mats(F_DIM)
    eye_g = jnp.eye(F_GROUPS, dtype=F32)
    bdc = jnp.kron(eye_g, c64)
    bds = jnp.kron(eye_g, s64)
    bdw = _block_diag(w_fourier)
    w_fab = _fourier_weights(w_in[:, :, :F_WIDTH], bdc, bds, bdw)
    o_q = F_WIDTH
    o_k = o_q + A_QK_WIDTH
    o_v = o_k + A_QK_WIDTH
    o_g = o_v + A_WIDTH
    w_big = jnp.concatenate([
        w_fab,
        w_in[:, :, o_q:o_k][:, :, perm],
        w_in[:, :, o_k:o_v][:, :, perm],
        w_in[:, :, o_v:],
    ], axis=-1).astype(BF16)
    wo = w_out.astype(BF16)
    w1 = moe_w1.astype(BF16)
    w3 = moe_w3.astype(BF16)
    w2 = moe_w2.astype(BF16)
    ws = sgu_w.astype(BF16)
    bs_full = jnp.repeat(jnp.swapaxes(sgu_b, 1, 2), G_DIM, axis=2)
    wr = jnp.pad(w_router, ((0, 0), (0, ROUTER_LANES - N_EXPERTS)))
    rb = jnp.pad(router_bias, (0, ROUTER_LANES - N_EXPERTS)).reshape(1, ROUTER_LANES)

    cosf, sinf = _rope_tables(n, n_ctx)
    ca, sa = _dft_mats(DFT_A)
    twc, tws = _twiddles(n, bn)
    cb, sb = _dft_mats(bn, scale=(n * F_DIM) ** -0.5)
    cc, sc = _dft_mats(n_ctx, scale=(n_ctx * F_DIM) ** -0.5)
    q_scale = (A_DH ** -0.5) * math.log2(math.e)

    tk = max(t for t in (1280, 1024, 768, 512, 256) if l % t == 0)
    moe_tile = min(MOE_TILE, n)
    assert n % moe_tile == 0

    xa = jnp.concatenate([x, ctx], axis=1)
    for li in range(depth):
        last = li == depth - 1
        lam_init = 0.8 - 0.6 * math.exp(-0.3 * li)
        fa, fb, q, k, v, sg = _inproj(
            xa, mods[li], w_big[li], cosf, sinf, sgu_ln_g[li][None], sgu_ln_b[li][None],
            ws[li], bs_full[li], n, q_scale)
        zr, zi = _fft_a(fa, fb, ca, sa, twc, tws, bn)
        yf = jnp.concatenate([_fft_b(zr, zi, cb, sb, bn), _dft_ctx(fa, fb, cc, sc, n, n_ctx)], axis=1)
        gain = diff_subln[li][None]
        kt, v_ext = _key_value_chunks(k, v, tk)
        att_x = _attention(q, kt, v_ext, diff_lambda[li], gain, q_block0=0, n_q=n, tq=ATTN_TQ,
                           lam_init=lam_init)
        kt_c, v_ext_c = _key_value_chunks(k[:, n:], v[:, n:], n_ctx)
        att_c = _attention(q, kt_c, v_ext_c, diff_lambda[li], gain, q_block0=n // n_ctx, n_q=n_ctx,
                           tq=n_ctx, lam_init=lam_init)
        att = jnp.concatenate([att_x, att_c], axis=1)
        x1, h, gates_t = _outproj(yf, att, sg, xa, mods[li], wo[li], ln_g[li, 0][None],
                                  ln_b[li, 0][None], wr, rb, n, alpha)
        gates = jnp.swapaxes(gates_t, 1, 2)
        lng2, lnb2 = ln_g[li, 1][None], ln_b[li, 1][None]
        xa = _moe(h, gates, x1, mods[li], w1[li], w3[li], w2[li], lng2, lnb2, row_block0=0,
                  n_rows=n, tm=moe_tile, seg=0, out_rows=n if last else l, alpha=alpha)
        if not last:
            xa = _moe(h, gates, x1, mods[li], w1[li], w3[li], w2[li], lng2, lnb2,
                      row_block0=n // n_ctx, n_rows=n_ctx, tm=n_ctx, seg=1, out_rows=l,
                      alpha=alpha, carry=xa)
    return xa
```

```python
import functools
import math

import jax
import jax.numpy as jnp
from jax import lax
from jax.experimental import pallas as pl
from jax.experimental.pallas import tpu as pltpu

F32 = jnp.float32
BF16 = jnp.bfloat16
HIGHEST = lax.Precision.HIGHEST

GRID_W = 64
F_GROUPS, F_DIM = 4, 64
F_WIDTH = F_GROUPS * F_DIM
A_HEADS, A_DH = 4, 64
A_DV = 2 * A_DH
A_QK_WIDTH = A_HEADS * 2 * A_DH
A_WIDTH = A_HEADS * A_DV
G_GROUPS, G_DIM = 4, 64
G_WIDTH = G_GROUPS * G_DIM
CHUNK = 128
ROPE_BASE = 10000.0
N_EXPERTS = 16
N_EXPERT_GROUPS = 4
EXPERTS_PER_GROUP = N_EXPERTS // N_EXPERT_GROUPS
LN_EPS = 1e-5
RMS_EPS = 1e-5

COL_FA = 0
COL_FB = COL_FA + F_WIDTH
COL_Q = COL_FB + F_WIDTH
COL_K = COL_Q + A_QK_WIDTH
COL_V = COL_K + A_QK_WIDTH
COL_G = COL_V + A_WIDTH
BIG_COLS = COL_G + 2 * G_WIDTH

LANES = 128
SUBLANES = 8
MOD_ROWS = 8
ROUTER_LANES = 128
DFT_A = 128
VMEM_LIMIT = 56 * 1024 * 1024

TOKEN_TILE = 256
ATTN_TQ = 256
ATTN_GROUP = 2048
ATTN_TK = 1280
MOE_TILE = 1024


def _cparams(sem):
    return pltpu.CompilerParams(dimension_semantics=sem, vmem_limit_bytes=VMEM_LIMIT)


def _hdot(a, b):
    return jnp.dot(a, b, precision=HIGHEST, preferred_element_type=F32)


def _bdot(a, b):
    return jnp.dot(a, b, preferred_element_type=F32)


def _layernorm(y, g, b):
    mu = jnp.mean(y, axis=-1, keepdims=True)
    d = y - mu
    var = jnp.mean(d * d, axis=-1, keepdims=True)
    return d * lax.rsqrt(var + LN_EPS) * g + b


def _ada_kernel(c_ref, w_ref, b_ref, o_ref):
    c = c_ref[...]
    a = c * jax.nn.sigmoid(c)
    o_ref[...] = _hdot(a, w_ref[...]) + b_ref[...]


def _ada(c_rows, w_ada, b_ada):
    depth, d, cols = w_ada.shape
    tn = 1536
    assert cols % tn == 0
    return pl.pallas_call(
        _ada_kernel,
        out_shape=jax.ShapeDtypeStruct((depth, MOD_ROWS, cols), F32),
        grid=(depth, cols // tn),
        in_specs=[
            pl.BlockSpec((MOD_ROWS, d), lambda l, j: (0, 0)),
            pl.BlockSpec((None, d, tn), lambda l, j: (l, 0, j)),
            pl.BlockSpec((None, 1, tn), lambda l, j: (l, 0, j)),
        ],
        out_specs=pl.BlockSpec((None, MOD_ROWS, tn), lambda l, j: (l, 0, j)),
        compiler_params=_cparams(("parallel", "parallel")),
        name="ada_mod",
    )(c_rows, w_ada, b_ada.reshape(depth, 1, cols))


def _fw_kernel(wf_ref, bdc_ref, bds_ref, bdw_ref, o_ref):
    bdw = bdw_ref[...]
    mc = _hdot(bdc_ref[...], bdw)
    ms = _hdot(bds_ref[...], bdw)
    wf = wf_ref[...]
    o_ref[:, :F_WIDTH] = _hdot(wf, mc)
    o_ref[:, F_WIDTH:] = _hdot(wf, ms)


def _fourier_weights(wf, bdc, bds, bdw):
    depth, d, _ = wf.shape
    return pl.pallas_call(
        _fw_kernel,
        out_shape=jax.ShapeDtypeStruct((depth, d, 2 * F_WIDTH), F32),
        grid=(depth,),
        in_specs=[
            pl.BlockSpec((None, d, F_WIDTH), lambda l: (l, 0, 0)),
            pl.BlockSpec((F_WIDTH, F_WIDTH), lambda l: (0, 0)),
            pl.BlockSpec((F_WIDTH, F_WIDTH), lambda l: (0, 0)),
            pl.BlockSpec((None, F_WIDTH, F_WIDTH), lambda l: (l, 0, 0)),
        ],
        out_specs=pl.BlockSpec((None, d, 2 * F_WIDTH), lambda l: (l, 0, 0)),
        compiler_params=_cparams(("parallel",)),
        name="fourier_weights",
    )(wf, bdc, bds, bdw)


def _gelu_tanh(x):
    c = math.sqrt(2.0 / math.pi)
    return x * (0.5 * (1.0 + jnp.tanh(c * (x + 0.044715 * (x * x * x)))))


def _inproj_kernel(x_ref, mod_ref, w_ref, cos_ref, sin_ref, lng_ref, lnb_ref, ws_ref, bs_ref,
                   fa_ref, fb_ref, q_ref, k_ref, v_ref, sg_ref, *, q_scale):
    x = x_ref[...]
    sh = mod_ref[0:1, :]
    sc = mod_ref[1:2, :]
    h = (x * (1.0 + sc) + sh).astype(BF16)
    p = _bdot(h, w_ref[...])
    fa_ref[...] = p[:, COL_FA:COL_FA + F_WIDTH]
    fb_ref[...] = p[:, COL_FB:COL_FB + F_WIDTH]
    cosf = cos_ref[...]
    sinf = sin_ref[...]
    for hh in range(A_HEADS):
        t = p[:, COL_Q + LANES * hh:COL_Q + LANES * (hh + 1)]
        r = t * cosf + pltpu.roll(t, LANES // 2, 1) * sinf
        q_ref[:, LANES * hh:LANES * (hh + 1)] = (r * q_scale).astype(BF16)
        t = p[:, COL_K + LANES * hh:COL_K + LANES * (hh + 1)]
        r = t * cosf + pltpu.roll(t, LANES // 2, 1) * sinf
        k_ref[:, LANES * hh:LANES * (hh + 1)] = r.astype(BF16)
    v_ref[...] = p[:, COL_V:COL_V + A_WIDTH].astype(BF16)

    g = _gelu_tanh(p[:, COL_G:COL_G + 2 * G_WIDTH])
    u = g[:, :G_WIDTH]
    vn = _layernorm(g[:, G_WIDTH:], lng_ref[...], lnb_ref[...])
    lane_group = lax.broadcasted_iota(jnp.int32, (CHUNK, G_WIDTH), 1) // G_DIM
    tm = x.shape[0]
    for c in range(tm // CHUNK):
        rows = slice(c * CHUNK, (c + 1) * CHUNK)
        vc = vn[rows]
        mixed = bs_ref[...]
        for gi in range(G_GROUPS):
            vm = jnp.where(lane_group == gi, vc, 0.0).astype(BF16)
            mixed = mixed + _bdot(ws_ref[gi], vm)
        sg_ref[rows, :] = (u[rows] * mixed).astype(BF16)


def _inproj(xa, mods_l, w_big, cosf, sinf, lng, lnb, ws, bs_full, n_lat, q_scale):
    b, l, d = xa.shape
    tm = TOKEN_TILE
    nlat = n_lat // tm
    seg = lambda i: jnp.where(i >= nlat, 1, 0)
    row = lambda width: pl.BlockSpec((None, tm, width), lambda bb, i: (bb, i, 0))
    const2 = lambda shape: pl.BlockSpec(shape, lambda bb, i: (0, 0))
    out_shapes = (
        jax.ShapeDtypeStruct((b, l, F_WIDTH), F32),
        jax.ShapeDtypeStruct((b, l, F_WIDTH), F32),
        jax.ShapeDtypeStruct((b, l, A_QK_WIDTH), BF16),
        jax.ShapeDtypeStruct((b, l, A_QK_WIDTH), BF16),
        jax.ShapeDtypeStruct((b, l, A_WIDTH), BF16),
        jax.ShapeDtypeStruct((b, l, G_WIDTH), BF16),
    )
    return pl.pallas_call(
        functools.partial(_inproj_kernel, q_scale=q_scale),
        out_shape=out_shapes,
        grid=(b, l // tm),
        in_specs=[
            row(d),
            pl.BlockSpec((None, None, MOD_ROWS, d), lambda bb, i: (bb, seg(i), 0, 0)),
            const2((d, BIG_COLS)),
            pl.BlockSpec((tm, LANES), lambda bb, i: (i, 0)),
            pl.BlockSpec((tm, LANES), lambda bb, i: (i, 0)),
            const2((1, G_WIDTH)),
            const2((1, G_WIDTH)),
            pl.BlockSpec((G_GROUPS, CHUNK, CHUNK), lambda bb, i: (0, 0, 0)),
            const2((CHUNK, G_WIDTH)),
        ],
        out_specs=(row(F_WIDTH), row(F_WIDTH), row(A_QK_WIDTH), row(A_QK_WIDTH),
                   row(A_WIDTH), row(G_WIDTH)),
        compiler_params=_cparams(("parallel", "parallel")),
        name="inproj",
    )(xa, mods_l, w_big, cosf, sinf, lng, lnb, ws, bs_full)


def _fft_a_kernel(fa_ref, fb_ref, c_ref, s_ref, tc_ref, ts_ref, zr_ref, zi_ref):
    fa = fa_ref[...]
    fb = fb_ref[...]
    cm = c_ref[...]
    sm = s_ref[...]
    zr = _hdot(cm, fa) - _hdot(sm, fb)
    zi = -(_hdot(cm, fb) + _hdot(sm, fa))
    tc = tc_ref[...]
    ts = ts_ref[...]
    zr_ref[...] = zr * tc + zi * ts
    zi_ref[...] = zi * tc - zr * ts


def _fft_a(fa, fb, cm, sm, twc, tws, bn):
    b, l, _ = fa.shape
    cols = bn * F_WIDTH
    fa3 = fa.reshape(b, l // bn, cols)
    fb3 = fb.reshape(b, l // bn, cols)
    tcw = min(cols, 2048)
    assert cols % tcw == 0
    blk = pl.BlockSpec((None, DFT_A, tcw), lambda bb, j: (bb, 0, j))
    tw = pl.BlockSpec((DFT_A, tcw), lambda bb, j: (0, j))
    mat = pl.BlockSpec((DFT_A, DFT_A), lambda bb, j: (0, 0))
    return pl.pallas_call(
        _fft_a_kernel,
        out_shape=(jax.ShapeDtypeStruct((b, DFT_A, cols), F32),) * 2,
        grid=(b, cols // tcw),
        in_specs=[blk, blk, mat, mat, tw, tw],
        out_specs=(blk, blk),
        compiler_params=_cparams(("parallel", "parallel")),
        name="fft_stage_a",
    )(fa3, fb3, cm, sm, twc, tws)


def _fft_b_kernel(zr_ref, zi_ref, c_ref, s_ref, o_ref):
    cb = c_ref[...]
    sb = s_ref[...]
    for j in range(zr_ref.shape[0]):
        y = _hdot(cb, zr_ref[j]) + _hdot(sb, zi_ref[j])
        o_ref[:, F_WIDTH * j:F_WIDTH * (j + 1)] = y.astype(o_ref.dtype)


def _fft_b(zr, zi, cb, sb, bn):
    b = zr.shape[0]
    kb = 8
    zr4 = zr.reshape(b, DFT_A, bn, F_WIDTH)
    zi4 = zi.reshape(b, DFT_A, bn, F_WIDTH)
    blk = pl.BlockSpec((None, kb, bn, F_WIDTH), lambda bb, i: (bb, i, 0, 0))
    mat = pl.BlockSpec((bn, bn), lambda bb, i: (0, 0))
    out = pl.pallas_call(
        _fft_b_kernel,
        out_shape=jax.ShapeDtypeStruct((b, bn, DFT_A * F_WIDTH), BF16),
        grid=(b, DFT_A // kb),
        in_specs=[blk, blk, mat, mat],
        out_specs=pl.BlockSpec((None, bn, kb * F_WIDTH), lambda bb, i: (bb, 0, i)),
        compiler_params=_cparams(("parallel", "parallel")),
        name="fft_stage_b",
    )(zr4, zi4, cb, sb)
    return out.reshape(b, bn * DFT_A, F_WIDTH)


def _dft_ctx_kernel(fa_ref, fb_ref, c_ref, s_ref, o_ref):
    y = _hdot(c_ref[...], fa_ref[...]) - _hdot(s_ref[...], fb_ref[...])
    o_ref[...] = y.astype(o_ref.dtype)


def _dft_ctx(fa, fb, cc, sc, n_lat, n_ctx):
    b = fa.shape[0]
    blk = pl.BlockSpec((None, n_ctx, F_WIDTH), lambda bb: (bb, n_lat // n_ctx, 0))
    mat = pl.BlockSpec((n_ctx, n_ctx), lambda bb: (0, 0))
    return pl.pallas_call(
        _dft_ctx_kernel,
        out_shape=jax.ShapeDtypeStruct((b, n_ctx, F_WIDTH), BF16),
        grid=(b,),
        in_specs=[blk, blk, mat, mat],
        out_specs=pl.BlockSpec((None, n_ctx, F_WIDTH), lambda bb: (bb, 0, 0)),
        compiler_params=_cparams(("parallel",)),
        name="dft_ctx",
    )(fa, fb, cc, sc)


def _score_slot(j):
    return 2 if j == 0 else (j - 1) % 2


def _attn_kernel(q_ref, kt_ref, v_ref, lam_ref, gain_ref, o_ref, q2_ref, s_ref, m_ref, acc_ref,
                 *, lam_init, tq):
    nsub = q_ref.shape[0] // tq
    nk, _, tk = kt_ref.shape
    lane = lax.broadcasted_iota(jnp.int32, (tq, A_DV), 1)
    comp0 = (lane % A_DH) < (A_DH // 2)
    lp = lam_ref[...]
    lam = (jnp.exp(jnp.sum(lp[0:1] * lp[1:2], keepdims=True))
           - jnp.exp(jnp.sum(lp[2:3] * lp[3:4], keepdims=True)) + lam_init)
    out_gain = gain_ref[...] * (1.0 - lam_init)

    def rows(i):
        return pl.ds(pl.multiple_of(i * tq, tq), tq)

    def load_q2(i):
        q = q_ref[rows(i), :]
        zero = jnp.zeros_like(q)
        q2_ref[0:tq, :] = jnp.where(comp0, q, zero)
        q2_ref[tq:2 * tq, :] = jnp.where(comp0, zero, q)

    def scores(j):
        s_ref[_score_slot(j)] = _bdot(q2_ref[...], kt_ref[j])

    def softmax_pv(j):
        s = s_ref[_score_slot(j)]
        m_old = m_ref[...]
        m_new = jnp.maximum(m_old, jnp.max(s, axis=1, keepdims=True))
        alpha = jnp.exp2(m_old - m_new)
        p = jnp.exp2(s - jnp.concatenate([m_new] * (tk // LANES), axis=1)).astype(BF16)
        pv = _bdot(p, v_ref[j])
        acc_ref[...] = acc_ref[...] * jnp.concatenate([alpha, alpha], axis=1) + pv
        m_ref[...] = m_new

    def finalize(i):
        acc = acc_ref[...]
        o0 = acc[0:tq, 0:LANES] / acc[0:tq, LANES:]
        o1 = acc[tq:, 0:LANES] / acc[tq:, LANES:]
        o = o0 - lam * o1
        ms = jnp.mean(o * o, axis=-1, keepdims=True)
        o_ref[rows(i), :] = (o * lax.rsqrt(ms + RMS_EPS) * out_gain).astype(o_ref.dtype)

    def next_tile_first_scores(i):
        load_q2(jnp.minimum(i + 1, nsub - 1))
        scores(0)

    load_q2(0)
    scores(0)

    def tile(i, carry):
        m_ref[...] = jnp.full(m_ref.shape, -jnp.inf, F32)
        acc_ref[...] = jnp.zeros(acc_ref.shape, F32)
        for j in range(nk):
            if j + 1 < nk:
                scores(j + 1)
                softmax_pv(j)
            elif nk > 1:
                next_tile_first_scores(i)
                softmax_pv(j)
            else:
                softmax_pv(j)
                next_tile_first_scores(i)
        finalize(i)
        return carry

    lax.fori_loop(0, nsub, tile, 0)


def _attention(q, kt, v_ext, lam_p, gain, *, q_block0, n_q, tq, group, lam_init):
    b = q.shape[0]
    _, _, nk, _, tk = kt.shape
    assert n_q % group == 0 and group % tq == 0
    return pl.pallas_call(
        functools.partial(_attn_kernel, lam_init=lam_init, tq=tq),
        out_shape=jax.ShapeDtypeStruct((b, n_q, A_WIDTH), BF16),
        grid=(b, A_HEADS, n_q // group),
        in_specs=[
            pl.BlockSpec((None, group, A_DV), lambda bb, h, i: (bb, q_block0 + i, h)),
            pl.BlockSpec((None, None, nk, A_DV, tk), lambda bb, h, i: (bb, h, 0, 0, 0)),
            pl.BlockSpec((None, None, nk, tk, 2 * LANES), lambda bb, h, i: (bb, h, 0, 0, 0)),
            pl.BlockSpec((4, A_DH), lambda bb, h, i: (0, 0)),
            pl.BlockSpec((1, A_DV), lambda bb, h, i: (0, 0)),
        ],
        out_specs=pl.BlockSpec((None, group, A_DV), lambda bb, h, i: (bb, i, h)),
        scratch_shapes=[
            pltpu.VMEM((2 * tq, A_DV), BF16),
            pltpu.VMEM((3, 2 * tq, tk), F32),
            pltpu.VMEM((2 * tq, LANES), F32),
            pltpu.VMEM((2 * tq, 2 * LANES), F32),
        ],
        compiler_params=_cparams(("parallel", "parallel", "arbitrary")),
        name="diff_attn",
    )(q, kt, v_ext, lam_p, gain)


def _key_value_chunks(k, v, tk):
    b, nkeys, _ = k.shape
    nk = nkeys // tk
    kt = k.reshape(b, nk, tk, A_HEADS, A_DV).transpose(0, 3, 1, 4, 2)
    vh = v.reshape(b, nk, tk, A_HEADS, A_DV).transpose(0, 3, 1, 2, 4)
    v_ext = jnp.concatenate([vh, jnp.ones_like(vh)], axis=-1)
    return kt, v_ext


def _router_gates(sc_t, sel_t, gates_ref):
    s_rows = [sc_t[e:e + 1, :] for e in range(N_EXPERTS)]
    v_rows = [sel_t[e:e + 1, :] for e in range(N_EXPERTS)]
    in_top2 = []
    group_score = []
    for g in range(N_EXPERT_GROUPS):
        vs = v_rows[g * EXPERTS_PER_GROUP:(g + 1) * EXPERTS_PER_GROUP]
        tops = []
        for jj in range(EXPERTS_PER_GROUP):
            rank = jnp.zeros_like(vs[jj])
            for ii in range(EXPERTS_PER_GROUP):
                if ii == jj:
                    continue
                beats = (vs[ii] >= vs[jj]) if ii < jj else (vs[ii] > vs[jj])
                rank = rank + jnp.where(beats, 1.0, 0.0)
            tops.append(rank < 2.0)
        in_top2 += tops
        gs = jnp.zeros_like(vs[0])
        for jj in range(EXPERTS_PER_GROUP):
            gs = gs + jnp.where(tops[jj], vs[jj], 0.0)
        group_score.append(gs)
    best = group_score[0]
    gidx = jnp.zeros_like(best)
    for g in range(1, N_EXPERT_GROUPS):
        upd = group_score[g] > best
        best = jnp.where(upd, group_score[g], best)
        gidx = jnp.where(upd, float(g), gidx)
    chosen = [jnp.logical_and(in_top2[e], gidx == float(e // EXPERTS_PER_GROUP))
              for e in range(N_EXPERTS)]
    denom = jnp.zeros_like(best)
    for e in range(N_EXPERTS):
        denom = denom + jnp.where(chosen[e], s_rows[e], 0.0)
    for e in range(N_EXPERTS):
        gates_ref[e:e + 1, :] = jnp.where(chosen[e], s_rows[e] / denom, 0.0)


def _outproj_kernel(yf_ref, att_ref, sg_ref, x_ref, mod_ref, wo_ref, lng_ref, lnb_ref,
                    wr_ref, rb_ref, x1_ref, h_ref, gates_ref, *, alpha):
    mix = _bdot(yf_ref[...], wo_ref[0:F_WIDTH, :])
    mix = mix + _bdot(att_ref[...], wo_ref[F_WIDTH:F_WIDTH + A_WIDTH, :])
    mix = mix + _bdot(sg_ref[...], wo_ref[F_WIDTH + A_WIDTH:, :])
    g1 = mod_ref[2:3, :]
    x1 = _layernorm(alpha * x_ref[...] + g1 * mix, lng_ref[...], lnb_ref[...])
    x1_ref[...] = x1
    h = x1 * (1.0 + mod_ref[4:5, :]) + mod_ref[3:4, :]
    h_ref[...] = h.astype(BF16)
    scores = jax.nn.sigmoid(_hdot(h, wr_ref[...]))
    sel = scores + rb_ref[...]
    _router_gates(scores.T[0:N_EXPERTS, :], sel.T[0:N_EXPERTS, :], gates_ref)


def _outproj(yf, att, sg, xa, mods_l, wo, lng, lnb, wr, rb, n_lat, alpha):
    b, l, d = xa.shape
    tm = TOKEN_TILE
    nlat = n_lat // tm
    seg = lambda i: jnp.where(i >= nlat, 1, 0)
    row = lambda width: pl.BlockSpec((None, tm, width), lambda bb, i: (bb, i, 0))
    const2 = lambda shape: pl.BlockSpec(shape, lambda bb, i: (0, 0))
    return pl.pallas_call(
        functools.partial(_outproj_kernel, alpha=alpha),
        out_shape=(
            jax.ShapeDtypeStruct((b, l, d), F32),
            jax.ShapeDtypeStruct((b, l, d), BF16),
            jax.ShapeDtypeStruct((b, N_EXPERTS, l), F32),
        ),
        grid=(b, l // tm),
        in_specs=[
            row(F_WIDTH), row(A_WIDTH), row(G_WIDTH), row(d),
            pl.BlockSpec((None, None, MOD_ROWS, d), lambda bb, i: (bb, seg(i), 0, 0)),
            const2((d, d)), const2((1, d)), const2((1, d)),
            const2((d, ROUTER_LANES)), const2((1, ROUTER_LANES)),
        ],
        out_specs=(row(d), row(d),
                   pl.BlockSpec((None, N_EXPERTS, tm), lambda bb, i: (bb, 0, i))),
        compiler_params=_cparams(("parallel", "parallel")),
        name="outproj_ln_router",
    )(yf, att, sg, xa, mods_l, wo, lng, lnb, wr, rb)


def _moe_kernel(*refs, alpha, aliased):
    if aliased:
        refs = refs[1:]
    (h_ref, gates_ref, x_ref, mod_ref, w1_ref, w3_ref, w2_ref, lng_ref, lnb_ref,
     o_ref, acc_ref) = refs
    e = pl.program_id(2)

    @pl.when(e == 0)
    def _init():
        acc_ref[...] = jnp.zeros(acc_ref.shape, F32)

    h = h_ref[...]
    a = _bdot(h, w1_ref[...])
    a = (a * jax.nn.sigmoid(a)) * _bdot(h, w3_ref[...])
    y = _bdot(a.astype(BF16), w2_ref[...])
    gates = gates_ref[...]
    lane = lax.broadcasted_iota(jnp.int32, gates.shape, 1)
    ge = jnp.sum(jnp.where(lane == e, gates, 0.0), axis=1, keepdims=True)
    acc_ref[...] += ge * y

    @pl.when(e == N_EXPERTS - 1)
    def _fin():
        g2 = mod_ref[5:6, :]
        o_ref[...] = _layernorm(alpha * x_ref[...] + g2 * acc_ref[...], lng_ref[...], lnb_ref[...])


def _moe(h, gates, x1, mods_l, w1, w3, w2, lng, lnb, *, row_block0, n_rows, tm, seg, out_rows,
         alpha, carry=None):
    b, _, d = x1.shape
    de = w1.shape[-1]
    row = lambda width: pl.BlockSpec((None, tm, width), lambda bb, i, e: (bb, row_block0 + i, 0))
    const2 = lambda shape: pl.BlockSpec(shape, lambda bb, i, e: (0, 0))
    in_specs = [
        row(d), row(N_EXPERTS), row(d),
        pl.BlockSpec((None, None, MOD_ROWS, d), lambda bb, i, e: (bb, seg, 0, 0)),
        pl.BlockSpec((None, d, de), lambda bb, i, e: (e, 0, 0)),
        pl.BlockSpec((None, d, de), lambda bb, i, e: (e, 0, 0)),
        pl.BlockSpec((None, de, d), lambda bb, i, e: (e, 0, 0)),
        const2((1, d)), const2((1, d)),
    ]
    args = [h, gates, x1, mods_l, w1, w3, w2, lng, lnb]
    aliases = {}
    if carry is not None:
        in_specs = [pl.BlockSpec(memory_space=pl.ANY)] + in_specs
        args = [carry] + args
        aliases = {0: 0}
    return pl.pallas_call(
        functools.partial(_moe_kernel, alpha=alpha, aliased=carry is not None),
        out_shape=jax.ShapeDtypeStruct((b, out_rows, d), F32),
        grid=(b, n_rows // tm, N_EXPERTS),
        in_specs=in_specs,
        out_specs=pl.BlockSpec((None, tm, d), lambda bb, i, e: (bb, row_block0 + i, 0)),
        scratch_shapes=[pltpu.VMEM((tm, d), F32)],
        input_output_aliases=aliases,
        compiler_params=_cparams(("parallel", "parallel", "arbitrary")),
        name="moe_ln",
    )(*args)


def _head_lane_fields():
    j = jnp.arange(LANES)
    half = j // (LANES // 2)
    comp = (j % (LANES // 2)) // (A_DH // 2)
    axis = (j % (A_DH // 2)) // (A_DH // 4)
    freq = j % (A_DH // 4)
    return half, comp, axis, freq


def _qk_column_perm():
    half, comp, axis, freq = _head_lane_fields()
    orig = comp * A_DH + axis * (A_DH // 2) + half * (A_DH // 4) + freq
    return (jnp.arange(A_HEADS)[:, None] * LANES + orig[None, :]).reshape(-1)


def _rope_tables(n_lat, n_ctx):
    half, _, axis, freq = _head_lane_fields()
    rows = n_lat // GRID_W
    row = jnp.repeat(jnp.arange(rows, dtype=F32), GRID_W)
    col = jnp.tile(jnp.arange(GRID_W, dtype=F32), rows)
    hd = A_DH // 2
    inv = ROPE_BASE ** (-jnp.arange(0, hd, 2, dtype=F32) / hd)
    pos = jnp.where(axis[None, :] == 0, row[:, None], col[:, None])
    ang = pos * inv[freq][None, :]
    cosf = jnp.cos(ang)
    sinf = jnp.sin(ang) * jnp.where(half == 0, -1.0, 1.0)[None, :]
    cosf = jnp.concatenate([cosf, jnp.ones((n_ctx, LANES), F32)], axis=0)
    sinf = jnp.concatenate([sinf, jnp.zeros((n_ctx, LANES), F32)], axis=0)
    return cosf, sinf


def _dft_mats(n, scale=1.0):
    i = jnp.arange(n, dtype=jnp.int32)
    ang = ((i[:, None] * i[None, :]) % n).astype(F32) * (2.0 * math.pi / n)
    return jnp.cos(ang) * scale, jnp.sin(ang) * scale


def _twiddles(n, bn):
    ka = jnp.arange(DFT_A, dtype=jnp.int32)
    bb = jnp.arange(bn, dtype=jnp.int32)
    ang = ((ka[:, None] * bb[None, :]) % n).astype(F32) * (2.0 * math.pi / n)
    shape = (DFT_A, bn, F_WIDTH)
    twc = jnp.broadcast_to(jnp.cos(ang)[:, :, None], shape).reshape(DFT_A, bn * F_WIDTH)
    tws = jnp.broadcast_to(jnp.sin(ang)[:, :, None], shape).reshape(DFT_A, bn * F_WIDTH)
    return twc, tws


def _block_diag(blocks):
    g, n = blocks.shape[-3], blocks.shape[-1]
    eye = jnp.eye(g, dtype=blocks.dtype)
    out = blocks[..., :, :, None, :] * eye[:, None, :, None]
    return out.reshape(blocks.shape[:-3] + (g * n, g * n))


def kernel(x, c, ctx, c_ctx, w_ada, b_ada, w_in, w_fourier, diff_lambda, diff_subln,
           sgu_ln_g, sgu_ln_b, sgu_w, sgu_b, w_out, ln_g, ln_b, w_router, router_bias,
           moe_w1, moe_w3, moe_w2):
    b, n, d = x.shape
    n_ctx = ctx.shape[1]
    depth = w_ada.shape[0]
    l = n + n_ctx
    alpha = (2 * depth) ** 0.25
    bn = n // DFT_A
    assert n % DFT_A == 0 and bn % SUBLANES == 0 and n % GRID_W == 0
    assert n_ctx % TOKEN_TILE == 0 and n % TOKEN_TILE == 0 and n % n_ctx == 0
    assert b + 1 <= MOD_ROWS

    c_rows = jnp.concatenate([c, c_ctx[None, :], jnp.zeros((MOD_ROWS - b - 1, d), F32)], axis=0)
    mod = _ada(c_rows, w_ada, b_ada).reshape(depth, MOD_ROWS, 6, d)
    lat = mod[:, :b]
    cm = jnp.broadcast_to(mod[:, b:b + 1], lat.shape)
    mods = jnp.stack([lat, cm], axis=2)
    mods = jnp.pad(mods, ((0, 0), (0, 0), (0, 0), (0, MOD_ROWS - 6), (0, 0)))

    perm = _qk_column_perm()
    c64, s64 = _dft_mats(F_DIM)
    eye_g = jnp.eye(F_GROUPS, dtype=F32)
    bdc = jnp.kron(eye_g, c64)
    bds = jnp.kron(eye_g, s64)
    bdw = _block_diag(w_fourier)
    w_fab = _fourier_weights(w_in[:, :, :F_WIDTH], bdc, bds, bdw)
    o_q = F_WIDTH
    o_k = o_q + A_QK_WIDTH
    o_v = o_k + A_QK_WIDTH
    o_g = o_v + A_WIDTH
    w_big = jnp.concatenate([
        w_fab,
        w_in[:, :, o_q:o_k][:, :, perm],
        w_in[:, :, o_k:o_v][:, :, perm],
        w_in[:, :, o_v:],
    ], axis=-1).astype(BF16)
    wo = w_out.astype(BF16)
    w1 = moe_w1.astype(BF16)
    w3 = moe_w3.astype(BF16)
    w2 = moe_w2.astype(BF16)
    ws = sgu_w.astype(BF16)
    bs_full = jnp.repeat(jnp.swapaxes(sgu_b, 1, 2), G_DIM, axis=2)
    wr = jnp.pad(w_router, ((0, 0), (0, ROUTER_LANES - N_EXPERTS)))
    rb = jnp.pad(router_bias, (0, ROUTER_LANES - N_EXPERTS)).reshape(1, ROUTER_LANES)

    cosf, sinf = _rope_tables(n, n_ctx)
    ca, sa = _dft_mats(DFT_A)
    twc, tws = _twiddles(n, bn)
    cb, sb = _dft_mats(bn, scale=(n * F_DIM) ** -0.5)
    cc, sc = _dft_mats(n_ctx, scale=(n_ctx * F_DIM) ** -0.5)
    q_scale = (A_DH ** -0.5) * math.log2(math.e)

    tk = ATTN_TK
    assert l % tk == 0
    moe_tile = min(MOE_TILE, n)
    assert n % moe_tile == 0

    xa = jnp.concatenate([x, ctx], axis=1)
    for li in range(depth):
        last = li == depth - 1
        lam_init = 0.8 - 0.6 * math.exp(-0.3 * li)
        fa, fb, q, k, v, sg = _inproj(
            xa, mods[li], w_big[li], cosf, sinf, sgu_ln_g[li][None], sgu_ln_b[li][None],
            ws[li], bs_full[li], n, q_scale)
        zr, zi = _fft_a(fa, fb, ca, sa, twc, tws, bn)
        yf = jnp.concatenate([_fft_b(zr, zi, cb, sb, bn), _dft_ctx(fa, fb, cc, sc, n, n_ctx)], axis=1)
        gain = diff_subln[li][None]
        kt, v_ext = _key_value_chunks(k, v, tk)
        att_x = _attention(q, kt, v_ext, diff_lambda[li], gain, q_block0=0, n_q=n, tq=ATTN_TQ,
                           group=min(ATTN_GROUP, n), lam_init=lam_init)
        kt_c, v_ext_c = _key_value_chunks(k[:, n:], v[:, n:], n_ctx)
        att_c = _attention(q, kt_c, v_ext_c, diff_lambda[li], gain, q_block0=n // n_ctx, n_q=n_ctx,
                           tq=n_ctx, group=n_ctx, lam_init=lam_init)
        att = jnp.concatenate([att_x, att_c], axis=1)
        x1, h, gates_t = _outproj(yf, att, sg, xa, mods[li], wo[li], ln_g[li, 0][None],
                                  ln_b[li, 0][None], wr, rb, n, alpha)
        gates = jnp.swapaxes(gates_t, 1, 2)
        lng2, lnb2 = ln_g[li, 1][None], ln_b[li, 1][None]
        xa = _moe(h, gates, x1, mods[li], w1[li], w3[li], w2[li], lng2, lnb2, row_block0=0,
                  n_rows=n, tm=moe_tile, seg=0, out_rows=n if last else l, alpha=alpha)
        if not last:
            xa = _moe(h, gates, x1, mods[li], w1[li], w3[li], w2[li], lng2, lnb2,
                      row_block0=n // n_ctx, n_rows=n_ctx, tm=n_ctx, seg=1, out_rows=l,
                      alpha=alpha, carry=xa)
    return xa
```

```python
import functools
import math

import numpy as np
import jax
import jax.numpy as jnp
from jax import lax
from jax.experimental import pallas as pl
from jax.experimental.pallas import tpu as pltpu

F32 = jnp.float32
BF16 = jnp.bfloat16
HIGHEST = lax.Precision.HIGHEST

GRID_W = 64
F_GROUPS, F_DIM = 4, 64
F_WIDTH = F_GROUPS * F_DIM
A_HEADS, A_DH = 4, 64
A_DV = 2 * A_DH
A_QK_WIDTH = A_HEADS * 2 * A_DH
A_WIDTH = A_HEADS * A_DV
G_GROUPS, G_DIM = 4, 64
G_WIDTH = G_GROUPS * G_DIM
CHUNK = 128
ROPE_BASE = 10000.0
N_EXPERTS = 16
N_EXPERT_GROUPS = 4
EXPERTS_PER_GROUP = N_EXPERTS // N_EXPERT_GROUPS
LN_EPS = 1e-5
RMS_EPS = 1e-5

COL_FA = 0
COL_FB = COL_FA + F_WIDTH
COL_Q = COL_FB + F_WIDTH
COL_K = COL_Q + A_QK_WIDTH
COL_V = COL_K + A_QK_WIDTH
COL_G = COL_V + A_WIDTH
BIG_COLS = COL_G + 2 * G_WIDTH

LANES = 128
SUBLANES = 8
MOD_ROWS = 8
ROUTER_LANES = 128
DFT_A = 128
VMEM_LIMIT = 56 * 1024 * 1024

TOKEN_TILE = 256
ATTN_TQ = 256
ATTN_GROUP = 2048
ATTN_TK = 1280
MOE_TILE = 256


def _cparams(sem):
    return pltpu.CompilerParams(dimension_semantics=sem, vmem_limit_bytes=VMEM_LIMIT)


def _hdot(a, b):
    return jnp.dot(a, b, precision=HIGHEST, preferred_element_type=F32)


def _bdot(a, b):
    return jnp.dot(a, b, preferred_element_type=F32)


def _layernorm(y, g, b):
    mu = jnp.mean(y, axis=-1, keepdims=True)
    d = y - mu
    var = jnp.mean(d * d, axis=-1, keepdims=True)
    return d * lax.rsqrt(var + LN_EPS) * g + b


def _ada_kernel(c_ref, w_ref, b_ref, o_ref):
    c = c_ref[...]
    a = c * jax.nn.sigmoid(c)
    o_ref[...] = _hdot(a, w_ref[...]) + b_ref[...]


def _ada(c_rows, w_ada, b_ada):
    depth, d, cols = w_ada.shape
    tn = 1536
    assert cols % tn == 0
    return pl.pallas_call(
        _ada_kernel,
        out_shape=jax.ShapeDtypeStruct((depth, MOD_ROWS, cols), F32),
        grid=(depth, cols // tn),
        in_specs=[
            pl.BlockSpec((MOD_ROWS, d), lambda l, j: (0, 0)),
            pl.BlockSpec((None, d, tn), lambda l, j: (l, 0, j)),
            pl.BlockSpec((None, 1, tn), lambda l, j: (l, 0, j)),
        ],
        out_specs=pl.BlockSpec((None, MOD_ROWS, tn), lambda l, j: (l, 0, j)),
        compiler_params=_cparams(("parallel", "parallel")),
        name="ada_mod",
    )(c_rows, w_ada, b_ada.reshape(depth, 1, cols))


def _fw_kernel(wf_ref, bdc_ref, bds_ref, bdw_ref, o_ref):
    bdw = bdw_ref[...]
    mc = _hdot(bdc_ref[...], bdw)
    ms = _hdot(bds_ref[...], bdw)
    wf = wf_ref[...]
    o_ref[:, :F_WIDTH] = _hdot(wf, mc)
    o_ref[:, F_WIDTH:] = _hdot(wf, ms)


def _fourier_weights(wf, bdc, bds, bdw):
    depth, d, _ = wf.shape
    return pl.pallas_call(
        _fw_kernel,
        out_shape=jax.ShapeDtypeStruct((depth, d, 2 * F_WIDTH), F32),
        grid=(depth,),
        in_specs=[
            pl.BlockSpec((None, d, F_WIDTH), lambda l: (l, 0, 0)),
            pl.BlockSpec((F_WIDTH, F_WIDTH), lambda l: (0, 0)),
            pl.BlockSpec((F_WIDTH, F_WIDTH), lambda l: (0, 0)),
            pl.BlockSpec((None, F_WIDTH, F_WIDTH), lambda l: (l, 0, 0)),
        ],
        out_specs=pl.BlockSpec((None, d, 2 * F_WIDTH), lambda l: (l, 0, 0)),
        compiler_params=_cparams(("parallel",)),
        name="fourier_weights",
    )(wf, bdc, bds, bdw)


def _gelu_tanh(x):
    c = math.sqrt(2.0 / math.pi)
    return x * (0.5 * (1.0 + jnp.tanh(c * (x + 0.044715 * (x * x * x)))))


def _inproj_kernel(x_ref, mod_ref, w_ref, cos_ref, sin_ref, lng_ref, lnb_ref, ws_ref, bs_ref,
                   fa_ref, fb_ref, q_ref, k_ref, v_ref, sg_ref, *, q_scale):
    x = x_ref[...]
    sh = mod_ref[0:1, :]
    sc = mod_ref[1:2, :]
    h = (x * (1.0 + sc) + sh).astype(BF16)
    p = _bdot(h, w_ref[...])
    fa_ref[...] = p[:, COL_FA:COL_FA + F_WIDTH]
    fb_ref[...] = p[:, COL_FB:COL_FB + F_WIDTH]
    cosf = cos_ref[...]
    sinf = sin_ref[...]
    for hh in range(A_HEADS):
        t = p[:, COL_Q + LANES * hh:COL_Q + LANES * (hh + 1)]
        r = t * cosf + pltpu.roll(t, LANES // 2, 1) * sinf
        q_ref[:, LANES * hh:LANES * (hh + 1)] = (r * q_scale).astype(BF16)
        t = p[:, COL_K + LANES * hh:COL_K + LANES * (hh + 1)]
        r = t * cosf + pltpu.roll(t, LANES // 2, 1) * sinf
        k_ref[:, LANES * hh:LANES * (hh + 1)] = r.astype(BF16)
    v_ref[...] = p[:, COL_V:COL_V + A_WIDTH].astype(BF16)

    g = _gelu_tanh(p[:, COL_G:COL_G + 2 * G_WIDTH])
    u = g[:, :G_WIDTH]
    vn = _layernorm(g[:, G_WIDTH:], lng_ref[...], lnb_ref[...])
    lane_group = lax.broadcasted_iota(jnp.int32, (CHUNK, G_WIDTH), 1) // G_DIM
    tm = x.shape[0]
    for c in range(tm // CHUNK):
        rows = slice(c * CHUNK, (c + 1) * CHUNK)
        vc = vn[rows]
        mixed = bs_ref[...]
        for gi in range(G_GROUPS):
            vm = jnp.where(lane_group == gi, vc, 0.0).astype(BF16)
            mixed = mixed + _bdot(ws_ref[gi], vm)
        sg_ref[rows, :] = (u[rows] * mixed).astype(BF16)


def _inproj(xa, mods_l, w_big, cosf, sinf, lng, lnb, ws, bs_full, n_lat, q_scale):
    b, l, d = xa.shape
    tm = TOKEN_TILE
    nlat = n_lat // tm
    seg = lambda i: jnp.where(i >= nlat, 1, 0)
    row = lambda width: pl.BlockSpec((None, tm, width), lambda bb, i: (bb, i, 0))
    const2 = lambda shape: pl.BlockSpec(shape, lambda bb, i: (0, 0))
    out_shapes = (
        jax.ShapeDtypeStruct((b, l, F_WIDTH), F32),
        jax.ShapeDtypeStruct((b, l, F_WIDTH), F32),
        jax.ShapeDtypeStruct((b, l, A_QK_WIDTH), BF16),
        jax.ShapeDtypeStruct((b, l, A_QK_WIDTH), BF16),
        jax.ShapeDtypeStruct((b, l, A_WIDTH), BF16),
        jax.ShapeDtypeStruct((b, l, G_WIDTH), BF16),
    )
    return pl.pallas_call(
        functools.partial(_inproj_kernel, q_scale=q_scale),
        out_shape=out_shapes,
        grid=(b, l // tm),
        in_specs=[
            row(d),
            pl.BlockSpec((None, None, MOD_ROWS, d), lambda bb, i: (bb, seg(i), 0, 0)),
            const2((d, BIG_COLS)),
            pl.BlockSpec((tm, LANES), lambda bb, i: (i, 0)),
            pl.BlockSpec((tm, LANES), lambda bb, i: (i, 0)),
            const2((1, G_WIDTH)),
            const2((1, G_WIDTH)),
            pl.BlockSpec((G_GROUPS, CHUNK, CHUNK), lambda bb, i: (0, 0, 0)),
            const2((CHUNK, G_WIDTH)),
        ],
        out_specs=(row(F_WIDTH), row(F_WIDTH), row(A_QK_WIDTH), row(A_QK_WIDTH),
                   row(A_WIDTH), row(G_WIDTH)),
        compiler_params=_cparams(("parallel", "parallel")),
        name="inproj",
    )(xa, mods_l, w_big, cosf, sinf, lng, lnb, ws, bs_full)


def _fft_a_kernel(fa_ref, fb_ref, c_ref, s_ref, tc_ref, ts_ref, zr_ref, zi_ref):
    fa = fa_ref[...]
    fb = fb_ref[...]
    cm = c_ref[...]
    sm = s_ref[...]
    zr = _hdot(cm, fa) - _hdot(sm, fb)
    zi = -(_hdot(cm, fb) + _hdot(sm, fa))
    tc = tc_ref[...]
    ts = ts_ref[...]
    zr_ref[...] = zr * tc + zi * ts
    zi_ref[...] = zi * tc - zr * ts


def _fft_a(fa, fb, cm, sm, twc, tws, bn):
    b, l, _ = fa.shape
    cols = bn * F_WIDTH
    fa3 = fa.reshape(b, l // bn, cols)
    fb3 = fb.reshape(b, l // bn, cols)
    tcw = min(cols, 2048)
    assert cols % tcw == 0
    blk = pl.BlockSpec((None, DFT_A, tcw), lambda bb, j: (bb, 0, j))
    tw = pl.BlockSpec((DFT_A, tcw), lambda bb, j: (0, j))
    mat = pl.BlockSpec((DFT_A, DFT_A), lambda bb, j: (0, 0))
    return pl.pallas_call(
        _fft_a_kernel,
        out_shape=(jax.ShapeDtypeStruct((b, DFT_A, cols), F32),) * 2,
        grid=(b, cols // tcw),
        in_specs=[blk, blk, mat, mat, tw, tw],
        out_specs=(blk, blk),
        compiler_params=_cparams(("parallel", "parallel")),
        name="fft_stage_a",
    )(fa3, fb3, cm, sm, twc, tws)


def _fft_b_kernel(zr_ref, zi_ref, c_ref, s_ref, o_ref):
    cb = c_ref[...]
    sb = s_ref[...]
    for j in range(zr_ref.shape[0]):
        y = _hdot(cb, zr_ref[j]) + _hdot(sb, zi_ref[j])
        o_ref[:, F_WIDTH * j:F_WIDTH * (j + 1)] = y.astype(o_ref.dtype)


def _fft_b(zr, zi, cb, sb, bn):
    b = zr.shape[0]
    kb = 8
    zr4 = zr.reshape(b, DFT_A, bn, F_WIDTH)
    zi4 = zi.reshape(b, DFT_A, bn, F_WIDTH)
    blk = pl.BlockSpec((None, kb, bn, F_WIDTH), lambda bb, i: (bb, i, 0, 0))
    mat = pl.BlockSpec((bn, bn), lambda bb, i: (0, 0))
    out = pl.pallas_call(
        _fft_b_kernel,
        out_shape=jax.ShapeDtypeStruct((b, bn, DFT_A * F_WIDTH), BF16),
        grid=(b, DFT_A // kb),
        in_specs=[blk, blk, mat, mat],
        out_specs=pl.BlockSpec((None, bn, kb * F_WIDTH), lambda bb, i: (bb, 0, i)),
        compiler_params=_cparams(("parallel", "parallel")),
        name="fft_stage_b",
    )(zr4, zi4, cb, sb)
    return out.reshape(b, bn * DFT_A, F_WIDTH)


def _dft_ctx_kernel(fa_ref, fb_ref, c_ref, s_ref, o_ref):
    y = _hdot(c_ref[...], fa_ref[...]) - _hdot(s_ref[...], fb_ref[...])
    o_ref[...] = y.astype(o_ref.dtype)


def _dft_ctx(fa, fb, cc, sc, n_lat, n_ctx):
    b = fa.shape[0]
    blk = pl.BlockSpec((None, n_ctx, F_WIDTH), lambda bb: (bb, n_lat // n_ctx, 0))
    mat = pl.BlockSpec((n_ctx, n_ctx), lambda bb: (0, 0))
    return pl.pallas_call(
        _dft_ctx_kernel,
        out_shape=jax.ShapeDtypeStruct((b, n_ctx, F_WIDTH), BF16),
        grid=(b,),
        in_specs=[blk, blk, mat, mat],
        out_specs=pl.BlockSpec((None, n_ctx, F_WIDTH), lambda bb: (bb, 0, 0)),
        compiler_params=_cparams(("parallel",)),
        name="dft_ctx",
    )(fa, fb, cc, sc)


def _score_slot(j):
    return 2 if j == 0 else (j - 1) % 2


def _attn_kernel(q_ref, kt_ref, v_ref, lam_ref, gain_ref, o_ref, q2_ref, s_ref, m_ref, acc_ref,
                 *, lam_init, tq):
    nsub = q_ref.shape[0] // tq
    nk, _, tk = kt_ref.shape
    lane = lax.broadcasted_iota(jnp.int32, (tq, A_DV), 1)
    comp0 = (lane % A_DH) < (A_DH // 2)
    lp = lam_ref[...]
    lam = (jnp.exp(jnp.sum(lp[0:1] * lp[1:2], keepdims=True))
           - jnp.exp(jnp.sum(lp[2:3] * lp[3:4], keepdims=True)) + lam_init)
    out_gain = gain_ref[...] * (1.0 - lam_init)

    def rows(i):
        return pl.ds(pl.multiple_of(i * tq, tq), tq)

    def load_q2(i):
        q = q_ref[rows(i), :]
        zero = jnp.zeros_like(q)
        q2_ref[0:tq, :] = jnp.where(comp0, q, zero)
        q2_ref[tq:2 * tq, :] = jnp.where(comp0, zero, q)

    def scores(j):
        s_ref[_score_slot(j)] = _bdot(q2_ref[...], kt_ref[j])

    def softmax_pv(j):
        s = s_ref[_score_slot(j)]
        m_old = m_ref[...]
        m_new = jnp.maximum(m_old, jnp.max(s, axis=1, keepdims=True))
        alpha = jnp.exp2(m_old - m_new)
        p = jnp.exp2(s - jnp.concatenate([m_new] * (tk // LANES), axis=1)).astype(BF16)
        pv = _bdot(p, v_ref[j])
        acc_ref[...] = acc_ref[...] * jnp.concatenate([alpha, alpha], axis=1) + pv
        m_ref[...] = m_new

    def finalize(i):
        acc = acc_ref[...]
        o0 = acc[0:tq, 0:LANES] / acc[0:tq, LANES:]
        o1 = acc[tq:, 0:LANES] / acc[tq:, LANES:]
        o = o0 - lam * o1
        ms = jnp.mean(o * o, axis=-1, keepdims=True)
        o_ref[rows(i), :] = (o * lax.rsqrt(ms + RMS_EPS) * out_gain).astype(o_ref.dtype)

    def next_tile_first_scores(i):
        load_q2(jnp.minimum(i + 1, nsub - 1))
        scores(0)

    load_q2(0)
    scores(0)

    def tile(i, carry):
        m_ref[...] = jnp.full(m_ref.shape, -jnp.inf, F32)
        acc_ref[...] = jnp.zeros(acc_ref.shape, F32)
        for j in range(nk):
            if j + 1 < nk:
                scores(j + 1)
                softmax_pv(j)
            elif nk > 1:
                next_tile_first_scores(i)
                softmax_pv(j)
            else:
                softmax_pv(j)
                next_tile_first_scores(i)
        finalize(i)
        return carry

    lax.fori_loop(0, nsub, tile, 0)


def _attention(q, kt, v_ext, lam_p, gain, *, q_block0, n_q, tq, group, lam_init):
    b = q.shape[0]
    _, _, nk, _, tk = kt.shape
    assert n_q % group == 0 and group % tq == 0
    return pl.pallas_call(
        functools.partial(_attn_kernel, lam_init=lam_init, tq=tq),
        out_shape=jax.ShapeDtypeStruct((b, n_q, A_WIDTH), BF16),
        grid=(b, A_HEADS, n_q // group),
        in_specs=[
            pl.BlockSpec((None, group, A_DV), lambda bb, h, i: (bb, q_block0 + i, h)),
            pl.BlockSpec((None, None, nk, A_DV, tk), lambda bb, h, i: (bb, h, 0, 0, 0)),
            pl.BlockSpec((None, None, nk, tk, 2 * LANES), lambda bb, h, i: (bb, h, 0, 0, 0)),
            pl.BlockSpec((4, A_DH), lambda bb, h, i: (0, 0)),
            pl.BlockSpec((1, A_DV), lambda bb, h, i: (0, 0)),
        ],
        out_specs=pl.BlockSpec((None, group, A_DV), lambda bb, h, i: (bb, i, h)),
        scratch_shapes=[
            pltpu.VMEM((2 * tq, A_DV), BF16),
            pltpu.VMEM((3, 2 * tq, tk), F32),
            pltpu.VMEM((2 * tq, LANES), F32),
            pltpu.VMEM((2 * tq, 2 * LANES), F32),
        ],
        compiler_params=_cparams(("parallel", "parallel", "arbitrary")),
        name="diff_attn",
    )(q, kt, v_ext, lam_p, gain)


def _key_value_chunks(k, v, tk):
    b, nkeys, _ = k.shape
    nk = nkeys // tk
    kt = k.reshape(b, nk, tk, A_HEADS, A_DV).transpose(0, 3, 1, 4, 2)
    vh = v.reshape(b, nk, tk, A_HEADS, A_DV).transpose(0, 3, 1, 2, 4)
    v_ext = jnp.concatenate([vh, jnp.ones_like(vh)], axis=-1)
    return kt, v_ext


_HI16 = 0xFFFF0000


def _pack_bf16_pair(lo, hi):
    ulo = lax.bitcast_convert_type(lo.astype(BF16).astype(F32), jnp.uint32) >> 16
    uhi = lax.bitcast_convert_type(hi.astype(BF16).astype(F32), jnp.uint32) & jnp.uint32(_HI16)
    return ulo | uhi


def _unpack_bf16_pair(u):
    lo = lax.bitcast_convert_type(u << 16, F32)
    hi = lax.bitcast_convert_type(u & jnp.uint32(_HI16), F32)
    return lo, hi


def _router_gates(sc_t, sel_t, gates_ref, chosen_ref):
    s_rows = [sc_t[e:e + 1, :] for e in range(N_EXPERTS)]
    v_rows = [sel_t[e:e + 1, :] for e in range(N_EXPERTS)]
    in_top2 = []
    group_score = []
    for g in range(N_EXPERT_GROUPS):
        vs = v_rows[g * EXPERTS_PER_GROUP:(g + 1) * EXPERTS_PER_GROUP]
        tops = []
        for jj in range(EXPERTS_PER_GROUP):
            rank = jnp.zeros_like(vs[jj])
            for ii in range(EXPERTS_PER_GROUP):
                if ii == jj:
                    continue
                beats = (vs[ii] >= vs[jj]) if ii < jj else (vs[ii] > vs[jj])
                rank = rank + jnp.where(beats, 1.0, 0.0)
            tops.append(rank < 2.0)
        in_top2 += tops
        gs = jnp.zeros_like(vs[0])
        for jj in range(EXPERTS_PER_GROUP):
            gs = gs + jnp.where(tops[jj], vs[jj], 0.0)
        group_score.append(gs)
    best = group_score[0]
    gidx = jnp.zeros_like(best)
    for g in range(1, N_EXPERT_GROUPS):
        upd = group_score[g] > best
        best = jnp.where(upd, group_score[g], best)
        gidx = jnp.where(upd, float(g), gidx)
    chosen = [jnp.logical_and(in_top2[e], gidx == float(e // EXPERTS_PER_GROUP))
              for e in range(N_EXPERTS)]
    denom = jnp.zeros_like(best)
    for e in range(N_EXPERTS):
        denom = denom + jnp.where(chosen[e], s_rows[e], 0.0)
    for e in range(N_EXPERTS):
        gates_ref[e:e + 1, :] = jnp.where(chosen[e], s_rows[e] / denom, 0.0)
        chosen_ref[e:e + 1, :] = jnp.where(chosen[e], 1.0, 0.0)


def _outproj_kernel(yf_ref, att_ref, sg_ref, x_ref, mod_ref, wo_ref, lng_ref, lnb_ref,
                    wr_ref, rb_ref, x1_ref, h_ref, gates_ref, chosen_ref, *, alpha):
    mix = _bdot(yf_ref[...], wo_ref[0:F_WIDTH, :])
    mix = mix + _bdot(att_ref[...], wo_ref[F_WIDTH:F_WIDTH + A_WIDTH, :])
    mix = mix + _bdot(sg_ref[...], wo_ref[F_WIDTH + A_WIDTH:, :])
    g1 = mod_ref[2:3, :]
    x1 = _layernorm(alpha * x_ref[...] + g1 * mix, lng_ref[...], lnb_ref[...])
    x1_ref[...] = x1
    h = x1 * (1.0 + mod_ref[4:5, :]) + mod_ref[3:4, :]
    half = h.shape[1] // 2
    h_ref[...] = _pack_bf16_pair(h[:, :half], h[:, half:])
    scores = jax.nn.sigmoid(_hdot(h, wr_ref[...]))
    sel = scores + rb_ref[...]
    _router_gates(scores.T[0:N_EXPERTS, :], sel.T[0:N_EXPERTS, :], gates_ref, chosen_ref)


def _outproj(yf, att, sg, xa, mods_l, wo, lng, lnb, wr, rb, n_lat, alpha):
    b, l, d = xa.shape
    tm = TOKEN_TILE
    nlat = n_lat // tm
    seg = lambda i: jnp.where(i >= nlat, 1, 0)
    row = lambda width: pl.BlockSpec((None, tm, width), lambda bb, i: (bb, i, 0))
    const2 = lambda shape: pl.BlockSpec(shape, lambda bb, i: (0, 0))
    return pl.pallas_call(
        functools.partial(_outproj_kernel, alpha=alpha),
        out_shape=(
            jax.ShapeDtypeStruct((b, l, d), F32),
            jax.ShapeDtypeStruct((b, l, d // 2), jnp.uint32),
            jax.ShapeDtypeStruct((b, N_EXPERTS, l), F32),
            jax.ShapeDtypeStruct((b, N_EXPERTS, l), F32),
        ),
        grid=(b, l // tm),
        in_specs=[
            row(F_WIDTH), row(A_WIDTH), row(G_WIDTH), row(d),
            pl.BlockSpec((None, None, MOD_ROWS, d), lambda bb, i: (bb, seg(i), 0, 0)),
            const2((d, d)), const2((1, d)), const2((1, d)),
            const2((d, ROUTER_LANES)), const2((1, ROUTER_LANES)),
        ],
        out_specs=(row(d), row(d // 2),
                   pl.BlockSpec((None, N_EXPERTS, tm), lambda bb, i: (bb, 0, i)),
                   pl.BlockSpec((None, N_EXPERTS, tm), lambda bb, i: (bb, 0, i))),
        compiler_params=_cparams(("parallel", "parallel")),
        name="outproj_ln_router",
    )(yf, att, sg, xa, mods_l, wo, lng, lnb, wr, rb)


_PAIRS = [(a, c) for a in range(EXPERTS_PER_GROUP) for c in range(a + 1, EXPERTS_PER_GROUP)]
N_PAIR_CLASSES = N_EXPERT_GROUPS * len(_PAIRS)


def _moe_plan(gates_t, chosen_t, tm):
    b, _, l = gates_t.shape
    t = b * l
    npair = len(_PAIRS)
    pair_index = np.zeros((EXPERTS_PER_GROUP, EXPERTS_PER_GROUP), np.int32)
    cls_e1 = np.zeros((N_PAIR_CLASSES,), np.int32)
    cls_e2 = np.zeros((N_PAIR_CLASSES,), np.int32)
    for k, (a, c) in enumerate(_PAIRS):
        pair_index[a, c] = k
        for g in range(N_EXPERT_GROUPS):
            cls_e1[g * npair + k] = g * EXPERTS_PER_GROUP + a
            cls_e2[g * npair + k] = g * EXPERTS_PER_GROUP + c
    e_lo = jnp.argmax(chosen_t, axis=1).astype(jnp.int32)
    e_hi = (N_EXPERTS - 1 - jnp.argmax(chosen_t[:, ::-1, :], axis=1)).astype(jnp.int32)
    g_lo = jnp.take_along_axis(gates_t, e_lo[:, None, :], axis=1)[:, 0, :]
    g_hi = jnp.take_along_axis(gates_t, e_hi[:, None, :], axis=1)[:, 0, :]
    pair_gates = jnp.concatenate([jnp.broadcast_to(g_lo[:, :, None], (b, l, LANES)),
                                  jnp.broadcast_to(g_hi[:, :, None], (b, l, LANES))], axis=-1)
    cls = ((e_lo // EXPERTS_PER_GROUP) * npair
           + jnp.asarray(pair_index)[e_lo % EXPERTS_PER_GROUP, e_hi % EXPERTS_PER_GROUP]).reshape(t)
    onehot = (cls[:, None] == jnp.arange(N_PAIR_CLASSES, dtype=jnp.int32)[None, :]).astype(jnp.int32)
    csum = jnp.cumsum(onehot, axis=0)
    rank = jnp.take_along_axis(csum, cls[:, None], axis=1)[:, 0] - 1
    padded = (csum[-1] + tm - 1) // tm * tm
    off_end = jnp.cumsum(padded)
    dest = (off_end - padded)[cls] + rank
    nt = t // tm + N_PAIR_CLASSES
    n_valid = (off_end[-1] // tm).astype(jnp.int32)
    first_row = jnp.minimum(jnp.arange(nt, dtype=jnp.int32), n_valid - 1) * tm
    tile_cls = jnp.minimum(jnp.searchsorted(off_end, first_row, side="right"), N_PAIR_CLASSES - 1)
    return (dest.astype(jnp.int32), pair_gates, jnp.asarray(cls_e1)[tile_cls],
            jnp.asarray(cls_e2)[tile_cls], n_valid.reshape(1), nt)


def _row_copy(src_ref, src_row, dst_ref, dst_row, sem):
    return pltpu.make_async_copy(src_ref.at[pl.ds(src_row, 1), :], dst_ref.at[pl.ds(dst_row, 1), :], sem)


def _dispatch_kernel(dest_ref, h_ref, hs_init_ref, hs_ref, sem):
    del hs_init_ref
    tm = h_ref.shape[0]
    for r in range(tm):
        _row_copy(h_ref, r, hs_ref, dest_ref[0, r], sem).start()
    pltpu.make_async_copy(h_ref, hs_ref.at[pl.ds(0, tm), :], sem).wait()


def _dispatch(h_packed, dest, nt, tm):
    t, w = h_packed.shape
    hs_init = jnp.zeros((nt * tm, w), jnp.uint32)
    return pl.pallas_call(
        _dispatch_kernel,
        out_shape=jax.ShapeDtypeStruct((nt * tm, w), jnp.uint32),
        grid=(t // tm,),
        in_specs=[
            pl.BlockSpec((None, 1, tm), lambda i: (i, 0, 0), memory_space=pltpu.SMEM),
            pl.BlockSpec((tm, w), lambda i: (i, 0)),
            pl.BlockSpec(memory_space=pl.ANY),
        ],
        out_specs=pl.BlockSpec(memory_space=pl.ANY),
        scratch_shapes=[pltpu.SemaphoreType.DMA(())],
        input_output_aliases={2: 0},
        compiler_params=_cparams(("arbitrary",)),
        name="moe_dispatch",
    )(dest.reshape(t // tm, 1, tm), h_packed, hs_init)


def _expert_pair_kernel(e1_ref, e2_ref, nv_ref, hs_ref, w1a_ref, w3a_ref, w2a_ref,
                        w1b_ref, w3b_ref, w2b_ref, y_ref):
    del e1_ref, e2_ref
    i = pl.program_id(0)

    @pl.when(i < nv_ref[0])
    def _compute():
        lo, hi = _unpack_bf16_pair(hs_ref[...])
        h = jnp.concatenate([lo, hi], axis=1).astype(BF16)

        def ffn(w1_ref, w3_ref, w2_ref):
            a = _bdot(h, w1_ref[...])
            a = (a * jax.nn.sigmoid(a)) * _bdot(h, w3_ref[...])
            return _bdot(a.astype(BF16), w2_ref[...])

        y_ref[...] = _pack_bf16_pair(ffn(w1a_ref, w3a_ref, w2a_ref), ffn(w1b_ref, w3b_ref, w2b_ref))

    @pl.when(i >= nv_ref[0])
    def _unused_tile():
        y_ref[...] = jnp.zeros(y_ref.shape, jnp.uint32)


def _expert_pairs(hs, tile_e1, tile_e2, n_valid, w1, w3, w2, tm):
    rows, w = hs.shape
    _, d, de = w1.shape
    first = lambda shape: pl.BlockSpec(shape, lambda i, e1, e2, nv: (e1[i], 0, 0))
    second = lambda shape: pl.BlockSpec(shape, lambda i, e1, e2, nv: (e2[i], 0, 0))
    grid_spec = pltpu.PrefetchScalarGridSpec(
        num_scalar_prefetch=3,
        grid=(rows // tm,),
        in_specs=[
            pl.BlockSpec((tm, w), lambda i, e1, e2, nv: (jnp.minimum(i, nv[0] - 1), 0)),
            first((None, d, de)), first((None, d, de)), first((None, de, d)),
            second((None, d, de)), second((None, d, de)), second((None, de, d)),
        ],
        out_specs=pl.BlockSpec((tm, d), lambda i, e1, e2, nv: (i, 0)),
    )
    return pl.pallas_call(
        _expert_pair_kernel,
        out_shape=jax.ShapeDtypeStruct((rows, d), jnp.uint32),
        grid_spec=grid_spec,
        compiler_params=_cparams(("arbitrary",)),
        name="moe_expert_pairs",
    )(tile_e1, tile_e2, n_valid, hs, w1, w3, w2, w1, w3, w2)


def _combine_kernel(dest_ref, y2_ref, x_ref, pg_ref, mod_ref, lng_ref, lnb_ref, o_ref, ybuf, sem,
                    *, alpha):
    tm, d = x_ref.shape
    for r in range(tm):
        _row_copy(y2_ref, dest_ref[0, r], ybuf, r, sem).start()
    pltpu.make_async_copy(y2_ref.at[pl.ds(0, tm), :], ybuf, sem).wait()
    y_lo, y_hi = _unpack_bf16_pair(ybuf[...])
    pg = pg_ref[...]
    reps = d // LANES
    g_lo = jnp.concatenate([pg[:, :LANES]] * reps, axis=1)
    g_hi = jnp.concatenate([pg[:, LANES:]] * reps, axis=1)
    y = g_lo * y_lo + g_hi * y_hi
    o_ref[...] = _layernorm(alpha * x_ref[...] + mod_ref[5:6, :] * y, lng_ref[...], lnb_ref[...])


def _combine(y2, dest, x1, pair_gates, mods_l, lng, lnb, *, n_rows, n_lat, tm, alpha):
    b, l, d = x1.shape
    nlat = n_lat // tm
    seg = lambda i: jnp.where(i >= nlat, 1, 0)
    tiles_per_batch = l // tm
    row = lambda width: pl.BlockSpec((None, tm, width), lambda bb, i: (bb, i, 0))
    const2 = lambda shape: pl.BlockSpec(shape, lambda bb, i: (0, 0))
    return pl.pallas_call(
        functools.partial(_combine_kernel, alpha=alpha),
        out_shape=jax.ShapeDtypeStruct((b, n_rows, d), F32),
        grid=(b, n_rows // tm),
        in_specs=[
            pl.BlockSpec((None, 1, tm), lambda bb, i: (bb * tiles_per_batch + i, 0, 0),
                         memory_space=pltpu.SMEM),
            pl.BlockSpec(memory_space=pl.ANY),
            row(d), row(2 * LANES),
            pl.BlockSpec((None, None, MOD_ROWS, d), lambda bb, i: (bb, seg(i), 0, 0)),
            const2((1, d)), const2((1, d)),
        ],
        out_specs=row(d),
        scratch_shapes=[pltpu.VMEM((tm, d), jnp.uint32), pltpu.SemaphoreType.DMA(())],
        compiler_params=_cparams(("arbitrary", "arbitrary")),
        name="moe_combine_ln",
    )(dest.reshape(b * tiles_per_batch, 1, tm), y2, x1, pair_gates, mods_l, lng, lnb)


def _head_lane_fields():
    j = jnp.arange(LANES)
    half = j // (LANES // 2)
    comp = (j % (LANES // 2)) // (A_DH // 2)
    axis = (j % (A_DH // 2)) // (A_DH // 4)
    freq = j % (A_DH // 4)
    return half, comp, axis, freq


def _qk_column_perm():
    half, comp, axis, freq = _head_lane_fields()
    orig = comp * A_DH + axis * (A_DH // 2) + half * (A_DH // 4) + freq
    return (jnp.arange(A_HEADS)[:, None] * LANES + orig[None, :]).reshape(-1)


def _rope_tables(n_lat, n_ctx):
    half, _, axis, freq = _head_lane_fields()
    rows = n_lat // GRID_W
    row = jnp.repeat(jnp.arange(rows, dtype=F32), GRID_W)
    col = jnp.tile(jnp.arange(GRID_W, dtype=F32), rows)
    hd = A_DH // 2
    inv = ROPE_BASE ** (-jnp.arange(0, hd, 2, dtype=F32) / hd)
    pos = jnp.where(axis[None, :] == 0, row[:, None], col[:, None])
    ang = pos * inv[freq][None, :]
    cosf = jnp.cos(ang)
    sinf = jnp.sin(ang) * jnp.where(half == 0, -1.0, 1.0)[None, :]
    cosf = jnp.concatenate([cosf, jnp.ones((n_ctx, LANES), F32)], axis=0)
    sinf = jnp.concatenate([sinf, jnp.zeros((n_ctx, LANES), F32)], axis=0)
    return cosf, sinf


def _dft_mats(n, scale=1.0):
    i = jnp.arange(n, dtype=jnp.int32)
    ang = ((i[:, None] * i[None, :]) % n).astype(F32) * (2.0 * math.pi / n)
    return jnp.cos(ang) * scale, jnp.sin(ang) * scale


def _twiddles(n, bn):
    ka = jnp.arange(DFT_A, dtype=jnp.int32)
    bb = jnp.arange(bn, dtype=jnp.int32)
    ang = ((ka[:, None] * bb[None, :]) % n).astype(F32) * (2.0 * math.pi / n)
    shape = (DFT_A, bn, F_WIDTH)
    twc = jnp.broadcast_to(jnp.cos(ang)[:, :, None], shape).reshape(DFT_A, bn * F_WIDTH)
    tws = jnp.broadcast_to(jnp.sin(ang)[:, :, None], shape).reshape(DFT_A, bn * F_WIDTH)
    return twc, tws


def _block_diag(blocks):
    g, n = blocks.shape[-3], blocks.shape[-1]
    eye = jnp.eye(g, dtype=blocks.dtype)
    out = blocks[..., :, :, None, :] * eye[:, None, :, None]
    return out.reshape(blocks.shape[:-3] + (g * n, g * n))


def kernel(x, c, ctx, c_ctx, w_ada, b_ada, w_in, w_fourier, diff_lambda, diff_subln,
           sgu_ln_g, sgu_ln_b, sgu_w, sgu_b, w_out, ln_g, ln_b, w_router, router_bias,
           moe_w1, moe_w3, moe_w2):
    b, n, d = x.shape
    n_ctx = ctx.shape[1]
    depth = w_ada.shape[0]
    l = n + n_ctx
    alpha = (2 * depth) ** 0.25
    bn = n // DFT_A
    assert n % DFT_A == 0 and bn % SUBLANES == 0 and n % GRID_W == 0
    assert n_ctx % TOKEN_TILE == 0 and n % TOKEN_TILE == 0 and n % n_ctx == 0
    assert b + 1 <= MOD_ROWS

    c_rows = jnp.concatenate([c, c_ctx[None, :], jnp.zeros((MOD_ROWS - b - 1, d), F32)], axis=0)
    mod = _ada(c_rows, w_ada, b_ada).reshape(depth, MOD_ROWS, 6, d)
    lat = mod[:, :b]
    cm = jnp.broadcast_to(mod[:, b:b + 1], lat.shape)
    mods = jnp.stack([lat, cm], axis=2)
    mods = jnp.pad(mods, ((0, 0), (0, 0), (0, 0), (0, MOD_ROWS - 6), (0, 0)))

    perm = _qk_column_perm()
    c64, s64 = _dft_mats(F_DIM)
    eye_g = jnp.eye(F_GROUPS, dtype=F32)
    bdc = jnp.kron(eye_g, c64)
    bds = jnp.kron(eye_g, s64)
    bdw = _block_diag(w_fourier)
    w_fab = _fourier_weights(w_in[:, :, :F_WIDTH], bdc, bds, bdw)
    o_q = F_WIDTH
    o_k = o_q + A_QK_WIDTH
    o_v = o_k + A_QK_WIDTH
    o_g = o_v + A_WIDTH
    w_big = jnp.concatenate([
        w_fab,
        w_in[:, :, o_q:o_k][:, :, perm],
        w_in[:, :, o_k:o_v][:, :, perm],
        w_in[:, :, o_v:],
    ], axis=-1).astype(BF16)
    wo = w_out.astype(BF16)
    w1 = moe_w1.astype(BF16)
    w3 = moe_w3.astype(BF16)
    w2 = moe_w2.astype(BF16)
    ws = sgu_w.astype(BF16)
    bs_full = jnp.repeat(jnp.swapaxes(sgu_b, 1, 2), G_DIM, axis=2)
    wr = jnp.pad(w_router, ((0, 0), (0, ROUTER_LANES - N_EXPERTS)))
    rb = jnp.pad(router_bias, (0, ROUTER_LANES - N_EXPERTS)).reshape(1, ROUTER_LANES)

    cosf, sinf = _rope_tables(n, n_ctx)
    ca, sa = _dft_mats(DFT_A)
    twc, tws = _twiddles(n, bn)
    cb, sb = _dft_mats(bn, scale=(n * F_DIM) ** -0.5)
    cc, sc = _dft_mats(n_ctx, scale=(n_ctx * F_DIM) ** -0.5)
    q_scale = (A_DH ** -0.5) * math.log2(math.e)

    tk = ATTN_TK
    assert l % tk == 0
    assert n % MOE_TILE == 0 and n_ctx % MOE_TILE == 0

    xa = jnp.concatenate([x, ctx], axis=1)
    for li in range(depth):
        last = li == depth - 1
        lam_init = 0.8 - 0.6 * math.exp(-0.3 * li)
        fa, fb, q, k, v, sg = _inproj(
            xa, mods[li], w_big[li], cosf, sinf, sgu_ln_g[li][None], sgu_ln_b[li][None],
            ws[li], bs_full[li], n, q_scale)
        zr, zi = _fft_a(fa, fb, ca, sa, twc, tws, bn)
        yf = jnp.concatenate([_fft_b(zr, zi, cb, sb, bn), _dft_ctx(fa, fb, cc, sc, n, n_ctx)], axis=1)
        gain = diff_subln[li][None]
        kt, v_ext = _key_value_chunks(k, v, tk)
        att_x = _attention(q, kt, v_ext, diff_lambda[li], gain, q_block0=0, n_q=n, tq=ATTN_TQ,
                           group=min(ATTN_GROUP, n), lam_init=lam_init)
        kt_c, v_ext_c = _key_value_chunks(k[:, n:], v[:, n:], n_ctx)
        att_c = _attention(q, kt_c, v_ext_c, diff_lambda[li], gain, q_block0=n // n_ctx, n_q=n_ctx,
                           tq=n_ctx, group=n_ctx, lam_init=lam_init)
        att = jnp.concatenate([att_x, att_c], axis=1)
        x1, h_packed, gates_t, chosen_t = _outproj(
            yf, att, sg, xa, mods[li], wo[li], ln_g[li, 0][None], ln_b[li, 0][None], wr, rb, n, alpha)
        dest, pair_gates, tile_e1, tile_e2, n_valid, nt = _moe_plan(gates_t, chosen_t, MOE_TILE)
        hs = _dispatch(h_packed.reshape(b * l, d // 2), dest, nt, MOE_TILE)
        y2 = _expert_pairs(hs, tile_e1, tile_e2, n_valid, w1[li], w3[li], w2[li], MOE_TILE)
        xa = _combine(y2, dest, x1, pair_gates, mods[li], ln_g[li, 1][None], ln_b[li, 1][None],
                      n_rows=n if last else l, n_lat=n, tm=MOE_TILE, alpha=alpha)
    return xa
```

```python
import functools
import math

import numpy as np
import jax
import jax.numpy as jnp
from jax import lax
from jax.experimental import pallas as pl
from jax.experimental.pallas import tpu as pltpu

F32 = jnp.float32
BF16 = jnp.bfloat16
HIGHEST = lax.Precision.HIGHEST

GRID_W = 64
F_GROUPS, F_DIM = 4, 64
F_WIDTH = F_GROUPS * F_DIM
A_HEADS, A_DH = 4, 64
A_DV = 2 * A_DH
A_QK_WIDTH = A_HEADS * 2 * A_DH
A_WIDTH = A_HEADS * A_DV
G_GROUPS, G_DIM = 4, 64
G_WIDTH = G_GROUPS * G_DIM
CHUNK = 128
ROPE_BASE = 10000.0
N_EXPERTS = 16
N_EXPERT_GROUPS = 4
EXPERTS_PER_GROUP = N_EXPERTS // N_EXPERT_GROUPS
LN_EPS = 1e-5
RMS_EPS = 1e-5

COL_FA = 0
COL_FB = COL_FA + F_WIDTH
COL_Q = COL_FB + F_WIDTH
COL_K = COL_Q + A_QK_WIDTH
COL_V = COL_K + A_QK_WIDTH
COL_G = COL_V + A_WIDTH
BIG_COLS = COL_G + 2 * G_WIDTH

LANES = 128
SUBLANES = 8
MOD_ROWS = 8
ROUTER_LANES = 128
DFT_A = 128
VMEM_LIMIT = 56 * 1024 * 1024

TOKEN_TILE = 256
ATTN_TQ = 256
ATTN_GROUP = 2048
ATTN_TK = 1280
MOE_TILE = 256


def _cparams(sem):
    return pltpu.CompilerParams(dimension_semantics=sem, vmem_limit_bytes=VMEM_LIMIT)


def _hdot(a, b):
    return jnp.dot(a, b, precision=HIGHEST, preferred_element_type=F32)


def _bdot(a, b):
    return jnp.dot(a, b, preferred_element_type=F32)


def _layernorm(y, g, b):
    mu = jnp.mean(y, axis=-1, keepdims=True)
    d = y - mu
    var = jnp.mean(d * d, axis=-1, keepdims=True)
    return d * lax.rsqrt(var + LN_EPS) * g + b


def _ada_kernel(c_ref, w_ref, b_ref, o_ref):
    c = c_ref[...]
    a = c * jax.nn.sigmoid(c)
    o_ref[...] = _hdot(a, w_ref[...]) + b_ref[...]


def _ada(c_rows, w_ada, b_ada):
    depth, d, cols = w_ada.shape
    tn = 1536
    assert cols % tn == 0
    return pl.pallas_call(
        _ada_kernel,
        out_shape=jax.ShapeDtypeStruct((depth, MOD_ROWS, cols), F32),
        grid=(depth, cols // tn),
        in_specs=[
            pl.BlockSpec((MOD_ROWS, d), lambda l, j: (0, 0)),
            pl.BlockSpec((None, d, tn), lambda l, j: (l, 0, j)),
            pl.BlockSpec((None, 1, tn), lambda l, j: (l, 0, j)),
        ],
        out_specs=pl.BlockSpec((None, MOD_ROWS, tn), lambda l, j: (l, 0, j)),
        compiler_params=_cparams(("parallel", "parallel")),
        name="ada_mod",
    )(c_rows, w_ada, b_ada.reshape(depth, 1, cols))


def _fw_kernel(wf_ref, bdc_ref, bds_ref, bdw_ref, o_ref):
    bdw = bdw_ref[...]
    mc = _hdot(bdc_ref[...], bdw)
    ms = _hdot(bds_ref[...], bdw)
    wf = wf_ref[...]
    o_ref[:, :F_WIDTH] = _hdot(wf, mc)
    o_ref[:, F_WIDTH:] = _hdot(wf, ms)


def _fourier_weights(wf, bdc, bds, bdw):
    depth, d, _ = wf.shape
    return pl.pallas_call(
        _fw_kernel,
        out_shape=jax.ShapeDtypeStruct((depth, d, 2 * F_WIDTH), F32),
        grid=(depth,),
        in_specs=[
            pl.BlockSpec((None, d, F_WIDTH), lambda l: (l, 0, 0)),
            pl.BlockSpec((F_WIDTH, F_WIDTH), lambda l: (0, 0)),
            pl.BlockSpec((F_WIDTH, F_WIDTH), lambda l: (0, 0)),
            pl.BlockSpec((None, F_WIDTH, F_WIDTH), lambda l: (l, 0, 0)),
        ],
        out_specs=pl.BlockSpec((None, d, 2 * F_WIDTH), lambda l: (l, 0, 0)),
        compiler_params=_cparams(("parallel",)),
        name="fourier_weights",
    )(wf, bdc, bds, bdw)


def _gelu_tanh(x):
    c = math.sqrt(2.0 / math.pi)
    return x * (0.5 * (1.0 + jnp.tanh(c * (x + 0.044715 * (x * x * x)))))


def _inproj_kernel(x_ref, mod_ref, w_ref, cos_ref, sin_ref, lng_ref, lnb_ref, ws_ref, bs_ref,
                   fa_ref, fb_ref, q_ref, k_ref, v_ref, sg_ref, *, q_scale):
    x = x_ref[...]
    sh = mod_ref[0:1, :]
    sc = mod_ref[1:2, :]
    h = (x * (1.0 + sc) + sh).astype(BF16)
    p = _bdot(h, w_ref[...])
    fa_ref[...] = p[:, COL_FA:COL_FA + F_WIDTH]
    fb_ref[...] = p[:, COL_FB:COL_FB + F_WIDTH]
    cosf = cos_ref[...]
    sinf = sin_ref[...]
    for hh in range(A_HEADS):
        t = p[:, COL_Q + LANES * hh:COL_Q + LANES * (hh + 1)]
        r = t * cosf + pltpu.roll(t, LANES // 2, 1) * sinf
        q_ref[:, LANES * hh:LANES * (hh + 1)] = (r * q_scale).astype(BF16)
        t = p[:, COL_K + LANES * hh:COL_K + LANES * (hh + 1)]
        r = t * cosf + pltpu.roll(t, LANES // 2, 1) * sinf
        k_ref[:, LANES * hh:LANES * (hh + 1)] = r.astype(BF16)
    v_ref[...] = p[:, COL_V:COL_V + A_WIDTH].astype(BF16)

    g = _gelu_tanh(p[:, COL_G:COL_G + 2 * G_WIDTH])
    u = g[:, :G_WIDTH]
    vn = _layernorm(g[:, G_WIDTH:], lng_ref[...], lnb_ref[...])
    lane_group = lax.broadcasted_iota(jnp.int32, (CHUNK, G_WIDTH), 1) // G_DIM
    tm = x.shape[0]
    for c in range(tm // CHUNK):
        rows = slice(c * CHUNK, (c + 1) * CHUNK)
        vc = vn[rows]
        mixed = bs_ref[...]
        for gi in range(G_GROUPS):
            vm = jnp.where(lane_group == gi, vc, 0.0).astype(BF16)
            mixed = mixed + _bdot(ws_ref[gi], vm)
        sg_ref[rows, :] = (u[rows] * mixed).astype(BF16)


def _inproj(xa, mods_l, w_big, cosf, sinf, lng, lnb, ws, bs_full, n_lat, q_scale):
    b, l, d = xa.shape
    tm = TOKEN_TILE
    nlat = n_lat // tm
    seg = lambda i: jnp.where(i >= nlat, 1, 0)
    row = lambda width: pl.BlockSpec((None, tm, width), lambda bb, i: (bb, i, 0))
    const2 = lambda shape: pl.BlockSpec(shape, lambda bb, i: (0, 0))
    out_shapes = (
        jax.ShapeDtypeStruct((b, l, F_WIDTH), F32),
        jax.ShapeDtypeStruct((b, l, F_WIDTH), F32),
        jax.ShapeDtypeStruct((b, l, A_QK_WIDTH), BF16),
        jax.ShapeDtypeStruct((b, l, A_QK_WIDTH), BF16),
        jax.ShapeDtypeStruct((b, l, A_WIDTH), BF16),
        jax.ShapeDtypeStruct((b, l, G_WIDTH), BF16),
    )
    return pl.pallas_call(
        functools.partial(_inproj_kernel, q_scale=q_scale),
        out_shape=out_shapes,
        grid=(b, l // tm),
        in_specs=[
            row(d),
            pl.BlockSpec((None, None, MOD_ROWS, d), lambda bb, i: (bb, seg(i), 0, 0)),
            const2((d, BIG_COLS)),
            pl.BlockSpec((tm, LANES), lambda bb, i: (i, 0)),
            pl.BlockSpec((tm, LANES), lambda bb, i: (i, 0)),
            const2((1, G_WIDTH)),
            const2((1, G_WIDTH)),
            pl.BlockSpec((G_GROUPS, CHUNK, CHUNK), lambda bb, i: (0, 0, 0)),
            const2((CHUNK, G_WIDTH)),
        ],
        out_specs=(row(F_WIDTH), row(F_WIDTH), row(A_QK_WIDTH), row(A_QK_WIDTH),
                   row(A_WIDTH), row(G_WIDTH)),
        compiler_params=_cparams(("parallel", "parallel")),
        name="inproj",
    )(xa, mods_l, w_big, cosf, sinf, lng, lnb, ws, bs_full)


def _fft_a_kernel(fa_ref, fb_ref, c_ref, s_ref, tc_ref, ts_ref, zr_ref, zi_ref):
    fa = fa_ref[...]
    fb = fb_ref[...]
    cm = c_ref[...]
    sm = s_ref[...]
    zr = _hdot(cm, fa) - _hdot(sm, fb)
    zi = -(_hdot(cm, fb) + _hdot(sm, fa))
    tc = tc_ref[...]
    ts = ts_ref[...]
    zr_ref[...] = zr * tc + zi * ts
    zi_ref[...] = zi * tc - zr * ts


def _fft_a(fa, fb, cm, sm, twc, tws, bn):
    b, l, _ = fa.shape
    cols = bn * F_WIDTH
    fa3 = fa.reshape(b, l // bn, cols)
    fb3 = fb.reshape(b, l // bn, cols)
    tcw = min(cols, 2048)
    assert cols % tcw == 0
    blk = pl.BlockSpec((None, DFT_A, tcw), lambda bb, j: (bb, 0, j))
    tw = pl.BlockSpec((DFT_A, tcw), lambda bb, j: (0, j))
    mat = pl.BlockSpec((DFT_A, DFT_A), lambda bb, j: (0, 0))
    return pl.pallas_call(
        _fft_a_kernel,
        out_shape=(jax.ShapeDtypeStruct((b, DFT_A, cols), F32),) * 2,
        grid=(b, cols // tcw),
        in_specs=[blk, blk, mat, mat, tw, tw],
        out_specs=(blk, blk),
        compiler_params=_cparams(("parallel", "parallel")),
        name="fft_stage_a",
    )(fa3, fb3, cm, sm, twc, tws)


def _fft_b_kernel(zr_ref, zi_ref, c_ref, s_ref, o_ref):
    cb = c_ref[...]
    sb = s_ref[...]
    for j in range(zr_ref.shape[0]):
        y = _hdot(cb, zr_ref[j]) + _hdot(sb, zi_ref[j])
        o_ref[:, F_WIDTH * j:F_WIDTH * (j + 1)] = y.astype(o_ref.dtype)


def _fft_b(zr, zi, cb, sb, bn):
    b = zr.shape[0]
    kb = 8
    zr4 = zr.reshape(b, DFT_A, bn, F_WIDTH)
    zi4 = zi.reshape(b, DFT_A, bn, F_WIDTH)
    blk = pl.BlockSpec((None, kb, bn, F_WIDTH), lambda bb, i: (bb, i, 0, 0))
    mat = pl.BlockSpec((bn, bn), lambda bb, i: (0, 0))
    out = pl.pallas_call(
        _fft_b_kernel,
        out_shape=jax.ShapeDtypeStruct((b, bn, DFT_A * F_WIDTH), BF16),
        grid=(b, DFT_A // kb),
        in_specs=[blk, blk, mat, mat],
        out_specs=pl.BlockSpec((None, bn, kb * F_WIDTH), lambda bb, i: (bb, 0, i)),
        compiler_params=_cparams(("parallel", "parallel")),
        name="fft_stage_b",
    )(zr4, zi4, cb, sb)
    return out.reshape(b, bn * DFT_A, F_WIDTH)


def _dft_ctx_kernel(fa_ref, fb_ref, c_ref, s_ref, o_ref):
    y = _hdot(c_ref[...], fa_ref[...]) - _hdot(s_ref[...], fb_ref[...])
    o_ref[...] = y.astype(o_ref.dtype)


def _dft_ctx(fa, fb, cc, sc, n_lat, n_ctx):
    b = fa.shape[0]
    blk = pl.BlockSpec((None, n_ctx, F_WIDTH), lambda bb: (bb, n_lat // n_ctx, 0))
    mat = pl.BlockSpec((n_ctx, n_ctx), lambda bb: (0, 0))
    return pl.pallas_call(
        _dft_ctx_kernel,
        out_shape=jax.ShapeDtypeStruct((b, n_ctx, F_WIDTH), BF16),
        grid=(b,),
        in_specs=[blk, blk, mat, mat],
        out_specs=pl.BlockSpec((None, n_ctx, F_WIDTH), lambda bb: (bb, 0, 0)),
        compiler_params=_cparams(("parallel",)),
        name="dft_ctx",
    )(fa, fb, cc, sc)


def _score_slot(j):
    return 2 if j == 0 else (j - 1) % 2


def _attn_kernel(q_ref, kt_ref, v_ref, lam_ref, gain_ref, o_ref, q2_ref, s_ref, m_ref, acc_ref,
                 *, lam_init, tq):
    nsub = q_ref.shape[0] // tq
    nk, _, tk = kt_ref.shape
    lane = lax.broadcasted_iota(jnp.int32, (tq, A_DV), 1)
    comp0 = (lane % A_DH) < (A_DH // 2)
    lp = lam_ref[...]
    lam = (jnp.exp(jnp.sum(lp[0:1] * lp[1:2], keepdims=True))
           - jnp.exp(jnp.sum(lp[2:3] * lp[3:4], keepdims=True)) + lam_init)
    out_gain = gain_ref[...] * (1.0 - lam_init)

    def rows(i):
        return pl.ds(pl.multiple_of(i * tq, tq), tq)

    def load_q2(i):
        q = q_ref[rows(i), :]
        zero = jnp.zeros_like(q)
        q2_ref[0:tq, :] = jnp.where(comp0, q, zero)
        q2_ref[tq:2 * tq, :] = jnp.where(comp0, zero, q)

    def scores(j):
        s_ref[_score_slot(j)] = _bdot(q2_ref[...], kt_ref[j])

    def softmax_pv(j):
        s = s_ref[_score_slot(j)]
        m_old = m_ref[...]
        m_new = jnp.maximum(m_old, jnp.max(s, axis=1, keepdims=True))
        alpha = jnp.exp2(m_old - m_new)
        p = jnp.exp2(s - jnp.concatenate([m_new] * (tk // LANES), axis=1)).astype(BF16)
        pv = _bdot(p, v_ref[j])
        acc_ref[...] = acc_ref[...] * jnp.concatenate([alpha, alpha], axis=1) + pv
        m_ref[...] = m_new

    def finalize(i):
        acc = acc_ref[...]
        o0 = acc[0:tq, 0:LANES] / acc[0:tq, LANES:]
        o1 = acc[tq:, 0:LANES] / acc[tq:, LANES:]
        o = o0 - lam * o1
        ms = jnp.mean(o * o, axis=-1, keepdims=True)
        o_ref[rows(i), :] = (o * lax.rsqrt(ms + RMS_EPS) * out_gain).astype(o_ref.dtype)

    def next_tile_first_scores(i):
        load_q2(jnp.minimum(i + 1, nsub - 1))
        scores(0)

    load_q2(0)
    scores(0)

    def tile(i, carry):
        m_ref[...] = jnp.full(m_ref.shape, -jnp.inf, F32)
        acc_ref[...] = jnp.zeros(acc_ref.shape, F32)
        for j in range(nk):
            if j + 1 < nk:
                scores(j + 1)
                softmax_pv(j)
            elif nk > 1:
                next_tile_first_scores(i)
                softmax_pv(j)
            else:
                softmax_pv(j)
                next_tile_first_scores(i)
        finalize(i)
        return carry

    lax.fori_loop(0, nsub, tile, 0)


def _attention(q, kt, v_ext, lam_p, gain, *, q_block0, n_q, tq, group, lam_init):
    b = q.shape[0]
    _, _, nk, _, tk = kt.shape
    assert n_q % group == 0 and group % tq == 0
    return pl.pallas_call(
        functools.partial(_attn_kernel, lam_init=lam_init, tq=tq),
        out_shape=jax.ShapeDtypeStruct((b, n_q, A_WIDTH), BF16),
        grid=(b, A_HEADS, n_q // group),
        in_specs=[
            pl.BlockSpec((None, group, A_DV), lambda bb, h, i: (bb, q_block0 + i, h)),
            pl.BlockSpec((None, None, nk, A_DV, tk), lambda bb, h, i: (bb, h, 0, 0, 0)),
            pl.BlockSpec((None, None, nk, tk, 2 * LANES), lambda bb, h, i: (bb, h, 0, 0, 0)),
            pl.BlockSpec((4, A_DH), lambda bb, h, i: (0, 0)),
            pl.BlockSpec((1, A_DV), lambda bb, h, i: (0, 0)),
        ],
        out_specs=pl.BlockSpec((None, group, A_DV), lambda bb, h, i: (bb, i, h)),
        scratch_shapes=[
            pltpu.VMEM((2 * tq, A_DV), BF16),
            pltpu.VMEM((3, 2 * tq, tk), F32),
            pltpu.VMEM((2 * tq, LANES), F32),
            pltpu.VMEM((2 * tq, 2 * LANES), F32),
        ],
        compiler_params=_cparams(("parallel", "parallel", "arbitrary")),
        name="diff_attn",
    )(q, kt, v_ext, lam_p, gain)


def _key_value_chunks(k, v, tk):
    b, nkeys, _ = k.shape
    nk = nkeys // tk
    kt = k.reshape(b, nk, tk, A_HEADS, A_DV).transpose(0, 3, 1, 4, 2)
    vh = v.reshape(b, nk, tk, A_HEADS, A_DV).transpose(0, 3, 1, 2, 4)
    v_ext = jnp.concatenate([vh, jnp.ones_like(vh)], axis=-1)
    return kt, v_ext


_HI16 = 0xFFFF0000


def _pack_bf16_pair(lo, hi):
    ulo = lax.bitcast_convert_type(lo.astype(BF16).astype(F32), jnp.uint32) >> 16
    uhi = lax.bitcast_convert_type(hi.astype(BF16).astype(F32), jnp.uint32) & jnp.uint32(_HI16)
    return ulo | uhi


def _top_half_bits(x):
    return lax.bitcast_convert_type(lax.bitcast_convert_type(x, jnp.uint32) & jnp.uint32(_HI16), F32)


def _unpack_bf16_pair(u):
    lo = lax.bitcast_convert_type(u << 16, F32)
    hi = lax.bitcast_convert_type(u & jnp.uint32(_HI16), F32)
    return lo, hi


def _router_gates(sc_t, sel_t, gates_ref, chosen_ref):
    s_rows = [sc_t[e:e + 1, :] for e in range(N_EXPERTS)]
    v_rows = [sel_t[e:e + 1, :] for e in range(N_EXPERTS)]
    in_top2 = []
    group_score = []
    for g in range(N_EXPERT_GROUPS):
        vs = v_rows[g * EXPERTS_PER_GROUP:(g + 1) * EXPERTS_PER_GROUP]
        tops = []
        for jj in range(EXPERTS_PER_GROUP):
            rank = jnp.zeros_like(vs[jj])
            for ii in range(EXPERTS_PER_GROUP):
                if ii == jj:
                    continue
                beats = (vs[ii] >= vs[jj]) if ii < jj else (vs[ii] > vs[jj])
                rank = rank + jnp.where(beats, 1.0, 0.0)
            tops.append(rank < 2.0)
        in_top2 += tops
        gs = jnp.zeros_like(vs[0])
        for jj in range(EXPERTS_PER_GROUP):
            gs = gs + jnp.where(tops[jj], vs[jj], 0.0)
        group_score.append(gs)
    best = group_score[0]
    gidx = jnp.zeros_like(best)
    for g in range(1, N_EXPERT_GROUPS):
        upd = group_score[g] > best
        best = jnp.where(upd, group_score[g], best)
        gidx = jnp.where(upd, float(g), gidx)
    chosen = [jnp.logical_and(in_top2[e], gidx == float(e // EXPERTS_PER_GROUP))
              for e in range(N_EXPERTS)]
    denom = jnp.zeros_like(best)
    for e in range(N_EXPERTS):
        denom = denom + jnp.where(chosen[e], s_rows[e], 0.0)
    for e in range(N_EXPERTS):
        gates_ref[e:e + 1, :] = jnp.where(chosen[e], s_rows[e] / denom, 0.0)
        chosen_ref[e:e + 1, :] = jnp.where(chosen[e], 1.0, 0.0)


def _outproj_kernel(yf_ref, att_ref, sg_ref, x_ref, mod_ref, wo_ref, lng_ref, lnb_ref,
                    wr_ref, rb_ref, x1_ref, h_ref, gates_ref, chosen_ref, *, alpha):
    mix = _bdot(yf_ref[...], wo_ref[0:F_WIDTH, :])
    mix = mix + _bdot(att_ref[...], wo_ref[F_WIDTH:F_WIDTH + A_WIDTH, :])
    mix = mix + _bdot(sg_ref[...], wo_ref[F_WIDTH + A_WIDTH:, :])
    g1 = mod_ref[2:3, :]
    x1 = _layernorm(alpha * x_ref[...] + g1 * mix, lng_ref[...], lnb_ref[...])
    x1_ref[...] = x1
    h = x1 * (1.0 + mod_ref[4:5, :]) + mod_ref[3:4, :]
    half = h.shape[1] // 2
    h_ref[...] = _pack_bf16_pair(h[:, :half], h[:, half:])
    h_top = _top_half_bits(h)
    h_hi = h_top.astype(BF16)
    h_lo = (h - h_top).astype(BF16)
    hw = _bdot(h_hi, wr_ref[...])
    logits = hw[:, :ROUTER_LANES] + (hw[:, ROUTER_LANES:] + _bdot(h_lo, wr_ref[:, :ROUTER_LANES]))
    scores = jax.nn.sigmoid(logits)
    sel = scores + rb_ref[...]
    _router_gates(scores.T[0:N_EXPERTS, :], sel.T[0:N_EXPERTS, :], gates_ref, chosen_ref)


def _outproj(yf, att, sg, xa, mods_l, wo, lng, lnb, wr, rb, n_lat, alpha):
    b, l, d = xa.shape
    tm = TOKEN_TILE
    nlat = n_lat // tm
    seg = lambda i: jnp.where(i >= nlat, 1, 0)
    row = lambda width: pl.BlockSpec((None, tm, width), lambda bb, i: (bb, i, 0))
    const2 = lambda shape: pl.BlockSpec(shape, lambda bb, i: (0, 0))
    return pl.pallas_call(
        functools.partial(_outproj_kernel, alpha=alpha),
        out_shape=(
            jax.ShapeDtypeStruct((b, l, d), F32),
            jax.ShapeDtypeStruct((b, l, d // 2), jnp.uint32),
            jax.ShapeDtypeStruct((b, N_EXPERTS, l), F32),
            jax.ShapeDtypeStruct((b, N_EXPERTS, l), F32),
        ),
        grid=(b, l // tm),
        in_specs=[
            row(F_WIDTH), row(A_WIDTH), row(G_WIDTH), row(d),
            pl.BlockSpec((None, None, MOD_ROWS, d), lambda bb, i: (bb, seg(i), 0, 0)),
            const2((d, d)), const2((1, d)), const2((1, d)),
            const2((d, 2 * ROUTER_LANES)), const2((1, ROUTER_LANES)),
        ],
        out_specs=(row(d), row(d // 2),
                   pl.BlockSpec((None, N_EXPERTS, tm), lambda bb, i: (bb, 0, i)),
                   pl.BlockSpec((None, N_EXPERTS, tm), lambda bb, i: (bb, 0, i))),
        compiler_params=_cparams(("parallel", "parallel")),
        name="outproj_ln_router",
    )(yf, att, sg, xa, mods_l, wo, lng, lnb, wr, rb)


_PAIRS = [(a, c) for a in range(EXPERTS_PER_GROUP) for c in range(a + 1, EXPERTS_PER_GROUP)]
N_PAIR_CLASSES = N_EXPERT_GROUPS * len(_PAIRS)


def _moe_plan(gates_t, chosen_t, tm):
    b, _, l = gates_t.shape
    t = b * l
    npair = len(_PAIRS)
    cls_e1 = np.zeros((N_PAIR_CLASSES,), np.int32)
    cls_e2 = np.zeros((N_PAIR_CLASSES,), np.int32)
    for k, (a, c) in enumerate(_PAIRS):
        assert k == a * (2 * EXPERTS_PER_GROUP - 1 - a) // 2 + (c - a - 1)
        for g in range(N_EXPERT_GROUPS):
            cls_e1[g * npair + k] = g * EXPERTS_PER_GROUP + a
            cls_e2[g * npair + k] = g * EXPERTS_PER_GROUP + c
    eidx = jnp.arange(N_EXPERTS, dtype=jnp.int32)[None, :, None]
    picked = chosen_t > 0.5
    e_lo = jnp.min(jnp.where(picked, eidx, N_EXPERTS), axis=1)
    e_hi = jnp.max(jnp.where(picked, eidx, -1), axis=1)
    g_lo = jnp.sum(jnp.where(eidx == e_lo[:, None, :], gates_t, 0.0), axis=1)
    g_hi = jnp.sum(jnp.where(eidx == e_hi[:, None, :], gates_t, 0.0), axis=1)
    pair_gates = jnp.concatenate([jnp.broadcast_to(g_lo[:, :, None], (b, l, LANES)),
                                  jnp.broadcast_to(g_hi[:, :, None], (b, l, LANES))], axis=-1)
    a_lo = e_lo % EXPERTS_PER_GROUP
    a_hi = e_hi % EXPERTS_PER_GROUP
    pair_rank = a_lo * (2 * EXPERTS_PER_GROUP - 1 - a_lo) // 2 + (a_hi - a_lo - 1)
    cls = ((e_lo // EXPERTS_PER_GROUP) * npair + pair_rank).reshape(t)
    onehot = (cls[:, None] == jnp.arange(N_PAIR_CLASSES, dtype=jnp.int32)[None, :]).astype(jnp.int32)
    csum = jnp.cumsum(onehot, axis=0)
    rank = jnp.sum(csum * onehot, axis=1) - 1
    padded = (csum[-1] + tm - 1) // tm * tm
    off_end = jnp.cumsum(padded)
    dest = jnp.sum(onehot * (off_end - padded)[None, :], axis=1) + rank
    nt = t // tm + N_PAIR_CLASSES
    n_valid = (off_end[-1] // tm).astype(jnp.int32)
    first_row = jnp.minimum(jnp.arange(nt, dtype=jnp.int32), n_valid - 1) * tm
    tile_cls = jnp.minimum(jnp.searchsorted(off_end, first_row, side="right"), N_PAIR_CLASSES - 1)
    return (dest.astype(jnp.int32), pair_gates, jnp.asarray(cls_e1)[tile_cls],
            jnp.asarray(cls_e2)[tile_cls], n_valid.reshape(1), nt)


def _dispatch_kernel(dest_ref, h_ref, hs_init_ref, hs_ref, stage, sems):
    del hs_init_ref
    s = pl.program_id(0)
    n = pl.num_programs(0)
    tm = h_ref.shape[0]
    slot = s % 2

    def wait_slot(k):
        pltpu.make_async_copy(stage.at[k], hs_ref.at[pl.ds(0, tm), :], sems.at[k]).wait()

    @pl.when(s >= 2)
    def _reuse():
        wait_slot(slot)

    stage[slot] = h_ref[...]
    for r in range(tm):
        pltpu.make_async_copy(stage.at[slot, pl.ds(r, 1), :],
                              hs_ref.at[pl.ds(dest_ref[0, r], 1), :], sems.at[slot]).start()

    @pl.when(s == n - 1)
    def _drain():
        @pl.when(n >= 2)
        def _other():
            wait_slot(1 - slot)
        wait_slot(slot)


def _dispatch(h_packed, dest, nt, tm):
    t, w = h_packed.shape
    hs_init = jnp.zeros((nt * tm, w), jnp.uint32)
    return pl.pallas_call(
        _dispatch_kernel,
        out_shape=jax.ShapeDtypeStruct((nt * tm, w), jnp.uint32),
        grid=(t // tm,),
        in_specs=[
            pl.BlockSpec((None, 1, tm), lambda i: (i, 0, 0), memory_space=pltpu.SMEM),
            pl.BlockSpec((tm, w), lambda i: (i, 0)),
            pl.BlockSpec(memory_space=pl.ANY),
        ],
        out_specs=pl.BlockSpec(memory_space=pl.ANY),
        scratch_shapes=[pltpu.VMEM((2, tm, w), jnp.uint32), pltpu.SemaphoreType.DMA((2,))],
        input_output_aliases={2: 0},
        compiler_params=_cparams(("arbitrary",)),
        name="moe_dispatch",
    )(dest.reshape(t // tm, 1, tm), h_packed, hs_init)


def _expert_pair_kernel(e1_ref, e2_ref, nv_ref, hs_ref, w1a_ref, w3a_ref, w2a_ref,
                        w1b_ref, w3b_ref, w2b_ref, y_ref):
    del e1_ref, e2_ref
    i = pl.program_id(0)

    @pl.when(i < nv_ref[0])
    def _compute():
        lo, hi = _unpack_bf16_pair(hs_ref[...])
        h = jnp.concatenate([lo, hi], axis=1).astype(BF16)

        def ffn(w1_ref, w3_ref, w2_ref):
            a = _bdot(h, w1_ref[...])
            a = (a * jax.nn.sigmoid(a)) * _bdot(h, w3_ref[...])
            return _bdot(a.astype(BF16), w2_ref[...])

        y_ref[...] = _pack_bf16_pair(ffn(w1a_ref, w3a_ref, w2a_ref), ffn(w1b_ref, w3b_ref, w2b_ref))

    @pl.when(i >= nv_ref[0])
    def _unused_tile():
        y_ref[...] = jnp.zeros(y_ref.shape, jnp.uint32)


def _expert_pairs(hs, tile_e1, tile_e2, n_valid, w1, w3, w2, tm):
    rows, w = hs.shape
    _, d, de = w1.shape
    first = lambda shape: pl.BlockSpec(shape, lambda i, e1, e2, nv: (e1[i], 0, 0))
    second = lambda shape: pl.BlockSpec(shape, lambda i, e1, e2, nv: (e2[i], 0, 0))
    grid_spec = pltpu.PrefetchScalarGridSpec(
        num_scalar_prefetch=3,
        grid=(rows // tm,),
        in_specs=[
            pl.BlockSpec((tm, w), lambda i, e1, e2, nv: (jnp.minimum(i, nv[0] - 1), 0)),
            first((None, d, de)), first((None, d, de)), first((None, de, d)),
            second((None, d, de)), second((None, d, de)), second((None, de, d)),
        ],
        out_specs=pl.BlockSpec((tm, d), lambda i, e1, e2, nv: (i, 0)),
    )
    return pl.pallas_call(
        _expert_pair_kernel,
        out_shape=jax.ShapeDtypeStruct((rows, d), jnp.uint32),
        grid_spec=grid_spec,
        compiler_params=_cparams(("arbitrary",)),
        name="moe_expert_pairs",
    )(tile_e1, tile_e2, n_valid, hs, w1, w3, w2, w1, w3, w2)


def _combine_kernel(dest_ref, dest_next_ref, y2_ref, x_ref, pg_ref, mod_ref, lng_ref, lnb_ref,
                    o_ref, ybuf, sems, *, alpha):
    s = pl.program_id(0)
    n = pl.num_programs(0)
    tm, d = x_ref.shape
    slot = s % 2

    def start_gather(idx_ref, k):
        for r in range(tm):
            pltpu.make_async_copy(y2_ref.at[pl.ds(idx_ref[0, r], 1), :],
                                  ybuf.at[k, pl.ds(r, 1), :], sems.at[k]).start()

    @pl.when(s == 0)
    def _first():
        start_gather(dest_ref, 0)

    @pl.when(s + 1 < n)
    def _prefetch():
        start_gather(dest_next_ref, 1 - slot)

    pltpu.make_async_copy(y2_ref.at[pl.ds(0, tm), :], ybuf.at[slot], sems.at[slot]).wait()
    y_lo, y_hi = _unpack_bf16_pair(ybuf[slot])
    pg = pg_ref[...]
    reps = d // LANES
    g_lo = jnp.concatenate([pg[:, :LANES]] * reps, axis=1)
    g_hi = jnp.concatenate([pg[:, LANES:]] * reps, axis=1)
    y = g_lo * y_lo + g_hi * y_hi
    o_ref[...] = _layernorm(alpha * x_ref[...] + mod_ref[5:6, :] * y, lng_ref[...], lnb_ref[...])


def _combine(y2, dest, x1, pair_gates, mods_l, lng, lnb, *, n_rows, n_lat, tm, alpha):
    b, l, d = x1.shape
    nlat = n_lat // tm
    tiles_per_batch = l // tm
    per_batch = n_rows // tm
    n_steps = b * per_batch
    batch = lambda s: s // per_batch
    tile = lambda s: s % per_batch
    seg = lambda s: jnp.where(tile(s) >= nlat, 1, 0)
    token_tile = lambda s: batch(s) * tiles_per_batch + tile(s)
    row = lambda width: pl.BlockSpec((None, tm, width), lambda s: (batch(s), tile(s), 0))
    const2 = lambda shape: pl.BlockSpec(shape, lambda s: (0, 0))
    dest3 = dest.reshape(b * tiles_per_batch, 1, tm)
    return pl.pallas_call(
        functools.partial(_combine_kernel, alpha=alpha),
        out_shape=jax.ShapeDtypeStruct((b, n_rows, d), F32),
        grid=(n_steps,),
        in_specs=[
            pl.BlockSpec((None, 1, tm), lambda s: (token_tile(s), 0, 0), memory_space=pltpu.SMEM),
            pl.BlockSpec((None, 1, tm), lambda s: (token_tile(jnp.minimum(s + 1, n_steps - 1)), 0, 0),
                         memory_space=pltpu.SMEM),
            pl.BlockSpec(memory_space=pl.ANY),
            row(d), row(2 * LANES),
            pl.BlockSpec((None, None, MOD_ROWS, d), lambda s: (batch(s), seg(s), 0, 0)),
            const2((1, d)), const2((1, d)),
        ],
        out_specs=row(d),
        scratch_shapes=[pltpu.VMEM((2, tm, d), jnp.uint32), pltpu.SemaphoreType.DMA((2,))],
        compiler_params=_cparams(("arbitrary",)),
        name="moe_combine_ln",
    )(dest3, dest3, y2, x1, pair_gates, mods_l, lng, lnb)


def _head_lane_fields():
    j = jnp.arange(LANES)
    half = j // (LANES // 2)
    comp = (j % (LANES // 2)) // (A_DH // 2)
    axis = (j % (A_DH // 2)) // (A_DH // 4)
    freq = j % (A_DH // 4)
    return half, comp, axis, freq


def _qk_column_perm():
    half, comp, axis, freq = _head_lane_fields()
    orig = comp * A_DH + axis * (A_DH // 2) + half * (A_DH // 4) + freq
    return (jnp.arange(A_HEADS)[:, None] * LANES + orig[None, :]).reshape(-1)


def _rope_tables(n_lat, n_ctx):
    half, _, axis, freq = _head_lane_fields()
    rows = n_lat // GRID_W
    row = jnp.repeat(jnp.arange(rows, dtype=F32), GRID_W)
    col = jnp.tile(jnp.arange(GRID_W, dtype=F32), rows)
    hd = A_DH // 2
    inv = ROPE_BASE ** (-jnp.arange(0, hd, 2, dtype=F32) / hd)
    pos = jnp.where(axis[None, :] == 0, row[:, None], col[:, None])
    ang = pos * inv[freq][None, :]
    cosf = jnp.cos(ang)
    sinf = jnp.sin(ang) * jnp.where(half == 0, -1.0, 1.0)[None, :]
    cosf = jnp.concatenate([cosf, jnp.ones((n_ctx, LANES), F32)], axis=0)
    sinf = jnp.concatenate([sinf, jnp.zeros((n_ctx, LANES), F32)], axis=0)
    return cosf, sinf


def _dft_mats(n, scale=1.0):
    i = jnp.arange(n, dtype=jnp.int32)
    ang = ((i[:, None] * i[None, :]) % n).astype(F32) * (2.0 * math.pi / n)
    return jnp.cos(ang) * scale, jnp.sin(ang) * scale


def _twiddles(n, bn):
    ka = jnp.arange(DFT_A, dtype=jnp.int32)
    bb = jnp.arange(bn, dtype=jnp.int32)
    ang = ((ka[:, None] * bb[None, :]) % n).astype(F32) * (2.0 * math.pi / n)
    shape = (DFT_A, bn, F_WIDTH)
    twc = jnp.broadcast_to(jnp.cos(ang)[:, :, None], shape).reshape(DFT_A, bn * F_WIDTH)
    tws = jnp.broadcast_to(jnp.sin(ang)[:, :, None], shape).reshape(DFT_A, bn * F_WIDTH)
    return twc, tws


def _block_diag(blocks):
    g, n = blocks.shape[-3], blocks.shape[-1]
    eye = jnp.eye(g, dtype=blocks.dtype)
    out = blocks[..., :, :, None, :] * eye[:, None, :, None]
    return out.reshape(blocks.shape[:-3] + (g * n, g * n))


def kernel(x, c, ctx, c_ctx, w_ada, b_ada, w_in, w_fourier, diff_lambda, diff_subln,
           sgu_ln_g, sgu_ln_b, sgu_w, sgu_b, w_out, ln_g, ln_b, w_router, router_bias,
           moe_w1, moe_w3, moe_w2):
    b, n, d = x.shape
    n_ctx = ctx.shape[1]
    depth = w_ada.shape[0]
    l = n + n_ctx
    alpha = (2 * depth) ** 0.25
    bn = n // DFT_A
    assert n % DFT_A == 0 and bn % SUBLANES == 0 and n % GRID_W == 0
    assert n_ctx % TOKEN_TILE == 0 and n % TOKEN_TILE == 0 and n % n_ctx == 0
    assert b + 1 <= MOD_ROWS

    c_rows = jnp.concatenate([c, c_ctx[None, :], jnp.zeros((MOD_ROWS - b - 1, d), F32)], axis=0)
    mod = _ada(c_rows, w_ada, b_ada).reshape(depth, MOD_ROWS, 6, d)
    lat = mod[:, :b]
    cm = jnp.broadcast_to(mod[:, b:b + 1], lat.shape)
    mods = jnp.stack([lat, cm], axis=2)
    mods = jnp.pad(mods, ((0, 0), (0, 0), (0, 0), (0, MOD_ROWS - 6), (0, 0)))

    perm = _qk_column_perm()
    c64, s64 = _dft_mats(F_DIM)
    eye_g = jnp.eye(F_GROUPS, dtype=F32)
    bdc = jnp.kron(eye_g, c64)
    bds = jnp.kron(eye_g, s64)
    bdw = _block_diag(w_fourier)
    w_fab = _fourier_weights(w_in[:, :, :F_WIDTH], bdc, bds, bdw)
    o_q = F_WIDTH
    o_k = o_q + A_QK_WIDTH
    o_v = o_k + A_QK_WIDTH
    o_g = o_v + A_WIDTH
    w_big = jnp.concatenate([
        w_fab,
        w_in[:, :, o_q:o_k][:, :, perm],
        w_in[:, :, o_k:o_v][:, :, perm],
        w_in[:, :, o_v:],
    ], axis=-1).astype(BF16)
    wo = w_out.astype(BF16)
    w1 = moe_w1.astype(BF16)
    w3 = moe_w3.astype(BF16)
    w2 = moe_w2.astype(BF16)
    ws = sgu_w.astype(BF16)
    bs_full = jnp.repeat(jnp.swapaxes(sgu_b, 1, 2), G_DIM, axis=2)
    wr_f32 = jnp.pad(w_router, ((0, 0), (0, ROUTER_LANES - N_EXPERTS)))
    wr_top = _top_half_bits(wr_f32)
    wr = jnp.concatenate([wr_top.astype(BF16), (wr_f32 - wr_top).astype(BF16)], axis=1)
    rb =jnp.pad(router_bias, (0, ROUTER_LANES - N_EXPERTS)).reshape(1, ROUTER_LANES)

    cosf, sinf = _rope_tables(n, n_ctx)
    ca, sa = _dft_mats(DFT_A)
    twc, tws = _twiddles(n, bn)
    cb, sb = _dft_mats(bn, scale=(n * F_DIM) ** -0.5)
    cc, sc = _dft_mats(n_ctx, scale=(n_ctx * F_DIM) ** -0.5)
    q_scale = (A_DH ** -0.5) * math.log2(math.e)

    tk = ATTN_TK
    assert l % tk == 0
    assert n % MOE_TILE == 0 and n_ctx % MOE_TILE == 0

    xa = jnp.concatenate([x, ctx], axis=1)
    for li in range(depth):
        last = li == depth - 1
        lam_init = 0.8 - 0.6 * math.exp(-0.3 * li)
        fa, fb, q, k, v, sg = _inproj(
            xa, mods[li], w_big[li], cosf, sinf, sgu_ln_g[li][None], sgu_ln_b[li][None],
            ws[li], bs_full[li], n, q_scale)
        zr, zi = _fft_a(fa, fb, ca, sa, twc, tws, bn)
        yf = jnp.concatenate([_fft_b(zr, zi, cb, sb, bn), _dft_ctx(fa, fb, cc, sc, n, n_ctx)], axis=1)
        gain = diff_subln[li][None]
        kt, v_ext = _key_value_chunks(k, v, tk)
        att_x = _attention(q, kt, v_ext, diff_lambda[li], gain, q_block0=0, n_q=n, tq=ATTN_TQ,
                           group=min(ATTN_GROUP, n), lam_init=lam_init)
        kt_c, v_ext_c = _key_value_chunks(k[:, n:], v[:, n:], n_ctx)
        att_c = _attention(q, kt_c, v_ext_c, diff_lambda[li], gain, q_block0=n // n_ctx, n_q=n_ctx,
                           tq=n_ctx, group=n_ctx, lam_init=lam_init)
        att = jnp.concatenate([att_x, att_c], axis=1)
        x1, h_packed, gates_t, chosen_t = _outproj(
            yf, att, sg, xa, mods[li], wo[li], ln_g[li, 0][None], ln_b[li, 0][None], wr, rb, n, alpha)
        dest, pair_gates, tile_e1, tile_e2, n_valid, nt = _moe_plan(gates_t, chosen_t, MOE_TILE)
        hs = _dispatch(h_packed.reshape(b * l, d // 2), dest, nt, MOE_TILE)
        y2 = _expert_pairs(hs, tile_e1, tile_e2, n_valid, w1[li], w3[li], w2[li], MOE_TILE)
        xa = _combine(y2, dest, x1, pair_gates, mods[li], ln_g[li, 1][None], ln_b[li, 1][None],
                      n_rows=n if last else l, n_lat=n, tm=MOE_TILE, alpha=alpha)
    return xa
```

```python
import functools
import math

import numpy as np
import jax
import jax.numpy as jnp
from jax import lax
from jax.experimental import pallas as pl
from jax.experimental.pallas import tpu as pltpu

F32 = jnp.float32
BF16 = jnp.bfloat16
HIGHEST = lax.Precision.HIGHEST

GRID_W = 64
F_GROUPS, F_DIM = 4, 64
F_WIDTH = F_GROUPS * F_DIM
A_HEADS, A_DH = 4, 64
A_DV = 2 * A_DH
A_QK_WIDTH = A_HEADS * 2 * A_DH
A_WIDTH = A_HEADS * A_DV
G_GROUPS, G_DIM = 4, 64
G_WIDTH = G_GROUPS * G_DIM
CHUNK = 128
ROPE_BASE = 10000.0
N_EXPERTS = 16
N_EXPERT_GROUPS = 4
EXPERTS_PER_GROUP = N_EXPERTS // N_EXPERT_GROUPS
LN_EPS = 1e-5
RMS_EPS = 1e-5

COL_FA = 0
COL_FB = COL_FA + F_WIDTH
COL_Q = COL_FB + F_WIDTH
COL_K = COL_Q + A_QK_WIDTH
COL_V = COL_K + A_QK_WIDTH
COL_G = COL_V + A_WIDTH
BIG_COLS = COL_G + 2 * G_WIDTH

LANES = 128
SUBLANES = 8
MOD_ROWS = 8
ROUTER_LANES = 128
DFT_A = 128
VMEM_LIMIT = 56 * 1024 * 1024

TOKEN_TILE = 256
ATTN_TQ = 256
ATTN_GROUP = 2048
ATTN_TK = 1280
MOE_TILE = 256


def _cparams(sem):
    return pltpu.CompilerParams(dimension_semantics=sem, vmem_limit_bytes=VMEM_LIMIT)


def _hdot(a, b):
    return jnp.dot(a, b, precision=HIGHEST, preferred_element_type=F32)


def _bdot(a, b):
    return jnp.dot(a, b, preferred_element_type=F32)


def _layernorm(y, g, b):
    mu = jnp.mean(y, axis=-1, keepdims=True)
    d = y - mu
    var = jnp.mean(d * d, axis=-1, keepdims=True)
    return d * lax.rsqrt(var + LN_EPS) * g + b


def _ada_kernel(c_ref, w_ref, b_ref, o_ref):
    c = c_ref[...]
    a = c * jax.nn.sigmoid(c)
    o_ref[...] = _hdot(a, w_ref[...]) + b_ref[...]


def _ada(c_rows, w_ada, b_ada):
    depth, d, cols = w_ada.shape
    tn = 1536
    assert cols % tn == 0
    return pl.pallas_call(
        _ada_kernel,
        out_shape=jax.ShapeDtypeStruct((depth, MOD_ROWS, cols), F32),
        grid=(depth, cols // tn),
        in_specs=[
            pl.BlockSpec((MOD_ROWS, d), lambda l, j: (0, 0)),
            pl.BlockSpec((None, d, tn), lambda l, j: (l, 0, j)),
            pl.BlockSpec((None, 1, tn), lambda l, j: (l, 0, j)),
        ],
        out_specs=pl.BlockSpec((None, MOD_ROWS, tn), lambda l, j: (l, 0, j)),
        compiler_params=_cparams(("parallel", "parallel")),
        name="ada_mod",
    )(c_rows, w_ada, b_ada.reshape(depth, 1, cols))


def _fw_kernel(wf_ref, bdc_ref, bds_ref, bdw_ref, o_ref):
    bdw = bdw_ref[...]
    mc = _hdot(bdc_ref[...], bdw)
    ms = _hdot(bds_ref[...], bdw)
    wf = wf_ref[...]
    o_ref[:, :F_WIDTH] = _hdot(wf, mc)
    o_ref[:, F_WIDTH:] = _hdot(wf, ms)


def _fourier_weights(wf, bdc, bds, bdw):
    depth, d, _ = wf.shape
    return pl.pallas_call(
        _fw_kernel,
        out_shape=jax.ShapeDtypeStruct((depth, d, 2 * F_WIDTH), F32),
        grid=(depth,),
        in_specs=[
            pl.BlockSpec((None, d, F_WIDTH), lambda l: (l, 0, 0)),
            pl.BlockSpec((F_WIDTH, F_WIDTH), lambda l: (0, 0)),
            pl.BlockSpec((F_WIDTH, F_WIDTH), lambda l: (0, 0)),
            pl.BlockSpec((None, F_WIDTH, F_WIDTH), lambda l: (l, 0, 0)),
        ],
        out_specs=pl.BlockSpec((None, d, 2 * F_WIDTH), lambda l: (l, 0, 0)),
        compiler_params=_cparams(("parallel",)),
        name="fourier_weights",
    )(wf, bdc, bds, bdw)


def _gelu_tanh(x):
    c = math.sqrt(2.0 / math.pi)
    return x * (0.5 * (1.0 + jnp.tanh(c * (x + 0.044715 * (x * x * x)))))


def _inproj_kernel(x_ref, mod_ref, w_ref, cos_ref, sin_ref, lng_ref, lnb_ref, ws_ref, bs_ref,
                   fa_ref, fb_ref, q_ref, kt_ref, v_ref, sg_ref, *, q_scale):
    x = x_ref[...]
    sh = mod_ref[0:1, :]
    sc = mod_ref[1:2, :]
    h = (x * (1.0 + sc) + sh).astype(BF16)
    p = _bdot(h, w_ref[...])
    fa_ref[...] = p[:, COL_FA:COL_FA + F_WIDTH]
    fb_ref[...] = p[:, COL_FB:COL_FB + F_WIDTH]
    cosf = cos_ref[...]
    sinf = sin_ref[...]
    for hh in range(A_HEADS):
        t = p[:, COL_Q + LANES * hh:COL_Q + LANES * (hh + 1)]
        r = t * cosf + pltpu.roll(t, LANES // 2, 1) * sinf
        q_ref[:, LANES * hh:LANES * (hh + 1)] = (r * q_scale).astype(BF16)
        t = p[:, COL_K + LANES * hh:COL_K + LANES * (hh + 1)]
        r = t * cosf + pltpu.roll(t, LANES // 2, 1) * sinf
        kt_ref[hh] = r.T.astype(BF16)
        v_ref[:, 2 * LANES * hh:2 * LANES * hh + LANES] = (
            p[:, COL_V + LANES * hh:COL_V + LANES * (hh + 1)].astype(BF16))
        v_ref[:, 2 * LANES * hh + LANES:2 * LANES * (hh + 1)] = jnp.ones((t.shape[0], LANES), BF16)

    g = _gelu_tanh(p[:, COL_G:COL_G + 2 * G_WIDTH])
    u = g[:, :G_WIDTH]
    vn = _layernorm(g[:, G_WIDTH:], lng_ref[...], lnb_ref[...])
    lane_group = lax.broadcasted_iota(jnp.int32, (CHUNK, G_WIDTH), 1) // G_DIM
    tm = x.shape[0]
    for c in range(tm // CHUNK):
        rows = slice(c * CHUNK, (c + 1) * CHUNK)
        vc = vn[rows]
        mixed = bs_ref[...]
        for gi in range(G_GROUPS):
            vm = jnp.where(lane_group == gi, vc, 0.0).astype(BF16)
            mixed = mixed + _bdot(ws_ref[gi], vm)
        sg_ref[rows, :] = (u[rows] * mixed).astype(BF16)


def _inproj(xa, mods_l, w_big, cosf, sinf, lng, lnb, ws, bs_full, n_lat, q_scale):
    b, l, d = xa.shape
    tm = TOKEN_TILE
    nlat = n_lat // tm
    seg = lambda i: jnp.where(i >= nlat, 1, 0)
    row = lambda width: pl.BlockSpec((None, tm, width), lambda bb, i: (bb, i, 0))
    const2 = lambda shape: pl.BlockSpec(shape, lambda bb, i: (0, 0))
    out_shapes = (
        jax.ShapeDtypeStruct((b, l, F_WIDTH), F32),
        jax.ShapeDtypeStruct((b, l, F_WIDTH), F32),
        jax.ShapeDtypeStruct((b, l, A_QK_WIDTH), BF16),
        jax.ShapeDtypeStruct((b, A_HEADS, A_DV, l), BF16),
        jax.ShapeDtypeStruct((b, l, 2 * A_WIDTH), BF16),
        jax.ShapeDtypeStruct((b, l, G_WIDTH), BF16),
    )
    return pl.pallas_call(
        functools.partial(_inproj_kernel, q_scale=q_scale),
        out_shape=out_shapes,
        grid=(b, l // tm),
        in_specs=[
            row(d),
            pl.BlockSpec((None, None, MOD_ROWS, d), lambda bb, i: (bb, seg(i), 0, 0)),
            const2((d, BIG_COLS)),
            pl.BlockSpec((tm, LANES), lambda bb, i: (i, 0)),
            pl.BlockSpec((tm, LANES), lambda bb, i: (i, 0)),
            const2((1, G_WIDTH)),
            const2((1, G_WIDTH)),
            pl.BlockSpec((G_GROUPS, CHUNK, CHUNK), lambda bb, i: (0, 0, 0)),
            const2((CHUNK, G_WIDTH)),
        ],
        out_specs=(row(F_WIDTH), row(F_WIDTH), row(A_QK_WIDTH),
                   pl.BlockSpec((None, A_HEADS, A_DV, tm), lambda bb, i: (bb, 0, 0, i)),
                   row(2 * A_WIDTH), row(G_WIDTH)),
        compiler_params=_cparams(("parallel", "parallel")),
        name="inproj",
    )(xa, mods_l, w_big, cosf, sinf, lng, lnb, ws, bs_full)


def _fft_a_kernel(fa_ref, fb_ref, c_ref, s_ref, tc_ref, ts_ref, zr_ref, zi_ref):
    fa = fa_ref[...]
    fb = fb_ref[...]
    cm = c_ref[...]
    sm = s_ref[...]
    zr = _hdot(cm, fa) - _hdot(sm, fb)
    zi = -(_hdot(cm, fb) + _hdot(sm, fa))
    tc = tc_ref[...]
    ts = ts_ref[...]
    zr_ref[...] = zr * tc + zi * ts
    zi_ref[...] = zi * tc - zr * ts


def _fft_a(fa, fb, cm, sm, twc, tws, bn):
    b, l, _ = fa.shape
    cols = bn * F_WIDTH
    fa3 = fa.reshape(b, l // bn, cols)
    fb3 = fb.reshape(b, l // bn, cols)
    tcw = min(cols, 2048)
    assert cols % tcw == 0
    blk = pl.BlockSpec((None, DFT_A, tcw), lambda bb, j: (bb, 0, j))
    tw = pl.BlockSpec((DFT_A, tcw), lambda bb, j: (0, j))
    mat = pl.BlockSpec((DFT_A, DFT_A), lambda bb, j: (0, 0))
    return pl.pallas_call(
        _fft_a_kernel,
        out_shape=(jax.ShapeDtypeStruct((b, DFT_A, cols), F32),) * 2,
        grid=(b, cols // tcw),
        in_specs=[blk, blk, mat, mat, tw, tw],
        out_specs=(blk, blk),
        compiler_params=_cparams(("parallel", "parallel")),
        name="fft_stage_a",
    )(fa3, fb3, cm, sm, twc, tws)


def _fft_b_kernel(zr_ref, zi_ref, c_ref, s_ref, o_ref):
    cb = c_ref[...]
    sb = s_ref[...]
    for j in range(zr_ref.shape[0]):
        y = _hdot(cb, zr_ref[j]) + _hdot(sb, zi_ref[j])
        o_ref[:, F_WIDTH * j:F_WIDTH * (j + 1)] = y.astype(o_ref.dtype)


def _fft_b(zr, zi, cb, sb, bn):
    b = zr.shape[0]
    kb = 8
    zr4 = zr.reshape(b, DFT_A, bn, F_WIDTH)
    zi4 = zi.reshape(b, DFT_A, bn, F_WIDTH)
    blk = pl.BlockSpec((None, kb, bn, F_WIDTH), lambda bb, i: (bb, i, 0, 0))
    mat = pl.BlockSpec((bn, bn), lambda bb, i: (0, 0))
    out = pl.pallas_call(
        _fft_b_kernel,
        out_shape=jax.ShapeDtypeStruct((b, bn, DFT_A * F_WIDTH), BF16),
        grid=(b, DFT_A // kb),
        in_specs=[blk, blk, mat, mat],
        out_specs=pl.BlockSpec((None, bn, kb * F_WIDTH), lambda bb, i: (bb, 0, i)),
        compiler_params=_cparams(("parallel", "parallel")),
        name="fft_stage_b",
    )(zr4, zi4, cb, sb)
    return out.reshape(b, bn * DFT_A, F_WIDTH)


def _dft_ctx_kernel(fa_ref, fb_ref, c_ref, s_ref, o_ref):
    y = _hdot(c_ref[...], fa_ref[...]) - _hdot(s_ref[...], fb_ref[...])
    o_ref[...] = y.astype(o_ref.dtype)


def _dft_ctx(fa, fb, cc, sc, n_lat, n_ctx):
    b = fa.shape[0]
    blk = pl.BlockSpec((None, n_ctx, F_WIDTH), lambda bb: (bb, n_lat // n_ctx, 0))
    mat = pl.BlockSpec((n_ctx, n_ctx), lambda bb: (0, 0))
    return pl.pallas_call(
        _dft_ctx_kernel,
        out_shape=jax.ShapeDtypeStruct((b, n_ctx, F_WIDTH), BF16),
        grid=(b,),
        in_specs=[blk, blk, mat, mat],
        out_specs=pl.BlockSpec((None, n_ctx, F_WIDTH), lambda bb: (bb, 0, 0)),
        compiler_params=_cparams(("parallel",)),
        name="dft_ctx",
    )(fa, fb, cc, sc)


def _score_slot(j):
    return 2 if j == 0 else (j - 1) % 2


def _attn_kernel(q_ref, kt_ref, v_ref, lam_ref, gain_ref, o_ref, q2_ref, s_ref, m_ref, acc_ref,
                 *, lam_init, tq, tk):
    nsub = q_ref.shape[0] // tq
    nk = kt_ref.shape[1] // tk

    def keys(j):
        return slice(j * tk, (j + 1) * tk)
    lane = lax.broadcasted_iota(jnp.int32, (tq, A_DV), 1)
    comp0 = (lane % A_DH) < (A_DH // 2)
    lp = lam_ref[...]
    lam = (jnp.exp(jnp.sum(lp[0:1] * lp[1:2], keepdims=True))
           - jnp.exp(jnp.sum(lp[2:3] * lp[3:4], keepdims=True)) + lam_init)
    out_gain = gain_ref[...] * (1.0 - lam_init)

    def rows(i):
        return pl.ds(pl.multiple_of(i * tq, tq), tq)

    def load_q2(i):
        q = q_ref[rows(i), :]
        zero = jnp.zeros_like(q)
        q2_ref[0:tq, :] = jnp.where(comp0, q, zero)
        q2_ref[tq:2 * tq, :] = jnp.where(comp0, zero, q)

    def scores(j):
        s_ref[_score_slot(j)] = _bdot(q2_ref[...], kt_ref[:, keys(j)])

    def softmax_pv(j):
        s = s_ref[_score_slot(j)]
        m_old = m_ref[...]
        m_new = jnp.maximum(m_old, jnp.max(s, axis=1, keepdims=True))
        alpha = jnp.exp2(m_old - m_new)
        p = jnp.exp2(s - jnp.concatenate([m_new] * (tk // LANES), axis=1)).astype(BF16)
        pv = _bdot(p, v_ref[keys(j), :])
        acc_ref[...] = acc_ref[...] * jnp.concatenate([alpha, alpha], axis=1) + pv
        m_ref[...] = m_new

    def finalize(i):
        acc = acc_ref[...]
        o0 = acc[0:tq, 0:LANES] / acc[0:tq, LANES:]
        o1 = acc[tq:, 0:LANES] / acc[tq:, LANES:]
        o = o0 - lam * o1
        ms = jnp.mean(o * o, axis=-1, keepdims=True)
        o_ref[rows(i), :] = (o * lax.rsqrt(ms + RMS_EPS) * out_gain).astype(o_ref.dtype)

    def next_tile_first_scores(i):
        load_q2(jnp.minimum(i + 1, nsub - 1))
        scores(0)

    load_q2(0)
    scores(0)

    def tile(i, carry):
        m_ref[...] = jnp.full(m_ref.shape, -jnp.inf, F32)
        acc_ref[...] = jnp.zeros(acc_ref.shape, F32)
        for j in range(nk):
            if j + 1 < nk:
                scores(j + 1)
                softmax_pv(j)
            elif nk > 1:
                next_tile_first_scores(i)
                softmax_pv(j)
            else:
                softmax_pv(j)
                next_tile_first_scores(i)
        finalize(i)
        return carry

    lax.fori_loop(0, nsub, tile, 0)


def _attention(q, kt, v_ext, lam_p, gain, *, q_block0, n_q, key_block0, n_keys, tq, tk, group,
               lam_init):
    b = q.shape[0]
    assert n_q % group == 0 and group % tq == 0 and n_keys % tk == 0
    return pl.pallas_call(
        functools.partial(_attn_kernel, lam_init=lam_init, tq=tq, tk=tk),
        out_shape=jax.ShapeDtypeStruct((b, n_q, A_WIDTH), BF16),
        grid=(b, A_HEADS, n_q // group),
        in_specs=[
            pl.BlockSpec((None, group, A_DV), lambda bb, h, i: (bb, q_block0 + i, h)),
            pl.BlockSpec((None, None, A_DV, n_keys), lambda bb, h, i: (bb, h, 0, key_block0)),
            pl.BlockSpec((None, n_keys, 2 * LANES), lambda bb, h, i: (bb, key_block0, h)),
            pl.BlockSpec((4, A_DH), lambda bb, h, i: (0, 0)),
            pl.BlockSpec((1, A_DV), lambda bb, h, i: (0, 0)),
        ],
        out_specs=pl.BlockSpec((None, group, A_DV), lambda bb, h, i: (bb, i, h)),
        scratch_shapes=[
            pltpu.VMEM((2 * tq, A_DV), BF16),
            pltpu.VMEM((3, 2 * tq, tk), F32),
            pltpu.VMEM((2 * tq, LANES), F32),
            pltpu.VMEM((2 * tq, 2 * LANES), F32),
        ],
        compiler_params=_cparams(("parallel", "parallel", "arbitrary")),
        name="diff_attn",
    )(q, kt, v_ext, lam_p, gain)


_HI16 = 0xFFFF0000


def _pack_bf16_pair(lo, hi):
    ulo = lax.bitcast_convert_type(lo.astype(BF16).astype(F32), jnp.uint32) >> 16
    uhi = lax.bitcast_convert_type(hi.astype(BF16).astype(F32), jnp.uint32) & jnp.uint32(_HI16)
    return ulo | uhi


def _top_half_bits(x):
    return lax.bitcast_convert_type(lax.bitcast_convert_type(x, jnp.uint32) & jnp.uint32(_HI16), F32)


def _unpack_bf16_pair(u):
    lo = lax.bitcast_convert_type(u << 16, F32)
    hi = lax.bitcast_convert_type(u & jnp.uint32(_HI16), F32)
    return lo, hi


ROUTE_G_LO, ROUTE_G_HI, ROUTE_CLASS = 0, 1, 2
N_PAIRS = EXPERTS_PER_GROUP * (EXPERTS_PER_GROUP - 1) // 2
N_PAIR_CLASSES = N_EXPERT_GROUPS * N_PAIRS


def _router_gates(sc_t, sel_t, route_ref):
    s_rows = [sc_t[e:e + 1, :] for e in range(N_EXPERTS)]
    v_rows = [sel_t[e:e + 1, :] for e in range(N_EXPERTS)]
    in_top2 = []
    group_score = []
    for g in range(N_EXPERT_GROUPS):
        vs = v_rows[g * EXPERTS_PER_GROUP:(g + 1) * EXPERTS_PER_GROUP]
        tops = []
        for jj in range(EXPERTS_PER_GROUP):
            rank = jnp.zeros_like(vs[jj])
            for ii in range(EXPERTS_PER_GROUP):
                if ii == jj:
                    continue
                beats = (vs[ii] >= vs[jj]) if ii < jj else (vs[ii] > vs[jj])
                rank = rank + jnp.where(beats, 1.0, 0.0)
            tops.append(rank < 2.0)
        in_top2 += tops
        gs = jnp.zeros_like(vs[0])
        for jj in range(EXPERTS_PER_GROUP):
            gs = gs + jnp.where(tops[jj], vs[jj], 0.0)
        group_score.append(gs)
    best = group_score[0]
    gidx = jnp.zeros_like(best)
    for g in range(1, N_EXPERT_GROUPS):
        upd = group_score[g] > best
        best = jnp.where(upd, group_score[g], best)
        gidx = jnp.where(upd, float(g), gidx)
    chosen = [jnp.logical_and(in_top2[e], gidx == float(e // EXPERTS_PER_GROUP))
              for e in range(N_EXPERTS)]
    denom = jnp.zeros_like(best)
    for e in range(N_EXPERTS):
        denom = denom + jnp.where(chosen[e], s_rows[e], 0.0)
    e_lo = jnp.zeros_like(best)
    g_lo = jnp.zeros_like(best)
    for e in reversed(range(N_EXPERTS)):
        e_lo = jnp.where(chosen[e], float(e), e_lo)
        g_lo = jnp.where(chosen[e], s_rows[e] / denom, g_lo)
    e_hi = jnp.zeros_like(best)
    g_hi = jnp.zeros_like(best)
    for e in range(N_EXPERTS):
        e_hi = jnp.where(chosen[e], float(e), e_hi)
        g_hi = jnp.where(chosen[e], s_rows[e] / denom, g_hi)
    a_lo = e_lo - EXPERTS_PER_GROUP * gidx
    a_hi = e_hi - EXPERTS_PER_GROUP * gidx
    pair_rank = a_lo * (2 * EXPERTS_PER_GROUP - 1 - a_lo) * 0.5 + (a_hi - a_lo - 1.0)
    route_ref[...] = jnp.zeros(route_ref.shape, F32)
    route_ref[ROUTE_G_LO:ROUTE_G_LO + 1, :] = g_lo
    route_ref[ROUTE_G_HI:ROUTE_G_HI + 1, :] = g_hi
    route_ref[ROUTE_CLASS:ROUTE_CLASS + 1, :] = gidx * float(N_PAIRS) + pair_rank


def _outproj_kernel(yf_ref, att_ref, sg_ref, x_ref, mod_ref, wo_ref, lng_ref, lnb_ref,
                    wr_ref, rb_ref, x1_ref, h_ref, route_ref, *, alpha):
    mix = _bdot(yf_ref[...], wo_ref[0:F_WIDTH, :])
    mix = mix + _bdot(att_ref[...], wo_ref[F_WIDTH:F_WIDTH + A_WIDTH, :])
    mix = mix + _bdot(sg_ref[...], wo_ref[F_WIDTH + A_WIDTH:, :])
    g1 = mod_ref[2:3, :]
    x1 = _layernorm(alpha * x_ref[...] + g1 * mix, lng_ref[...], lnb_ref[...])
    x1_ref[...] = x1
    h = x1 * (1.0 + mod_ref[4:5, :]) + mod_ref[3:4, :]
    half = h.shape[1] // 2
    h_ref[...] = _pack_bf16_pair(h[:, :half], h[:, half:])
    h_top = _top_half_bits(h)
    h_hi = h_top.astype(BF16)
    h_lo = (h - h_top).astype(BF16)
    hw = _bdot(h_hi, wr_ref[...])
    logits = hw[:, :ROUTER_LANES] + (hw[:, ROUTER_LANES:] + _bdot(h_lo, wr_ref[:, :ROUTER_LANES]))
    scores = jax.nn.sigmoid(logits)
    sel = scores + rb_ref[...]
    _router_gates(scores.T[0:N_EXPERTS, :], sel.T[0:N_EXPERTS, :], route_ref)


def _outproj(yf, att, sg, xa, mods_l, wo, lng, lnb, wr, rb, n_lat, alpha):
    b, l, d = xa.shape
    tm = TOKEN_TILE
    nlat = n_lat // tm
    seg = lambda i: jnp.where(i >= nlat, 1, 0)
    row = lambda width: pl.BlockSpec((None, tm, width), lambda bb, i: (bb, i, 0))
    const2 = lambda shape: pl.BlockSpec(shape, lambda bb, i: (0, 0))
    return pl.pallas_call(
        functools.partial(_outproj_kernel, alpha=alpha),
        out_shape=(
            jax.ShapeDtypeStruct((b, l, d), F32),
            jax.ShapeDtypeStruct((b, l, d // 2), jnp.uint32),
            jax.ShapeDtypeStruct((b, SUBLANES, l), F32),
        ),
        grid=(b, l // tm),
        in_specs=[
            row(F_WIDTH), row(A_WIDTH), row(G_WIDTH), row(d),
            pl.BlockSpec((None, None, MOD_ROWS, d), lambda bb, i: (bb, seg(i), 0, 0)),
            const2((d, d)), const2((1, d)), const2((1, d)),
            const2((d, 2 * ROUTER_LANES)), const2((1, ROUTER_LANES)),
        ],
        out_specs=(row(d), row(d // 2),
                   pl.BlockSpec((None, SUBLANES, tm), lambda bb, i: (bb, 0, i))),
        compiler_params=_cparams(("parallel", "parallel")),
        name="outproj_ln_router",
    )(yf, att, sg, xa, mods_l, wo, lng, lnb, wr, rb)


def _moe_plan(route, tm):
    b, _, l = route.shape
    t = b * l
    pairs = [(a, c) for a in range(EXPERTS_PER_GROUP) for c in range(a + 1, EXPERTS_PER_GROUP)]
    cls_e1 = np.zeros((N_PAIR_CLASSES,), np.int32)
    cls_e2 = np.zeros((N_PAIR_CLASSES,), np.int32)
    for k, (a, c) in enumerate(pairs):
        assert k == a * (2 * EXPERTS_PER_GROUP - 1 - a) // 2 + (c - a - 1)
        for g in range(N_EXPERT_GROUPS):
            cls_e1[g * N_PAIRS + k] = g * EXPERTS_PER_GROUP + a
            cls_e2[g * N_PAIRS + k] = g * EXPERTS_PER_GROUP + c
    g_lo = route[:, ROUTE_G_LO, :]
    g_hi = route[:, ROUTE_G_HI, :]
    pair_gates = jnp.concatenate([jnp.broadcast_to(g_lo[:, :, None], (b, l, LANES)),
                                  jnp.broadcast_to(g_hi[:, :, None], (b, l, LANES))], axis=-1)
    cls = route[:, ROUTE_CLASS, :].astype(jnp.int32).reshape(t)
    onehot = cls[:, None] == jnp.arange(N_PAIR_CLASSES, dtype=jnp.int32)[None, :]
    oh_tiles = onehot.reshape(t // tm, tm, N_PAIR_CLASSES).astype(BF16)
    tri = jnp.tril(jnp.ones((tm, tm), BF16))
    within = jnp.einsum("ij,njc->nic", tri, oh_tiles, preferred_element_type=F32)
    tile_total = within[:, -1, :]
    before = jnp.cumsum(tile_total, axis=0) - tile_total
    csum = (within + before[:, None, :]).reshape(t, N_PAIR_CLASSES).astype(jnp.int32)
    onehot = onehot.astype(jnp.int32)
    rank = jnp.sum(csum * onehot, axis=1) - 1
    padded = (csum[-1] + tm - 1) // tm * tm
    off_end = jnp.cumsum(padded)
    dest = jnp.sum(onehot * (off_end - padded)[None, :], axis=1) + rank
    nt = t // tm + N_PAIR_CLASSES
    n_valid = (off_end[-1] // tm).astype(jnp.int32)
    first_row = jnp.minimum(jnp.arange(nt, dtype=jnp.int32), n_valid - 1) * tm
    tile_cls = jnp.minimum(jnp.searchsorted(off_end, first_row, side="right"), N_PAIR_CLASSES - 1)
    return (dest.astype(jnp.int32), pair_gates, jnp.asarray(cls_e1)[tile_cls],
            jnp.asarray(cls_e2)[tile_cls], n_valid.reshape(1), nt)


def _dispatch_kernel(dest_ref, h_ref, hs_init_ref, hs_ref, stage, sems):
    del hs_init_ref
    s = pl.program_id(0)
    n = pl.num_programs(0)
    tm = h_ref.shape[0]
    slot = s % 2

    def wait_slot(k):
        pltpu.make_async_copy(stage.at[k], hs_ref.at[pl.ds(0, tm), :], sems.at[k]).wait()

    @pl.when(s >= 2)
    def _reuse():
        wait_slot(slot)

    stage[slot] = h_ref[...]
    for r in range(tm):
        pltpu.make_async_copy(stage.at[slot, pl.ds(r, 1), :],
                              hs_ref.at[pl.ds(dest_ref[0, r], 1), :], sems.at[slot]).start()

    @pl.when(s == n - 1)
    def _drain():
        @pl.when(n >= 2)
        def _other():
            wait_slot(1 - slot)
        wait_slot(slot)


def _dispatch(h_packed, dest, nt, tm):
    t, w = h_packed.shape
    hs_init = jnp.zeros((nt * tm, w), jnp.uint32)
    return pl.pallas_call(
        _dispatch_kernel,
        out_shape=jax.ShapeDtypeStruct((nt * tm, w), jnp.uint32),
        grid=(t // tm,),
        in_specs=[
            pl.BlockSpec((None, 1, tm), lambda i: (i, 0, 0), memory_space=pltpu.SMEM),
            pl.BlockSpec((tm, w), lambda i: (i, 0)),
            pl.BlockSpec(memory_space=pl.ANY),
        ],
        out_specs=pl.BlockSpec(memory_space=pl.ANY),
        scratch_shapes=[pltpu.VMEM((2, tm, w), jnp.uint32), pltpu.SemaphoreType.DMA((2,))],
        input_output_aliases={2: 0},
        compiler_params=_cparams(("arbitrary",)),
        name="moe_dispatch",
    )(dest.reshape(t // tm, 1, tm), h_packed, hs_init)


def _expert_pair_kernel(e1_ref, e2_ref, nv_ref, hs_ref, w1a_ref, w3a_ref, w2a_ref,
                        w1b_ref, w3b_ref, w2b_ref, y_ref):
    del e1_ref, e2_ref
    i = pl.program_id(0)

    @pl.when(i < nv_ref[0])
    def _compute():
        lo, hi = _unpack_bf16_pair(hs_ref[...])
        h = jnp.concatenate([lo, hi], axis=1).astype(BF16)

        def ffn(w1_ref, w3_ref, w2_ref):
            a = _bdot(h, w1_ref[...])
            a = (a * jax.nn.sigmoid(a)) * _bdot(h, w3_ref[...])
            return _bdot(a.astype(BF16), w2_ref[...])

        y_ref[...] = _pack_bf16_pair(ffn(w1a_ref, w3a_ref, w2a_ref), ffn(w1b_ref, w3b_ref, w2b_ref))

    @pl.when(i >= nv_ref[0])
    def _unused_tile():
        y_ref[...] = jnp.zeros(y_ref.shape, jnp.uint32)


def _expert_pairs(hs, tile_e1, tile_e2, n_valid, w1, w3, w2, tm):
    rows, w = hs.shape
    _, d, de = w1.shape
    first = lambda shape: pl.BlockSpec(shape, lambda i, e1, e2, nv: (e1[i], 0, 0))
    second = lambda shape: pl.BlockSpec(shape, lambda i, e1, e2, nv: (e2[i], 0, 0))
    grid_spec = pltpu.PrefetchScalarGridSpec(
        num_scalar_prefetch=3,
        grid=(rows // tm,),
        in_specs=[
            pl.BlockSpec((tm, w), lambda i, e1, e2, nv: (jnp.minimum(i, nv[0] - 1), 0)),
            first((None, d, de)), first((None, d, de)), first((None, de, d)),
            second((None, d, de)), second((None, d, de)), second((None, de, d)),
        ],
        out_specs=pl.BlockSpec((tm, d), lambda i, e1, e2, nv: (i, 0)),
    )
    return pl.pallas_call(
        _expert_pair_kernel,
        out_shape=jax.ShapeDtypeStruct((rows, d), jnp.uint32),
        grid_spec=grid_spec,
        compiler_params=_cparams(("arbitrary",)),
        name="moe_expert_pairs",
    )(tile_e1, tile_e2, n_valid, hs, w1, w3, w2, w1, w3, w2)


def _combine_kernel(dest_ref, dest_next_ref, y2_ref, x_ref, pg_ref, mod_ref, lng_ref, lnb_ref,
                    o_ref, ybuf, sems, *, alpha):
    s = pl.program_id(0)
    n = pl.num_programs(0)
    tm, d = x_ref.shape
    slot = s % 2

    def start_gather(idx_ref, k):
        for r in range(tm):
            pltpu.make_async_copy(y2_ref.at[pl.ds(idx_ref[0, r], 1), :],
                                  ybuf.at[k, pl.ds(r, 1), :], sems.at[k]).start()

    @pl.when(s == 0)
    def _first():
        start_gather(dest_ref, 0)

    @pl.when(s + 1 < n)
    def _prefetch():
        start_gather(dest_next_ref, 1 - slot)

    pltpu.make_async_copy(y2_ref.at[pl.ds(0, tm), :], ybuf.at[slot], sems.at[slot]).wait()
    y_lo, y_hi = _unpack_bf16_pair(ybuf[slot])
    pg = pg_ref[...]
    reps = d // LANES
    g_lo = jnp.concatenate([pg[:, :LANES]] * reps, axis=1)
    g_hi = jnp.concatenate([pg[:, LANES:]] * reps, axis=1)
    y = g_lo * y_lo + g_hi * y_hi
    o_ref[...] = _layernorm(alpha * x_ref[...] + mod_ref[5:6, :] * y, lng_ref[...], lnb_ref[...])


def _combine(y2, dest, x1, pair_gates, mods_l, lng, lnb, *, n_rows, n_lat, tm, alpha):
    b, l, d = x1.shape
    nlat = n_lat // tm
    tiles_per_batch = l // tm
    per_batch = n_rows // tm
    n_steps = b * per_batch
    batch = lambda s: s // per_batch
    tile = lambda s: s % per_batch
    seg = lambda s: jnp.where(tile(s) >= nlat, 1, 0)
    token_tile = lambda s: batch(s) * tiles_per_batch + tile(s)
    row = lambda width: pl.BlockSpec((None, tm, width), lambda s: (batch(s), tile(s), 0))
    const2 = lambda shape: pl.BlockSpec(shape, lambda s: (0, 0))
    dest3 = dest.reshape(b * tiles_per_batch, 1, tm)
    return pl.pallas_call(
        functools.partial(_combine_kernel, alpha=alpha),
        out_shape=jax.ShapeDtypeStruct((b, n_rows, d), F32),
        grid=(n_steps,),
        in_specs=[
            pl.BlockSpec((None, 1, tm), lambda s: (token_tile(s), 0, 0), memory_space=pltpu.SMEM),
            pl.BlockSpec((None, 1, tm), lambda s: (token_tile(jnp.minimum(s + 1, n_steps - 1)), 0, 0),
                         memory_space=pltpu.SMEM),
            pl.BlockSpec(memory_space=pl.ANY),
            row(d), row(2 * LANES),
            pl.BlockSpec((None, None, MOD_ROWS, d), lambda s: (batch(s), seg(s), 0, 0)),
            const2((1, d)), const2((1, d)),
        ],
        out_specs=row(d),
        scratch_shapes=[pltpu.VMEM((2, tm, d), jnp.uint32), pltpu.SemaphoreType.DMA((2,))],
        compiler_params=_cparams(("arbitrary",)),
        name="moe_combine_ln",
    )(dest3, dest3, y2, x1, pair_gates, mods_l, lng, lnb)


def _head_lane_fields():
    j = jnp.arange(LANES)
    half = j // (LANES // 2)
    comp = (j % (LANES // 2)) // (A_DH // 2)
    axis = (j % (A_DH // 2)) // (A_DH // 4)
    freq = j % (A_DH // 4)
    return half, comp, axis, freq


def _qk_column_perm():
    half, comp, axis, freq = _head_lane_fields()
    orig = comp * A_DH + axis * (A_DH // 2) + half * (A_DH // 4) + freq
    return (jnp.arange(A_HEADS)[:, None] * LANES + orig[None, :]).reshape(-1)


def _rope_tables(n_lat, n_ctx):
    half, _, axis, freq = _head_lane_fields()
    rows = n_lat // GRID_W
    row = jnp.repeat(jnp.arange(rows, dtype=F32), GRID_W)
    col = jnp.tile(jnp.arange(GRID_W, dtype=F32), rows)
    hd = A_DH // 2
    inv = ROPE_BASE ** (-jnp.arange(0, hd, 2, dtype=F32) / hd)
    pos = jnp.where(axis[None, :] == 0, row[:, None], col[:, None])
    ang = pos * inv[freq][None, :]
    cosf = jnp.cos(ang)
    sinf = jnp.sin(ang) * jnp.where(half == 0, -1.0, 1.0)[None, :]
    cosf = jnp.concatenate([cosf, jnp.ones((n_ctx, LANES), F32)], axis=0)
    sinf = jnp.concatenate([sinf, jnp.zeros((n_ctx, LANES), F32)], axis=0)
    return cosf, sinf


def _dft_mats(n, scale=1.0):
    i = jnp.arange(n, dtype=jnp.int32)
    ang = ((i[:, None] * i[None, :]) % n).astype(F32) * (2.0 * math.pi / n)
    return jnp.cos(ang) * scale, jnp.sin(ang) * scale


def _twiddles(n, bn):
    ka = jnp.arange(DFT_A, dtype=jnp.int32)
    bb = jnp.arange(bn, dtype=jnp.int32)
    ang = ((ka[:, None] * bb[None, :]) % n).astype(F32) * (2.0 * math.pi / n)
    shape = (DFT_A, bn, F_WIDTH)
    twc = jnp.broadcast_to(jnp.cos(ang)[:, :, None], shape).reshape(DFT_A, bn * F_WIDTH)
    tws = jnp.broadcast_to(jnp.sin(ang)[:, :, None], shape).reshape(DFT_A, bn * F_WIDTH)
    return twc, tws


def _block_diag(blocks):
    g, n = blocks.shape[-3], blocks.shape[-1]
    eye = jnp.eye(g, dtype=blocks.dtype)
    out = blocks[..., :, :, None, :] * eye[:, None, :, None]
    return out.reshape(blocks.shape[:-3] + (g * n, g * n))


def kernel(x, c, ctx, c_ctx, w_ada, b_ada, w_in, w_fourier, diff_lambda, diff_subln,
           sgu_ln_g, sgu_ln_b, sgu_w, sgu_b, w_out, ln_g, ln_b, w_router, router_bias,
           moe_w1, moe_w3, moe_w2):
    b, n, d = x.shape
    n_ctx = ctx.shape[1]
    depth = w_ada.shape[0]
    l = n + n_ctx
    alpha = (2 * depth) ** 0.25
    bn = n // DFT_A
    assert n % DFT_A == 0 and bn % SUBLANES == 0 and n % GRID_W == 0
    assert n_ctx % TOKEN_TILE == 0 and n % TOKEN_TILE == 0 and n % n_ctx == 0
    assert b + 1 <= MOD_ROWS

    c_rows = jnp.concatenate([c, c_ctx[None, :], jnp.zeros((MOD_ROWS - b - 1, d), F32)], axis=0)
    mod = _ada(c_rows, w_ada, b_ada).reshape(depth, MOD_ROWS, 6, d)
    lat = mod[:, :b]
    cm = jnp.broadcast_to(mod[:, b:b + 1], lat.shape)
    mods = jnp.stack([lat, cm], axis=2)
    mods = jnp.pad(mods, ((0, 0), (0, 0), (0, 0), (0, MOD_ROWS - 6), (0, 0)))

    perm = _qk_column_perm()
    c64, s64 = _dft_mats(F_DIM)
    eye_g = jnp.eye(F_GROUPS, dtype=F32)
    bdc = jnp.kron(eye_g, c64)
    bds = jnp.kron(eye_g, s64)
    bdw = _block_diag(w_fourier)
    w_fab = _fourier_weights(w_in[:, :, :F_WIDTH], bdc, bds, bdw)
    o_q = F_WIDTH
    o_k = o_q + A_QK_WIDTH
    o_v = o_k + A_QK_WIDTH
    o_g = o_v + A_WIDTH
    w_big = jnp.concatenate([
        w_fab,
        w_in[:, :, o_q:o_k][:, :, perm],
        w_in[:, :, o_k:o_v][:, :, perm],
        w_in[:, :, o_v:],
    ], axis=-1).astype(BF16)
    wo = w_out.astype(BF16)
    w1 = moe_w1.astype(BF16)
    w3 = moe_w3.astype(BF16)
    w2 = moe_w2.astype(BF16)
    ws = sgu_w.astype(BF16)
    bs_full = jnp.repeat(jnp.swapaxes(sgu_b, 1, 2), G_DIM, axis=2)
    wr_f32 = jnp.pad(w_router, ((0, 0), (0, ROUTER_LANES - N_EXPERTS)))
    wr_top = _top_half_bits(wr_f32)
    wr = jnp.concatenate([wr_top.astype(BF16), (wr_f32 - wr_top).astype(BF16)], axis=1)
    rb =jnp.pad(router_bias, (0, ROUTER_LANES - N_EXPERTS)).reshape(1, ROUTER_LANES)

    cosf, sinf = _rope_tables(n, n_ctx)
    ca, sa = _dft_mats(DFT_A)
    twc, tws = _twiddles(n, bn)
    cb, sb = _dft_mats(bn, scale=(n * F_DIM) ** -0.5)
    cc, sc = _dft_mats(n_ctx, scale=(n_ctx * F_DIM) ** -0.5)
    q_scale = (A_DH ** -0.5) * math.log2(math.e)

    tk = ATTN_TK
    assert l % tk == 0
    assert n % MOE_TILE == 0 and n_ctx % MOE_TILE == 0

    xa = jnp.concatenate([x, ctx], axis=1)
    for li in range(depth):
        last = li == depth - 1
        lam_init = 0.8 - 0.6 * math.exp(-0.3 * li)
        fa, fb, q, kt, v_ext, sg = _inproj(
            xa, mods[li], w_big[li], cosf, sinf, sgu_ln_g[li][None], sgu_ln_b[li][None],
            ws[li], bs_full[li], n, q_scale)
        zr, zi = _fft_a(fa, fb, ca, sa, twc, tws, bn)
        yf = jnp.concatenate([_fft_b(zr, zi, cb, sb, bn), _dft_ctx(fa, fb, cc, sc, n, n_ctx)], axis=1)
        gain = diff_subln[li][None]
        att_x = _attention(q, kt, v_ext, diff_lambda[li], gain, q_block0=0, n_q=n, key_block0=0,
                           n_keys=l, tq=ATTN_TQ, tk=tk, group=min(ATTN_GROUP, n), lam_init=lam_init)
        att_c = _attention(q, kt, v_ext, diff_lambda[li], gain, q_block0=n // n_ctx, n_q=n_ctx,
                           key_block0=n // n_ctx, n_keys=n_ctx, tq=n_ctx, tk=n_ctx, group=n_ctx,
                           lam_init=lam_init)
        att = jnp.concatenate([att_x, att_c], axis=1)
        x1, h_packed, route = _outproj(
            yf, att, sg, xa, mods[li], wo[li], ln_g[li, 0][None], ln_b[li, 0][None], wr, rb, n, alpha)
        dest, pair_gates, tile_e1, tile_e2, n_valid, nt = _moe_plan(route, MOE_TILE)
        hs = _dispatch(h_packed.reshape(b * l, d // 2), dest, nt, MOE_TILE)
        y2 = _expert_pairs(hs, tile_e1, tile_e2, n_valid, w1[li], w3[li], w2[li], MOE_TILE)
        xa = _combine(y2, dest, x1, pair_gates, mods[li], ln_g[li, 1][None], ln_b[li, 1][None],
                      n_rows=n if last else l, n_lat=n, tm=MOE_TILE, alpha=alpha)
    return xa
```

```python
import functools
import math

import numpy as np
import jax
import jax.numpy as jnp
from jax import lax
from jax.experimental import pallas as pl
from jax.experimental.pallas import tpu as pltpu

F32 = jnp.float32
BF16 = jnp.bfloat16
HIGHEST = lax.Precision.HIGHEST

GRID_W = 64
F_GROUPS, F_DIM = 4, 64
F_WIDTH = F_GROUPS * F_DIM
A_HEADS, A_DH = 4, 64
A_DV = 2 * A_DH
A_QK_WIDTH = A_HEADS * 2 * A_DH
A_WIDTH = A_HEADS * A_DV
G_GROUPS, G_DIM = 4, 64
G_WIDTH = G_GROUPS * G_DIM
CHUNK = 128
ROPE_BASE = 10000.0
N_EXPERTS = 16
N_EXPERT_GROUPS = 4
EXPERTS_PER_GROUP = N_EXPERTS // N_EXPERT_GROUPS
LN_EPS = 1e-5
RMS_EPS = 1e-5

COL_FA = 0
COL_FB = COL_FA + F_WIDTH
COL_Q = COL_FB + F_WIDTH
COL_K = COL_Q + A_QK_WIDTH
COL_V = COL_K + A_QK_WIDTH
COL_G = COL_V + A_WIDTH
BIG_COLS = COL_G + 2 * G_WIDTH

LANES = 128
SUBLANES = 8
MOD_ROWS = 8
ROUTER_LANES = 128
DFT_A = 128
VMEM_LIMIT = 56 * 1024 * 1024

TOKEN_TILE = 256
ATTN_TQ = 256
ATTN_GROUP = 8192
ATTN_TK = 1280
MOE_TILE = 256


def _cparams(sem):
    return pltpu.CompilerParams(dimension_semantics=sem, vmem_limit_bytes=VMEM_LIMIT)


def _hdot(a, b):
    return jnp.dot(a, b, precision=HIGHEST, preferred_element_type=F32)


def _bdot(a, b):
    return jnp.dot(a, b, preferred_element_type=F32)


def _layernorm(y, g, b):
    mu = jnp.mean(y, axis=-1, keepdims=True)
    d = y - mu
    var = jnp.mean(d * d, axis=-1, keepdims=True)
    return d * lax.rsqrt(var + LN_EPS) * g + b


def _ada_kernel(c_ref, w_ref, b_ref, o_ref):
    c = c_ref[...]
    a = c * jax.nn.sigmoid(c)
    o_ref[...] = _hdot(a, w_ref[...]) + b_ref[...]


def _ada(c_rows, w_ada, b_ada):
    depth, d, cols = w_ada.shape
    tn = 1536
    assert cols % tn == 0
    return pl.pallas_call(
        _ada_kernel,
        out_shape=jax.ShapeDtypeStruct((depth, MOD_ROWS, cols), F32),
        grid=(depth, cols // tn),
        in_specs=[
            pl.BlockSpec((MOD_ROWS, d), lambda l, j: (0, 0)),
            pl.BlockSpec((None, d, tn), lambda l, j: (l, 0, j)),
            pl.BlockSpec((None, 1, tn), lambda l, j: (l, 0, j)),
        ],
        out_specs=pl.BlockSpec((None, MOD_ROWS, tn), lambda l, j: (l, 0, j)),
        compiler_params=_cparams(("parallel", "parallel")),
        name="ada_mod",
    )(c_rows, w_ada, b_ada.reshape(depth, 1, cols))


def _fw_kernel(wf_ref, bdc_ref, bds_ref, bdw_ref, o_ref):
    bdw = bdw_ref[...]
    mc = _hdot(bdc_ref[...], bdw)
    ms = _hdot(bds_ref[...], bdw)
    wf = wf_ref[...]
    o_ref[:, :F_WIDTH] = _hdot(wf, mc)
    o_ref[:, F_WIDTH:] = _hdot(wf, ms)


def _fourier_weights(wf, bdc, bds, bdw):
    depth, d, _ = wf.shape
    return pl.pallas_call(
        _fw_kernel,
        out_shape=jax.ShapeDtypeStruct((depth, d, 2 * F_WIDTH), F32),
        grid=(depth,),
        in_specs=[
            pl.BlockSpec((None, d, F_WIDTH), lambda l: (l, 0, 0)),
            pl.BlockSpec((F_WIDTH, F_WIDTH), lambda l: (0, 0)),
            pl.BlockSpec((F_WIDTH, F_WIDTH), lambda l: (0, 0)),
            pl.BlockSpec((None, F_WIDTH, F_WIDTH), lambda l: (l, 0, 0)),
        ],
        out_specs=pl.BlockSpec((None, d, 2 * F_WIDTH), lambda l: (l, 0, 0)),
        compiler_params=_cparams(("parallel",)),
        name="fourier_weights",
    )(wf, bdc, bds, bdw)


def _gelu_tanh(x):
    c = math.sqrt(2.0 / math.pi)
    return x * (0.5 * (1.0 + jnp.tanh(c * (x + 0.044715 * (x * x * x)))))


def _inproj_kernel(x_ref, mod_ref, w_ref, cos_ref, sin_ref, lng_ref, lnb_ref, ws_ref, bs_ref,
                   fa_ref, fb_ref, q_ref, kt_ref, v_ref, sg_ref, *, q_scale):
    x = x_ref[...]
    sh = mod_ref[0:1, :]
    sc = mod_ref[1:2, :]
    h = (x * (1.0 + sc) + sh).astype(BF16)
    p = _bdot(h, w_ref[...])
    fa_ref[...] = p[:, COL_FA:COL_FA + F_WIDTH]
    fb_ref[...] = p[:, COL_FB:COL_FB + F_WIDTH]
    cosf = cos_ref[...]
    sinf = sin_ref[...]
    for hh in range(A_HEADS):
        t = p[:, COL_Q + LANES * hh:COL_Q + LANES * (hh + 1)]
        r = t * cosf + pltpu.roll(t, LANES // 2, 1) * sinf
        q_ref[:, LANES * hh:LANES * (hh + 1)] = (r * q_scale).astype(BF16)
        t = p[:, COL_K + LANES * hh:COL_K + LANES * (hh + 1)]
        r = t * cosf + pltpu.roll(t, LANES // 2, 1) * sinf
        kt_ref[hh] = r.T.astype(BF16)
        v_ref[:, 2 * LANES * hh:2 * LANES * hh + LANES] = (
            p[:, COL_V + LANES * hh:COL_V + LANES * (hh + 1)].astype(BF16))
        v_ref[:, 2 * LANES * hh + LANES:2 * LANES * (hh + 1)] = jnp.ones((t.shape[0], LANES), BF16)

    g = _gelu_tanh(p[:, COL_G:COL_G + 2 * G_WIDTH])
    u = g[:, :G_WIDTH]
    vn = _layernorm(g[:, G_WIDTH:], lng_ref[...], lnb_ref[...])
    lane_group = lax.broadcasted_iota(jnp.int32, (CHUNK, G_WIDTH), 1) // G_DIM
    tm = x.shape[0]
    for c in range(tm // CHUNK):
        rows = slice(c * CHUNK, (c + 1) * CHUNK)
        vc = vn[rows]
        mixed = bs_ref[...]
        for gi in range(G_GROUPS):
            vm = jnp.where(lane_group == gi, vc, 0.0).astype(BF16)
            mixed = mixed + _bdot(ws_ref[gi], vm)
        sg_ref[rows, :] = (u[rows] * mixed).astype(BF16)


def _inproj(xa, mods_l, w_big, cosf, sinf, lng, lnb, ws, bs_full, n_lat, q_scale):
    b, l, d = xa.shape
    tm = TOKEN_TILE
    nlat = n_lat // tm
    seg = lambda i: jnp.where(i >= nlat, 1, 0)
    row = lambda width: pl.BlockSpec((None, tm, width), lambda bb, i: (bb, i, 0))
    const2 = lambda shape: pl.BlockSpec(shape, lambda bb, i: (0, 0))
    out_shapes = (
        jax.ShapeDtypeStruct((b, l, F_WIDTH), F32),
        jax.ShapeDtypeStruct((b, l, F_WIDTH), F32),
        jax.ShapeDtypeStruct((b, l, A_QK_WIDTH), BF16),
        jax.ShapeDtypeStruct((b, A_HEADS, A_DV, l), BF16),
        jax.ShapeDtypeStruct((b, l, 2 * A_WIDTH), BF16),
        jax.ShapeDtypeStruct((b, l, G_WIDTH), BF16),
    )
    return pl.pallas_call(
        functools.partial(_inproj_kernel, q_scale=q_scale),
        out_shape=out_shapes,
        grid=(b, l // tm),
        in_specs=[
            row(d),
            pl.BlockSpec((None, None, MOD_ROWS, d), lambda bb, i: (bb, seg(i), 0, 0)),
            const2((d, BIG_COLS)),
            pl.BlockSpec((tm, LANES), lambda bb, i: (i, 0)),
            pl.BlockSpec((tm, LANES), lambda bb, i: (i, 0)),
            const2((1, G_WIDTH)),
            const2((1, G_WIDTH)),
            pl.BlockSpec((G_GROUPS, CHUNK, CHUNK), lambda bb, i: (0, 0, 0)),
            const2((CHUNK, G_WIDTH)),
        ],
        out_specs=(row(F_WIDTH), row(F_WIDTH), row(A_QK_WIDTH),
                   pl.BlockSpec((None, A_HEADS, A_DV, tm), lambda bb, i: (bb, 0, 0, i)),
                   row(2 * A_WIDTH), row(G_WIDTH)),
        compiler_params=_cparams(("parallel", "parallel")),
        name="inproj",
    )(xa, mods_l, w_big, cosf, sinf, lng, lnb, ws, bs_full)


def _split3(x):
    top = _top_half_bits(x)
    hi = top.astype(BF16)
    return [hi, (x - top).astype(BF16), hi]


def _stack3(mats):
    cols = []
    for m in mats:
        top = _top_half_bits(m)
        cols += [top.astype(BF16), top.astype(BF16), (m - top).astype(BF16)]
    return jnp.concatenate(cols, axis=1)


def _fft_a_kernel(fa_ref, fb_ref, l_ref, tc_ref, ts_ref, zr_ref, zi_ref):
    rhs = jnp.concatenate(_split3(fa_ref[...]) + _split3(fb_ref[...]), axis=0)
    z = _bdot(l_ref[...], rhs)
    zr = z[:DFT_A]
    zi = z[DFT_A:]
    tc = tc_ref[...]
    ts = ts_ref[...]
    zr_ref[...] = zr * tc + zi * ts
    zi_ref[...] = zi * tc - zr * ts


def _fft_a(fa, fb, l_a, twc, tws, bn):
    b, l, _ = fa.shape
    cols = bn * F_WIDTH
    fa3 = fa.reshape(b, l // bn, cols)
    fb3 = fb.reshape(b, l // bn, cols)
    tcw = min(cols, 2048)
    assert cols % tcw == 0
    blk = pl.BlockSpec((None, DFT_A, tcw), lambda bb, j: (bb, 0, j))
    tw = pl.BlockSpec((DFT_A, tcw), lambda bb, j: (0, j))
    zr, zi = pl.pallas_call(
        _fft_a_kernel,
        out_shape=(jax.ShapeDtypeStruct((b, DFT_A, cols), F32),) * 2,
        grid=(b, cols // tcw),
        in_specs=[blk, blk, pl.BlockSpec(l_a.shape, lambda bb, j: (0, 0)), tw, tw],
        out_specs=(blk, blk),
        compiler_params=_cparams(("parallel", "parallel")),
        name="fft_stage_a",
    )(fa3, fb3, l_a, twc, tws)
    return zr.reshape(b, DFT_A, bn, F_WIDTH), zi.reshape(b, DFT_A, bn, F_WIDTH)


def _fft_b_kernel(zr_ref, zi_ref, l_ref, o_ref):
    lm = l_ref[...]
    for j in range(zr_ref.shape[0]):
        rhs = jnp.concatenate(_split3(zr_ref[j]) + _split3(zi_ref[j]), axis=0)
        o_ref[:, F_WIDTH * j:F_WIDTH * (j + 1)] = _bdot(lm, rhs).astype(o_ref.dtype)


def _fft_b(zr4, zi4, l_b, bn):
    b = zr4.shape[0]
    kb = 8
    blk = pl.BlockSpec((None, kb, bn, F_WIDTH), lambda bb, i: (bb, i, 0, 0))
    out = pl.pallas_call(
        _fft_b_kernel,
        out_shape=jax.ShapeDtypeStruct((b, bn, DFT_A * F_WIDTH), BF16),
        grid=(b, DFT_A // kb),
        in_specs=[blk, blk, pl.BlockSpec(l_b.shape, lambda bb, i: (0, 0))],
        out_specs=pl.BlockSpec((None, bn, kb * F_WIDTH), lambda bb, i: (bb, 0, i)),
        compiler_params=_cparams(("parallel", "parallel")),
        name="fft_stage_b",
    )(zr4, zi4, l_b)
    return out.reshape(b, bn * DFT_A, F_WIDTH)


def _dft_ctx_kernel(fa_ref, fb_ref, l_ref, o_ref):
    rhs = jnp.concatenate(_split3(fa_ref[...]) + _split3(fb_ref[...]), axis=0)
    o_ref[...] = _bdot(l_ref[...], rhs).astype(o_ref.dtype)


def _dft_ctx(fa, fb, l_c, n_lat, n_ctx):
    b = fa.shape[0]
    blk = pl.BlockSpec((None, n_ctx, F_WIDTH), lambda bb: (bb, n_lat // n_ctx, 0))
    return pl.pallas_call(
        _dft_ctx_kernel,
        out_shape=jax.ShapeDtypeStruct((b, n_ctx, F_WIDTH), BF16),
        grid=(b,),
        in_specs=[blk, blk, pl.BlockSpec(l_c.shape, lambda bb: (0, 0))],
        out_specs=pl.BlockSpec((None, n_ctx, F_WIDTH), lambda bb: (bb, 0, 0)),
        compiler_params=_cparams(("parallel",)),
        name="dft_ctx",
    )(fa, fb, l_c)


def _score_slot(j):
    return 2 if j == 0 else (j - 1) % 2


def _attn_kernel(q_ref, kt_ref, v_ref, lam_ref, gain_ref, o_ref, q2_ref, s_ref, m_ref, acc_ref,
                 *, lam_init, tq, tk):
    nsub = q_ref.shape[0] // tq
    nk = kt_ref.shape[1] // tk

    def keys(j):
        return slice(j * tk, (j + 1) * tk)
    lane = lax.broadcasted_iota(jnp.int32, (tq, A_DV), 1)
    comp0 = (lane % A_DH) < (A_DH // 2)
    lp = lam_ref[...]
    lam = (jnp.exp(jnp.sum(lp[0:1] * lp[1:2], keepdims=True))
           - jnp.exp(jnp.sum(lp[2:3] * lp[3:4], keepdims=True)) + lam_init)
    out_gain = gain_ref[...] * (1.0 - lam_init)

    def rows(i):
        return pl.ds(pl.multiple_of(i * tq, tq), tq)

    def load_q2(i):
        q = q_ref[rows(i), :]
        zero = jnp.zeros_like(q)
        q2_ref[0:tq, :] = jnp.where(comp0, q, zero)
        q2_ref[tq:2 * tq, :] = jnp.where(comp0, zero, q)

    def scores(j):
        s_ref[_score_slot(j)] = _bdot(q2_ref[...], kt_ref[:, keys(j)])

    def softmax_pv(j):
        s = s_ref[_score_slot(j)]
        m_old = m_ref[...]
        m_new = jnp.maximum(m_old, jnp.max(s, axis=1, keepdims=True))
        alpha = jnp.exp2(m_old - m_new)
        p = jnp.exp2(s - jnp.concatenate([m_new] * (tk // LANES), axis=1)).astype(BF16)
        pv = _bdot(p, v_ref[keys(j), :])
        acc_ref[...] = acc_ref[...] * jnp.concatenate([alpha, alpha], axis=1) + pv
        m_ref[...] = m_new

    def finalize(i):
        acc = acc_ref[...]
        o0 = acc[0:tq, 0:LANES] / acc[0:tq, LANES:]
        o1 = acc[tq:, 0:LANES] / acc[tq:, LANES:]
        o = o0 - lam * o1
        ms = jnp.mean(o * o, axis=-1, keepdims=True)
        o_ref[rows(i), :] = (o * lax.rsqrt(ms + RMS_EPS) * out_gain).astype(o_ref.dtype)

    def next_tile_first_scores(i):
        load_q2(jnp.minimum(i + 1, nsub - 1))
        scores(0)

    load_q2(0)
    scores(0)

    def tile(i, carry):
        m_ref[...] = jnp.full(m_ref.shape, -jnp.inf, F32)
        acc_ref[...] = jnp.zeros(acc_ref.shape, F32)
        for j in range(nk):
            if j + 1 < nk:
                scores(j + 1)
                softmax_pv(j)
            elif nk > 1:
                next_tile_first_scores(i)
                softmax_pv(j)
            else:
                softmax_pv(j)
                next_tile_first_scores(i)
        finalize(i)
        return carry

    lax.fori_loop(0, nsub, tile, 0)


def _attention(q, kt, v_ext, lam_p, gain, *, q_block0, n_q, key_block0, n_keys, tq, tk, group,
               lam_init):
    b = q.shape[0]
    assert n_q % group == 0 and group % tq == 0 and n_keys % tk == 0
    return pl.pallas_call(
        functools.partial(_attn_kernel, lam_init=lam_init, tq=tq, tk=tk),
        out_shape=jax.ShapeDtypeStruct((b, n_q, A_WIDTH), BF16),
        grid=(b, A_HEADS, n_q // group),
        in_specs=[
            pl.BlockSpec((None, group, A_DV), lambda bb, h, i: (bb, q_block0 + i, h)),
            pl.BlockSpec((None, None, A_DV, n_keys), lambda bb, h, i: (bb, h, 0, key_block0)),
            pl.BlockSpec((None, n_keys, 2 * LANES), lambda bb, h, i: (bb, key_block0, h)),
            pl.BlockSpec((4, A_DH), lambda bb, h, i: (0, 0)),
            pl.BlockSpec((1, A_DV), lambda bb, h, i: (0, 0)),
        ],
        out_specs=pl.BlockSpec((None, group, A_DV), lambda bb, h, i: (bb, i, h)),
        scratch_shapes=[
            pltpu.VMEM((2 * tq, A_DV), BF16),
            pltpu.VMEM((3, 2 * tq, tk), F32),
            pltpu.VMEM((2 * tq, LANES), F32),
            pltpu.VMEM((2 * tq, 2 * LANES), F32),
        ],
        compiler_params=_cparams(("parallel", "parallel", "arbitrary")),
        name="diff_attn",
    )(q, kt, v_ext, lam_p, gain)


_HI16 = 0xFFFF0000


def _pack_bf16_pair(lo, hi):
    ulo = lax.bitcast_convert_type(lo.astype(BF16).astype(F32), jnp.uint32) >> 16
    uhi = lax.bitcast_convert_type(hi.astype(BF16).astype(F32), jnp.uint32) & jnp.uint32(_HI16)
    return ulo | uhi


def _top_half_bits(x):
    return lax.bitcast_convert_type(lax.bitcast_convert_type(x, jnp.uint32) & jnp.uint32(_HI16), F32)


def _unpack_bf16_pair(u):
    lo = lax.bitcast_convert_type(u << 16, F32)
    hi = lax.bitcast_convert_type(u & jnp.uint32(_HI16), F32)
    return lo, hi


ROUTE_G_LO, ROUTE_G_HI, ROUTE_CLASS = 0, 1, 2
N_PAIRS = EXPERTS_PER_GROUP * (EXPERTS_PER_GROUP - 1) // 2
N_PAIR_CLASSES = N_EXPERT_GROUPS * N_PAIRS


def _router_gates(sc_t, sel_t, route_ref):
    s_rows = [sc_t[e:e + 1, :] for e in range(N_EXPERTS)]
    v_rows = [sel_t[e:e + 1, :] for e in range(N_EXPERTS)]
    in_top2 = []
    group_score = []
    for g in range(N_EXPERT_GROUPS):
        vs = v_rows[g * EXPERTS_PER_GROUP:(g + 1) * EXPERTS_PER_GROUP]
        tops = []
        for jj in range(EXPERTS_PER_GROUP):
            rank = jnp.zeros_like(vs[jj])
            for ii in range(EXPERTS_PER_GROUP):
                if ii == jj:
                    continue
                beats = (vs[ii] >= vs[jj]) if ii < jj else (vs[ii] > vs[jj])
                rank = rank + jnp.where(beats, 1.0, 0.0)
            tops.append(rank < 2.0)
        in_top2 += tops
        gs = jnp.zeros_like(vs[0])
        for jj in range(EXPERTS_PER_GROUP):
            gs = gs + jnp.where(tops[jj], vs[jj], 0.0)
        group_score.append(gs)
    best = group_score[0]
    gidx = jnp.zeros_like(best)
    for g in range(1, N_EXPERT_GROUPS):
        upd = group_score[g] > best
        best = jnp.where(upd, group_score[g], best)
        gidx = jnp.where(upd, float(g), gidx)
    chosen = [jnp.logical_and(in_top2[e], gidx == float(e // EXPERTS_PER_GROUP))
              for e in range(N_EXPERTS)]
    denom = jnp.zeros_like(best)
    for e in range(N_EXPERTS):
        denom = denom + jnp.where(chosen[e], s_rows[e], 0.0)
    e_lo = jnp.zeros_like(best)
    g_lo = jnp.zeros_like(best)
    for e in reversed(range(N_EXPERTS)):
        e_lo = jnp.where(chosen[e], float(e), e_lo)
        g_lo = jnp.where(chosen[e], s_rows[e] / denom, g_lo)
    e_hi = jnp.zeros_like(best)
    g_hi = jnp.zeros_like(best)
    for e in range(N_EXPERTS):
        e_hi = jnp.where(chosen[e], float(e), e_hi)
        g_hi = jnp.where(chosen[e], s_rows[e] / denom, g_hi)
    a_lo = e_lo - EXPERTS_PER_GROUP * gidx
    a_hi = e_hi - EXPERTS_PER_GROUP * gidx
    pair_rank = a_lo * (2 * EXPERTS_PER_GROUP - 1 - a_lo) * 0.5 + (a_hi - a_lo - 1.0)
    route_ref[...] = jnp.zeros(route_ref.shape, F32)
    route_ref[ROUTE_G_LO:ROUTE_G_LO + 1, :] = g_lo
    route_ref[ROUTE_G_HI:ROUTE_G_HI + 1, :] = g_hi
    route_ref[ROUTE_CLASS:ROUTE_CLASS + 1, :] = gidx * float(N_PAIRS) + pair_rank


def _outproj_kernel(yf_ref, att_ref, sg_ref, x_ref, mod_ref, wo_ref, lng_ref, lnb_ref,
                    wr_ref, rb_ref, x1_ref, h_ref, route_ref, *, alpha):
    mix = _bdot(yf_ref[...], wo_ref[0:F_WIDTH, :])
    mix = mix + _bdot(att_ref[...], wo_ref[F_WIDTH:F_WIDTH + A_WIDTH, :])
    mix = mix + _bdot(sg_ref[...], wo_ref[F_WIDTH + A_WIDTH:, :])
    g1 = mod_ref[2:3, :]
    x1 = _layernorm(alpha * x_ref[...] + g1 * mix, lng_ref[...], lnb_ref[...])
    x1_ref[...] = x1
    h = x1 * (1.0 + mod_ref[4:5, :]) + mod_ref[3:4, :]
    half = h.shape[1] // 2
    h_ref[...] = _pack_bf16_pair(h[:, :half], h[:, half:])
    h_top = _top_half_bits(h)
    h_hi = h_top.astype(BF16)
    h_lo = (h - h_top).astype(BF16)
    hw = _bdot(h_hi, wr_ref[...])
    logits = hw[:, :ROUTER_LANES] + (hw[:, ROUTER_LANES:] + _bdot(h_lo, wr_ref[:, :ROUTER_LANES]))
    scores = jax.nn.sigmoid(logits)
    sel = scores + rb_ref[...]
    _router_gates(scores.T[0:N_EXPERTS, :], sel.T[0:N_EXPERTS, :], route_ref)


def _outproj(yf, att, sg, xa, mods_l, wo, lng, lnb, wr, rb, n_lat, alpha):
    b, l, d = xa.shape
    tm = TOKEN_TILE
    nlat = n_lat // tm
    seg = lambda i: jnp.where(i >= nlat, 1, 0)
    row = lambda width: pl.BlockSpec((None, tm, width), lambda bb, i: (bb, i, 0))
    const2 = lambda shape: pl.BlockSpec(shape, lambda bb, i: (0, 0))
    return pl.pallas_call(
        functools.partial(_outproj_kernel, alpha=alpha),
        out_shape=(
            jax.ShapeDtypeStruct((b, l, d), F32),
            jax.ShapeDtypeStruct((b, l, d // 2), jnp.uint32),
            jax.ShapeDtypeStruct((b, SUBLANES, l), F32),
        ),
        grid=(b, l // tm),
        in_specs=[
            row(F_WIDTH), row(A_WIDTH), row(G_WIDTH), row(d),
            pl.BlockSpec((None, None, MOD_ROWS, d), lambda bb, i: (bb, seg(i), 0, 0)),
            const2((d, d)), const2((1, d)), const2((1, d)),
            const2((d, 2 * ROUTER_LANES)), const2((1, ROUTER_LANES)),
        ],
        out_specs=(row(d), row(d // 2),
                   pl.BlockSpec((None, SUBLANES, tm), lambda bb, i: (bb, 0, i))),
        compiler_params=_cparams(("parallel", "parallel")),
        name="outproj_ln_router",
    )(yf, att, sg, xa, mods_l, wo, lng, lnb, wr, rb)


def _moe_plan(route, tm):
    b, _, l = route.shape
    t = b * l
    pairs = [(a, c) for a in range(EXPERTS_PER_GROUP) for c in range(a + 1, EXPERTS_PER_GROUP)]
    cls_e1 = np.zeros((N_PAIR_CLASSES,), np.int32)
    cls_e2 = np.zeros((N_PAIR_CLASSES,), np.int32)
    for k, (a, c) in enumerate(pairs):
        assert k == a * (2 * EXPERTS_PER_GROUP - 1 - a) // 2 + (c - a - 1)
        for g in range(N_EXPERT_GROUPS):
            cls_e1[g * N_PAIRS + k] = g * EXPERTS_PER_GROUP + a
            cls_e2[g * N_PAIRS + k] = g * EXPERTS_PER_GROUP + c
    g_lo = route[:, ROUTE_G_LO, :]
    g_hi = route[:, ROUTE_G_HI, :]
    pair_gates = jnp.concatenate([jnp.broadcast_to(g_lo[:, :, None], (b, l, LANES)),
                                  jnp.broadcast_to(g_hi[:, :, None], (b, l, LANES))], axis=-1)
    cls = route[:, ROUTE_CLASS, :].astype(jnp.int32).reshape(t)
    onehot = cls[:, None] == jnp.arange(N_PAIR_CLASSES, dtype=jnp.int32)[None, :]
    oh_tiles = onehot.reshape(t // tm, tm, N_PAIR_CLASSES).astype(BF16)
    tri = jnp.tril(jnp.ones((tm, tm), BF16))
    within = jnp.einsum("ij,njc->nic", tri, oh_tiles, preferred_element_type=F32)
    tile_total = within[:, -1, :]
    n_tiles = t // tm
    earlier = jnp.tril(jnp.ones((n_tiles, n_tiles), BF16), k=-1)
    before = jnp.dot(earlier, tile_total.astype(BF16), preferred_element_type=F32)
    csum = (within + before[:, None, :]).reshape(t, N_PAIR_CLASSES).astype(jnp.int32)
    onehot = onehot.astype(jnp.int32)
    rank = jnp.sum(csum * onehot, axis=1) - 1
    padded = (csum[-1] + tm - 1) // tm * tm
    upto = jnp.tril(jnp.ones((N_PAIR_CLASSES, N_PAIR_CLASSES), jnp.int32))
    off_end = jnp.sum(upto * padded[None, :], axis=1)
    dest = jnp.sum(onehot * (off_end - padded)[None, :], axis=1) + rank
    nt = t // tm + N_PAIR_CLASSES
    n_valid = (off_end[-1] // tm).astype(jnp.int32)
    first_row = jnp.minimum(jnp.arange(nt, dtype=jnp.int32), n_valid - 1) * tm
    tile_cls = jnp.minimum(jnp.searchsorted(off_end, first_row, side="right"), N_PAIR_CLASSES - 1)
    return (dest.astype(jnp.int32), pair_gates, jnp.asarray(cls_e1)[tile_cls],
            jnp.asarray(cls_e2)[tile_cls], n_valid.reshape(1), nt)


def _dispatch_kernel(dest_ref, h_ref, hs_init_ref, hs_ref, stage, sems):
    del hs_init_ref
    s = pl.program_id(0)
    n = pl.num_programs(0)
    tm = h_ref.shape[0]
    slot = s % 2

    def wait_slot(k):
        pltpu.make_async_copy(stage.at[k], hs_ref.at[pl.ds(0, tm), :], sems.at[k]).wait()

    @pl.when(s >= 2)
    def _reuse():
        wait_slot(slot)

    stage[slot] = h_ref[...]
    for r in range(tm):
        pltpu.make_async_copy(stage.at[slot, pl.ds(r, 1), :],
                              hs_ref.at[pl.ds(dest_ref[0, r], 1), :], sems.at[slot]).start()

    @pl.when(s == n - 1)
    def _drain():
        @pl.when(n >= 2)
        def _other():
            wait_slot(1 - slot)
        wait_slot(slot)


def _dispatch(h_packed, dest, nt, tm):
    t, w = h_packed.shape
    hs_init = jnp.zeros((nt * tm, w), jnp.uint32)
    return pl.pallas_call(
        _dispatch_kernel,
        out_shape=jax.ShapeDtypeStruct((nt * tm, w), jnp.uint32),
        grid=(t // tm,),
        in_specs=[
            pl.BlockSpec((None, 1, tm), lambda i: (i, 0, 0), memory_space=pltpu.SMEM),
            pl.BlockSpec((tm, w), lambda i: (i, 0)),
            pl.BlockSpec(memory_space=pl.ANY),
        ],
        out_specs=pl.BlockSpec(memory_space=pl.ANY),
        scratch_shapes=[pltpu.VMEM((2, tm, w), jnp.uint32), pltpu.SemaphoreType.DMA((2,))],
        input_output_aliases={2: 0},
        compiler_params=_cparams(("arbitrary",)),
        name="moe_dispatch",
    )(dest.reshape(t // tm, 1, tm), h_packed, hs_init)


def _expert_pair_kernel(e1_ref, e2_ref, nv_ref, hs_ref, w1a_ref, w3a_ref, w2a_ref,
                        w1b_ref, w3b_ref, w2b_ref, y_ref):
    del e1_ref, e2_ref
    i = pl.program_id(0)

    @pl.when(i < nv_ref[0])
    def _compute():
        lo, hi = _unpack_bf16_pair(hs_ref[...])
        h = jnp.concatenate([lo, hi], axis=1).astype(BF16)

        def ffn(w1_ref, w3_ref, w2_ref):
            a = _bdot(h, w1_ref[...])
            a = (a * jax.nn.sigmoid(a)) * _bdot(h, w3_ref[...])
            return _bdot(a.astype(BF16), w2_ref[...])

        y_ref[...] = _pack_bf16_pair(ffn(w1a_ref, w3a_ref, w2a_ref), ffn(w1b_ref, w3b_ref, w2b_ref))

    @pl.when(i >= nv_ref[0])
    def _unused_tile():
        y_ref[...] = jnp.zeros(y_ref.shape, jnp.uint32)


def _expert_pairs(hs, tile_e1, tile_e2, n_valid, w1, w3, w2, tm):
    rows, w = hs.shape
    _, d, de = w1.shape
    first = lambda shape: pl.BlockSpec(shape, lambda i, e1, e2, nv: (e1[i], 0, 0))
    second = lambda shape: pl.BlockSpec(shape, lambda i, e1, e2, nv: (e2[i], 0, 0))
    grid_spec = pltpu.PrefetchScalarGridSpec(
        num_scalar_prefetch=3,
        grid=(rows // tm,),
        in_specs=[
            pl.BlockSpec((tm, w), lambda i, e1, e2, nv: (jnp.minimum(i, nv[0] - 1), 0)),
            first((None, d, de)), first((None, d, de)), first((None, de, d)),
            second((None, d, de)), second((None, d, de)), second((None, de, d)),
        ],
        out_specs=pl.BlockSpec((tm, d), lambda i, e1, e2, nv: (i, 0)),
    )
    return pl.pallas_call(
        _expert_pair_kernel,
        out_shape=jax.ShapeDtypeStruct((rows, d), jnp.uint32),
        grid_spec=grid_spec,
        compiler_params=_cparams(("arbitrary",)),
        name="moe_expert_pairs",
    )(tile_e1, tile_e2, n_valid, hs, w1, w3, w2, w1, w3, w2)


def _combine_kernel(dest_ref, dest_next_ref, y2_ref, x_ref, pg_ref, mod_ref, lng_ref, lnb_ref,
                    o_ref, ybuf, sems, *, alpha):
    s = pl.program_id(0)
    n = pl.num_programs(0)
    tm, d = x_ref.shape
    slot = s % 2

    def start_gather(idx_ref, k):
        for r in range(tm):
            pltpu.make_async_copy(y2_ref.at[pl.ds(idx_ref[0, r], 1), :],
                                  ybuf.at[k, pl.ds(r, 1), :], sems.at[k]).start()

    @pl.when(s == 0)
    def _first():
        start_gather(dest_ref, 0)

    @pl.when(s + 1 < n)
    def _prefetch():
        start_gather(dest_next_ref, 1 - slot)

    pltpu.make_async_copy(y2_ref.at[pl.ds(0, tm), :], ybuf.at[slot], sems.at[slot]).wait()
    y_lo, y_hi = _unpack_bf16_pair(ybuf[slot])
    pg = pg_ref[...]
    reps = d // LANES
    g_lo = jnp.concatenate([pg[:, :LANES]] * reps, axis=1)
    g_hi = jnp.concatenate([pg[:, LANES:]] * reps, axis=1)
    y = g_lo * y_lo + g_hi * y_hi
    o_ref[...] = _layernorm(alpha * x_ref[...] + mod_ref[5:6, :] * y, lng_ref[...], lnb_ref[...])


def _combine(y2, dest, x1, pair_gates, mods_l, lng, lnb, *, n_rows, n_lat, tm, alpha):
    b, l, d = x1.shape
    nlat = n_lat // tm
    tiles_per_batch = l // tm
    per_batch = n_rows // tm
    n_steps = b * per_batch
    batch = lambda s: s // per_batch
    tile = lambda s: s % per_batch
    seg = lambda s: jnp.where(tile(s) >= nlat, 1, 0)
    token_tile = lambda s: batch(s) * tiles_per_batch + tile(s)
    row = lambda width: pl.BlockSpec((None, tm, width), lambda s: (batch(s), tile(s), 0))
    const2 = lambda shape: pl.BlockSpec(shape, lambda s: (0, 0))
    dest3 = dest.reshape(b * tiles_per_batch, 1, tm)
    return pl.pallas_call(
        functools.partial(_combine_kernel, alpha=alpha),
        out_shape=jax.ShapeDtypeStruct((b, n_rows, d), F32),
        grid=(n_steps,),
        in_specs=[
            pl.BlockSpec((None, 1, tm), lambda s: (token_tile(s), 0, 0), memory_space=pltpu.SMEM),
            pl.BlockSpec((None, 1, tm), lambda s: (token_tile(jnp.minimum(s + 1, n_steps - 1)), 0, 0),
                         memory_space=pltpu.SMEM),
            pl.BlockSpec(memory_space=pl.ANY),
            row(d), row(2 * LANES),
            pl.BlockSpec((None, None, MOD_ROWS, d), lambda s: (batch(s), seg(s), 0, 0)),
            const2((1, d)), const2((1, d)),
        ],
        out_specs=row(d),
        scratch_shapes=[pltpu.VMEM((2, tm, d), jnp.uint32), pltpu.SemaphoreType.DMA((2,))],
        compiler_params=_cparams(("arbitrary",)),
        name="moe_combine_ln",
    )(dest3, dest3, y2, x1, pair_gates, mods_l, lng, lnb)


def _head_lane_fields():
    j = jnp.arange(LANES)
    half = j // (LANES // 2)
    comp = (j % (LANES // 2)) // (A_DH // 2)
    axis = (j % (A_DH // 2)) // (A_DH // 4)
    freq = j % (A_DH // 4)
    return half, comp, axis, freq


def _qk_column_perm():
    half, comp, axis, freq = _head_lane_fields()
    orig = comp * A_DH + axis * (A_DH // 2) + half * (A_DH // 4) + freq
    return (jnp.arange(A_HEADS)[:, None] * LANES + orig[None, :]).reshape(-1)


def _rope_tables(n_lat, n_ctx):
    half, _, axis, freq = _head_lane_fields()
    rows = n_lat // GRID_W
    row = jnp.repeat(jnp.arange(rows, dtype=F32), GRID_W)
    col = jnp.tile(jnp.arange(GRID_W, dtype=F32), rows)
    hd = A_DH // 2
    inv = ROPE_BASE ** (-jnp.arange(0, hd, 2, dtype=F32) / hd)
    pos = jnp.where(axis[None, :] == 0, row[:, None], col[:, None])
    ang = pos * inv[freq][None, :]
    cosf = jnp.cos(ang)
    sinf = jnp.sin(ang) * jnp.where(half == 0, -1.0, 1.0)[None, :]
    cosf = jnp.concatenate([cosf, jnp.ones((n_ctx, LANES), F32)], axis=0)
    sinf = jnp.concatenate([sinf, jnp.zeros((n_ctx, LANES), F32)], axis=0)
    return cosf, sinf


def _dft_mats(n, scale=1.0):
    i = jnp.arange(n, dtype=jnp.int32)
    ang = ((i[:, None] * i[None, :]) % n).astype(F32) * (2.0 * math.pi / n)
    return jnp.cos(ang) * scale, jnp.sin(ang) * scale


def _twiddles(n, bn):
    ka = jnp.arange(DFT_A, dtype=jnp.int32)
    bb = jnp.arange(bn, dtype=jnp.int32)
    ang = ((ka[:, None] * bb[None, :]) % n).astype(F32) * (2.0 * math.pi / n)
    shape = (DFT_A, bn, F_WIDTH)
    twc = jnp.broadcast_to(jnp.cos(ang)[:, :, None], shape).reshape(DFT_A, bn * F_WIDTH)
    tws = jnp.broadcast_to(jnp.sin(ang)[:, :, None], shape).reshape(DFT_A, bn * F_WIDTH)
    return twc, tws


def _block_diag(blocks):
    g, n = blocks.shape[-3], blocks.shape[-1]
    eye = jnp.eye(g, dtype=blocks.dtype)
    out = blocks[..., :, :, None, :] * eye[:, None, :, None]
    return out.reshape(blocks.shape[:-3] + (g * n, g * n))


def kernel(x, c, ctx, c_ctx, w_ada, b_ada, w_in, w_fourier, diff_lambda, diff_subln,
           sgu_ln_g, sgu_ln_b, sgu_w, sgu_b, w_out, ln_g, ln_b, w_router, router_bias,
           moe_w1, moe_w3, moe_w2):
    b, n, d = x.shape
    n_ctx = ctx.shape[1]
    depth = w_ada.shape[0]
    l = n + n_ctx
    alpha = (2 * depth) ** 0.25
    bn = n // DFT_A
    assert n % DFT_A == 0 and bn % SUBLANES == 0 and n % GRID_W == 0
    assert n_ctx % TOKEN_TILE == 0 and n % TOKEN_TILE == 0 and n % n_ctx == 0
    assert b + 1 <= MOD_ROWS

    c_rows = jnp.concatenate([c, c_ctx[None, :], jnp.zeros((MOD_ROWS - b - 1, d), F32)], axis=0)
    mod = _ada(c_rows, w_ada, b_ada).reshape(depth, MOD_ROWS, 6, d)
    lat = mod[:, :b]
    cm = jnp.broadcast_to(mod[:, b:b + 1], lat.shape)
    mods = jnp.stack([lat, cm], axis=2)
    mods = jnp.pad(mods, ((0, 0), (0, 0), (0, 0), (0, MOD_ROWS - 6), (0, 0)))

    perm = _qk_column_perm()
    c64, s64 = _dft_mats(F_DIM)
    eye_g = jnp.eye(F_GROUPS, dtype=F32)
    bdc = jnp.kron(eye_g, c64)
    bds = jnp.kron(eye_g, s64)
    bdw = _block_diag(w_fourier)
    w_fab = _fourier_weights(w_in[:, :, :F_WIDTH], bdc, bds, bdw)
    o_q = F_WIDTH
    o_k = o_q + A_QK_WIDTH
    o_v = o_k + A_QK_WIDTH
    o_g = o_v + A_WIDTH
    w_big = jnp.concatenate([
        w_fab,
        w_in[:, :, o_q:o_k][:, :, perm],
        w_in[:, :, o_k:o_v][:, :, perm],
        w_in[:, :, o_v:],
    ], axis=-1).astype(BF16)
    wo = w_out.astype(BF16)
    w1 = moe_w1.astype(BF16)
    w3 = moe_w3.astype(BF16)
    w2 = moe_w2.astype(BF16)
    ws = sgu_w.astype(BF16)
    bs_full = jnp.repeat(jnp.swapaxes(sgu_b, 1, 2), G_DIM, axis=2)
    wr_f32 = jnp.pad(w_router, ((0, 0), (0, ROUTER_LANES - N_EXPERTS)))
    wr_top = _top_half_bits(wr_f32)
    wr = jnp.concatenate([wr_top.astype(BF16), (wr_f32 - wr_top).astype(BF16)], axis=1)
    rb =jnp.pad(router_bias, (0, ROUTER_LANES - N_EXPERTS)).reshape(1, ROUTER_LANES)

    cosf, sinf = _rope_tables(n, n_ctx)
    ca, sa = _dft_mats(DFT_A)
    l_a = jnp.concatenate([_stack3([ca, -sa]), _stack3([-sa, -ca])], axis=0)
    twc, tws = _twiddles(n, bn)
    l_b = _stack3(list(_dft_mats(bn, scale=(n * F_DIM) ** -0.5)))
    cc, sc = _dft_mats(n_ctx, scale=(n_ctx * F_DIM) ** -0.5)
    l_c = _stack3([cc, -sc])
    q_scale = (A_DH ** -0.5) * math.log2(math.e)

    tk = ATTN_TK
    assert l % tk == 0
    assert n % MOE_TILE == 0 and n_ctx % MOE_TILE == 0

    xa = jnp.concatenate([x, ctx], axis=1)
    for li in range(depth):
        last = li == depth - 1
        lam_init = 0.8 - 0.6 * math.exp(-0.3 * li)
        fa, fb, q, kt, v_ext, sg = _inproj(
            xa, mods[li], w_big[li], cosf, sinf, sgu_ln_g[li][None], sgu_ln_b[li][None],
            ws[li], bs_full[li], n, q_scale)
        zr, zi = _fft_a(fa, fb, l_a, twc, tws, bn)
        yf = jnp.concatenate([_fft_b(zr, zi, l_b, bn), _dft_ctx(fa, fb, l_c, n, n_ctx)], axis=1)
        gain = diff_subln[li][None]
        att_x = _attention(q, kt, v_ext, diff_lambda[li], gain, q_block0=0, n_q=n, key_block0=0,
                           n_keys=l, tq=ATTN_TQ, tk=tk, group=min(ATTN_GROUP, n), lam_init=lam_init)
        att_c = _attention(q, kt, v_ext, diff_lambda[li], gain, q_block0=n // n_ctx, n_q=n_ctx,
                           key_block0=n // n_ctx, n_keys=n_ctx, tq=n_ctx, tk=n_ctx, group=n_ctx,
                           lam_init=lam_init)
        att = jnp.concatenate([att_x, att_c], axis=1)
        x1, h_packed, route = _outproj(
            yf, att, sg, xa, mods[li], wo[li], ln_g[li, 0][None], ln_b[li, 0][None], wr, rb, n, alpha)
        dest, pair_gates, tile_e1, tile_e2, n_valid, nt = _moe_plan(route, MOE_TILE)
        hs = _dispatch(h_packed.reshape(b * l, d // 2), dest, nt, MOE_TILE)
        y2 = _expert_pairs(hs, tile_e1, tile_e2, n_valid, w1[li], w3[li], w2[li], MOE_TILE)
        xa = _combine(y2, dest, x1, pair_gates, mods[li], ln_g[li, 1][None], ln_b[li, 1][None],
                      n_rows=n if last else l, n_lat=n, tm=MOE_TILE, alpha=alpha)
    return xa
```

```python
import functools
import math

import numpy as np
import jax
import jax.numpy as jnp
from jax import lax
from jax.experimental import pallas as pl
from jax.experimental.pallas import tpu as pltpu

F32 = jnp.float32
BF16 = jnp.bfloat16
HIGHEST = lax.Precision.HIGHEST

GRID_W = 64
F_GROUPS, F_DIM = 4, 64
F_WIDTH = F_GROUPS * F_DIM
A_HEADS, A_DH = 4, 64
A_DV = 2 * A_DH
A_QK_WIDTH = A_HEADS * 2 * A_DH
A_WIDTH = A_HEADS * A_DV
G_GROUPS, G_DIM = 4, 64
G_WIDTH = G_GROUPS * G_DIM
CHUNK = 128
ROPE_BASE = 10000.0
N_EXPERTS = 16
N_EXPERT_GROUPS = 4
EXPERTS_PER_GROUP = N_EXPERTS // N_EXPERT_GROUPS
LN_EPS = 1e-5
RMS_EPS = 1e-5

COL_FA = 0
COL_FB = COL_FA + F_WIDTH
COL_Q = COL_FB + F_WIDTH
COL_K = COL_Q + A_QK_WIDTH
COL_V = COL_K + A_QK_WIDTH
COL_G = COL_V + A_WIDTH
BIG_COLS = COL_G + 2 * G_WIDTH

LANES = 128
SUBLANES = 8
MOD_ROWS = 8
ROUTER_LANES = 128
DFT_A = 128
VMEM_LIMIT = 56 * 1024 * 1024

TOKEN_TILE = 256
ATTN_TQ = 256
ATTN_GROUP = 8192
ATTN_TK = 1280
MOE_TILE = 256


def _cparams(sem):
    return pltpu.CompilerParams(dimension_semantics=sem, vmem_limit_bytes=VMEM_LIMIT)


def _hdot(a, b):
    return jnp.dot(a, b, precision=HIGHEST, preferred_element_type=F32)


def _bdot(a, b):
    return jnp.dot(a, b, preferred_element_type=F32)


def _layernorm(y, g, b):
    mu = jnp.mean(y, axis=-1, keepdims=True)
    d = y - mu
    var = jnp.mean(d * d, axis=-1, keepdims=True)
    return d * lax.rsqrt(var + LN_EPS) * g + b


def _ada_kernel(c_ref, w_ref, b_ref, o_ref):
    c = c_ref[...]
    a = c * jax.nn.sigmoid(c)
    o_ref[...] = _hdot(a, w_ref[...]) + b_ref[...]


def _ada(c_rows, w_ada, b_ada):
    depth, d, cols = w_ada.shape
    tn = 1536
    assert cols % tn == 0
    return pl.pallas_call(
        _ada_kernel,
        out_shape=jax.ShapeDtypeStruct((depth, MOD_ROWS, cols), F32),
        grid=(depth, cols // tn),
        in_specs=[
            pl.BlockSpec((MOD_ROWS, d), lambda l, j: (0, 0)),
            pl.BlockSpec((None, d, tn), lambda l, j: (l, 0, j)),
            pl.BlockSpec((None, 1, tn), lambda l, j: (l, 0, j)),
        ],
        out_specs=pl.BlockSpec((None, MOD_ROWS, tn), lambda l, j: (l, 0, j)),
        compiler_params=_cparams(("parallel", "parallel")),
        name="ada_mod",
    )(c_rows, w_ada, b_ada.reshape(depth, 1, cols))


def _fw_kernel(wf_ref, bdc_ref, bds_ref, bdw_ref, o_ref):
    bdw = bdw_ref[...]
    mc = _hdot(bdc_ref[...], bdw)
    ms = _hdot(bds_ref[...], bdw)
    wf = wf_ref[...]
    o_ref[:, :F_WIDTH] = _hdot(wf, mc)
    o_ref[:, F_WIDTH:] = _hdot(wf, ms)


def _fourier_weights(wf, bdc, bds, bdw):
    depth, d, _ = wf.shape
    return pl.pallas_call(
        _fw_kernel,
        out_shape=jax.ShapeDtypeStruct((depth, d, 2 * F_WIDTH), F32),
        grid=(depth,),
        in_specs=[
            pl.BlockSpec((None, d, F_WIDTH), lambda l: (l, 0, 0)),
            pl.BlockSpec((F_WIDTH, F_WIDTH), lambda l: (0, 0)),
            pl.BlockSpec((F_WIDTH, F_WIDTH), lambda l: (0, 0)),
            pl.BlockSpec((None, F_WIDTH, F_WIDTH), lambda l: (l, 0, 0)),
        ],
        out_specs=pl.BlockSpec((None, d, 2 * F_WIDTH), lambda l: (l, 0, 0)),
        compiler_params=_cparams(("parallel",)),
        name="fourier_weights",
    )(wf, bdc, bds, bdw)


def _gelu_tanh(x):
    c = math.sqrt(2.0 / math.pi)
    return x * (0.5 * (1.0 + jnp.tanh(c * (x + 0.044715 * (x * x * x)))))


def _inproj_kernel(x_ref, *refs, q_scale):
    _inproj_body(x_ref[...], *refs, q_scale=q_scale)


def _inproj_body(x, mod_ref, w_ref, cos_ref, sin_ref, lng_ref, lnb_ref, ws_ref, bs_ref,
                 fa_ref, fb_ref, q_ref, kt_ref, v_ref, sg_ref, *, q_scale):
    sh = mod_ref[0:1, :]
    sc = mod_ref[1:2, :]
    h = (x * (1.0 + sc) + sh).astype(BF16)
    p = _bdot(h, w_ref[...])
    fa_ref[...] = p[:, COL_FA:COL_FA + F_WIDTH]
    fb_ref[...] = p[:, COL_FB:COL_FB + F_WIDTH]
    cosf = cos_ref[...]
    sinf = sin_ref[...]
    for hh in range(A_HEADS):
        t = p[:, COL_Q + LANES * hh:COL_Q + LANES * (hh + 1)]
        r = t * cosf + pltpu.roll(t, LANES // 2, 1) * sinf
        q_ref[:, LANES * hh:LANES * (hh + 1)] = (r * q_scale).astype(BF16)
        t = p[:, COL_K + LANES * hh:COL_K + LANES * (hh + 1)]
        r = t * cosf + pltpu.roll(t, LANES // 2, 1) * sinf
        kt_ref[hh] = r.T.astype(BF16)
        v_ref[:, 2 * LANES * hh:2 * LANES * hh + LANES] = (
            p[:, COL_V + LANES * hh:COL_V + LANES * (hh + 1)].astype(BF16))
        v_ref[:, 2 * LANES * hh + LANES:2 * LANES * (hh + 1)] = jnp.ones((t.shape[0], LANES), BF16)

    g = _gelu_tanh(p[:, COL_G:COL_G + 2 * G_WIDTH])
    u = g[:, :G_WIDTH]
    vn = _layernorm(g[:, G_WIDTH:], lng_ref[...], lnb_ref[...])
    lane_group = lax.broadcasted_iota(jnp.int32, (CHUNK, G_WIDTH), 1) // G_DIM
    tm = x.shape[0]
    for c in range(tm // CHUNK):
        rows = slice(c * CHUNK, (c + 1) * CHUNK)
        vc = vn[rows]
        mixed = bs_ref[...]
        for gi in range(G_GROUPS):
            vm = jnp.where(lane_group == gi, vc, 0.0).astype(BF16)
            mixed = mixed + _bdot(ws_ref[gi], vm)
        sg_ref[rows, :] = (u[rows] * mixed).astype(BF16)


def _inproj_out_shapes(b, l):
    return (
        jax.ShapeDtypeStruct((b, l, F_WIDTH), F32),
        jax.ShapeDtypeStruct((b, l, F_WIDTH), F32),
        jax.ShapeDtypeStruct((b, l, A_QK_WIDTH), BF16),
        jax.ShapeDtypeStruct((b, A_HEADS, A_DV, l), BF16),
        jax.ShapeDtypeStruct((b, l, 2 * A_WIDTH), BF16),
        jax.ShapeDtypeStruct((b, l, G_WIDTH), BF16),
    )


def _inproj(xa, mods_l, w_big, cosf, sinf, lng, lnb, ws, bs_full, n_lat, q_scale):
    b, l, d = xa.shape
    tm = TOKEN_TILE
    nlat = n_lat // tm
    seg = lambda i: jnp.where(i >= nlat, 1, 0)
    row = lambda width: pl.BlockSpec((None, tm, width), lambda bb, i: (bb, i, 0))
    const2 = lambda shape: pl.BlockSpec(shape, lambda bb, i: (0, 0))
    return pl.pallas_call(
        functools.partial(_inproj_kernel, q_scale=q_scale),
        out_shape=_inproj_out_shapes(b, l),
        grid=(b, l // tm),
        in_specs=[
            row(d),
            pl.BlockSpec((None, None, MOD_ROWS, d), lambda bb, i: (bb, seg(i), 0, 0)),
            const2((d, BIG_COLS)),
            pl.BlockSpec((tm, LANES), lambda bb, i: (i, 0)),
            pl.BlockSpec((tm, LANES), lambda bb, i: (i, 0)),
            const2((1, G_WIDTH)),
            const2((1, G_WIDTH)),
            pl.BlockSpec((G_GROUPS, CHUNK, CHUNK), lambda bb, i: (0, 0, 0)),
            const2((CHUNK, G_WIDTH)),
        ],
        out_specs=(row(F_WIDTH), row(F_WIDTH), row(A_QK_WIDTH),
                   pl.BlockSpec((None, A_HEADS, A_DV, tm), lambda bb, i: (bb, 0, 0, i)),
                   row(2 * A_WIDTH), row(G_WIDTH)),
        compiler_params=_cparams(("parallel", "parallel")),
        name="inproj",
    )(xa, mods_l, w_big, cosf, sinf, lng, lnb, ws, bs_full)


def _split3(x):
    top = _top_half_bits(x)
    hi = top.astype(BF16)
    return [hi, (x - top).astype(BF16), hi]


def _stack3(mats):
    cols = []
    for m in mats:
        top = _top_half_bits(m)
        cols += [top.astype(BF16), top.astype(BF16), (m - top).astype(BF16)]
    return jnp.concatenate(cols, axis=1)


def _fft_a_kernel(fa_ref, fb_ref, l_ref, tc_ref, ts_ref, zr_ref, zi_ref):
    rhs = jnp.concatenate(_split3(fa_ref[...]) + _split3(fb_ref[...]), axis=0)
    z = _bdot(l_ref[...], rhs)
    zr = z[:DFT_A]
    zi = z[DFT_A:]
    tc = tc_ref[...]
    ts = ts_ref[...]
    zr_ref[...] = zr * tc + zi * ts
    zi_ref[...] = zi * tc - zr * ts


def _fft_a(fa, fb, l_a, twc, tws, bn):
    b, l, _ = fa.shape
    cols = bn * F_WIDTH
    fa3 = fa.reshape(b, l // bn, cols)
    fb3 = fb.reshape(b, l // bn, cols)
    tcw = min(cols, 2048)
    assert cols % tcw == 0
    blk = pl.BlockSpec((None, DFT_A, tcw), lambda bb, j: (bb, 0, j))
    tw = pl.BlockSpec((DFT_A, tcw), lambda bb, j: (0, j))
    zr, zi = pl.pallas_call(
        _fft_a_kernel,
        out_shape=(jax.ShapeDtypeStruct((b, DFT_A, cols), F32),) * 2,
        grid=(b, cols // tcw),
        in_specs=[blk, blk, pl.BlockSpec(l_a.shape, lambda bb, j: (0, 0)), tw, tw],
        out_specs=(blk, blk),
        compiler_params=_cparams(("parallel", "parallel")),
        name="fft_stage_a",
    )(fa3, fb3, l_a, twc, tws)
    return zr.reshape(b, DFT_A, bn, F_WIDTH), zi.reshape(b, DFT_A, bn, F_WIDTH)


def _fft_b_kernel(zr_ref, zi_ref, l_ref, o_ref):
    lm = l_ref[...]
    for j in range(zr_ref.shape[0]):
        rhs = jnp.concatenate(_split3(zr_ref[j]) + _split3(zi_ref[j]), axis=0)
        o_ref[:, F_WIDTH * j:F_WIDTH * (j + 1)] = _bdot(lm, rhs).astype(o_ref.dtype)


def _fft_b(zr4, zi4, l_b, bn):
    b = zr4.shape[0]
    kb = 8
    blk = pl.BlockSpec((None, kb, bn, F_WIDTH), lambda bb, i: (bb, i, 0, 0))
    out = pl.pallas_call(
        _fft_b_kernel,
        out_shape=jax.ShapeDtypeStruct((b, bn, DFT_A * F_WIDTH), BF16),
        grid=(b, DFT_A // kb),
        in_specs=[blk, blk, pl.BlockSpec(l_b.shape, lambda bb, i: (0, 0))],
        out_specs=pl.BlockSpec((None, bn, kb * F_WIDTH), lambda bb, i: (bb, 0, i)),
        compiler_params=_cparams(("parallel", "parallel")),
        name="fft_stage_b",
    )(zr4, zi4, l_b)
    return out.reshape(b, bn * DFT_A, F_WIDTH)


def _dft_ctx_kernel(fa_ref, fb_ref, l_ref, o_ref):
    rhs = jnp.concatenate(_split3(fa_ref[...]) + _split3(fb_ref[...]), axis=0)
    o_ref[...] = _bdot(l_ref[...], rhs).astype(o_ref.dtype)


def _dft_ctx(fa, fb, l_c, n_lat, n_ctx):
    b = fa.shape[0]
    blk = pl.BlockSpec((None, n_ctx, F_WIDTH), lambda bb: (bb, n_lat // n_ctx, 0))
    return pl.pallas_call(
        _dft_ctx_kernel,
        out_shape=jax.ShapeDtypeStruct((b, n_ctx, F_WIDTH), BF16),
        grid=(b,),
        in_specs=[blk, blk, pl.BlockSpec(l_c.shape, lambda bb: (0, 0))],
        out_specs=pl.BlockSpec((None, n_ctx, F_WIDTH), lambda bb: (bb, 0, 0)),
        compiler_params=_cparams(("parallel",)),
        name="dft_ctx",
    )(fa, fb, l_c)


def _score_slot(j):
    return 2 if j == 0 else (j - 1) % 2


def _attn_kernel(q_ref, kt_ref, v_ref, lam_ref, gain_ref, o_ref, q2_ref, s_ref, m_ref, acc_ref,
                 *, lam_init, tq, tk):
    nsub = q_ref.shape[0] // tq
    nk = kt_ref.shape[1] // tk

    def keys(j):
        return slice(j * tk, (j + 1) * tk)
    lane = lax.broadcasted_iota(jnp.int32, (tq, A_DV), 1)
    comp0 = (lane % A_DH) < (A_DH // 2)
    lp = lam_ref[...]
    lam = (jnp.exp(jnp.sum(lp[0:1] * lp[1:2], keepdims=True))
           - jnp.exp(jnp.sum(lp[2:3] * lp[3:4], keepdims=True)) + lam_init)
    out_gain = gain_ref[...] * (1.0 - lam_init)

    def rows(i):
        return pl.ds(pl.multiple_of(i * tq, tq), tq)

    def load_q2(i):
        q = q_ref[rows(i), :]
        zero = jnp.zeros_like(q)
        q2_ref[0:tq, :] = jnp.where(comp0, q, zero)
        q2_ref[tq:2 * tq, :] = jnp.where(comp0, zero, q)

    def scores(j):
        s_ref[_score_slot(j)] = _bdot(q2_ref[...], kt_ref[:, keys(j)])

    def softmax_pv(j):
        s = s_ref[_score_slot(j)]
        m_old = m_ref[...]
        m_new = jnp.maximum(m_old, jnp.max(s, axis=1, keepdims=True))
        alpha = jnp.exp2(m_old - m_new)
        p = jnp.exp2(s - jnp.concatenate([m_new] * (tk // LANES), axis=1)).astype(BF16)
        pv = _bdot(p, v_ref[keys(j), :])
        acc_ref[...] = acc_ref[...] * jnp.concatenate([alpha, alpha], axis=1) + pv
        m_ref[...] = m_new

    def finalize(i):
        acc = acc_ref[...]
        o0 = acc[0:tq, 0:LANES] / acc[0:tq, LANES:]
        o1 = acc[tq:, 0:LANES] / acc[tq:, LANES:]
        o = o0 - lam * o1
        ms = jnp.mean(o * o, axis=-1, keepdims=True)
        o_ref[rows(i), :] = (o * lax.rsqrt(ms + RMS_EPS) * out_gain).astype(o_ref.dtype)

    def next_tile_first_scores(i):
        load_q2(jnp.minimum(i + 1, nsub - 1))
        scores(0)

    load_q2(0)
    scores(0)

    def tile(i, carry):
        m_ref[...] = jnp.full(m_ref.shape, -jnp.inf, F32)
        acc_ref[...] = jnp.zeros(acc_ref.shape, F32)
        for j in range(nk):
            if j + 1 < nk:
                scores(j + 1)
                softmax_pv(j)
            elif nk > 1:
                next_tile_first_scores(i)
                softmax_pv(j)
            else:
                softmax_pv(j)
                next_tile_first_scores(i)
        finalize(i)
        return carry

    lax.fori_loop(0, nsub, tile, 0)


def _attention(q, kt, v_ext, lam_p, gain, *, q_block0, n_q, key_block0, n_keys, tq, tk, group,
               lam_init):
    b = q.shape[0]
    assert n_q % group == 0 and group % tq == 0 and n_keys % tk == 0
    return pl.pallas_call(
        functools.partial(_attn_kernel, lam_init=lam_init, tq=tq, tk=tk),
        out_shape=jax.ShapeDtypeStruct((b, n_q, A_WIDTH), BF16),
        grid=(b, A_HEADS, n_q // group),
        in_specs=[
            pl.BlockSpec((None, group, A_DV), lambda bb, h, i: (bb, q_block0 + i, h)),
            pl.BlockSpec((None, None, A_DV, n_keys), lambda bb, h, i: (bb, h, 0, key_block0)),
            pl.BlockSpec((None, n_keys, 2 * LANES), lambda bb, h, i: (bb, key_block0, h)),
            pl.BlockSpec((4, A_DH), lambda bb, h, i: (0, 0)),
            pl.BlockSpec((1, A_DV), lambda bb, h, i: (0, 0)),
        ],
        out_specs=pl.BlockSpec((None, group, A_DV), lambda bb, h, i: (bb, i, h)),
        scratch_shapes=[
            pltpu.VMEM((2 * tq, A_DV), BF16),
            pltpu.VMEM((3, 2 * tq, tk), F32),
            pltpu.VMEM((2 * tq, LANES), F32),
            pltpu.VMEM((2 * tq, 2 * LANES), F32),
        ],
        compiler_params=_cparams(("parallel", "parallel", "arbitrary")),
        name="diff_attn",
    )(q, kt, v_ext, lam_p, gain)


_HI16 = 0xFFFF0000


def _pack_bf16_pair(lo, hi):
    ulo = lax.bitcast_convert_type(lo.astype(BF16).astype(F32), jnp.uint32) >> 16
    uhi = lax.bitcast_convert_type(hi.astype(BF16).astype(F32), jnp.uint32) & jnp.uint32(_HI16)
    return ulo | uhi


def _top_half_bits(x):
    return lax.bitcast_convert_type(lax.bitcast_convert_type(x, jnp.uint32) & jnp.uint32(_HI16), F32)


def _unpack_bf16_pair(u):
    lo = lax.bitcast_convert_type(u << 16, F32)
    hi = lax.bitcast_convert_type(u & jnp.uint32(_HI16), F32)
    return lo, hi


ROUTE_G_LO, ROUTE_G_HI, ROUTE_CLASS = 0, 1, 2
N_PAIRS = EXPERTS_PER_GROUP * (EXPERTS_PER_GROUP - 1) // 2
N_PAIR_CLASSES = N_EXPERT_GROUPS * N_PAIRS


def _router_gates(sc_t, sel_t, route_ref):
    s_rows = [sc_t[e:e + 1, :] for e in range(N_EXPERTS)]
    v_rows = [sel_t[e:e + 1, :] for e in range(N_EXPERTS)]
    in_top2 = []
    group_score = []
    for g in range(N_EXPERT_GROUPS):
        vs = v_rows[g * EXPERTS_PER_GROUP:(g + 1) * EXPERTS_PER_GROUP]
        tops = []
        for jj in range(EXPERTS_PER_GROUP):
            rank = jnp.zeros_like(vs[jj])
            for ii in range(EXPERTS_PER_GROUP):
                if ii == jj:
                    continue
                beats = (vs[ii] >= vs[jj]) if ii < jj else (vs[ii] > vs[jj])
                rank = rank + jnp.where(beats, 1.0, 0.0)
            tops.append(rank < 2.0)
        in_top2 += tops
        gs = jnp.zeros_like(vs[0])
        for jj in range(EXPERTS_PER_GROUP):
            gs = gs + jnp.where(tops[jj], vs[jj], 0.0)
        group_score.append(gs)
    best = group_score[0]
    gidx = jnp.zeros_like(best)
    for g in range(1, N_EXPERT_GROUPS):
        upd = group_score[g] > best
        best = jnp.where(upd, group_score[g], best)
        gidx = jnp.where(upd, float(g), gidx)
    chosen = [jnp.logical_and(in_top2[e], gidx == float(e // EXPERTS_PER_GROUP))
              for e in range(N_EXPERTS)]
    denom = jnp.zeros_like(best)
    for e in range(N_EXPERTS):
        denom = denom + jnp.where(chosen[e], s_rows[e], 0.0)
    e_lo = jnp.zeros_like(best)
    g_lo = jnp.zeros_like(best)
    for e in reversed(range(N_EXPERTS)):
        e_lo = jnp.where(chosen[e], float(e), e_lo)
        g_lo = jnp.where(chosen[e], s_rows[e] / denom, g_lo)
    e_hi = jnp.zeros_like(best)
    g_hi = jnp.zeros_like(best)
    for e in range(N_EXPERTS):
        e_hi = jnp.where(chosen[e], float(e), e_hi)
        g_hi = jnp.where(chosen[e], s_rows[e] / denom, g_hi)
    a_lo = e_lo - EXPERTS_PER_GROUP * gidx
    a_hi = e_hi - EXPERTS_PER_GROUP * gidx
    pair_rank = a_lo * (2 * EXPERTS_PER_GROUP - 1 - a_lo) * 0.5 + (a_hi - a_lo - 1.0)
    route_ref[...] = jnp.zeros(route_ref.shape, F32)
    route_ref[ROUTE_G_LO:ROUTE_G_LO + 1, :] = g_lo
    route_ref[ROUTE_G_HI:ROUTE_G_HI + 1, :] = g_hi
    route_ref[ROUTE_CLASS:ROUTE_CLASS + 1, :] = gidx * float(N_PAIRS) + pair_rank


def _outproj_kernel(yf_ref, att_ref, sg_ref, x_ref, mod_ref, wo_ref, lng_ref, lnb_ref,
                    wr_ref, rb_ref, x1_ref, h_ref, route_ref, *, alpha):
    mix = _bdot(yf_ref[...], wo_ref[0:F_WIDTH, :])
    mix = mix + _bdot(att_ref[...], wo_ref[F_WIDTH:F_WIDTH + A_WIDTH, :])
    mix = mix + _bdot(sg_ref[...], wo_ref[F_WIDTH + A_WIDTH:, :])
    g1 = mod_ref[2:3, :]
    x1 = _layernorm(alpha * x_ref[...] + g1 * mix, lng_ref[...], lnb_ref[...])
    x1_ref[...] = x1
    h = x1 * (1.0 + mod_ref[4:5, :]) + mod_ref[3:4, :]
    half = h.shape[1] // 2
    h_ref[...] = _pack_bf16_pair(h[:, :half], h[:, half:])
    h_top = _top_half_bits(h)
    h_hi = h_top.astype(BF16)
    h_lo = (h - h_top).astype(BF16)
    hw = _bdot(h_hi, wr_ref[...])
    logits = hw[:, :ROUTER_LANES] + (hw[:, ROUTER_LANES:] + _bdot(h_lo, wr_ref[:, :ROUTER_LANES]))
    scores = jax.nn.sigmoid(logits)
    sel = scores + rb_ref[...]
    _router_gates(scores.T[0:N_EXPERTS, :], sel.T[0:N_EXPERTS, :], route_ref)


def _outproj(yf, att, sg, xa, mods_l, wo, lng, lnb, wr, rb, n_lat, alpha):
    b, l, d = xa.shape
    tm = TOKEN_TILE
    nlat = n_lat // tm
    seg = lambda i: jnp.where(i >= nlat, 1, 0)
    row = lambda width: pl.BlockSpec((None, tm, width), lambda bb, i: (bb, i, 0))
    const2 = lambda shape: pl.BlockSpec(shape, lambda bb, i: (0, 0))
    return pl.pallas_call(
        functools.partial(_outproj_kernel, alpha=alpha),
        out_shape=(
            jax.ShapeDtypeStruct((b, l, d), F32),
            jax.ShapeDtypeStruct((b, l, d // 2), jnp.uint32),
            jax.ShapeDtypeStruct((b, SUBLANES, l), F32),
        ),
        grid=(b, l // tm),
        in_specs=[
            row(F_WIDTH), row(A_WIDTH), row(G_WIDTH), row(d),
            pl.BlockSpec((None, None, MOD_ROWS, d), lambda bb, i: (bb, seg(i), 0, 0)),
            const2((d, d)), const2((1, d)), const2((1, d)),
            const2((d, 2 * ROUTER_LANES)), const2((1, ROUTER_LANES)),
        ],
        out_specs=(row(d), row(d // 2),
                   pl.BlockSpec((None, SUBLANES, tm), lambda bb, i: (bb, 0, i))),
        compiler_params=_cparams(("parallel", "parallel")),
        name="outproj_ln_router",
    )(yf, att, sg, xa, mods_l, wo, lng, lnb, wr, rb)


def _moe_plan(route, tm):
    b, _, l = route.shape
    t = b * l
    pairs = [(a, c) for a in range(EXPERTS_PER_GROUP) for c in range(a + 1, EXPERTS_PER_GROUP)]
    cls_e1 = np.zeros((N_PAIR_CLASSES,), np.int32)
    cls_e2 = np.zeros((N_PAIR_CLASSES,), np.int32)
    for k, (a, c) in enumerate(pairs):
        assert k == a * (2 * EXPERTS_PER_GROUP - 1 - a) // 2 + (c - a - 1)
        for g in range(N_EXPERT_GROUPS):
            cls_e1[g * N_PAIRS + k] = g * EXPERTS_PER_GROUP + a
            cls_e2[g * N_PAIRS + k] = g * EXPERTS_PER_GROUP + c
    g_lo = route[:, ROUTE_G_LO, :]
    g_hi = route[:, ROUTE_G_HI, :]
    pair_gates = jnp.concatenate([jnp.broadcast_to(g_lo[:, :, None], (b, l, LANES)),
                                  jnp.broadcast_to(g_hi[:, :, None], (b, l, LANES))], axis=-1)
    cls = route[:, ROUTE_CLASS, :].astype(jnp.int32).reshape(t)
    onehot = cls[:, None] == jnp.arange(N_PAIR_CLASSES, dtype=jnp.int32)[None, :]
    oh_tiles = onehot.reshape(t // tm, tm, N_PAIR_CLASSES).astype(BF16)
    tri = jnp.tril(jnp.ones((tm, tm), BF16))
    within = jnp.einsum("ij,njc->nic", tri, oh_tiles, preferred_element_type=F32)
    tile_total = within[:, -1, :]
    n_tiles = t // tm
    earlier = jnp.tril(jnp.ones((n_tiles, n_tiles), BF16), k=-1)
    before = jnp.dot(earlier, tile_total.astype(BF16), preferred_element_type=F32)
    csum = (within + before[:, None, :]).reshape(t, N_PAIR_CLASSES).astype(jnp.int32)
    onehot = onehot.astype(jnp.int32)
    rank = jnp.sum(csum * onehot, axis=1) - 1
    padded = (csum[-1] + tm - 1) // tm * tm
    upto = jnp.tril(jnp.ones((N_PAIR_CLASSES, N_PAIR_CLASSES), jnp.int32))
    off_end = jnp.sum(upto * padded[None, :], axis=1)
    dest = jnp.sum(onehot * (off_end - padded)[None, :], axis=1) + rank
    nt = t // tm + N_PAIR_CLASSES
    n_valid = (off_end[-1] // tm).astype(jnp.int32)
    first_row = jnp.minimum(jnp.arange(nt, dtype=jnp.int32), n_valid - 1) * tm
    tile_cls = jnp.sum((off_end[None, :] <= first_row[:, None]).astype(jnp.int32), axis=1)
    tile_cls = jnp.minimum(tile_cls, N_PAIR_CLASSES - 1)
    return (dest.astype(jnp.int32), pair_gates, jnp.asarray(cls_e1)[tile_cls],
            jnp.asarray(cls_e2)[tile_cls], n_valid.reshape(1), nt)


def _dispatch_kernel(dest_ref, h_ref, hs_init_ref, hs_ref, stage, sems):
    del hs_init_ref
    s = pl.program_id(0)
    n = pl.num_programs(0)
    tm = h_ref.shape[0]
    slot = s % 2

    def wait_slot(k):
        pltpu.make_async_copy(stage.at[k], hs_ref.at[pl.ds(0, tm), :], sems.at[k]).wait()

    @pl.when(s >= 2)
    def _reuse():
        wait_slot(slot)

    stage[slot] = h_ref[...]
    for r in range(tm):
        pltpu.make_async_copy(stage.at[slot, pl.ds(r, 1), :],
                              hs_ref.at[pl.ds(dest_ref[0, r], 1), :], sems.at[slot]).start()

    @pl.when(s == n - 1)
    def _drain():
        @pl.when(n >= 2)
        def _other():
            wait_slot(1 - slot)
        wait_slot(slot)


def _dispatch(h_packed, dest, nt, tm):
    t, w = h_packed.shape
    hs_init = jnp.zeros((nt * tm, w), jnp.uint32)
    return pl.pallas_call(
        _dispatch_kernel,
        out_shape=jax.ShapeDtypeStruct((nt * tm, w), jnp.uint32),
        grid=(t // tm,),
        in_specs=[
            pl.BlockSpec((None, 1, tm), lambda i: (i, 0, 0), memory_space=pltpu.SMEM),
            pl.BlockSpec((tm, w), lambda i: (i, 0)),
            pl.BlockSpec(memory_space=pl.ANY),
        ],
        out_specs=pl.BlockSpec(memory_space=pl.ANY),
        scratch_shapes=[pltpu.VMEM((2, tm, w), jnp.uint32), pltpu.SemaphoreType.DMA((2,))],
        input_output_aliases={2: 0},
        compiler_params=_cparams(("arbitrary",)),
        name="moe_dispatch",
    )(dest.reshape(t // tm, 1, tm), h_packed, hs_init)


def _expert_pair_kernel(e1_ref, e2_ref, nv_ref, hs_ref, w1a_ref, w3a_ref, w2a_ref,
                        w1b_ref, w3b_ref, w2b_ref, y_ref):
    del e1_ref, e2_ref
    i = pl.program_id(0)

    @pl.when(i < nv_ref[0])
    def _compute():
        lo, hi = _unpack_bf16_pair(hs_ref[...])
        h = jnp.concatenate([lo, hi], axis=1).astype(BF16)

        def ffn(w1_ref, w3_ref, w2_ref):
            a = _bdot(h, w1_ref[...])
            a = (a * jax.nn.sigmoid(a)) * _bdot(h, w3_ref[...])
            return _bdot(a.astype(BF16), w2_ref[...])

        y_ref[...] = _pack_bf16_pair(ffn(w1a_ref, w3a_ref, w2a_ref), ffn(w1b_ref, w3b_ref, w2b_ref))

    @pl.when(i >= nv_ref[0])
    def _unused_tile():
        y_ref[...] = jnp.zeros(y_ref.shape, jnp.uint32)


def _expert_pairs(hs, tile_e1, tile_e2, n_valid, w1, w3, w2, tm):
    rows, w = hs.shape
    _, d, de = w1.shape
    first = lambda shape: pl.BlockSpec(shape, lambda i, e1, e2, nv: (e1[i], 0, 0))
    second = lambda shape: pl.BlockSpec(shape, lambda i, e1, e2, nv: (e2[i], 0, 0))
    grid_spec = pltpu.PrefetchScalarGridSpec(
        num_scalar_prefetch=3,
        grid=(rows // tm,),
        in_specs=[
            pl.BlockSpec((tm, w), lambda i, e1, e2, nv: (jnp.minimum(i, nv[0] - 1), 0)),
            first((None, d, de)), first((None, d, de)), first((None, de, d)),
            second((None, d, de)), second((None, d, de)), second((None, de, d)),
        ],
        out_specs=pl.BlockSpec((tm, d), lambda i, e1, e2, nv: (i, 0)),
    )
    return pl.pallas_call(
        _expert_pair_kernel,
        out_shape=jax.ShapeDtypeStruct((rows, d), jnp.uint32),
        grid_spec=grid_spec,
        compiler_params=_cparams(("arbitrary",)),
        name="moe_expert_pairs",
    )(tile_e1, tile_e2, n_valid, hs, w1, w3, w2, w1, w3, w2)


def _combine_kernel(dest_ref, dest_next_ref, y2_ref, x_ref, pg_ref, mod_ref, lng_ref, lnb_ref,
                    o_ref, ybuf, sems, *, alpha):
    o_ref[...] = _combine_body(dest_ref, dest_next_ref, y2_ref, x_ref, pg_ref, mod_ref, lng_ref,
                               lnb_ref, ybuf, sems, alpha=alpha)


def _combine_inproj_kernel(dest_ref, dest_next_ref, y2_ref, x_ref, pg_ref, mod_ref, lng_ref, lnb_ref,
                           *refs, alpha, q_scale):
    n_inproj_in = 8
    inproj_in = refs[:n_inproj_in]
    o_ref = refs[n_inproj_in]
    inproj_out = refs[n_inproj_in + 1:n_inproj_in + 7]
    ybuf, sems = refs[n_inproj_in + 7:]
    x_new = _combine_body(dest_ref, dest_next_ref, y2_ref, x_ref, pg_ref, mod_ref, lng_ref, lnb_ref,
                          ybuf, sems, alpha=alpha)
    o_ref[...] = x_new
    _inproj_body(x_new, *inproj_in, *inproj_out, q_scale=q_scale)


def _combine_body(dest_ref, dest_next_ref, y2_ref, x_ref, pg_ref, mod_ref, lng_ref, lnb_ref,
                  ybuf, sems, *, alpha):
    s = pl.program_id(0)
    n = pl.num_programs(0)
    tm, d = x_ref.shape
    slot = s % 2

    def start_gather(idx_ref, k):
        for r in range(tm):
            pltpu.make_async_copy(y2_ref.at[pl.ds(idx_ref[0, r], 1), :],
                                  ybuf.at[k, pl.ds(r, 1), :], sems.at[k]).start()

    @pl.when(s == 0)
    def _first():
        start_gather(dest_ref, 0)

    @pl.when(s + 1 < n)
    def _prefetch():
        start_gather(dest_next_ref, 1 - slot)

    pltpu.make_async_copy(y2_ref.at[pl.ds(0, tm), :], ybuf.at[slot], sems.at[slot]).wait()
    y_lo, y_hi = _unpack_bf16_pair(ybuf[slot])
    pg = pg_ref[...]
    reps = d // LANES
    g_lo = jnp.concatenate([pg[:, :LANES]] * reps, axis=1)
    g_hi = jnp.concatenate([pg[:, LANES:]] * reps, axis=1)
    y = g_lo * y_lo + g_hi * y_hi
    return _layernorm(alpha * x_ref[...] + mod_ref[5:6, :] * y, lng_ref[...], lnb_ref[...])


def _combine(y2, dest, x1, pair_gates, mods_l, lng, lnb, *, n_rows, n_lat, tm, alpha, next_inproj=None):
    b, l, d = x1.shape
    nlat = n_lat // tm
    tiles_per_batch = l // tm
    per_batch = n_rows // tm
    n_steps = b * per_batch
    batch = lambda s: s // per_batch
    tile = lambda s: s % per_batch
    seg = lambda s: jnp.where(tile(s) >= nlat, 1, 0)
    token_tile = lambda s: batch(s) * tiles_per_batch + tile(s)
    row = lambda width: pl.BlockSpec((None, tm, width), lambda s: (batch(s), tile(s), 0))
    const2 = lambda shape: pl.BlockSpec(shape, lambda s: (0, 0))
    mod_spec = pl.BlockSpec((None, None, MOD_ROWS, d), lambda s: (batch(s), seg(s), 0, 0))
    dest3 = dest.reshape(b * tiles_per_batch, 1, tm)
    in_specs = [
        pl.BlockSpec((None, 1, tm), lambda s: (token_tile(s), 0, 0), memory_space=pltpu.SMEM),
        pl.BlockSpec((None, 1, tm), lambda s: (token_tile(jnp.minimum(s + 1, n_steps - 1)), 0, 0),
                     memory_space=pltpu.SMEM),
        pl.BlockSpec(memory_space=pl.ANY),
        row(d), row(2 * LANES), mod_spec, const2((1, d)), const2((1, d)),
    ]
    args = [dest3, dest3, y2, x1, pair_gates, mods_l, lng, lnb]
    out_shape = [jax.ShapeDtypeStruct((b, n_rows, d), F32)]
    out_specs = [row(d)]
    if next_inproj is None:
        body = functools.partial(_combine_kernel, alpha=alpha)
        name = "moe_combine_ln"
    else:
        assert n_rows == l and tm == TOKEN_TILE
        mods_n, w_big, cosf, sinf, slng, slnb, ws, bs_full, q_scale = next_inproj
        in_specs += [
            mod_spec, const2((d, BIG_COLS)),
            pl.BlockSpec((tm, LANES), lambda s: (tile(s), 0)),
            pl.BlockSpec((tm, LANES), lambda s: (tile(s), 0)),
            const2((1, G_WIDTH)), const2((1, G_WIDTH)),
            pl.BlockSpec((G_GROUPS, CHUNK, CHUNK), lambda s: (0, 0, 0)),
            const2((CHUNK, G_WIDTH)),
        ]
        args += [mods_n, w_big, cosf, sinf, slng, slnb, ws, bs_full]
        out_shape += list(_inproj_out_shapes(b, l))
        out_specs += [row(F_WIDTH), row(F_WIDTH), row(A_QK_WIDTH),
                      pl.BlockSpec((None, A_HEADS, A_DV, tm), lambda s: (batch(s), 0, 0, tile(s))),
                      row(2 * A_WIDTH), row(G_WIDTH)]
        body = functools.partial(_combine_inproj_kernel, alpha=alpha, q_scale=q_scale)
        name = "moe_combine_ln_inproj"
    outs = pl.pallas_call(
        body,
        out_shape=tuple(out_shape),
        grid=(n_steps,),
        in_specs=in_specs,
        out_specs=tuple(out_specs),
        scratch_shapes=[pltpu.VMEM((2, tm, d), jnp.uint32), pltpu.SemaphoreType.DMA((2,))],
        compiler_params=_cparams(("arbitrary",)),
        name=name,
    )(*args)
    return outs[0] if next_inproj is None else outs


def _head_lane_fields():
    j = jnp.arange(LANES)
    half = j // (LANES // 2)
    comp = (j % (LANES // 2)) // (A_DH // 2)
    axis = (j % (A_DH // 2)) // (A_DH // 4)
    freq = j % (A_DH // 4)
    return half, comp, axis, freq


def _qk_column_perm():
    half, comp, axis, freq = _head_lane_fields()
    orig = comp * A_DH + axis * (A_DH // 2) + half * (A_DH // 4) + freq
    return (jnp.arange(A_HEADS)[:, None] * LANES + orig[None, :]).reshape(-1)


def _rope_tables(n_lat, n_ctx):
    half, _, axis, freq = _head_lane_fields()
    rows = n_lat // GRID_W
    row = jnp.repeat(jnp.arange(rows, dtype=F32), GRID_W)
    col = jnp.tile(jnp.arange(GRID_W, dtype=F32), rows)
    hd = A_DH // 2
    inv = ROPE_BASE ** (-jnp.arange(0, hd, 2, dtype=F32) / hd)
    pos = jnp.where(axis[None, :] == 0, row[:, None], col[:, None])
    ang = pos * inv[freq][None, :]
    cosf = jnp.cos(ang)
    sinf = jnp.sin(ang) * jnp.where(half == 0, -1.0, 1.0)[None, :]
    cosf = jnp.concatenate([cosf, jnp.ones((n_ctx, LANES), F32)], axis=0)
    sinf = jnp.concatenate([sinf, jnp.zeros((n_ctx, LANES), F32)], axis=0)
    return cosf, sinf


def _dft_mats(n, scale=1.0):
    i = jnp.arange(n, dtype=jnp.int32)
    ang = ((i[:, None] * i[None, :]) % n).astype(F32) * (2.0 * math.pi / n)
    return jnp.cos(ang) * scale, jnp.sin(ang) * scale


def _twiddles(n, bn):
    ka = jnp.arange(DFT_A, dtype=jnp.int32)
    bb = jnp.arange(bn, dtype=jnp.int32)
    ang = ((ka[:, None] * bb[None, :]) % n).astype(F32) * (2.0 * math.pi / n)
    shape = (DFT_A, bn, F_WIDTH)
    twc = jnp.broadcast_to(jnp.cos(ang)[:, :, None], shape).reshape(DFT_A, bn * F_WIDTH)
    tws = jnp.broadcast_to(jnp.sin(ang)[:, :, None], shape).reshape(DFT_A, bn * F_WIDTH)
    return twc, tws


def _block_diag(blocks):
    g, n = blocks.shape[-3], blocks.shape[-1]
    eye = jnp.eye(g, dtype=blocks.dtype)
    out = blocks[..., :, :, None, :] * eye[:, None, :, None]
    return out.reshape(blocks.shape[:-3] + (g * n, g * n))


def kernel(x, c, ctx, c_ctx, w_ada, b_ada, w_in, w_fourier, diff_lambda, diff_subln,
           sgu_ln_g, sgu_ln_b, sgu_w, sgu_b, w_out, ln_g, ln_b, w_router, router_bias,
           moe_w1, moe_w3, moe_w2):
    b, n, d = x.shape
    n_ctx = ctx.shape[1]
    depth = w_ada.shape[0]
    l = n + n_ctx
    alpha = (2 * depth) ** 0.25
    bn = n // DFT_A
    assert n % DFT_A == 0 and bn % SUBLANES == 0 and n % GRID_W == 0
    assert n_ctx % TOKEN_TILE == 0 and n % TOKEN_TILE == 0 and n % n_ctx == 0
    assert b + 1 <= MOD_ROWS

    c_rows = jnp.concatenate([c, c_ctx[None, :], jnp.zeros((MOD_ROWS - b - 1, d), F32)], axis=0)
    mod = _ada(c_rows, w_ada, b_ada).reshape(depth, MOD_ROWS, 6, d)
    lat = mod[:, :b]
    cm = jnp.broadcast_to(mod[:, b:b + 1], lat.shape)
    mods = jnp.stack([lat, cm], axis=2)
    mods = jnp.pad(mods, ((0, 0), (0, 0), (0, 0), (0, MOD_ROWS - 6), (0, 0)))

    perm = _qk_column_perm()
    c64, s64 = _dft_mats(F_DIM)
    eye_g = jnp.eye(F_GROUPS, dtype=F32)
    bdc = jnp.kron(eye_g, c64)
    bds = jnp.kron(eye_g, s64)
    bdw = _block_diag(w_fourier)
    w_fab = _fourier_weights(w_in[:, :, :F_WIDTH], bdc, bds, bdw)
    o_q = F_WIDTH
    o_k = o_q + A_QK_WIDTH
    o_v = o_k + A_QK_WIDTH
    o_g = o_v + A_WIDTH
    w_big = jnp.concatenate([
        w_fab,
        w_in[:, :, o_q:o_k][:, :, perm],
        w_in[:, :, o_k:o_v][:, :, perm],
        w_in[:, :, o_v:],
    ], axis=-1).astype(BF16)
    wo = w_out.astype(BF16)
    w1 = moe_w1.astype(BF16)
    w3 = moe_w3.astype(BF16)
    w2 = moe_w2.astype(BF16)
    ws = sgu_w.astype(BF16)
    bs_full = jnp.repeat(jnp.swapaxes(sgu_b, 1, 2), G_DIM, axis=2)
    wr_f32 = jnp.pad(w_router, ((0, 0), (0, ROUTER_LANES - N_EXPERTS)))
    wr_top = _top_half_bits(wr_f32)
    wr = jnp.concatenate([wr_top.astype(BF16), (wr_f32 - wr_top).astype(BF16)], axis=1)
    rb =jnp.pad(router_bias, (0, ROUTER_LANES - N_EXPERTS)).reshape(1, ROUTER_LANES)

    cosf, sinf = _rope_tables(n, n_ctx)
    ca, sa = _dft_mats(DFT_A)
    l_a = jnp.concatenate([_stack3([ca, -sa]), _stack3([-sa, -ca])], axis=0)
    twc, tws = _twiddles(n, bn)
    l_b = _stack3(list(_dft_mats(bn, scale=(n * F_DIM) ** -0.5)))
    cc, sc = _dft_mats(n_ctx, scale=(n_ctx * F_DIM) ** -0.5)
    l_c = _stack3([cc, -sc])
    q_scale = (A_DH ** -0.5) * math.log2(math.e)

    tk = ATTN_TK
    assert l % tk == 0
    assert n % MOE_TILE == 0 and n_ctx % MOE_TILE == 0

    def inproj_params(li):
        return (mods[li], w_big[li], cosf, sinf, sgu_ln_g[li][None], sgu_ln_b[li][None],
                ws[li], bs_full[li])

    xa = jnp.concatenate([x, ctx], axis=1)
    fa, fb, q, kt, v_ext, sg = _inproj(xa, *inproj_params(0), n, q_scale)
    for li in range(depth):
        last = li == depth - 1
        lam_init = 0.8 - 0.6 * math.exp(-0.3 * li)
        zr, zi = _fft_a(fa, fb, l_a, twc, tws, bn)
        yf = jnp.concatenate([_fft_b(zr, zi, l_b, bn), _dft_ctx(fa, fb, l_c, n, n_ctx)], axis=1)
        gain = diff_subln[li][None]
        att_x = _attention(q, kt, v_ext, diff_lambda[li], gain, q_block0=0, n_q=n, key_block0=0,
                           n_keys=l, tq=ATTN_TQ, tk=tk, group=min(ATTN_GROUP, n), lam_init=lam_init)
        att_c = _attention(q, kt, v_ext, diff_lambda[li], gain, q_block0=n // n_ctx, n_q=n_ctx,
                           key_block0=n // n_ctx, n_keys=n_ctx, tq=n_ctx, tk=n_ctx, group=n_ctx,
                           lam_init=lam_init)
        att = jnp.concatenate([att_x, att_c], axis=1)
        x1, h_packed, route = _outproj(
            yf, att, sg, xa, mods[li], wo[li], ln_g[li, 0][None], ln_b[li, 0][None], wr, rb, n, alpha)
        dest, pair_gates, tile_e1, tile_e2, n_valid, nt = _moe_plan(route, MOE_TILE)
        hs = _dispatch(h_packed.reshape(b * l, d // 2), dest, nt, MOE_TILE)
        y2 = _expert_pairs(hs, tile_e1, tile_e2, n_valid, w1[li], w3[li], w2[li], MOE_TILE)
        combine_args = (y2, dest, x1, pair_gates, mods[li], ln_g[li, 1][None], ln_b[li, 1][None])
        if last:
            xa = _combine(*combine_args, n_rows=n, n_lat=n, tm=MOE_TILE, alpha=alpha)
        else:
            xa, fa, fb, q, kt, v_ext, sg = _combine(
                *combine_args, n_rows=l, n_lat=n, tm=MOE_TILE, alpha=alpha,
                next_inproj=inproj_params(li + 1) + (q_scale,))
    return xa
```

```python
import functools
import math

import numpy as np
import jax
import jax.numpy as jnp
from jax import lax
from jax.experimental import pallas as pl
from jax.experimental.pallas import tpu as pltpu

F32 = jnp.float32
BF16 = jnp.bfloat16
HIGHEST = lax.Precision.HIGHEST

GRID_W = 64
F_GROUPS, F_DIM = 4, 64
F_WIDTH = F_GROUPS * F_DIM
A_HEADS, A_DH = 4, 64
A_DV = 2 * A_DH
A_QK_WIDTH = A_HEADS * 2 * A_DH
A_WIDTH = A_HEADS * A_DV
G_GROUPS, G_DIM = 4, 64
G_WIDTH = G_GROUPS * G_DIM
CHUNK = 128
ROPE_BASE = 10000.0
N_EXPERTS = 16
N_EXPERT_GROUPS = 4
EXPERTS_PER_GROUP = N_EXPERTS // N_EXPERT_GROUPS
LN_EPS = 1e-5
RMS_EPS = 1e-5

COL_FA = 0
COL_FB = COL_FA + F_WIDTH
COL_Q = COL_FB + F_WIDTH
COL_K = COL_Q + A_QK_WIDTH
COL_V = COL_K + A_QK_WIDTH
COL_G = COL_V + A_WIDTH
BIG_COLS = COL_G + 2 * G_WIDTH

LANES = 128
SUBLANES = 8
MOD_ROWS = 8
ROUTER_LANES = 128
DFT_A = 128
VMEM_LIMIT = 56 * 1024 * 1024

TOKEN_TILE = 256
ATTN_TQ = 256
ATTN_GROUP = 8192
ATTN_TK = 1280
MOE_TILE = 256


def _cparams(sem):
    return pltpu.CompilerParams(dimension_semantics=sem, vmem_limit_bytes=VMEM_LIMIT)


def _hdot(a, b):
    return jnp.dot(a, b, precision=HIGHEST, preferred_element_type=F32)


def _bdot(a, b):
    return jnp.dot(a, b, preferred_element_type=F32)


def _layernorm(y, g, b):
    mu = jnp.mean(y, axis=-1, keepdims=True)
    d = y - mu
    var = jnp.mean(d * d, axis=-1, keepdims=True)
    return d * lax.rsqrt(var + LN_EPS) * g + b


def _ada_kernel(c_ref, w_ref, b_ref, o_ref):
    c = c_ref[...]
    a = c * jax.nn.sigmoid(c)
    o_ref[...] = _hdot(a, w_ref[...]) + b_ref[...]


def _ada(c_rows, w_ada, b_ada):
    depth, d, cols = w_ada.shape
    tn = 1536
    assert cols % tn == 0
    return pl.pallas_call(
        _ada_kernel,
        out_shape=jax.ShapeDtypeStruct((depth, MOD_ROWS, cols), F32),
        grid=(depth, cols // tn),
        in_specs=[
            pl.BlockSpec((MOD_ROWS, d), lambda l, j: (0, 0)),
            pl.BlockSpec((None, d, tn), lambda l, j: (l, 0, j)),
            pl.BlockSpec((None, 1, tn), lambda l, j: (l, 0, j)),
        ],
        out_specs=pl.BlockSpec((None, MOD_ROWS, tn), lambda l, j: (l, 0, j)),
        compiler_params=_cparams(("parallel", "parallel")),
        name="ada_mod",
    )(c_rows, w_ada, b_ada.reshape(depth, 1, cols))


def _fw_kernel(wf_ref, bdc_ref, bds_ref, bdw_ref, o_ref):
    bdw = bdw_ref[...]
    mc = _hdot(bdc_ref[...], bdw)
    ms = _hdot(bds_ref[...], bdw)
    wf = wf_ref[...]
    o_ref[:, :F_WIDTH] = _hdot(wf, mc)
    o_ref[:, F_WIDTH:] = _hdot(wf, ms)


def _fourier_weights(wf, bdc, bds, bdw):
    depth, d, _ = wf.shape
    return pl.pallas_call(
        _fw_kernel,
        out_shape=jax.ShapeDtypeStruct((depth, d, 2 * F_WIDTH), F32),
        grid=(depth,),
        in_specs=[
            pl.BlockSpec((None, d, F_WIDTH), lambda l: (l, 0, 0)),
            pl.BlockSpec((F_WIDTH, F_WIDTH), lambda l: (0, 0)),
            pl.BlockSpec((F_WIDTH, F_WIDTH), lambda l: (0, 0)),
            pl.BlockSpec((None, F_WIDTH, F_WIDTH), lambda l: (l, 0, 0)),
        ],
        out_specs=pl.BlockSpec((None, d, 2 * F_WIDTH), lambda l: (l, 0, 0)),
        compiler_params=_cparams(("parallel",)),
        name="fourier_weights",
    )(wf, bdc, bds, bdw)


def _gelu_tanh(x):
    c = math.sqrt(2.0 / math.pi)
    return x * (0.5 * (1.0 + jnp.tanh(c * (x + 0.044715 * (x * x * x)))))


def _inproj_kernel(x_ref, *refs, q_scale):
    _inproj_body(x_ref[...], *refs, q_scale=q_scale)


def _inproj_body(x, mod_ref, w_ref, cos_ref, sin_ref, lng_ref, lnb_ref, ws_ref, bs_ref,
                 fa_ref, fb_ref, q_ref, kt_ref, v_ref, sg_ref, *, q_scale):
    sh = mod_ref[0:1, :]
    sc = mod_ref[1:2, :]
    h = (x * (1.0 + sc) + sh).astype(BF16)
    p = _bdot(h, w_ref[...])
    fa_ref[...] = p[:, COL_FA:COL_FA + F_WIDTH]
    fb_ref[...] = p[:, COL_FB:COL_FB + F_WIDTH]
    cosf = cos_ref[...]
    sinf = sin_ref[...]
    for hh in range(A_HEADS):
        t = p[:, COL_Q + LANES * hh:COL_Q + LANES * (hh + 1)]
        r = t * cosf + pltpu.roll(t, LANES // 2, 1) * sinf
        q_ref[:, LANES * hh:LANES * (hh + 1)] = (r * q_scale).astype(BF16)
        t = p[:, COL_K + LANES * hh:COL_K + LANES * (hh + 1)]
        r = t * cosf + pltpu.roll(t, LANES // 2, 1) * sinf
        kt_ref[hh] = r.T.astype(BF16)
        v_ref[:, 2 * LANES * hh:2 * LANES * hh + LANES] = (
            p[:, COL_V + LANES * hh:COL_V + LANES * (hh + 1)].astype(BF16))
        v_ref[:, 2 * LANES * hh + LANES:2 * LANES * (hh + 1)] = jnp.ones((t.shape[0], LANES), BF16)

    g = _gelu_tanh(p[:, COL_G:COL_G + 2 * G_WIDTH])
    u = g[:, :G_WIDTH]
    vn = _layernorm(g[:, G_WIDTH:], lng_ref[...], lnb_ref[...])
    lane_group = lax.broadcasted_iota(jnp.int32, (CHUNK, G_WIDTH), 1) // G_DIM
    tm = x.shape[0]
    for c in range(tm // CHUNK):
        rows = slice(c * CHUNK, (c + 1) * CHUNK)
        vc = vn[rows]
        mixed = bs_ref[...]
        for gi in range(G_GROUPS):
            vm = jnp.where(lane_group == gi, vc, 0.0).astype(BF16)
            mixed = mixed + _bdot(ws_ref[gi], vm)
        sg_ref[rows, :] = (u[rows] * mixed).astype(BF16)


def _inproj_out_shapes(b, l):
    return (
        jax.ShapeDtypeStruct((b, l, F_WIDTH), F32),
        jax.ShapeDtypeStruct((b, l, F_WIDTH), F32),
        jax.ShapeDtypeStruct((b, l, A_QK_WIDTH), BF16),
        jax.ShapeDtypeStruct((b, A_HEADS, A_DV, l), BF16),
        jax.ShapeDtypeStruct((b, l, 2 * A_WIDTH), BF16),
        jax.ShapeDtypeStruct((b, l, G_WIDTH), BF16),
    )


def _inproj(xa, mods_l, w_big, cosf, sinf, lng, lnb, ws, bs_full, n_lat, q_scale):
    b, l, d = xa.shape
    tm = TOKEN_TILE
    nlat = n_lat // tm
    seg = lambda i: jnp.where(i >= nlat, 1, 0)
    row = lambda width: pl.BlockSpec((None, tm, width), lambda bb, i: (bb, i, 0))
    const2 = lambda shape: pl.BlockSpec(shape, lambda bb, i: (0, 0))
    return pl.pallas_call(
        functools.partial(_inproj_kernel, q_scale=q_scale),
        out_shape=_inproj_out_shapes(b, l),
        grid=(b, l // tm),
        in_specs=[
            row(d),
            pl.BlockSpec((None, None, MOD_ROWS, d), lambda bb, i: (bb, seg(i), 0, 0)),
            const2((d, BIG_COLS)),
            pl.BlockSpec((tm, LANES), lambda bb, i: (i, 0)),
            pl.BlockSpec((tm, LANES), lambda bb, i: (i, 0)),
            const2((1, G_WIDTH)),
            const2((1, G_WIDTH)),
            pl.BlockSpec((G_GROUPS, CHUNK, CHUNK), lambda bb, i: (0, 0, 0)),
            const2((CHUNK, G_WIDTH)),
        ],
        out_specs=(row(F_WIDTH), row(F_WIDTH), row(A_QK_WIDTH),
                   pl.BlockSpec((None, A_HEADS, A_DV, tm), lambda bb, i: (bb, 0, 0, i)),
                   row(2 * A_WIDTH), row(G_WIDTH)),
        compiler_params=_cparams(("parallel", "parallel")),
        name="inproj",
    )(xa, mods_l, w_big, cosf, sinf, lng, lnb, ws, bs_full)


def _split3(x):
    top = _top_half_bits(x)
    hi = top.astype(BF16)
    return [hi, (x - top).astype(BF16), hi]


def _stack3(mats):
    cols = []
    for m in mats:
        top = _top_half_bits(m)
        cols += [top.astype(BF16), top.astype(BF16), (m - top).astype(BF16)]
    return jnp.concatenate(cols, axis=1)


def _fft_a_kernel(fa_ref, fb_ref, l_ref, tc_ref, ts_ref, zr_ref, zi_ref):
    rhs = jnp.concatenate(_split3(fa_ref[...]) + _split3(fb_ref[...]), axis=0)
    z = _bdot(l_ref[...], rhs)
    zr = z[:DFT_A]
    zi = z[DFT_A:]
    tc = tc_ref[...]
    ts = ts_ref[...]
    zr_ref[...] = zr * tc + zi * ts
    zi_ref[...] = zi * tc - zr * ts


def _fft_a(fa, fb, l_a, twc, tws, bn):
    b, l, _ = fa.shape
    cols = bn * F_WIDTH
    fa3 = fa.reshape(b, l // bn, cols)
    fb3 = fb.reshape(b, l // bn, cols)
    tcw = min(cols, 2048)
    assert cols % tcw == 0
    blk = pl.BlockSpec((None, DFT_A, tcw), lambda bb, j: (bb, 0, j))
    tw = pl.BlockSpec((DFT_A, tcw), lambda bb, j: (0, j))
    zr, zi = pl.pallas_call(
        _fft_a_kernel,
        out_shape=(jax.ShapeDtypeStruct((b, DFT_A, cols), F32),) * 2,
        grid=(b, cols // tcw),
        in_specs=[blk, blk, pl.BlockSpec(l_a.shape, lambda bb, j: (0, 0)), tw, tw],
        out_specs=(blk, blk),
        compiler_params=_cparams(("parallel", "parallel")),
        name="fft_stage_a",
    )(fa3, fb3, l_a, twc, tws)
    return zr.reshape(b, DFT_A, bn, F_WIDTH), zi.reshape(b, DFT_A, bn, F_WIDTH)


def _fft_b_kernel(zr_ref, zi_ref, l_ref, o_ref):
    lm = l_ref[...]
    for j in range(zr_ref.shape[0]):
        rhs = jnp.concatenate(_split3(zr_ref[j]) + _split3(zi_ref[j]), axis=0)
        o_ref[:, F_WIDTH * j:F_WIDTH * (j + 1)] = _bdot(lm, rhs).astype(o_ref.dtype)


def _fft_b(zr4, zi4, l_b, bn):
    b = zr4.shape[0]
    kb = 8
    blk = pl.BlockSpec((None, kb, bn, F_WIDTH), lambda bb, i: (bb, i, 0, 0))
    out = pl.pallas_call(
        _fft_b_kernel,
        out_shape=jax.ShapeDtypeStruct((b, bn, DFT_A * F_WIDTH), BF16),
        grid=(b, DFT_A // kb),
        in_specs=[blk, blk, pl.BlockSpec(l_b.shape, lambda bb, i: (0, 0))],
        out_specs=pl.BlockSpec((None, bn, kb * F_WIDTH), lambda bb, i: (bb, 0, i)),
        compiler_params=_cparams(("parallel", "parallel")),
        name="fft_stage_b",
    )(zr4, zi4, l_b)
    return out.reshape(b, bn * DFT_A, F_WIDTH)


def _dft_ctx_kernel(fa_ref, fb_ref, l_ref, o_ref):
    rhs = jnp.concatenate(_split3(fa_ref[...]) + _split3(fb_ref[...]), axis=0)
    o_ref[...] = _bdot(l_ref[...], rhs).astype(o_ref.dtype)


def _dft_ctx(fa, fb, l_c, n_lat, n_ctx):
    b = fa.shape[0]
    blk = pl.BlockSpec((None, n_ctx, F_WIDTH), lambda bb: (bb, n_lat // n_ctx, 0))
    return pl.pallas_call(
        _dft_ctx_kernel,
        out_shape=jax.ShapeDtypeStruct((b, n_ctx, F_WIDTH), BF16),
        grid=(b,),
        in_specs=[blk, blk, pl.BlockSpec(l_c.shape, lambda bb: (0, 0))],
        out_specs=pl.BlockSpec((None, n_ctx, F_WIDTH), lambda bb: (bb, 0, 0)),
        compiler_params=_cparams(("parallel",)),
        name="dft_ctx",
    )(fa, fb, l_c)


def _score_slot(j):
    return 2 if j == 0 else (j - 1) % 2


def _attn_kernel(q_ref, kt_ref, v_ref, lam_ref, gain_ref, o_ref, q2_ref, s_ref, m_ref, acc_ref,
                 *, lam_init, tq, tk):
    nsub = q_ref.shape[0] // tq
    nk = kt_ref.shape[1] // tk

    def keys(j):
        return slice(j * tk, (j + 1) * tk)
    lane = lax.broadcasted_iota(jnp.int32, (tq, A_DV), 1)
    comp0 = (lane % A_DH) < (A_DH // 2)
    lp = lam_ref[...]
    lam = (jnp.exp(jnp.sum(lp[0:1] * lp[1:2], keepdims=True))
           - jnp.exp(jnp.sum(lp[2:3] * lp[3:4], keepdims=True)) + lam_init)
    out_gain = gain_ref[...] * (1.0 - lam_init)

    def rows(i):
        return pl.ds(pl.multiple_of(i * tq, tq), tq)

    def load_q2(i):
        q = q_ref[rows(i), :]
        zero = jnp.zeros_like(q)
        q2_ref[0:tq, :] = jnp.where(comp0, q, zero)
        q2_ref[tq:2 * tq, :] = jnp.where(comp0, zero, q)

    def scores(j):
        s_ref[_score_slot(j)] = _bdot(q2_ref[...], kt_ref[:, keys(j)])

    def softmax_pv(j):
        s = s_ref[_score_slot(j)]
        m_old = m_ref[...]
        m_new = jnp.maximum(m_old, jnp.max(s, axis=1, keepdims=True))
        alpha = jnp.exp2(m_old - m_new)
        p = jnp.exp2(s - jnp.concatenate([m_new] * (tk // LANES), axis=1)).astype(BF16)
        pv = _bdot(p, v_ref[keys(j), :])
        acc_ref[...] = acc_ref[...] * jnp.concatenate([alpha, alpha], axis=1) + pv
        m_ref[...] = m_new

    def finalize(i):
        acc = acc_ref[...]
        o0 = acc[0:tq, 0:LANES] / acc[0:tq, LANES:]
        o1 = acc[tq:, 0:LANES] / acc[tq:, LANES:]
        o = o0 - lam * o1
        ms = jnp.mean(o * o, axis=-1, keepdims=True)
        o_ref[rows(i), :] = (o * lax.rsqrt(ms + RMS_EPS) * out_gain).astype(o_ref.dtype)

    def next_tile_first_scores(i):
        load_q2(jnp.minimum(i + 1, nsub - 1))
        scores(0)

    load_q2(0)
    scores(0)

    def tile(i, carry):
        m_ref[...] = jnp.full(m_ref.shape, -jnp.inf, F32)
        acc_ref[...] = jnp.zeros(acc_ref.shape, F32)
        for j in range(nk):
            if j + 1 < nk:
                scores(j + 1)
                softmax_pv(j)
            elif nk > 1:
                next_tile_first_scores(i)
                softmax_pv(j)
            else:
                softmax_pv(j)
                next_tile_first_scores(i)
        finalize(i)
        return carry

    lax.fori_loop(0, nsub, tile, 0)


def _attention(q, kt, v_ext, lam_p, gain, *, q_block0, n_q, key_block0, n_keys, tq, tk, group,
               lam_init):
    b = q.shape[0]
    assert n_q % group == 0 and group % tq == 0 and n_keys % tk == 0
    return pl.pallas_call(
        functools.partial(_attn_kernel, lam_init=lam_init, tq=tq, tk=tk),
        out_shape=jax.ShapeDtypeStruct((b, n_q, A_WIDTH), BF16),
        grid=(b, A_HEADS, n_q // group),
        in_specs=[
            pl.BlockSpec((None, group, A_DV), lambda bb, h, i: (bb, q_block0 + i, h)),
            pl.BlockSpec((None, None, A_DV, n_keys), lambda bb, h, i: (bb, h, 0, key_block0)),
            pl.BlockSpec((None, n_keys, 2 * LANES), lambda bb, h, i: (bb, key_block0, h)),
            pl.BlockSpec((4, A_DH), lambda bb, h, i: (0, 0)),
            pl.BlockSpec((1, A_DV), lambda bb, h, i: (0, 0)),
        ],
        out_specs=pl.BlockSpec((None, group, A_DV), lambda bb, h, i: (bb, i, h)),
        scratch_shapes=[
            pltpu.VMEM((2 * tq, A_DV), BF16),
            pltpu.VMEM((3, 2 * tq, tk), F32),
            pltpu.VMEM((2 * tq, LANES), F32),
            pltpu.VMEM((2 * tq, 2 * LANES), F32),
        ],
        compiler_params=_cparams(("parallel", "parallel", "arbitrary")),
        name="diff_attn",
    )(q, kt, v_ext, lam_p, gain)


_HI16 = 0xFFFF0000


def _pack_bf16_pair(lo, hi):
    ulo = lax.bitcast_convert_type(lo.astype(BF16).astype(F32), jnp.uint32) >> 16
    uhi = lax.bitcast_convert_type(hi.astype(BF16).astype(F32), jnp.uint32) & jnp.uint32(_HI16)
    return ulo | uhi


def _top_half_bits(x):
    return lax.bitcast_convert_type(lax.bitcast_convert_type(x, jnp.uint32) & jnp.uint32(_HI16), F32)


def _unpack_bf16_pair(u):
    lo = lax.bitcast_convert_type(u << 16, F32)
    hi = lax.bitcast_convert_type(u & jnp.uint32(_HI16), F32)
    return lo, hi


ROUTE_CLASS = 0
N_PAIRS = EXPERTS_PER_GROUP * (EXPERTS_PER_GROUP - 1) // 2
N_PAIR_CLASSES = N_EXPERT_GROUPS * N_PAIRS


def _router_gates(sc_t, sel_t, route_ref, pair_gates_ref):
    s_rows = [sc_t[e:e + 1, :] for e in range(N_EXPERTS)]
    v_rows = [sel_t[e:e + 1, :] for e in range(N_EXPERTS)]
    in_top2 = []
    group_score = []
    for g in range(N_EXPERT_GROUPS):
        vs = v_rows[g * EXPERTS_PER_GROUP:(g + 1) * EXPERTS_PER_GROUP]
        tops = []
        for jj in range(EXPERTS_PER_GROUP):
            rank = jnp.zeros_like(vs[jj])
            for ii in range(EXPERTS_PER_GROUP):
                if ii == jj:
                    continue
                beats = (vs[ii] >= vs[jj]) if ii < jj else (vs[ii] > vs[jj])
                rank = rank + jnp.where(beats, 1.0, 0.0)
            tops.append(rank < 2.0)
        in_top2 += tops
        gs = jnp.zeros_like(vs[0])
        for jj in range(EXPERTS_PER_GROUP):
            gs = gs + jnp.where(tops[jj], vs[jj], 0.0)
        group_score.append(gs)
    best = group_score[0]
    gidx = jnp.zeros_like(best)
    for g in range(1, N_EXPERT_GROUPS):
        upd = group_score[g] > best
        best = jnp.where(upd, group_score[g], best)
        gidx = jnp.where(upd, float(g), gidx)
    chosen = [jnp.logical_and(in_top2[e], gidx == float(e // EXPERTS_PER_GROUP))
              for e in range(N_EXPERTS)]
    denom = jnp.zeros_like(best)
    for e in range(N_EXPERTS):
        denom = denom + jnp.where(chosen[e], s_rows[e], 0.0)
    e_lo = jnp.zeros_like(best)
    g_lo = jnp.zeros_like(best)
    for e in reversed(range(N_EXPERTS)):
        e_lo = jnp.where(chosen[e], float(e), e_lo)
        g_lo = jnp.where(chosen[e], s_rows[e] / denom, g_lo)
    e_hi = jnp.zeros_like(best)
    g_hi = jnp.zeros_like(best)
    for e in range(N_EXPERTS):
        e_hi = jnp.where(chosen[e], float(e), e_hi)
        g_hi = jnp.where(chosen[e], s_rows[e] / denom, g_hi)
    a_lo = e_lo - EXPERTS_PER_GROUP * gidx
    a_hi = e_hi - EXPERTS_PER_GROUP * gidx
    pair_rank = a_lo * (2 * EXPERTS_PER_GROUP - 1 - a_lo) * 0.5 + (a_hi - a_lo - 1.0)
    route_ref[...] = jnp.zeros(route_ref.shape, F32)
    route_ref[ROUTE_CLASS:ROUTE_CLASS + 1, :] = gidx * float(N_PAIRS) + pair_rank
    tm = g_lo.shape[1]
    pair_gates_ref[:, :LANES] = jnp.broadcast_to(g_lo, (LANES, tm)).T
    pair_gates_ref[:, LANES:] = jnp.broadcast_to(g_hi, (LANES, tm)).T


def _outproj_kernel(yf_lat_ref, yf_ctx_ref, att_lat_ref, att_ctx_ref, sg_ref, x_ref, mod_ref, wo_ref,
                    lng_ref, lnb_ref, wr_ref, rb_ref, x1_ref, h_ref, route_ref, pair_gates_ref,
                    *, alpha, nlat):
    is_ctx = pl.program_id(1) >= nlat
    yf = jnp.where(is_ctx, yf_ctx_ref[...], yf_lat_ref[...])
    att = jnp.where(is_ctx, att_ctx_ref[...], att_lat_ref[...])
    mix = _bdot(yf, wo_ref[0:F_WIDTH, :])
    mix = mix + _bdot(att, wo_ref[F_WIDTH:F_WIDTH + A_WIDTH, :])
    mix = mix + _bdot(sg_ref[...], wo_ref[F_WIDTH + A_WIDTH:, :])
    g1 = mod_ref[2:3, :]
    x1 = _layernorm(alpha * x_ref[...] + g1 * mix, lng_ref[...], lnb_ref[...])
    x1_ref[...] = x1
    h = x1 * (1.0 + mod_ref[4:5, :]) + mod_ref[3:4, :]
    half = h.shape[1] // 2
    h_ref[...] = _pack_bf16_pair(h[:, :half], h[:, half:])
    h_top = _top_half_bits(h)
    h_hi = h_top.astype(BF16)
    h_lo = (h - h_top).astype(BF16)
    hw = _bdot(h_hi, wr_ref[...])
    logits = hw[:, :ROUTER_LANES] + (hw[:, ROUTER_LANES:] + _bdot(h_lo, wr_ref[:, :ROUTER_LANES]))
    scores = jax.nn.sigmoid(logits)
    sel = scores + rb_ref[...]
    _router_gates(scores.T[0:N_EXPERTS, :], sel.T[0:N_EXPERTS, :], route_ref, pair_gates_ref)


def _outproj(yf_lat, yf_ctx, att_lat, att_ctx, sg, xa, mods_l, wo, lng, lnb, wr, rb, n_lat, alpha):
    b, l, d = xa.shape
    tm = TOKEN_TILE
    nlat = n_lat // tm
    seg = lambda i: jnp.where(i >= nlat, 1, 0)
    row = lambda width: pl.BlockSpec((None, tm, width), lambda bb, i: (bb, i, 0))
    lat = lambda width: pl.BlockSpec((None, tm, width), lambda bb, i: (bb, jnp.minimum(i, nlat - 1), 0))
    ctx = lambda width: pl.BlockSpec((None, tm, width), lambda bb, i: (bb, jnp.maximum(i - nlat, 0), 0))
    const2 = lambda shape: pl.BlockSpec(shape, lambda bb, i: (0, 0))
    return pl.pallas_call(
        functools.partial(_outproj_kernel, alpha=alpha, nlat=nlat),
        out_shape=(
            jax.ShapeDtypeStruct((b, l, d), F32),
            jax.ShapeDtypeStruct((b, l, d // 2), jnp.uint32),
            jax.ShapeDtypeStruct((b, SUBLANES, l), F32),
            jax.ShapeDtypeStruct((b, l, 2 * LANES), F32),
        ),
        grid=(b, l // tm),
        in_specs=[
            lat(F_WIDTH), ctx(F_WIDTH), lat(A_WIDTH), ctx(A_WIDTH), row(G_WIDTH), row(d),
            pl.BlockSpec((None, None, MOD_ROWS, d), lambda bb, i: (bb, seg(i), 0, 0)),
            const2((d, d)), const2((1, d)), const2((1, d)),
            const2((d, 2 * ROUTER_LANES)), const2((1, ROUTER_LANES)),
        ],
        out_specs=(row(d), row(d // 2),
                   pl.BlockSpec((None, SUBLANES, tm), lambda bb, i: (bb, 0, i)), row(2 * LANES)),
        compiler_params=_cparams(("parallel", "parallel")),
        name="outproj_ln_router",
    )(yf_lat, yf_ctx, att_lat, att_ctx, sg, xa, mods_l, wo, lng, lnb, wr, rb)


def _moe_plan(route, tm):
    b, _, l = route.shape
    t = b * l
    pairs = [(a, c) for a in range(EXPERTS_PER_GROUP) for c in range(a + 1, EXPERTS_PER_GROUP)]
    cls_e1 = np.zeros((N_PAIR_CLASSES,), np.int32)
    cls_e2 = np.zeros((N_PAIR_CLASSES,), np.int32)
    for k, (a, c) in enumerate(pairs):
        assert k == a * (2 * EXPERTS_PER_GROUP - 1 - a) // 2 + (c - a - 1)
        for g in range(N_EXPERT_GROUPS):
            cls_e1[g * N_PAIRS + k] = g * EXPERTS_PER_GROUP + a
            cls_e2[g * N_PAIRS + k] = g * EXPERTS_PER_GROUP + c
    cls = route[:, ROUTE_CLASS, :].astype(jnp.int32).reshape(t)
    onehot = cls[:, None] == jnp.arange(N_PAIR_CLASSES, dtype=jnp.int32)[None, :]
    oh_tiles = onehot.reshape(t // tm, tm, N_PAIR_CLASSES).astype(BF16)
    tri = jnp.tril(jnp.ones((tm, tm), BF16))
    within = jnp.einsum("ij,njc->nic", tri, oh_tiles, preferred_element_type=F32)
    tile_total = within[:, -1, :]
    n_tiles = t // tm
    earlier = jnp.tril(jnp.ones((n_tiles, n_tiles), BF16), k=-1)
    before = jnp.dot(earlier, tile_total.astype(BF16), preferred_element_type=F32)
    csum = (within + before[:, None, :]).reshape(t, N_PAIR_CLASSES).astype(jnp.int32)
    onehot = onehot.astype(jnp.int32)
    rank = jnp.sum(csum * onehot, axis=1) - 1
    padded = (csum[-1] + tm - 1) // tm * tm
    upto = jnp.tril(jnp.ones((N_PAIR_CLASSES, N_PAIR_CLASSES), jnp.int32))
    off_end = jnp.sum(upto * padded[None, :], axis=1)
    dest = jnp.sum(onehot * (off_end - padded)[None, :], axis=1) + rank
    nt = t // tm + N_PAIR_CLASSES
    n_valid = (off_end[-1] // tm).astype(jnp.int32)
    first_row = jnp.minimum(jnp.arange(nt, dtype=jnp.int32), n_valid - 1) * tm
    tile_cls = jnp.sum((off_end[None, :] <= first_row[:, None]).astype(jnp.int32), axis=1)
    tile_cls = jnp.minimum(tile_cls, N_PAIR_CLASSES - 1)
    return (dest.astype(jnp.int32), jnp.asarray(cls_e1)[tile_cls], jnp.asarray(cls_e2)[tile_cls],
            n_valid.reshape(1), nt)


def _dispatch_kernel(dest_ref, h_ref, hs_init_ref, hs_ref, stage, sems):
    del hs_init_ref
    s = pl.program_id(0)
    n = pl.num_programs(0)
    tm = h_ref.shape[0]
    slot = s % 2

    def wait_slot(k):
        pltpu.make_async_copy(stage.at[k], hs_ref.at[pl.ds(0, tm), :], sems.at[k]).wait()

    @pl.when(s >= 2)
    def _reuse():
        wait_slot(slot)

    stage[slot] = h_ref[...]
    for r in range(tm):
        pltpu.make_async_copy(stage.at[slot, pl.ds(r, 1), :],
                              hs_ref.at[pl.ds(dest_ref[0, r], 1), :], sems.at[slot]).start()

    @pl.when(s == n - 1)
    def _drain():
        @pl.when(n >= 2)
        def _other():
            wait_slot(1 - slot)
        wait_slot(slot)


def _dispatch(h_packed, dest, nt, tm):
    t, w = h_packed.shape
    hs_init = jnp.zeros((nt * tm, w), jnp.uint32)
    return pl.pallas_call(
        _dispatch_kernel,
        out_shape=jax.ShapeDtypeStruct((nt * tm, w), jnp.uint32),
        grid=(t // tm,),
        in_specs=[
            pl.BlockSpec((None, 1, tm), lambda i: (i, 0, 0), memory_space=pltpu.SMEM),
            pl.BlockSpec((tm, w), lambda i: (i, 0)),
            pl.BlockSpec(memory_space=pl.ANY),
        ],
        out_specs=pl.BlockSpec(memory_space=pl.ANY),
        scratch_shapes=[pltpu.VMEM((2, tm, w), jnp.uint32), pltpu.SemaphoreType.DMA((2,))],
        input_output_aliases={2: 0},
        compiler_params=_cparams(("arbitrary",)),
        name="moe_dispatch",
    )(dest.reshape(t // tm, 1, tm), h_packed, hs_init)


def _expert_pair_kernel(e1_ref, e2_ref, nv_ref, hs_ref, w1a_ref, w3a_ref, w2a_ref,
                        w1b_ref, w3b_ref, w2b_ref, y_ref):
    del e1_ref, e2_ref
    i = pl.program_id(0)

    @pl.when(i < nv_ref[0])
    def _compute():
        lo, hi = _unpack_bf16_pair(hs_ref[...])
        h = jnp.concatenate([lo, hi], axis=1).astype(BF16)

        def ffn(w1_ref, w3_ref, w2_ref):
            a = _bdot(h, w1_ref[...])
            a = (a * jax.nn.sigmoid(a)) * _bdot(h, w3_ref[...])
            return _bdot(a.astype(BF16), w2_ref[...])

        y_ref[...] = _pack_bf16_pair(ffn(w1a_ref, w3a_ref, w2a_ref), ffn(w1b_ref, w3b_ref, w2b_ref))

    @pl.when(i >= nv_ref[0])
    def _unused_tile():
        y_ref[...] = jnp.zeros(y_ref.shape, jnp.uint32)


def _expert_pairs(hs, tile_e1, tile_e2, n_valid, w1, w3, w2, tm):
    rows, w = hs.shape
    _, d, de = w1.shape
    first = lambda shape: pl.BlockSpec(shape, lambda i, e1, e2, nv: (e1[i], 0, 0))
    second = lambda shape: pl.BlockSpec(shape, lambda i, e1, e2, nv: (e2[i], 0, 0))
    grid_spec = pltpu.PrefetchScalarGridSpec(
        num_scalar_prefetch=3,
        grid=(rows // tm,),
        in_specs=[
            pl.BlockSpec((tm, w), lambda i, e1, e2, nv: (jnp.minimum(i, nv[0] - 1), 0)),
            first((None, d, de)), first((None, d, de)), first((None, de, d)),
            second((None, d, de)), second((None, d, de)), second((None, de, d)),
        ],
        out_specs=pl.BlockSpec((tm, d), lambda i, e1, e2, nv: (i, 0)),
    )
    return pl.pallas_call(
        _expert_pair_kernel,
        out_shape=jax.ShapeDtypeStruct((rows, d), jnp.uint32),
        grid_spec=grid_spec,
        compiler_params=_cparams(("arbitrary",)),
        name="moe_expert_pairs",
    )(tile_e1, tile_e2, n_valid, hs, w1, w3, w2, w1, w3, w2)


def _combine_kernel(dest_ref, dest_next_ref, y2_ref, x_ref, pg_ref, mod_ref, lng_ref, lnb_ref,
                    o_ref, ybuf, sems, *, alpha):
    o_ref[...] = _combine_body(dest_ref, dest_next_ref, y2_ref, x_ref, pg_ref, mod_ref, lng_ref,
                               lnb_ref, ybuf, sems, alpha=alpha)


def _combine_inproj_kernel(dest_ref, dest_next_ref, y2_ref, x_ref, pg_ref, mod_ref, lng_ref, lnb_ref,
                           *refs, alpha, q_scale):
    n_inproj_in = 8
    inproj_in = refs[:n_inproj_in]
    o_ref = refs[n_inproj_in]
    inproj_out = refs[n_inproj_in + 1:n_inproj_in + 7]
    ybuf, sems = refs[n_inproj_in + 7:]
    x_new = _combine_body(dest_ref, dest_next_ref, y2_ref, x_ref, pg_ref, mod_ref, lng_ref, lnb_ref,
                          ybuf, sems, alpha=alpha)
    o_ref[...] = x_new
    _inproj_body(x_new, *inproj_in, *inproj_out, q_scale=q_scale)


def _combine_body(dest_ref, dest_next_ref, y2_ref, x_ref, pg_ref, mod_ref, lng_ref, lnb_ref,
                  ybuf, sems, *, alpha):
    s = pl.program_id(0)
    n = pl.num_programs(0)
    tm, d = x_ref.shape
    slot = s % 2

    def start_gather(idx_ref, k):
        for r in range(tm):
            pltpu.make_async_copy(y2_ref.at[pl.ds(idx_ref[0, r], 1), :],
                                  ybuf.at[k, pl.ds(r, 1), :], sems.at[k]).start()

    @pl.when(s == 0)
    def _first():
        start_gather(dest_ref, 0)

    @pl.when(s + 1 < n)
    def _prefetch():
        start_gather(dest_next_ref, 1 - slot)

    pltpu.make_async_copy(y2_ref.at[pl.ds(0, tm), :], ybuf.at[slot], sems.at[slot]).wait()
    y_lo, y_hi = _unpack_bf16_pair(ybuf[slot])
    pg = pg_ref[...]
    reps = d // LANES
    g_lo = jnp.concatenate([pg[:, :LANES]] * reps, axis=1)
    g_hi = jnp.concatenate([pg[:, LANES:]] * reps, axis=1)
    y = g_lo * y_lo + g_hi * y_hi
    return _layernorm(alpha * x_ref[...] + mod_ref[5:6, :] * y, lng_ref[...], lnb_ref[...])


def _combine(y2, dest, x1, pair_gates, mods_l, lng, lnb, *, n_rows, n_lat, tm, alpha, next_inproj=None):
    b, l, d = x1.shape
    nlat = n_lat // tm
    tiles_per_batch = l // tm
    per_batch = n_rows // tm
    n_steps = b * per_batch
    batch = lambda s: s // per_batch
    tile = lambda s: s % per_batch
    seg = lambda s: jnp.where(tile(s) >= nlat, 1, 0)
    token_tile = lambda s: batch(s) * tiles_per_batch + tile(s)
    row = lambda width: pl.BlockSpec((None, tm, width), lambda s: (batch(s), tile(s), 0))
    const2 = lambda shape: pl.BlockSpec(shape, lambda s: (0, 0))
    mod_spec = pl.BlockSpec((None, None, MOD_ROWS, d), lambda s: (batch(s), seg(s), 0, 0))
    dest3 = dest.reshape(b * tiles_per_batch, 1, tm)
    in_specs = [
        pl.BlockSpec((None, 1, tm), lambda s: (token_tile(s), 0, 0), memory_space=pltpu.SMEM),
        pl.BlockSpec((None, 1, tm), lambda s: (token_tile(jnp.minimum(s + 1, n_steps - 1)), 0, 0),
                     memory_space=pltpu.SMEM),
        pl.BlockSpec(memory_space=pl.ANY),
        row(d), row(2 * LANES), mod_spec, const2((1, d)), const2((1, d)),
    ]
    args = [dest3, dest3, y2, x1, pair_gates, mods_l, lng, lnb]
    out_shape = [jax.ShapeDtypeStruct((b, n_rows, d), F32)]
    out_specs = [row(d)]
    if next_inproj is None:
        body = functools.partial(_combine_kernel, alpha=alpha)
        name = "moe_combine_ln"
    else:
        assert n_rows == l and tm == TOKEN_TILE
        mods_n, w_big, cosf, sinf, slng, slnb, ws, bs_full, q_scale = next_inproj
        in_specs += [
            mod_spec, const2((d, BIG_COLS)),
            pl.BlockSpec((tm, LANES), lambda s: (tile(s), 0)),
            pl.BlockSpec((tm, LANES), lambda s: (tile(s), 0)),
            const2((1, G_WIDTH)), const2((1, G_WIDTH)),
            pl.BlockSpec((G_GROUPS, CHUNK, CHUNK), lambda s: (0, 0, 0)),
            const2((CHUNK, G_WIDTH)),
        ]
        args += [mods_n, w_big, cosf, sinf, slng, slnb, ws, bs_full]
        out_shape += list(_inproj_out_shapes(b, l))
        out_specs += [row(F_WIDTH), row(F_WIDTH), row(A_QK_WIDTH),
                      pl.BlockSpec((None, A_HEADS, A_DV, tm), lambda s: (batch(s), 0, 0, tile(s))),
                      row(2 * A_WIDTH), row(G_WIDTH)]
        body = functools.partial(_combine_inproj_kernel, alpha=alpha, q_scale=q_scale)
        name = "moe_combine_ln_inproj"
    outs = pl.pallas_call(
        body,
        out_shape=tuple(out_shape),
        grid=(n_steps,),
        in_specs=in_specs,
        out_specs=tuple(out_specs),
        scratch_shapes=[pltpu.VMEM((2, tm, d), jnp.uint32), pltpu.SemaphoreType.DMA((2,))],
        compiler_params=_cparams(("arbitrary",)),
        name=name,
    )(*args)
    return outs[0] if next_inproj is None else outs


def _head_lane_fields():
    j = jnp.arange(LANES)
    half = j // (LANES // 2)
    comp = (j % (LANES // 2)) // (A_DH // 2)
    axis = (j % (A_DH // 2)) // (A_DH // 4)
    freq = j % (A_DH // 4)
    return half, comp, axis, freq


def _qk_column_perm():
    half, comp, axis, freq = _head_lane_fields()
    orig = comp * A_DH + axis * (A_DH // 2) + half * (A_DH // 4) + freq
    return (jnp.arange(A_HEADS)[:, None] * LANES + orig[None, :]).reshape(-1)


def _rope_tables(n_lat, n_ctx):
    half, _, axis, freq = _head_lane_fields()
    rows = n_lat // GRID_W
    row = jnp.repeat(jnp.arange(rows, dtype=F32), GRID_W)
    col = jnp.tile(jnp.arange(GRID_W, dtype=F32), rows)
    hd = A_DH // 2
    inv = ROPE_BASE ** (-jnp.arange(0, hd, 2, dtype=F32) / hd)
    pos = jnp.where(axis[None, :] == 0, row[:, None], col[:, None])
    ang = pos * inv[freq][None, :]
    cosf = jnp.cos(ang)
    sinf = jnp.sin(ang) * jnp.where(half == 0, -1.0, 1.0)[None, :]
    cosf = jnp.concatenate([cosf, jnp.ones((n_ctx, LANES), F32)], axis=0)
    sinf = jnp.concatenate([sinf, jnp.zeros((n_ctx, LANES), F32)], axis=0)
    return cosf, sinf


def _dft_mats(n, scale=1.0):
    i = jnp.arange(n, dtype=jnp.int32)
    ang = ((i[:, None] * i[None, :]) % n).astype(F32) * (2.0 * math.pi / n)
    return jnp.cos(ang) * scale, jnp.sin(ang) * scale


def _twiddles(n, bn):
    ka = jnp.arange(DFT_A, dtype=jnp.int32)
    bb = jnp.arange(bn, dtype=jnp.int32)
    ang = ((ka[:, None] * bb[None, :]) % n).astype(F32) * (2.0 * math.pi / n)
    shape = (DFT_A, bn, F_WIDTH)
    twc = jnp.broadcast_to(jnp.cos(ang)[:, :, None], shape).reshape(DFT_A, bn * F_WIDTH)
    tws = jnp.broadcast_to(jnp.sin(ang)[:, :, None], shape).reshape(DFT_A, bn * F_WIDTH)
    return twc, tws


def _block_diag(blocks):
    g, n = blocks.shape[-3], blocks.shape[-1]
    eye = jnp.eye(g, dtype=blocks.dtype)
    out = blocks[..., :, :, None, :] * eye[:, None, :, None]
    return out.reshape(blocks.shape[:-3] + (g * n, g * n))


def kernel(x, c, ctx, c_ctx, w_ada, b_ada, w_in, w_fourier, diff_lambda, diff_subln,
           sgu_ln_g, sgu_ln_b, sgu_w, sgu_b, w_out, ln_g, ln_b, w_router, router_bias,
           moe_w1, moe_w3, moe_w2):
    b, n, d = x.shape
    n_ctx = ctx.shape[1]
    depth = w_ada.shape[0]
    l = n + n_ctx
    alpha = (2 * depth) ** 0.25
    bn = n // DFT_A
    assert n % DFT_A == 0 and bn % SUBLANES == 0 and n % GRID_W == 0
    assert n_ctx % TOKEN_TILE == 0 and n % TOKEN_TILE == 0 and n % n_ctx == 0
    assert b + 1 <= MOD_ROWS

    c_rows = jnp.concatenate([c, c_ctx[None, :], jnp.zeros((MOD_ROWS - b - 1, d), F32)], axis=0)
    mod = _ada(c_rows, w_ada, b_ada).reshape(depth, MOD_ROWS, 6, d)
    lat = mod[:, :b]
    cm = jnp.broadcast_to(mod[:, b:b + 1], lat.shape)
    mods = jnp.stack([lat, cm], axis=2)
    mods = jnp.pad(mods, ((0, 0), (0, 0), (0, 0), (0, MOD_ROWS - 6), (0, 0)))

    perm = _qk_column_perm()
    c64, s64 = _dft_mats(F_DIM)
    eye_g = jnp.eye(F_GROUPS, dtype=F32)
    bdc = jnp.kron(eye_g, c64)
    bds = jnp.kron(eye_g, s64)
    bdw = _block_diag(w_fourier)
    w_fab = _fourier_weights(w_in[:, :, :F_WIDTH], bdc, bds, bdw)
    o_q = F_WIDTH
    o_k = o_q + A_QK_WIDTH
    o_v = o_k + A_QK_WIDTH
    o_g = o_v + A_WIDTH
    w_big = jnp.concatenate([
        w_fab,
        w_in[:, :, o_q:o_k][:, :, perm],
        w_in[:, :, o_k:o_v][:, :, perm],
        w_in[:, :, o_v:],
    ], axis=-1).astype(BF16)
    wo = w_out.astype(BF16)
    w1 = moe_w1.astype(BF16)
    w3 = moe_w3.astype(BF16)
    w2 = moe_w2.astype(BF16)
    ws = sgu_w.astype(BF16)
    bs_full = jnp.repeat(jnp.swapaxes(sgu_b, 1, 2), G_DIM, axis=2)
    wr_f32 = jnp.pad(w_router, ((0, 0), (0, ROUTER_LANES - N_EXPERTS)))
    wr_top = _top_half_bits(wr_f32)
    wr = jnp.concatenate([wr_top.astype(BF16), (wr_f32 - wr_top).astype(BF16)], axis=1)
    rb =jnp.pad(router_bias, (0, ROUTER_LANES - N_EXPERTS)).reshape(1, ROUTER_LANES)

    cosf, sinf = _rope_tables(n, n_ctx)
    ca, sa = _dft_mats(DFT_A)
    l_a = jnp.concatenate([_stack3([ca, -sa]), _stack3([-sa, -ca])], axis=0)
    twc, tws = _twiddles(n, bn)
    l_b = _stack3(list(_dft_mats(bn, scale=(n * F_DIM) ** -0.5)))
    cc, sc = _dft_mats(n_ctx, scale=(n_ctx * F_DIM) ** -0.5)
    l_c = _stack3([cc, -sc])
    q_scale = (A_DH ** -0.5) * math.log2(math.e)

    tk = ATTN_TK
    assert l % tk == 0
    assert n % MOE_TILE == 0 and n_ctx % MOE_TILE == 0

    def inproj_params(li):
        return (mods[li], w_big[li], cosf, sinf, sgu_ln_g[li][None], sgu_ln_b[li][None],
                ws[li], bs_full[li])

    xa = jnp.concatenate([x, ctx], axis=1)
    fa, fb, q, kt, v_ext, sg = _inproj(xa, *inproj_params(0), n, q_scale)
    for li in range(depth):
        last = li == depth - 1
        lam_init = 0.8 - 0.6 * math.exp(-0.3 * li)
        zr, zi = _fft_a(fa, fb, l_a, twc, tws, bn)
        yf_x = _fft_b(zr, zi, l_b, bn)
        yf_c = _dft_ctx(fa, fb, l_c, n, n_ctx)
        gain = diff_subln[li][None]
        att_x = _attention(q, kt, v_ext, diff_lambda[li], gain, q_block0=0, n_q=n, key_block0=0,
                           n_keys=l, tq=ATTN_TQ, tk=tk, group=min(ATTN_GROUP, n), lam_init=lam_init)
        att_c = _attention(q, kt, v_ext, diff_lambda[li], gain, q_block0=n // n_ctx, n_q=n_ctx,
                           key_block0=n // n_ctx, n_keys=n_ctx, tq=n_ctx, tk=n_ctx, group=n_ctx,
                           lam_init=lam_init)
        x1, h_packed, route, pair_gates = _outproj(
            yf_x, yf_c, att_x, att_c, sg, xa, mods[li], wo[li], ln_g[li, 0][None], ln_b[li, 0][None], wr, rb, n, alpha)
        dest, tile_e1, tile_e2, n_valid, nt = _moe_plan(route, MOE_TILE)
        hs = _dispatch(h_packed.reshape(b * l, d // 2), dest, nt, MOE_TILE)
        y2 = _expert_pairs(hs, tile_e1, tile_e2, n_valid, w1[li], w3[li], w2[li], MOE_TILE)
        combine_args = (y2, dest, x1, pair_gates, mods[li], ln_g[li, 1][None], ln_b[li, 1][None])
        if last:
            xa = _combine(*combine_args, n_rows=n, n_lat=n, tm=MOE_TILE, alpha=alpha)
        else:
            xa, fa, fb, q, kt, v_ext, sg = _combine(
                *combine_args, n_rows=l, n_lat=n, tm=MOE_TILE, alpha=alpha,
                next_inproj=inproj_params(li + 1) + (q_scale,))
    return xa
```

```python
import functools
import math

import numpy as np
import jax
import jax.numpy as jnp
from jax import lax
from jax.experimental import pallas as pl
from jax.experimental.pallas import tpu as pltpu

F32 = jnp.float32
BF16 = jnp.bfloat16
HIGHEST = lax.Precision.HIGHEST

GRID_W = 64
F_GROUPS, F_DIM = 4, 64
F_WIDTH = F_GROUPS * F_DIM
A_HEADS, A_DH = 4, 64
A_DV = 2 * A_DH
A_QK_WIDTH = A_HEADS * 2 * A_DH
A_WIDTH = A_HEADS * A_DV
G_GROUPS, G_DIM = 4, 64
G_WIDTH = G_GROUPS * G_DIM
CHUNK = 128
ROPE_BASE = 10000.0
N_EXPERTS = 16
N_EXPERT_GROUPS = 4
EXPERTS_PER_GROUP = N_EXPERTS // N_EXPERT_GROUPS
LN_EPS = 1e-5
RMS_EPS = 1e-5

COL_FA = 0
COL_FB = COL_FA + F_WIDTH
COL_Q = COL_FB + F_WIDTH
COL_K = COL_Q + A_QK_WIDTH
COL_V = COL_K + A_QK_WIDTH
COL_G = COL_V + A_WIDTH
BIG_COLS = COL_G + 2 * G_WIDTH

LANES = 128
SUBLANES = 8
MOD_ROWS = 8
ROUTER_LANES = 128
DFT_A = 128
VMEM_LIMIT = 56 * 1024 * 1024

TOKEN_TILE = 256
ATTN_TQ = 256
ATTN_GROUP = 8192
ATTN_TK = 1280
MOE_TILE = 256


def _cparams(sem):
    return pltpu.CompilerParams(dimension_semantics=sem, vmem_limit_bytes=VMEM_LIMIT)


def _hdot(a, b):
    return jnp.dot(a, b, precision=HIGHEST, preferred_element_type=F32)


def _bdot(a, b):
    return jnp.dot(a, b, preferred_element_type=F32)


def _layernorm(y, g, b):
    mu = jnp.mean(y, axis=-1, keepdims=True)
    d = y - mu
    var = jnp.mean(d * d, axis=-1, keepdims=True)
    return d * lax.rsqrt(var + LN_EPS) * g + b


def _ada_kernel(c_ref, w_ref, b_ref, o_ref):
    c = c_ref[...]
    a = c * jax.nn.sigmoid(c)
    o_ref[...] = _hdot(a, w_ref[...]) + b_ref[...]


def _ada(c_rows, w_ada, b_ada):
    depth, d, cols = w_ada.shape
    tn = 1536
    assert cols % tn == 0
    return pl.pallas_call(
        _ada_kernel,
        out_shape=jax.ShapeDtypeStruct((depth, MOD_ROWS, cols), F32),
        grid=(depth, cols // tn),
        in_specs=[
            pl.BlockSpec((MOD_ROWS, d), lambda l, j: (0, 0)),
            pl.BlockSpec((None, d, tn), lambda l, j: (l, 0, j)),
            pl.BlockSpec((None, 1, tn), lambda l, j: (l, 0, j)),
        ],
        out_specs=pl.BlockSpec((None, MOD_ROWS, tn), lambda l, j: (l, 0, j)),
        compiler_params=_cparams(("parallel", "parallel")),
        name="ada_mod",
    )(c_rows, w_ada, b_ada.reshape(depth, 1, cols))


def _fw_kernel(wf_ref, bdc_ref, bds_ref, bdw_ref, o_ref):
    bdw = bdw_ref[...]
    mc = _hdot(bdc_ref[...], bdw)
    ms = _hdot(bds_ref[...], bdw)
    wf = wf_ref[...]
    o_ref[:, :F_WIDTH] = _hdot(wf, mc)
    o_ref[:, F_WIDTH:] = _hdot(wf, ms)


def _fourier_weights(wf, bdc, bds, bdw):
    depth, d, _ = wf.shape
    return pl.pallas_call(
        _fw_kernel,
        out_shape=jax.ShapeDtypeStruct((depth, d, 2 * F_WIDTH), F32),
        grid=(depth,),
        in_specs=[
            pl.BlockSpec((None, d, F_WIDTH), lambda l: (l, 0, 0)),
            pl.BlockSpec((F_WIDTH, F_WIDTH), lambda l: (0, 0)),
            pl.BlockSpec((F_WIDTH, F_WIDTH), lambda l: (0, 0)),
            pl.BlockSpec((None, F_WIDTH, F_WIDTH), lambda l: (l, 0, 0)),
        ],
        out_specs=pl.BlockSpec((None, d, 2 * F_WIDTH), lambda l: (l, 0, 0)),
        compiler_params=_cparams(("parallel",)),
        name="fourier_weights",
    )(wf, bdc, bds, bdw)


def _gelu_tanh(x):
    c = math.sqrt(2.0 / math.pi)
    return x * (0.5 * (1.0 + jnp.tanh(c * (x + 0.044715 * (x * x * x)))))


def _inproj_kernel(x_ref, *refs, q_scale):
    _inproj_body(x_ref[...], *refs, q_scale=q_scale)


def _inproj_body(x, mod_ref, w_ref, cos_ref, sin_ref, lng_ref, lnb_ref, ws_ref, bs_ref,
                 fa_ref, fb_ref, q_ref, kt_ref, v_ref, sg_ref, *, q_scale):
    sh = mod_ref[0:1, :]
    sc = mod_ref[1:2, :]
    h = (x * (1.0 + sc) + sh).astype(BF16)
    p = _bdot(h, w_ref[...])
    fa_ref[...] = p[:, COL_FA:COL_FA + F_WIDTH]
    fb_ref[...] = p[:, COL_FB:COL_FB + F_WIDTH]
    cosf = cos_ref[...]
    sinf = sin_ref[...]
    for hh in range(A_HEADS):
        t = p[:, COL_Q + LANES * hh:COL_Q + LANES * (hh + 1)]
        r = t * cosf + pltpu.roll(t, LANES // 2, 1) * sinf
        q_ref[:, LANES * hh:LANES * (hh + 1)] = (r * q_scale).astype(BF16)
        t = p[:, COL_K + LANES * hh:COL_K + LANES * (hh + 1)]
        r = t * cosf + pltpu.roll(t, LANES // 2, 1) * sinf
        kt_ref[hh] = r.T.astype(BF16)
        v_ref[:, 2 * LANES * hh:2 * LANES * hh + LANES] = (
            p[:, COL_V + LANES * hh:COL_V + LANES * (hh + 1)].astype(BF16))
        v_ref[:, 2 * LANES * hh + LANES:2 * LANES * (hh + 1)] = jnp.ones((t.shape[0], LANES), BF16)

    g = _gelu_tanh(p[:, COL_G:COL_G + 2 * G_WIDTH])
    u = g[:, :G_WIDTH]
    vn = _layernorm(g[:, G_WIDTH:], lng_ref[...], lnb_ref[...])
    lane_group = lax.broadcasted_iota(jnp.int32, (CHUNK, G_WIDTH), 1) // G_DIM
    tm = x.shape[0]
    for c in range(tm // CHUNK):
        rows = slice(c * CHUNK, (c + 1) * CHUNK)
        vc = vn[rows]
        mixed = bs_ref[...]
        for gi in range(G_GROUPS):
            vm = jnp.where(lane_group == gi, vc, 0.0).astype(BF16)
            mixed = mixed + _bdot(ws_ref[gi], vm)
        sg_ref[rows, :] = (u[rows] * mixed).astype(BF16)


def _inproj_out_shapes(b, l):
    return (
        jax.ShapeDtypeStruct((b, l, F_WIDTH), F32),
        jax.ShapeDtypeStruct((b, l, F_WIDTH), F32),
        jax.ShapeDtypeStruct((b, l, A_QK_WIDTH), BF16),
        jax.ShapeDtypeStruct((b, A_HEADS, A_DV, l), BF16),
        jax.ShapeDtypeStruct((b, l, 2 * A_WIDTH), BF16),
        jax.ShapeDtypeStruct((b, l, G_WIDTH), BF16),
    )


def _inproj(xa, mods_l, w_big, cosf, sinf, lng, lnb, ws, bs_full, n_lat, q_scale):
    b, l, d = xa.shape
    tm = TOKEN_TILE
    nlat = n_lat // tm
    seg = lambda i: jnp.where(i >= nlat, 1, 0)
    row = lambda width: pl.BlockSpec((None, tm, width), lambda bb, i: (bb, i, 0))
    const2 = lambda shape: pl.BlockSpec(shape, lambda bb, i: (0, 0))
    return pl.pallas_call(
        functools.partial(_inproj_kernel, q_scale=q_scale),
        out_shape=_inproj_out_shapes(b, l),
        grid=(b, l // tm),
        in_specs=[
            row(d),
            pl.BlockSpec((None, None, MOD_ROWS, d), lambda bb, i: (bb, seg(i), 0, 0)),
            const2((d, BIG_COLS)),
            pl.BlockSpec((tm, LANES), lambda bb, i: (i, 0)),
            pl.BlockSpec((tm, LANES), lambda bb, i: (i, 0)),
            const2((1, G_WIDTH)),
            const2((1, G_WIDTH)),
            pl.BlockSpec((G_GROUPS, CHUNK, CHUNK), lambda bb, i: (0, 0, 0)),
            const2((CHUNK, G_WIDTH)),
        ],
        out_specs=(row(F_WIDTH), row(F_WIDTH), row(A_QK_WIDTH),
                   pl.BlockSpec((None, A_HEADS, A_DV, tm), lambda bb, i: (bb, 0, 0, i)),
                   row(2 * A_WIDTH), row(G_WIDTH)),
        compiler_params=_cparams(("parallel", "parallel")),
        name="inproj",
    )(xa, mods_l, w_big, cosf, sinf, lng, lnb, ws, bs_full)


def _split3(x):
    top = _top_half_bits(x)
    hi = top.astype(BF16)
    return [hi, (x - top).astype(BF16), hi]


def _stack3(mats):
    cols = []
    for m in mats:
        top = _top_half_bits(m)
        cols += [top.astype(BF16), top.astype(BF16), (m - top).astype(BF16)]
    return jnp.concatenate(cols, axis=1)


def _fft_a_kernel(fa_ref, fb_ref, l_ref, tc_ref, ts_ref, zr_ref, zi_ref):
    rhs = jnp.concatenate(_split3(fa_ref[...]) + _split3(fb_ref[...]), axis=0)
    z = _bdot(l_ref[...], rhs)
    zr = z[:DFT_A]
    zi = z[DFT_A:]
    tc = tc_ref[...]
    ts = ts_ref[...]
    zr_ref[...] = zr * tc + zi * ts
    zi_ref[...] = zi * tc - zr * ts


def _fft_a(fa, fb, l_a, twc, tws, bn):
    b, l, _ = fa.shape
    cols = bn * F_WIDTH
    fa3 = fa.reshape(b, l // bn, cols)
    fb3 = fb.reshape(b, l // bn, cols)
    tcw = min(cols, 2048)
    assert cols % tcw == 0
    blk = pl.BlockSpec((None, DFT_A, tcw), lambda bb, j: (bb, 0, j))
    tw = pl.BlockSpec((DFT_A, tcw), lambda bb, j: (0, j))
    zr, zi = pl.pallas_call(
        _fft_a_kernel,
        out_shape=(jax.ShapeDtypeStruct((b, DFT_A, cols), F32),) * 2,
        grid=(b, cols // tcw),
        in_specs=[blk, blk, pl.BlockSpec(l_a.shape, lambda bb, j: (0, 0)), tw, tw],
        out_specs=(blk, blk),
        compiler_params=_cparams(("parallel", "parallel")),
        name="fft_stage_a",
    )(fa3, fb3, l_a, twc, tws)
    return zr.reshape(b, DFT_A, bn, F_WIDTH), zi.reshape(b, DFT_A, bn, F_WIDTH)


def _fft_b_kernel(zr_ref, zi_ref, l_ref, o_ref):
    lm = l_ref[...]
    for j in range(zr_ref.shape[0]):
        rhs = jnp.concatenate(_split3(zr_ref[j]) + _split3(zi_ref[j]), axis=0)
        o_ref[:, F_WIDTH * j:F_WIDTH * (j + 1)] = _bdot(lm, rhs).astype(o_ref.dtype)


def _fft_b(zr4, zi4, l_b, bn):
    b = zr4.shape[0]
    kb = 8
    blk = pl.BlockSpec((None, kb, bn, F_WIDTH), lambda bb, i: (bb, i, 0, 0))
    out = pl.pallas_call(
        _fft_b_kernel,
        out_shape=jax.ShapeDtypeStruct((b, bn, DFT_A * F_WIDTH), BF16),
        grid=(b, DFT_A // kb),
        in_specs=[blk, blk, pl.BlockSpec(l_b.shape, lambda bb, i: (0, 0))],
        out_specs=pl.BlockSpec((None, bn, kb * F_WIDTH), lambda bb, i: (bb, 0, i)),
        compiler_params=_cparams(("parallel", "parallel")),
        name="fft_stage_b",
    )(zr4, zi4, l_b)
    return out.reshape(b, bn * DFT_A, F_WIDTH)


def _dft_ctx_kernel(fa_ref, fb_ref, l_ref, o_ref):
    rhs = jnp.concatenate(_split3(fa_ref[...]) + _split3(fb_ref[...]), axis=0)
    o_ref[...] = _bdot(l_ref[...], rhs).astype(o_ref.dtype)


def _dft_ctx(fa, fb, l_c, n_lat, n_ctx):
    b = fa.shape[0]
    blk = pl.BlockSpec((None, n_ctx, F_WIDTH), lambda bb: (bb, n_lat // n_ctx, 0))
    return pl.pallas_call(
        _dft_ctx_kernel,
        out_shape=jax.ShapeDtypeStruct((b, n_ctx, F_WIDTH), BF16),
        grid=(b,),
        in_specs=[blk, blk, pl.BlockSpec(l_c.shape, lambda bb: (0, 0))],
        out_specs=pl.BlockSpec((None, n_ctx, F_WIDTH), lambda bb: (bb, 0, 0)),
        compiler_params=_cparams(("parallel",)),
        name="dft_ctx",
    )(fa, fb, l_c)


def _score_slot(j):
    return 2 if j == 0 else (j - 1) % 2


def _attn_kernel(q_ref, kt_ref, v_ref, lam_ref, gain_ref, o_ref, q2_ref, s_ref, m_ref, acc_ref,
                 *, lam_init, tq, tk):
    nsub = q_ref.shape[0] // tq
    nk = kt_ref.shape[1] // tk

    def keys(j):
        return slice(j * tk, (j + 1) * tk)
    lane = lax.broadcasted_iota(jnp.int32, (tq, A_DV), 1)
    comp0 = (lane % A_DH) < (A_DH // 2)
    lp = lam_ref[...]
    lam = (jnp.exp(jnp.sum(lp[0:1] * lp[1:2], keepdims=True))
           - jnp.exp(jnp.sum(lp[2:3] * lp[3:4], keepdims=True)) + lam_init)
    out_gain = gain_ref[...] * (1.0 - lam_init)

    def rows(i):
        return pl.ds(pl.multiple_of(i * tq, tq), tq)

    def load_q2(i):
        q = q_ref[rows(i), :]
        zero = jnp.zeros_like(q)
        q2_ref[0:tq, :] = jnp.where(comp0, q, zero)
        q2_ref[tq:2 * tq, :] = jnp.where(comp0, zero, q)

    def scores(j):
        s_ref[_score_slot(j)] = _bdot(q2_ref[...], kt_ref[:, keys(j)])

    def softmax_pv(j, k, first):
        s = s_ref[_score_slot(j)]
        s_max = jnp.max(s, axis=1, keepdims=True)
        if first:
            m_new = jnp.broadcast_to(s_max, (2 * tq, LANES))
        else:
            m_old = m_ref[k]
            m_new = jnp.maximum(m_old, s_max)
            alpha = jnp.exp2(m_old - m_new)
        p = jnp.exp2(s - jnp.concatenate([m_new] * (tk // LANES), axis=1)).astype(BF16)
        pv = _bdot(p, v_ref[keys(j), :])
        if first:
            acc_ref[k] = pv
        else:
            acc_ref[k] = acc_ref[k] * jnp.concatenate([alpha, alpha], axis=1) + pv
        m_ref[k] = m_new

    def finalize(i, k):
        acc = acc_ref[k]
        o0 = acc[0:tq, 0:LANES] / acc[0:tq, LANES:]
        o1 = acc[tq:, 0:LANES] / acc[tq:, LANES:]
        o = o0 - lam * o1
        ms = jnp.mean(o * o, axis=-1, keepdims=True)
        o_ref[rows(i), :] = (o * lax.rsqrt(ms + RMS_EPS) * out_gain).astype(o_ref.dtype)

    def next_tile_first_scores(i):
        load_q2(jnp.minimum(i + 1, nsub - 1))
        scores(0)

    def tile(i, k, finish_previous):
        for j in range(nk):
            if j + 1 < nk:
                scores(j + 1)
            elif nk > 1:
                next_tile_first_scores(i)
            if j == 0 and finish_previous:
                finalize(jnp.maximum(i - 1, 0), 1 - k)
            softmax_pv(j, k, first=(j == 0))
            if nk == 1:
                next_tile_first_scores(i)

    load_q2(0)
    scores(0)
    if nsub % 2 == 0:
        acc_ref[1] = jnp.ones(acc_ref.shape[1:], F32)

        def pair(ii, carry):
            i = 2 * ii
            tile(i, 0, True)
            tile(i + 1, 1, True)
            return carry

        lax.fori_loop(0, nsub // 2, pair, 0)
        finalize(nsub - 1, 1)
    else:
        def single(i, carry):
            tile(i, 0, False)
            finalize(i, 0)
            return carry

        lax.fori_loop(0, nsub, single, 0)


def _attention(q, kt, v_ext, lam_p, gain, *, q_block0, n_q, key_block0, n_keys, tq, tk, group,
               lam_init):
    b = q.shape[0]
    assert n_q % group == 0 and group % tq == 0 and n_keys % tk == 0
    return pl.pallas_call(
        functools.partial(_attn_kernel, lam_init=lam_init, tq=tq, tk=tk),
        out_shape=jax.ShapeDtypeStruct((b, n_q, A_WIDTH), BF16),
        grid=(b, A_HEADS, n_q // group),
        in_specs=[
            pl.BlockSpec((None, group, A_DV), lambda bb, h, i: (bb, q_block0 + i, h)),
            pl.BlockSpec((None, None, A_DV, n_keys), lambda bb, h, i: (bb, h, 0, key_block0)),
            pl.BlockSpec((None, n_keys, 2 * LANES), lambda bb, h, i: (bb, key_block0, h)),
            pl.BlockSpec((4, A_DH), lambda bb, h, i: (0, 0)),
            pl.BlockSpec((1, A_DV), lambda bb, h, i: (0, 0)),
        ],
        out_specs=pl.BlockSpec((None, group, A_DV), lambda bb, h, i: (bb, i, h)),
        scratch_shapes=[
            pltpu.VMEM((2 * tq, A_DV), BF16),
            pltpu.VMEM((3, 2 * tq, tk), F32),
            pltpu.VMEM((2, 2 * tq, LANES), F32),
            pltpu.VMEM((2, 2 * tq, 2 * LANES), F32),
        ],
        compiler_params=_cparams(("parallel", "parallel", "arbitrary")),
        name="diff_attn",
    )(q, kt, v_ext, lam_p, gain)


_HI16 = 0xFFFF0000


def _pack_bf16_pair(lo, hi):
    ulo = lax.bitcast_convert_type(lo.astype(BF16).astype(F32), jnp.uint32) >> 16
    uhi = lax.bitcast_convert_type(hi.astype(BF16).astype(F32), jnp.uint32) & jnp.uint32(_HI16)
    return ulo | uhi


def _top_half_bits(x):
    return lax.bitcast_convert_type(lax.bitcast_convert_type(x, jnp.uint32) & jnp.uint32(_HI16), F32)


def _unpack_bf16_pair(u):
    lo = lax.bitcast_convert_type(u << 16, F32)
    hi = lax.bitcast_convert_type(u & jnp.uint32(_HI16), F32)
    return lo, hi


ROUTE_CLASS = 0
N_PAIRS = EXPERTS_PER_GROUP * (EXPERTS_PER_GROUP - 1) // 2
N_PAIR_CLASSES = N_EXPERT_GROUPS * N_PAIRS


def _router_gates(sc_t, sel_t, route_ref, pair_gates_ref):
    s_rows = [sc_t[e:e + 1, :] for e in range(N_EXPERTS)]
    v_rows = [sel_t[e:e + 1, :] for e in range(N_EXPERTS)]
    in_top2 = []
    group_score = []
    for g in range(N_EXPERT_GROUPS):
        vs = v_rows[g * EXPERTS_PER_GROUP:(g + 1) * EXPERTS_PER_GROUP]
        tops = []
        for jj in range(EXPERTS_PER_GROUP):
            rank = jnp.zeros_like(vs[jj])
            for ii in range(EXPERTS_PER_GROUP):
                if ii == jj:
                    continue
                beats = (vs[ii] >= vs[jj]) if ii < jj else (vs[ii] > vs[jj])
                rank = rank + jnp.where(beats, 1.0, 0.0)
            tops.append(rank < 2.0)
        in_top2 += tops
        gs = jnp.zeros_like(vs[0])
        for jj in range(EXPERTS_PER_GROUP):
            gs = gs + jnp.where(tops[jj], vs[jj], 0.0)
        group_score.append(gs)
    best = group_score[0]
    gidx = jnp.zeros_like(best)
    for g in range(1, N_EXPERT_GROUPS):
        upd = group_score[g] > best
        best = jnp.where(upd, group_score[g], best)
        gidx = jnp.where(upd, float(g), gidx)
    chosen = [jnp.logical_and(in_top2[e], gidx == float(e // EXPERTS_PER_GROUP))
              for e in range(N_EXPERTS)]
    denom = jnp.zeros_like(best)
    for e in range(N_EXPERTS):
        denom = denom + jnp.where(chosen[e], s_rows[e], 0.0)
    e_lo = jnp.zeros_like(best)
    g_lo = jnp.zeros_like(best)
    for e in reversed(range(N_EXPERTS)):
        e_lo = jnp.where(chosen[e], float(e), e_lo)
        g_lo = jnp.where(chosen[e], s_rows[e] / denom, g_lo)
    e_hi = jnp.zeros_like(best)
    g_hi = jnp.zeros_like(best)
    for e in range(N_EXPERTS):
        e_hi = jnp.where(chosen[e], float(e), e_hi)
        g_hi = jnp.where(chosen[e], s_rows[e] / denom, g_hi)
    a_lo = e_lo - EXPERTS_PER_GROUP * gidx
    a_hi = e_hi - EXPERTS_PER_GROUP * gidx
    pair_rank = a_lo * (2 * EXPERTS_PER_GROUP - 1 - a_lo) * 0.5 + (a_hi - a_lo - 1.0)
    route_ref[...] = jnp.zeros(route_ref.shape, F32)
    route_ref[ROUTE_CLASS:ROUTE_CLASS + 1, :] = gidx * float(N_PAIRS) + pair_rank
    tm = g_lo.shape[1]
    pair_gates_ref[:, :LANES] = jnp.broadcast_to(g_lo, (LANES, tm)).T
    pair_gates_ref[:, LANES:] = jnp.broadcast_to(g_hi, (LANES, tm)).T


def _outproj_kernel(yf_lat_ref, yf_ctx_ref, att_lat_ref, att_ctx_ref, sg_ref, x_ref, mod_ref, wo_ref,
                    lng_ref, lnb_ref, wr_ref, rb_ref, x1_ref, h_ref, route_ref, pair_gates_ref,
                    *, alpha, nlat):
    is_ctx = pl.program_id(1) >= nlat
    yf = jnp.where(is_ctx, yf_ctx_ref[...], yf_lat_ref[...])
    att = jnp.where(is_ctx, att_ctx_ref[...], att_lat_ref[...])
    mix = _bdot(yf, wo_ref[0:F_WIDTH, :])
    mix = mix + _bdot(att, wo_ref[F_WIDTH:F_WIDTH + A_WIDTH, :])
    mix = mix + _bdot(sg_ref[...], wo_ref[F_WIDTH + A_WIDTH:, :])
    g1 = mod_ref[2:3, :]
    x1 = _layernorm(alpha * x_ref[...] + g1 * mix, lng_ref[...], lnb_ref[...])
    x1_ref[...] = x1
    h = x1 * (1.0 + mod_ref[4:5, :]) + mod_ref[3:4, :]
    half = h.shape[1] // 2
    h_ref[...] = _pack_bf16_pair(h[:, :half], h[:, half:])
    h_top = _top_half_bits(h)
    h_hi = h_top.astype(BF16)
    h_lo = (h - h_top).astype(BF16)
    hw = _bdot(h_hi, wr_ref[...])
    logits = hw[:, :ROUTER_LANES] + (hw[:, ROUTER_LANES:] + _bdot(h_lo, wr_ref[:, :ROUTER_LANES]))
    scores = jax.nn.sigmoid(logits)
    sel = scores + rb_ref[...]
    _router_gates(scores.T[0:N_EXPERTS, :], sel.T[0:N_EXPERTS, :], route_ref, pair_gates_ref)


def _outproj(yf_lat, yf_ctx, att_lat, att_ctx, sg, xa, mods_l, wo, lng, lnb, wr, rb, n_lat, alpha):
    b, l, d = xa.shape
    tm = TOKEN_TILE
    nlat = n_lat // tm
    seg = lambda i: jnp.where(i >= nlat, 1, 0)
    row = lambda width: pl.BlockSpec((None, tm, width), lambda bb, i: (bb, i, 0))
    lat = lambda width: pl.BlockSpec((None, tm, width), lambda bb, i: (bb, jnp.minimum(i, nlat - 1), 0))
    ctx = lambda width: pl.BlockSpec((None, tm, width), lambda bb, i: (bb, jnp.maximum(i - nlat, 0), 0))
    const2 = lambda shape: pl.BlockSpec(shape, lambda bb, i: (0, 0))
    return pl.pallas_call(
        functools.partial(_outproj_kernel, alpha=alpha, nlat=nlat),
        out_shape=(
            jax.ShapeDtypeStruct((b, l, d), F32),
            jax.ShapeDtypeStruct((b, l, d // 2), jnp.uint32),
            jax.ShapeDtypeStruct((b, SUBLANES, l), F32),
            jax.ShapeDtypeStruct((b, l, 2 * LANES), F32),
        ),
        grid=(b, l // tm),
        in_specs=[
            lat(F_WIDTH), ctx(F_WIDTH), lat(A_WIDTH), ctx(A_WIDTH), row(G_WIDTH), row(d),
            pl.BlockSpec((None, None, MOD_ROWS, d), lambda bb, i: (bb, seg(i), 0, 0)),
            const2((d, d)), const2((1, d)), const2((1, d)),
            const2((d, 2 * ROUTER_LANES)), const2((1, ROUTER_LANES)),
        ],
        out_specs=(row(d), row(d // 2),
                   pl.BlockSpec((None, SUBLANES, tm), lambda bb, i: (bb, 0, i)), row(2 * LANES)),
        compiler_params=_cparams(("parallel", "parallel")),
        name="outproj_ln_router",
    )(yf_lat, yf_ctx, att_lat, att_ctx, sg, xa, mods_l, wo, lng, lnb, wr, rb)


def _moe_plan(route, tm):
    b, _, l = route.shape
    t = b * l
    pairs = [(a, c) for a in range(EXPERTS_PER_GROUP) for c in range(a + 1, EXPERTS_PER_GROUP)]
    cls_e1 = np.zeros((N_PAIR_CLASSES,), np.int32)
    cls_e2 = np.zeros((N_PAIR_CLASSES,), np.int32)
    for k, (a, c) in enumerate(pairs):
        assert k == a * (2 * EXPERTS_PER_GROUP - 1 - a) // 2 + (c - a - 1)
        for g in range(N_EXPERT_GROUPS):
            cls_e1[g * N_PAIRS + k] = g * EXPERTS_PER_GROUP + a
            cls_e2[g * N_PAIRS + k] = g * EXPERTS_PER_GROUP + c
    cls = route[:, ROUTE_CLASS, :].astype(jnp.int32).reshape(t)
    onehot = cls[:, None] == jnp.arange(N_PAIR_CLASSES, dtype=jnp.int32)[None, :]
    oh_tiles = onehot.reshape(t // tm, tm, N_PAIR_CLASSES).astype(BF16)
    tri = jnp.tril(jnp.ones((tm, tm), BF16))
    within = jnp.einsum("ij,njc->nic", tri, oh_tiles, preferred_element_type=F32)
    tile_total = within[:, -1, :]
    n_tiles = t // tm
    earlier = jnp.tril(jnp.ones((n_tiles, n_tiles), BF16), k=-1)
    before = jnp.dot(earlier, tile_total.astype(BF16), preferred_element_type=F32)
    csum = (within + before[:, None, :]).reshape(t, N_PAIR_CLASSES).astype(jnp.int32)
    onehot = onehot.astype(jnp.int32)
    rank = jnp.sum(csum * onehot, axis=1) - 1
    padded = (csum[-1] + tm - 1) // tm * tm
    upto = jnp.tril(jnp.ones((N_PAIR_CLASSES, N_PAIR_CLASSES), jnp.int32))
    off_end = jnp.sum(upto * padded[None, :], axis=1)
    dest = jnp.sum(onehot * (off_end - padded)[None, :], axis=1) + rank
    nt = t // tm + N_PAIR_CLASSES
    n_valid = (off_end[-1] // tm).astype(jnp.int32)
    first_row = jnp.minimum(jnp.arange(nt, dtype=jnp.int32), n_valid - 1) * tm
    tile_cls = jnp.sum((off_end[None, :] <= first_row[:, None]).astype(jnp.int32), axis=1)
    tile_cls = jnp.minimum(tile_cls, N_PAIR_CLASSES - 1)
    return (dest.astype(jnp.int32), jnp.asarray(cls_e1)[tile_cls], jnp.asarray(cls_e2)[tile_cls],
            n_valid.reshape(1), nt)


def _dispatch_kernel(dest_ref, h_ref, hs_init_ref, hs_ref, stage, sems):
    del hs_init_ref
    s = pl.program_id(0)
    n = pl.num_programs(0)
    tm = h_ref.shape[0]
    slot = s % 2

    def wait_slot(k):
        pltpu.make_async_copy(stage.at[k], hs_ref.at[pl.ds(0, tm), :], sems.at[k]).wait()

    @pl.when(s >= 2)
    def _reuse():
        wait_slot(slot)

    stage[slot] = h_ref[...]
    for r in range(tm):
        pltpu.make_async_copy(stage.at[slot, pl.ds(r, 1), :],
                              hs_ref.at[pl.ds(dest_ref[0, r], 1), :], sems.at[slot]).start()

    @pl.when(s == n - 1)
    def _drain():
        @pl.when(n >= 2)
        def _other():
            wait_slot(1 - slot)
        wait_slot(slot)


def _dispatch(h_packed, dest, nt, tm):
    t, w = h_packed.shape
    hs_init = jnp.zeros((nt * tm, w), jnp.uint32)
    return pl.pallas_call(
        _dispatch_kernel,
        out_shape=jax.ShapeDtypeStruct((nt * tm, w), jnp.uint32),
        grid=(t // tm,),
        in_specs=[
            pl.BlockSpec((None, 1, tm), lambda i: (i, 0, 0), memory_space=pltpu.SMEM),
            pl.BlockSpec((tm, w), lambda i: (i, 0)),
            pl.BlockSpec(memory_space=pl.ANY),
        ],
        out_specs=pl.BlockSpec(memory_space=pl.ANY),
        scratch_shapes=[pltpu.VMEM((2, tm, w), jnp.uint32), pltpu.SemaphoreType.DMA((2,))],
        input_output_aliases={2: 0},
        compiler_params=_cparams(("arbitrary",)),
        name="moe_dispatch",
    )(dest.reshape(t // tm, 1, tm), h_packed, hs_init)


def _expert_pair_kernel(e1_ref, e2_ref, nv_ref, hs_ref, w1a_ref, w3a_ref, w2a_ref,
                        w1b_ref, w3b_ref, w2b_ref, y_ref):
    del e1_ref, e2_ref
    i = pl.program_id(0)

    @pl.when(i < nv_ref[0])
    def _compute():
        lo, hi = _unpack_bf16_pair(hs_ref[...])
        h = jnp.concatenate([lo, hi], axis=1).astype(BF16)

        def ffn(w1_ref, w3_ref, w2_ref):
            a = _bdot(h, w1_ref[...])
            a = (a * jax.nn.sigmoid(a)) * _bdot(h, w3_ref[...])
            return _bdot(a.astype(BF16), w2_ref[...])

        y_ref[...] = _pack_bf16_pair(ffn(w1a_ref, w3a_ref, w2a_ref), ffn(w1b_ref, w3b_ref, w2b_ref))

    @pl.when(i >= nv_ref[0])
    def _unused_tile():
        y_ref[...] = jnp.zeros(y_ref.shape, jnp.uint32)


def _expert_pairs(hs, tile_e1, tile_e2, n_valid, w1, w3, w2, tm):
    rows, w = hs.shape
    _, d, de = w1.shape
    first = lambda shape: pl.BlockSpec(shape, lambda i, e1, e2, nv: (e1[i], 0, 0))
    second = lambda shape: pl.BlockSpec(shape, lambda i, e1, e2, nv: (e2[i], 0, 0))
    grid_spec = pltpu.PrefetchScalarGridSpec(
        num_scalar_prefetch=3,
        grid=(rows // tm,),
        in_specs=[
            pl.BlockSpec((tm, w), lambda i, e1, e2, nv: (jnp.minimum(i, nv[0] - 1), 0)),
            first((None, d, de)), first((None, d, de)), first((None, de, d)),
            second((None, d, de)), second((None, d, de)), second((None, de, d)),
        ],
        out_specs=pl.BlockSpec((tm, d), lambda i, e1, e2, nv: (i, 0)),
    )
    return pl.pallas_call(
        _expert_pair_kernel,
        out_shape=jax.ShapeDtypeStruct((rows, d), jnp.uint32),
        grid_spec=grid_spec,
        compiler_params=_cparams(("arbitrary",)),
        name="moe_expert_pairs",
    )(tile_e1, tile_e2, n_valid, hs, w1, w3, w2, w1, w3, w2)


def _combine_kernel(dest_ref, dest_next_ref, y2_ref, x_ref, pg_ref, mod_ref, lng_ref, lnb_ref,
                    o_ref, ybuf, sems, *, alpha):
    o_ref[...] = _combine_body(dest_ref, dest_next_ref, y2_ref, x_ref, pg_ref, mod_ref, lng_ref,
                               lnb_ref, ybuf, sems, alpha=alpha)


def _combine_inproj_kernel(dest_ref, dest_next_ref, y2_ref, x_ref, pg_ref, mod_ref, lng_ref, lnb_ref,
                           *refs, alpha, q_scale):
    n_inproj_in = 8
    inproj_in = refs[:n_inproj_in]
    o_ref = refs[n_inproj_in]
    inproj_out = refs[n_inproj_in + 1:n_inproj_in + 7]
    ybuf, sems = refs[n_inproj_in + 7:]
    x_new = _combine_body(dest_ref, dest_next_ref, y2_ref, x_ref, pg_ref, mod_ref, lng_ref, lnb_ref,
                          ybuf, sems, alpha=alpha)
    o_ref[...] = x_new
    _inproj_body(x_new, *inproj_in, *inproj_out, q_scale=q_scale)


def _combine_body(dest_ref, dest_next_ref, y2_ref, x_ref, pg_ref, mod_ref, lng_ref, lnb_ref,
                  ybuf, sems, *, alpha):
    s = pl.program_id(0)
    n = pl.num_programs(0)
    tm, d = x_ref.shape
    slot = s % 2

    def start_gather(idx_ref, k):
        for r in range(tm):
            pltpu.make_async_copy(y2_ref.at[pl.ds(idx_ref[0, r], 1), :],
                                  ybuf.at[k, pl.ds(r, 1), :], sems.at[k]).start()

    @pl.when(s == 0)
    def _first():
        start_gather(dest_ref, 0)

    @pl.when(s + 1 < n)
    def _prefetch():
        start_gather(dest_next_ref, 1 - slot)

    pltpu.make_async_copy(y2_ref.at[pl.ds(0, tm), :], ybuf.at[slot], sems.at[slot]).wait()
    y_lo, y_hi = _unpack_bf16_pair(ybuf[slot])
    pg = pg_ref[...]
    reps = d // LANES
    g_lo = jnp.concatenate([pg[:, :LANES]] * reps, axis=1)
    g_hi = jnp.concatenate([pg[:, LANES:]] * reps, axis=1)
    y = g_lo * y_lo + g_hi * y_hi
    return _layernorm(alpha * x_ref[...] + mod_ref[5:6, :] * y, lng_ref[...], lnb_ref[...])


def _combine(y2, dest, x1, pair_gates, mods_l, lng, lnb, *, n_rows, n_lat, tm, alpha, next_inproj=None):
    b, l, d = x1.shape
    nlat = n_lat // tm
    tiles_per_batch = l // tm
    per_batch = n_rows // tm
    n_steps = b * per_batch
    batch = lambda s: s // per_batch
    tile = lambda s: s % per_batch
    seg = lambda s: jnp.where(tile(s) >= nlat, 1, 0)
    token_tile = lambda s: batch(s) * tiles_per_batch + tile(s)
    row = lambda width: pl.BlockSpec((None, tm, width), lambda s: (batch(s), tile(s), 0))
    const2 = lambda shape: pl.BlockSpec(shape, lambda s: (0, 0))
    mod_spec = pl.BlockSpec((None, None, MOD_ROWS, d), lambda s: (batch(s), seg(s), 0, 0))
    dest3 = dest.reshape(b * tiles_per_batch, 1, tm)
    in_specs = [
        pl.BlockSpec((None, 1, tm), lambda s: (token_tile(s), 0, 0), memory_space=pltpu.SMEM),
        pl.BlockSpec((None, 1, tm), lambda s: (token_tile(jnp.minimum(s + 1, n_steps - 1)), 0, 0),
                     memory_space=pltpu.SMEM),
        pl.BlockSpec(memory_space=pl.ANY),
        row(d), row(2 * LANES), mod_spec, const2((1, d)), const2((1, d)),
    ]
    args = [dest3, dest3, y2, x1, pair_gates, mods_l, lng, lnb]
    out_shape = [jax.ShapeDtypeStruct((b, n_rows, d), F32)]
    out_specs = [row(d)]
    if next_inproj is None:
        body = functools.partial(_combine_kernel, alpha=alpha)
        name = "moe_combine_ln"
    else:
        assert n_rows == l and tm == TOKEN_TILE
        mods_n, w_big, cosf, sinf, slng, slnb, ws, bs_full, q_scale = next_inproj
        in_specs += [
            mod_spec, const2((d, BIG_COLS)),
            pl.BlockSpec((tm, LANES), lambda s: (tile(s), 0)),
            pl.BlockSpec((tm, LANES), lambda s: (tile(s), 0)),
            const2((1, G_WIDTH)), const2((1, G_WIDTH)),
            pl.BlockSpec((G_GROUPS, CHUNK, CHUNK), lambda s: (0, 0, 0)),
            const2((CHUNK, G_WIDTH)),
        ]
        args += [mods_n, w_big, cosf, sinf, slng, slnb, ws, bs_full]
        out_shape += list(_inproj_out_shapes(b, l))
        out_specs += [row(F_WIDTH), row(F_WIDTH), row(A_QK_WIDTH),
                      pl.BlockSpec((None, A_HEADS, A_DV, tm), lambda s: (batch(s), 0, 0, tile(s))),
                      row(2 * A_WIDTH), row(G_WIDTH)]
        body = functools.partial(_combine_inproj_kernel, alpha=alpha, q_scale=q_scale)
        name = "moe_combine_ln_inproj"
    outs = pl.pallas_call(
        body,
        out_shape=tuple(out_shape),
        grid=(n_steps,),
        in_specs=in_specs,
        out_specs=tuple(out_specs),
        scratch_shapes=[pltpu.VMEM((2, tm, d), jnp.uint32), pltpu.SemaphoreType.DMA((2,))],
        compiler_params=_cparams(("arbitrary",)),
        name=name,
    )(*args)
    return outs[0] if next_inproj is None else outs


def _head_lane_fields():
    j = jnp.arange(LANES)
    half = j // (LANES // 2)
    comp = (j % (LANES // 2)) // (A_DH // 2)
    axis = (j % (A_DH // 2)) // (A_DH // 4)
    freq = j % (A_DH // 4)
    return half, comp, axis, freq


def _qk_column_perm():
    half, comp, axis, freq = _head_lane_fields()
    orig = comp * A_DH + axis * (A_DH // 2) + half * (A_DH // 4) + freq
    return (jnp.arange(A_HEADS)[:, None] * LANES + orig[None, :]).reshape(-1)


def _rope_tables(n_lat, n_ctx):
    half, _, axis, freq = _head_lane_fields()
    rows = n_lat // GRID_W
    row = jnp.repeat(jnp.arange(rows, dtype=F32), GRID_W)
    col = jnp.tile(jnp.arange(GRID_W, dtype=F32), rows)
    hd = A_DH // 2
    inv = ROPE_BASE ** (-jnp.arange(0, hd, 2, dtype=F32) / hd)
    pos = jnp.where(axis[None, :] == 0, row[:, None], col[:, None])
    ang = pos * inv[freq][None, :]
    cosf = jnp.cos(ang)
    sinf = jnp.sin(ang) * jnp.where(half == 0, -1.0, 1.0)[None, :]
    cosf = jnp.concatenate([cosf, jnp.ones((n_ctx, LANES), F32)], axis=0)
    sinf = jnp.concatenate([sinf, jnp.zeros((n_ctx, LANES), F32)], axis=0)
    return cosf, sinf


def _dft_mats(n, scale=1.0):
    i = jnp.arange(n, dtype=jnp.int32)
    ang = ((i[:, None] * i[None, :]) % n).astype(F32) * (2.0 * math.pi / n)
    return jnp.cos(ang) * scale, jnp.sin(ang) * scale


def _twiddles(n, bn):
    ka = jnp.arange(DFT_A, dtype=jnp.int32)
    bb = jnp.arange(bn, dtype=jnp.int32)
    ang = ((ka[:, None] * bb[None, :]) % n).astype(F32) * (2.0 * math.pi / n)
    shape = (DFT_A, bn, F_WIDTH)
    twc = jnp.broadcast_to(jnp.cos(ang)[:, :, None], shape).reshape(DFT_A, bn * F_WIDTH)
    tws = jnp.broadcast_to(jnp.sin(ang)[:, :, None], shape).reshape(DFT_A, bn * F_WIDTH)
    return twc, tws


def _block_diag(blocks):
    g, n = blocks.shape[-3], blocks.shape[-1]
    eye = jnp.eye(g, dtype=blocks.dtype)
    out = blocks[..., :, :, None, :] * eye[:, None, :, None]
    return out.reshape(blocks.shape[:-3] + (g * n, g * n))


def kernel(x, c, ctx, c_ctx, w_ada, b_ada, w_in, w_fourier, diff_lambda, diff_subln,
           sgu_ln_g, sgu_ln_b, sgu_w, sgu_b, w_out, ln_g, ln_b, w_router, router_bias,
           moe_w1, moe_w3, moe_w2):
    b, n, d = x.shape
    n_ctx = ctx.shape[1]
    depth = w_ada.shape[0]
    l = n + n_ctx
    alpha = (2 * depth) ** 0.25
    bn = n // DFT_A
    assert n % DFT_A == 0 and bn % SUBLANES == 0 and n % GRID_W == 0
    assert n_ctx % TOKEN_TILE == 0 and n % TOKEN_TILE == 0 and n % n_ctx == 0
    assert b + 1 <= MOD_ROWS

    c_rows = jnp.concatenate([c, c_ctx[None, :], jnp.zeros((MOD_ROWS - b - 1, d), F32)], axis=0)
    mod = _ada(c_rows, w_ada, b_ada).reshape(depth, MOD_ROWS, 6, d)
    lat = mod[:, :b]
    cm = jnp.broadcast_to(mod[:, b:b + 1], lat.shape)
    mods = jnp.stack([lat, cm], axis=2)
    mods = jnp.pad(mods, ((0, 0), (0, 0), (0, 0), (0, MOD_ROWS - 6), (0, 0)))

    perm = _qk_column_perm()
    c64, s64 = _dft_mats(F_DIM)
    eye_g = jnp.eye(F_GROUPS, dtype=F32)
    bdc = jnp.kron(eye_g, c64)
    bds = jnp.kron(eye_g, s64)
    bdw = _block_diag(w_fourier)
    w_fab = _fourier_weights(w_in[:, :, :F_WIDTH], bdc, bds, bdw)
    o_q = F_WIDTH
    o_k = o_q + A_QK_WIDTH
    o_v = o_k + A_QK_WIDTH
    o_g = o_v + A_WIDTH
    w_big = jnp.concatenate([
        w_fab,
        w_in[:, :, o_q:o_k][:, :, perm],
        w_in[:, :, o_k:o_v][:, :, perm],
        w_in[:, :, o_v:],
    ], axis=-1).astype(BF16)
    wo = w_out.astype(BF16)
    w1 = moe_w1.astype(BF16)
    w3 = moe_w3.astype(BF16)
    w2 = moe_w2.astype(BF16)
    ws = sgu_w.astype(BF16)
    bs_full = jnp.repeat(jnp.swapaxes(sgu_b, 1, 2), G_DIM, axis=2)
    wr_f32 = jnp.pad(w_router, ((0, 0), (0, ROUTER_LANES - N_EXPERTS)))
    wr_top = _top_half_bits(wr_f32)
    wr = jnp.concatenate([wr_top.astype(BF16), (wr_f32 - wr_top).astype(BF16)], axis=1)
    rb =jnp.pad(router_bias, (0, ROUTER_LANES - N_EXPERTS)).reshape(1, ROUTER_LANES)

    cosf, sinf = _rope_tables(n, n_ctx)
    ca, sa = _dft_mats(DFT_A)
    l_a = jnp.concatenate([_stack3([ca, -sa]), _stack3([-sa, -ca])], axis=0)
    twc, tws = _twiddles(n, bn)
    l_b = _stack3(list(_dft_mats(bn, scale=(n * F_DIM) ** -0.5)))
    cc, sc = _dft_mats(n_ctx, scale=(n_ctx * F_DIM) ** -0.5)
    l_c = _stack3([cc, -sc])
    q_scale = (A_DH ** -0.5) * math.log2(math.e)

    tk = ATTN_TK
    assert l % tk == 0
    assert n % MOE_TILE == 0 and n_ctx % MOE_TILE == 0

    def inproj_params(li):
        return (mods[li], w_big[li], cosf, sinf, sgu_ln_g[li][None], sgu_ln_b[li][None],
                ws[li], bs_full[li])

    xa = jnp.concatenate([x, ctx], axis=1)
    fa, fb, q, kt, v_ext, sg = _inproj(xa, *inproj_params(0), n, q_scale)
    for li in range(depth):
        last = li == depth - 1
        lam_init = 0.8 - 0.6 * math.exp(-0.3 * li)
        zr, zi = _fft_a(fa, fb, l_a, twc, tws, bn)
        yf_x = _fft_b(zr, zi, l_b, bn)
        yf_c = _dft_ctx(fa, fb, l_c, n, n_ctx)
        gain = diff_subln[li][None]
        att_x = _attention(q, kt, v_ext, diff_lambda[li], gain, q_block0=0, n_q=n, key_block0=0,
                           n_keys=l, tq=ATTN_TQ, tk=tk, group=min(ATTN_GROUP, n), lam_init=lam_init)
        att_c = _attention(q, kt, v_ext, diff_lambda[li], gain, q_block0=n // n_ctx, n_q=n_ctx,
                           key_block0=n // n_ctx, n_keys=n_ctx, tq=n_ctx, tk=n_ctx, group=n_ctx,
                           lam_init=lam_init)
        x1, h_packed, route, pair_gates = _outproj(
            yf_x, yf_c, att_x, att_c, sg, xa, mods[li], wo[li], ln_g[li, 0][None], ln_b[li, 0][None], wr, rb, n, alpha)
        dest, tile_e1, tile_e2, n_valid, nt = _moe_plan(route, MOE_TILE)
        hs = _dispatch(h_packed.reshape(b * l, d // 2), dest, nt, MOE_TILE)
        y2 = _expert_pairs(hs, tile_e1, tile_e2, n_valid, w1[li], w3[li], w2[li], MOE_TILE)
        combine_args = (y2, dest, x1, pair_gates, mods[li], ln_g[li, 1][None], ln_b[li, 1][None])
        if last:
            xa = _combine(*combine_args, n_rows=n, n_lat=n, tm=MOE_TILE, alpha=alpha)
        else:
            xa, fa, fb, q, kt, v_ext, sg = _combine(
                *combine_args, n_rows=l, n_lat=n, tm=MOE_TILE, alpha=alpha,
                next_inproj=inproj_params(li + 1) + (q_scale,))
    return xa
```

```python
import functools
import math

import numpy as np
import jax
import jax.numpy as jnp
from jax import lax
from jax.experimental import pallas as pl
from jax.experimental.pallas import tpu as pltpu

F32 = jnp.float32
BF16 = jnp.bfloat16
HIGHEST = lax.Precision.HIGHEST

GRID_W = 64
F_GROUPS, F_DIM = 4, 64
F_WIDTH = F_GROUPS * F_DIM
A_HEADS, A_DH = 4, 64
A_DV = 2 * A_DH
A_QK_WIDTH = A_HEADS * 2 * A_DH
A_WIDTH = A_HEADS * A_DV
G_GROUPS, G_DIM = 4, 64
G_WIDTH = G_GROUPS * G_DIM
CHUNK = 128
ROPE_BASE = 10000.0
N_EXPERTS = 16
N_EXPERT_GROUPS = 4
EXPERTS_PER_GROUP = N_EXPERTS // N_EXPERT_GROUPS
LN_EPS = 1e-5
RMS_EPS = 1e-5

COL_FA = 0
COL_FB = COL_FA + F_WIDTH
COL_Q = COL_FB + F_WIDTH
COL_K = COL_Q + A_QK_WIDTH
COL_V = COL_K + A_QK_WIDTH
COL_G = COL_V + A_WIDTH
BIG_COLS = COL_G + 2 * G_WIDTH

LANES = 128
SUBLANES = 8
MOD_ROWS = 8
ROUTER_LANES = 128
DFT_A = 128
VMEM_LIMIT = 56 * 1024 * 1024

TOKEN_TILE = 256
ATTN_TQ = 256
ATTN_GROUP = 8192
ATTN_TK = 1280
MOE_TILE = 256


def _cparams(sem):
    return pltpu.CompilerParams(dimension_semantics=sem, vmem_limit_bytes=VMEM_LIMIT)


def _hdot(a, b):
    return jnp.dot(a, b, precision=HIGHEST, preferred_element_type=F32)


def _bdot(a, b):
    return jnp.dot(a, b, preferred_element_type=F32)


def _layernorm(y, g, b):
    mu = jnp.mean(y, axis=-1, keepdims=True)
    d = y - mu
    var = jnp.mean(d * d, axis=-1, keepdims=True)
    return d * lax.rsqrt(var + LN_EPS) * g + b


def _ada_kernel(c_ref, w_ref, b_ref, o_ref):
    c = c_ref[...]
    a = c * jax.nn.sigmoid(c)
    o_ref[...] = _hdot(a, w_ref[...]) + b_ref[...]


def _ada(c_rows, w_ada, b_ada):
    depth, d, cols = w_ada.shape
    tn = 1536
    assert cols % tn == 0
    return pl.pallas_call(
        _ada_kernel,
        out_shape=jax.ShapeDtypeStruct((depth, MOD_ROWS, cols), F32),
        grid=(depth, cols // tn),
        in_specs=[
            pl.BlockSpec((MOD_ROWS, d), lambda l, j: (0, 0)),
            pl.BlockSpec((None, d, tn), lambda l, j: (l, 0, j)),
            pl.BlockSpec((None, 1, tn), lambda l, j: (l, 0, j)),
        ],
        out_specs=pl.BlockSpec((None, MOD_ROWS, tn), lambda l, j: (l, 0, j)),
        compiler_params=_cparams(("parallel", "parallel")),
        name="ada_mod",
    )(c_rows, w_ada, b_ada.reshape(depth, 1, cols))


def _fw_kernel(wf_ref, bdc_ref, bds_ref, bdw_ref, o_ref):
    bdw = bdw_ref[...]
    mc = _hdot(bdc_ref[...], bdw)
    ms = _hdot(bds_ref[...], bdw)
    wf = wf_ref[...]
    o_ref[:, :F_WIDTH] = _hdot(wf, mc)
    o_ref[:, F_WIDTH:] = _hdot(wf, ms)


def _fourier_weights(wf, bdc, bds, bdw):
    depth, d, _ = wf.shape
    return pl.pallas_call(
        _fw_kernel,
        out_shape=jax.ShapeDtypeStruct((depth, d, 2 * F_WIDTH), F32),
        grid=(depth,),
        in_specs=[
            pl.BlockSpec((None, d, F_WIDTH), lambda l: (l, 0, 0)),
            pl.BlockSpec((F_WIDTH, F_WIDTH), lambda l: (0, 0)),
            pl.BlockSpec((F_WIDTH, F_WIDTH), lambda l: (0, 0)),
            pl.BlockSpec((None, F_WIDTH, F_WIDTH), lambda l: (l, 0, 0)),
        ],
        out_specs=pl.BlockSpec((None, d, 2 * F_WIDTH), lambda l: (l, 0, 0)),
        compiler_params=_cparams(("parallel",)),
        name="fourier_weights",
    )(wf, bdc, bds, bdw)


def _gelu_tanh(x):
    c = math.sqrt(2.0 / math.pi)
    return x * (0.5 * (1.0 + jnp.tanh(c * (x + 0.044715 * (x * x * x)))))


def _inproj_kernel(x_lat_ref, x_ctx_ref, *refs, q_scale, nlat):
    x = jnp.where(pl.program_id(1) >= nlat, x_ctx_ref[...], x_lat_ref[...])
    _inproj_body(x, *refs, q_scale=q_scale)


def _inproj_body(x, mod_ref, w_ref, cos_ref, sin_ref, lng_ref, lnb_ref, ws_ref, bs_ref,
                 fa_ref, fb_ref, q_ref, kt_ref, v_ref, sg_ref, *, q_scale):
    sh = mod_ref[0:1, :]
    sc = mod_ref[1:2, :]
    h = (x * (1.0 + sc) + sh).astype(BF16)
    p = _bdot(h, w_ref[...])
    fa_ref[...] = p[:, COL_FA:COL_FA + F_WIDTH]
    fb_ref[...] = p[:, COL_FB:COL_FB + F_WIDTH]
    cosf = cos_ref[...]
    sinf = sin_ref[...]
    for hh in range(A_HEADS):
        t = p[:, COL_Q + LANES * hh:COL_Q + LANES * (hh + 1)]
        r = t * cosf + pltpu.roll(t, LANES // 2, 1) * sinf
        q_ref[:, LANES * hh:LANES * (hh + 1)] = (r * q_scale).astype(BF16)
        t = p[:, COL_K + LANES * hh:COL_K + LANES * (hh + 1)]
        r = t * cosf + pltpu.roll(t, LANES // 2, 1) * sinf
        kt_ref[hh] = r.T.astype(BF16)
        v_ref[:, 2 * LANES * hh:2 * LANES * hh + LANES] = (
            p[:, COL_V + LANES * hh:COL_V + LANES * (hh + 1)].astype(BF16))
        v_ref[:, 2 * LANES * hh + LANES:2 * LANES * (hh + 1)] = jnp.ones((t.shape[0], LANES), BF16)

    g = _gelu_tanh(p[:, COL_G:COL_G + 2 * G_WIDTH])
    u = g[:, :G_WIDTH]
    vn = _layernorm(g[:, G_WIDTH:], lng_ref[...], lnb_ref[...])
    lane_group = lax.broadcasted_iota(jnp.int32, (CHUNK, G_WIDTH), 1) // G_DIM
    tm = x.shape[0]
    for c in range(tm // CHUNK):
        rows = slice(c * CHUNK, (c + 1) * CHUNK)
        vc = vn[rows]
        mixed = bs_ref[...]
        for gi in range(G_GROUPS):
            vm = jnp.where(lane_group == gi, vc, 0.0).astype(BF16)
            mixed = mixed + _bdot(ws_ref[gi], vm)
        sg_ref[rows, :] = (u[rows] * mixed).astype(BF16)


def _inproj_out_shapes(b, l):
    return (
        jax.ShapeDtypeStruct((b, l, F_WIDTH), F32),
        jax.ShapeDtypeStruct((b, l, F_WIDTH), F32),
        jax.ShapeDtypeStruct((b, l, A_QK_WIDTH), BF16),
        jax.ShapeDtypeStruct((b, A_HEADS, A_DV, l), BF16),
        jax.ShapeDtypeStruct((b, l, 2 * A_WIDTH), BF16),
        jax.ShapeDtypeStruct((b, l, G_WIDTH), BF16),
    )


def _inproj(x_lat, x_ctx, mods_l, w_big, cosf, sinf, lng, lnb, ws, bs_full, q_scale):
    b, n_lat, d = x_lat.shape
    l = n_lat + x_ctx.shape[1]
    tm = TOKEN_TILE
    nlat = n_lat // tm
    seg = lambda i: jnp.where(i >= nlat, 1, 0)
    row = lambda width: pl.BlockSpec((None, tm, width), lambda bb, i: (bb, i, 0))
    const2 = lambda shape: pl.BlockSpec(shape, lambda bb, i: (0, 0))
    return pl.pallas_call(
        functools.partial(_inproj_kernel, q_scale=q_scale, nlat=nlat),
        out_shape=_inproj_out_shapes(b, l),
        grid=(b, l // tm),
        in_specs=[
            pl.BlockSpec((None, tm, d), lambda bb, i: (bb, jnp.minimum(i, nlat - 1), 0)),
            pl.BlockSpec((None, tm, d), lambda bb, i: (bb, jnp.maximum(i - nlat, 0), 0)),
            pl.BlockSpec((None, None, MOD_ROWS, d), lambda bb, i: (bb, seg(i), 0, 0)),
            const2((d, BIG_COLS)),
            pl.BlockSpec((tm, LANES), lambda bb, i: (i, 0)),
            pl.BlockSpec((tm, LANES), lambda bb, i: (i, 0)),
            const2((1, G_WIDTH)),
            const2((1, G_WIDTH)),
            pl.BlockSpec((G_GROUPS, CHUNK, CHUNK), lambda bb, i: (0, 0, 0)),
            const2((CHUNK, G_WIDTH)),
        ],
        out_specs=(row(F_WIDTH), row(F_WIDTH), row(A_QK_WIDTH),
                   pl.BlockSpec((None, A_HEADS, A_DV, tm), lambda bb, i: (bb, 0, 0, i)),
                   row(2 * A_WIDTH), row(G_WIDTH)),
        compiler_params=_cparams(("parallel", "parallel")),
        name="inproj",
    )(x_lat, x_ctx, mods_l, w_big, cosf, sinf, lng, lnb, ws, bs_full)


def _split3(x):
    top = _top_half_bits(x)
    hi = top.astype(BF16)
    return [hi, (x - top).astype(BF16), hi]


def _stack3(mats):
    cols = []
    for m in mats:
        top = _top_half_bits(m)
        cols += [top.astype(BF16), top.astype(BF16), (m - top).astype(BF16)]
    return jnp.concatenate(cols, axis=1)


def _fft_a_kernel(fa_ref, fb_ref, l_ref, tc_ref, ts_ref, zr_ref, zi_ref):
    rhs = jnp.concatenate(_split3(fa_ref[...]) + _split3(fb_ref[...]), axis=0)
    z = _bdot(l_ref[...], rhs)
    zr = z[:DFT_A]
    zi = z[DFT_A:]
    tc = tc_ref[...]
    ts = ts_ref[...]
    zr_ref[...] = zr * tc + zi * ts
    zi_ref[...] = zi * tc - zr * ts


def _fft_a(fa, fb, l_a, twc, tws, bn):
    b, l, _ = fa.shape
    cols = bn * F_WIDTH
    fa3 = fa.reshape(b, l // bn, cols)
    fb3 = fb.reshape(b, l // bn, cols)
    tcw = min(cols, 2048)
    assert cols % tcw == 0
    blk = pl.BlockSpec((None, DFT_A, tcw), lambda bb, j: (bb, 0, j))
    tw = pl.BlockSpec((DFT_A, tcw), lambda bb, j: (0, j))
    zr, zi = pl.pallas_call(
        _fft_a_kernel,
        out_shape=(jax.ShapeDtypeStruct((b, DFT_A, cols), F32),) * 2,
        grid=(b, cols // tcw),
        in_specs=[blk, blk, pl.BlockSpec(l_a.shape, lambda bb, j: (0, 0)), tw, tw],
        out_specs=(blk, blk),
        compiler_params=_cparams(("parallel", "parallel")),
        name="fft_stage_a",
    )(fa3, fb3, l_a, twc, tws)
    return zr.reshape(b, DFT_A, bn, F_WIDTH), zi.reshape(b, DFT_A, bn, F_WIDTH)


def _fft_b_kernel(zr_ref, zi_ref, l_ref, o_ref):
    lm = l_ref[...]
    for j in range(zr_ref.shape[0]):
        rhs = jnp.concatenate(_split3(zr_ref[j]) + _split3(zi_ref[j]), axis=0)
        o_ref[:, F_WIDTH * j:F_WIDTH * (j + 1)] = _bdot(lm, rhs).astype(o_ref.dtype)


def _fft_b(zr4, zi4, l_b, bn):
    b = zr4.shape[0]
    kb = 8
    blk = pl.BlockSpec((None, kb, bn, F_WIDTH), lambda bb, i: (bb, i, 0, 0))
    out = pl.pallas_call(
        _fft_b_kernel,
        out_shape=jax.ShapeDtypeStruct((b, bn, DFT_A * F_WIDTH), BF16),
        grid=(b, DFT_A // kb),
        in_specs=[blk, blk, pl.BlockSpec(l_b.shape, lambda bb, i: (0, 0))],
        out_specs=pl.BlockSpec((None, bn, kb * F_WIDTH), lambda bb, i: (bb, 0, i)),
        compiler_params=_cparams(("parallel", "parallel")),
        name="fft_stage_b",
    )(zr4, zi4, l_b)
    return out.reshape(b, bn * DFT_A, F_WIDTH)


def _dft_ctx_kernel(fa_ref, fb_ref, l_ref, o_ref):
    rhs = jnp.concatenate(_split3(fa_ref[...]) + _split3(fb_ref[...]), axis=0)
    o_ref[...] = _bdot(l_ref[...], rhs).astype(o_ref.dtype)


def _dft_ctx(fa, fb, l_c, n_lat, n_ctx):
    b = fa.shape[0]
    blk = pl.BlockSpec((None, n_ctx, F_WIDTH), lambda bb: (bb, n_lat // n_ctx, 0))
    return pl.pallas_call(
        _dft_ctx_kernel,
        out_shape=jax.ShapeDtypeStruct((b, n_ctx, F_WIDTH), BF16),
        grid=(b,),
        in_specs=[blk, blk, pl.BlockSpec(l_c.shape, lambda bb: (0, 0))],
        out_specs=pl.BlockSpec((None, n_ctx, F_WIDTH), lambda bb: (bb, 0, 0)),
        compiler_params=_cparams(("parallel",)),
        name="dft_ctx",
    )(fa, fb, l_c)


def _score_slot(j):
    return 2 if j == 0 else (j - 1) % 2


def _attn_kernel(q_ref, kt_ref, v_ref, lam_ref, gain_ref, o_ref, q2_ref, s_ref, m_ref, acc_ref,
                 *, lam_init, tq, tk):
    nsub = q_ref.shape[0] // tq
    nk = kt_ref.shape[1] // tk

    def keys(j):
        return slice(j * tk, (j + 1) * tk)
    lane = lax.broadcasted_iota(jnp.int32, (tq, A_DV), 1)
    comp0 = (lane % A_DH) < (A_DH // 2)
    lp = lam_ref[...]
    lam = (jnp.exp(jnp.sum(lp[0:1] * lp[1:2], keepdims=True))
           - jnp.exp(jnp.sum(lp[2:3] * lp[3:4], keepdims=True)) + lam_init)
    out_gain = gain_ref[...] * (1.0 - lam_init)

    def rows(i):
        return pl.ds(pl.multiple_of(i * tq, tq), tq)

    def load_q2(i):
        q = q_ref[rows(i), :]
        zero = jnp.zeros_like(q)
        q2_ref[0:tq, :] = jnp.where(comp0, q, zero)
        q2_ref[tq:2 * tq, :] = jnp.where(comp0, zero, q)

    def scores(j):
        s_ref[_score_slot(j)] = _bdot(q2_ref[...], kt_ref[:, keys(j)])

    def softmax_pv(j, k, first):
        s = s_ref[_score_slot(j)]
        s_max = jnp.max(s, axis=1, keepdims=True)
        if first:
            m_new = jnp.broadcast_to(s_max, (2 * tq, LANES))
        else:
            m_old = m_ref[k]
            m_new = jnp.maximum(m_old, s_max)
            alpha = jnp.exp2(m_old - m_new)
        p = jnp.exp2(s - jnp.concatenate([m_new] * (tk // LANES), axis=1)).astype(BF16)
        pv = _bdot(p, v_ref[keys(j), :])
        if first:
            acc_ref[k] = pv
        else:
            acc_ref[k] = acc_ref[k] * jnp.concatenate([alpha, alpha], axis=1) + pv
        m_ref[k] = m_new

    def finalize(i, k):
        acc = acc_ref[k]
        o0 = acc[0:tq, 0:LANES] / acc[0:tq, LANES:]
        o1 = acc[tq:, 0:LANES] / acc[tq:, LANES:]
        o = o0 - lam * o1
        ms = jnp.mean(o * o, axis=-1, keepdims=True)
        o_ref[rows(i), :] = (o * lax.rsqrt(ms + RMS_EPS) * out_gain).astype(o_ref.dtype)

    def next_tile_first_scores(i):
        load_q2(jnp.minimum(i + 1, nsub - 1))
        scores(0)

    def tile(i, k, finish_previous):
        for j in range(nk):
            if j + 1 < nk:
                scores(j + 1)
            elif nk > 1:
                next_tile_first_scores(i)
            if j == 0 and finish_previous:
                finalize(jnp.maximum(i - 1, 0), 1 - k)
            softmax_pv(j, k, first=(j == 0))
            if nk == 1:
                next_tile_first_scores(i)

    load_q2(0)
    scores(0)
    if nsub % 2 == 0:
        acc_ref[1] = jnp.ones(acc_ref.shape[1:], F32)

        def pair(ii, carry):
            i = 2 * ii
            tile(i, 0, True)
            tile(i + 1, 1, True)
            return carry

        lax.fori_loop(0, nsub // 2, pair, 0)
        finalize(nsub - 1, 1)
    else:
        def single(i, carry):
            tile(i, 0, False)
            finalize(i, 0)
            return carry

        lax.fori_loop(0, nsub, single, 0)


def _attention(q, kt, v_ext, lam_p, gain, *, q_block0, n_q, key_block0, n_keys, tq, tk, group,
               lam_init):
    b = q.shape[0]
    assert n_q % group == 0 and group % tq == 0 and n_keys % tk == 0
    return pl.pallas_call(
        functools.partial(_attn_kernel, lam_init=lam_init, tq=tq, tk=tk),
        out_shape=jax.ShapeDtypeStruct((b, n_q, A_WIDTH), BF16),
        grid=(b, A_HEADS, n_q // group),
        in_specs=[
            pl.BlockSpec((None, group, A_DV), lambda bb, h, i: (bb, q_block0 + i, h)),
            pl.BlockSpec((None, None, A_DV, n_keys), lambda bb, h, i: (bb, h, 0, key_block0)),
            pl.BlockSpec((None, n_keys, 2 * LANES), lambda bb, h, i: (bb, key_block0, h)),
            pl.BlockSpec((4, A_DH), lambda bb, h, i: (0, 0)),
            pl.BlockSpec((1, A_DV), lambda bb, h, i: (0, 0)),
        ],
        out_specs=pl.BlockSpec((None, group, A_DV), lambda bb, h, i: (bb, i, h)),
        scratch_shapes=[
            pltpu.VMEM((2 * tq, A_DV), BF16),
            pltpu.VMEM((3, 2 * tq, tk), F32),
            pltpu.VMEM((2, 2 * tq, LANES), F32),
            pltpu.VMEM((2, 2 * tq, 2 * LANES), F32),
        ],
        compiler_params=_cparams(("parallel", "parallel", "arbitrary")),
        name="diff_attn",
    )(q, kt, v_ext, lam_p, gain)


_HI16 = 0xFFFF0000


def _pack_bf16_pair(lo, hi):
    ulo = lax.bitcast_convert_type(lo.astype(BF16).astype(F32), jnp.uint32) >> 16
    uhi = lax.bitcast_convert_type(hi.astype(BF16).astype(F32), jnp.uint32) & jnp.uint32(_HI16)
    return ulo | uhi


def _top_half_bits(x):
    return lax.bitcast_convert_type(lax.bitcast_convert_type(x, jnp.uint32) & jnp.uint32(_HI16), F32)


def _unpack_bf16_pair(u):
    lo = lax.bitcast_convert_type(u << 16, F32)
    hi = lax.bitcast_convert_type(u & jnp.uint32(_HI16), F32)
    return lo, hi


ROUTE_CLASS = 0
N_PAIRS = EXPERTS_PER_GROUP * (EXPERTS_PER_GROUP - 1) // 2
N_PAIR_CLASSES = N_EXPERT_GROUPS * N_PAIRS


def _router_gates(sc_t, sel_t, route_ref, pair_gates_ref):
    s_rows = [sc_t[e:e + 1, :] for e in range(N_EXPERTS)]
    v_rows = [sel_t[e:e + 1, :] for e in range(N_EXPERTS)]
    in_top2 = []
    group_score = []
    for g in range(N_EXPERT_GROUPS):
        vs = v_rows[g * EXPERTS_PER_GROUP:(g + 1) * EXPERTS_PER_GROUP]
        tops = []
        for jj in range(EXPERTS_PER_GROUP):
            rank = jnp.zeros_like(vs[jj])
            for ii in range(EXPERTS_PER_GROUP):
                if ii == jj:
                    continue
                beats = (vs[ii] >= vs[jj]) if ii < jj else (vs[ii] > vs[jj])
                rank = rank + jnp.where(beats, 1.0, 0.0)
            tops.append(rank < 2.0)
        in_top2 += tops
        gs = jnp.zeros_like(vs[0])
        for jj in range(EXPERTS_PER_GROUP):
            gs = gs + jnp.where(tops[jj], vs[jj], 0.0)
        group_score.append(gs)
    best = group_score[0]
    gidx = jnp.zeros_like(best)
    for g in range(1, N_EXPERT_GROUPS):
        upd = group_score[g] > best
        best = jnp.where(upd, group_score[g], best)
        gidx = jnp.where(upd, float(g), gidx)
    chosen = [jnp.logical_and(in_top2[e], gidx == float(e // EXPERTS_PER_GROUP))
              for e in range(N_EXPERTS)]
    denom = jnp.zeros_like(best)
    for e in range(N_EXPERTS):
        denom = denom + jnp.where(chosen[e], s_rows[e], 0.0)
    e_lo = jnp.zeros_like(best)
    g_lo = jnp.zeros_like(best)
    for e in reversed(range(N_EXPERTS)):
        e_lo = jnp.where(chosen[e], float(e), e_lo)
        g_lo = jnp.where(chosen[e], s_rows[e] / denom, g_lo)
    e_hi = jnp.zeros_like(best)
    g_hi = jnp.zeros_like(best)
    for e in range(N_EXPERTS):
        e_hi = jnp.where(chosen[e], float(e), e_hi)
        g_hi = jnp.where(chosen[e], s_rows[e] / denom, g_hi)
    a_lo = e_lo - EXPERTS_PER_GROUP * gidx
    a_hi = e_hi - EXPERTS_PER_GROUP * gidx
    pair_rank = a_lo * (2 * EXPERTS_PER_GROUP - 1 - a_lo) * 0.5 + (a_hi - a_lo - 1.0)
    route_ref[...] = jnp.zeros(route_ref.shape, F32)
    route_ref[ROUTE_CLASS:ROUTE_CLASS + 1, :] = gidx * float(N_PAIRS) + pair_rank
    tm = g_lo.shape[1]
    pair_gates_ref[:, :LANES] = jnp.broadcast_to(g_lo, (LANES, tm)).T
    pair_gates_ref[:, LANES:] = jnp.broadcast_to(g_hi, (LANES, tm)).T


def _outproj_kernel(yf_lat_ref, yf_ctx_ref, att_lat_ref, att_ctx_ref, sg_ref, x_lat_ref, x_ctx_ref,
                    mod_ref, wo_ref, lng_ref, lnb_ref, wr_ref, rb_ref, x1_ref, h_ref, route_ref,
                    pair_gates_ref, *, alpha, nlat):
    is_ctx = pl.program_id(1) >= nlat
    x = jnp.where(is_ctx, x_ctx_ref[...], x_lat_ref[...])
    yf = jnp.where(is_ctx, yf_ctx_ref[...], yf_lat_ref[...])
    att = jnp.where(is_ctx, att_ctx_ref[...], att_lat_ref[...])
    mix = _bdot(yf, wo_ref[0:F_WIDTH, :])
    mix = mix + _bdot(att, wo_ref[F_WIDTH:F_WIDTH + A_WIDTH, :])
    mix = mix + _bdot(sg_ref[...], wo_ref[F_WIDTH + A_WIDTH:, :])
    g1 = mod_ref[2:3, :]
    x1 = _layernorm(alpha * x + g1 * mix, lng_ref[...], lnb_ref[...])
    x1_ref[...] = x1
    h = x1 * (1.0 + mod_ref[4:5, :]) + mod_ref[3:4, :]
    half = h.shape[1] // 2
    h_ref[...] = _pack_bf16_pair(h[:, :half], h[:, half:])
    h_top = _top_half_bits(h)
    h_hi = h_top.astype(BF16)
    h_lo = (h - h_top).astype(BF16)
    hw = _bdot(h_hi, wr_ref[...])
    logits = hw[:, :ROUTER_LANES] + (hw[:, ROUTER_LANES:] + _bdot(h_lo, wr_ref[:, :ROUTER_LANES]))
    scores = jax.nn.sigmoid(logits)
    sel = scores + rb_ref[...]
    _router_gates(scores.T[0:N_EXPERTS, :], sel.T[0:N_EXPERTS, :], route_ref, pair_gates_ref)


def _outproj(yf_lat, yf_ctx, att_lat, att_ctx, sg, x_lat, x_ctx, x_ctx_row0, mods_l, wo, lng, lnb, wr, rb,
             n_lat, alpha):
    b, _, d = x_lat.shape
    l = sg.shape[1]
    tm = TOKEN_TILE
    nlat = n_lat // tm
    ctx_tile0 = x_ctx_row0 // tm
    seg = lambda i: jnp.where(i >= nlat, 1, 0)
    row = lambda width: pl.BlockSpec((None, tm, width), lambda bb, i: (bb, i, 0))
    lat = lambda width: pl.BlockSpec((None, tm, width), lambda bb, i: (bb, jnp.minimum(i, nlat - 1), 0))
    ctx = lambda width: pl.BlockSpec((None, tm, width), lambda bb, i: (bb, jnp.maximum(i - nlat, 0), 0))
    const2 = lambda shape: pl.BlockSpec(shape, lambda bb, i: (0, 0))
    return pl.pallas_call(
        functools.partial(_outproj_kernel, alpha=alpha, nlat=nlat),
        out_shape=(
            jax.ShapeDtypeStruct((b, l, d), F32),
            jax.ShapeDtypeStruct((b, l, d // 2), jnp.uint32),
            jax.ShapeDtypeStruct((b, SUBLANES, l), F32),
            jax.ShapeDtypeStruct((b, l, 2 * LANES), F32),
        ),
        grid=(b, l // tm),
        in_specs=[
            lat(F_WIDTH), ctx(F_WIDTH), lat(A_WIDTH), ctx(A_WIDTH), row(G_WIDTH), lat(d),
            pl.BlockSpec((None, tm, d), lambda bb, i: (bb, ctx_tile0 + jnp.maximum(i - nlat, 0), 0)),
            pl.BlockSpec((None, None, MOD_ROWS, d), lambda bb, i: (bb, seg(i), 0, 0)),
            const2((d, d)), const2((1, d)), const2((1, d)),
            const2((d, 2 * ROUTER_LANES)), const2((1, ROUTER_LANES)),
        ],
        out_specs=(row(d), row(d // 2),
                   pl.BlockSpec((None, SUBLANES, tm), lambda bb, i: (bb, 0, i)), row(2 * LANES)),
        compiler_params=_cparams(("parallel", "parallel")),
        name="outproj_ln_router",
    )(yf_lat, yf_ctx, att_lat, att_ctx, sg, x_lat, x_ctx, mods_l, wo, lng, lnb, wr, rb)


def _moe_plan(route, tm):
    b, _, l = route.shape
    t = b * l
    pairs = [(a, c) for a in range(EXPERTS_PER_GROUP) for c in range(a + 1, EXPERTS_PER_GROUP)]
    cls_e1 = np.zeros((N_PAIR_CLASSES,), np.int32)
    cls_e2 = np.zeros((N_PAIR_CLASSES,), np.int32)
    for k, (a, c) in enumerate(pairs):
        assert k == a * (2 * EXPERTS_PER_GROUP - 1 - a) // 2 + (c - a - 1)
        for g in range(N_EXPERT_GROUPS):
            cls_e1[g * N_PAIRS + k] = g * EXPERTS_PER_GROUP + a
            cls_e2[g * N_PAIRS + k] = g * EXPERTS_PER_GROUP + c
    cls = route[:, ROUTE_CLASS, :].astype(jnp.int32).reshape(t)
    onehot = cls[:, None] == jnp.arange(N_PAIR_CLASSES, dtype=jnp.int32)[None, :]
    oh_tiles = onehot.reshape(t // tm, tm, N_PAIR_CLASSES).astype(BF16)
    tri = jnp.tril(jnp.ones((tm, tm), BF16))
    within = jnp.einsum("ij,njc->nic", tri, oh_tiles, preferred_element_type=F32)
    tile_total = within[:, -1, :]
    n_tiles = t // tm
    earlier = jnp.tril(jnp.ones((n_tiles, n_tiles), BF16), k=-1)
    before = jnp.dot(earlier, tile_total.astype(BF16), preferred_element_type=F32)
    csum = (within + before[:, None, :]).reshape(t, N_PAIR_CLASSES).astype(jnp.int32)
    onehot = onehot.astype(jnp.int32)
    rank = jnp.sum(csum * onehot, axis=1) - 1
    padded = (csum[-1] + tm - 1) // tm * tm
    upto = jnp.tril(jnp.ones((N_PAIR_CLASSES, N_PAIR_CLASSES), jnp.int32))
    off_end = jnp.sum(upto * padded[None, :], axis=1)
    dest = jnp.sum(onehot * (off_end - padded)[None, :], axis=1) + rank
    nt = t // tm + N_PAIR_CLASSES
    n_valid = (off_end[-1] // tm).astype(jnp.int32)
    first_row = jnp.minimum(jnp.arange(nt, dtype=jnp.int32), n_valid - 1) * tm
    tile_cls = jnp.sum((off_end[None, :] <= first_row[:, None]).astype(jnp.int32), axis=1)
    tile_cls = jnp.minimum(tile_cls, N_PAIR_CLASSES - 1)
    return (dest.astype(jnp.int32), jnp.asarray(cls_e1)[tile_cls], jnp.asarray(cls_e2)[tile_cls],
            n_valid.reshape(1), nt)


def _dispatch_kernel(dest_ref, h_ref, hs_init_ref, hs_ref, stage, sems):
    del hs_init_ref
    s = pl.program_id(0)
    n = pl.num_programs(0)
    tm = h_ref.shape[0]
    slot = s % 2

    def wait_slot(k):
        pltpu.make_async_copy(stage.at[k], hs_ref.at[pl.ds(0, tm), :], sems.at[k]).wait()

    @pl.when(s >= 2)
    def _reuse():
        wait_slot(slot)

    stage[slot] = h_ref[...]
    for r in range(tm):
        pltpu.make_async_copy(stage.at[slot, pl.ds(r, 1), :],
                              hs_ref.at[pl.ds(dest_ref[0, r], 1), :], sems.at[slot]).start()

    @pl.when(s == n - 1)
    def _drain():
        @pl.when(n >= 2)
        def _other():
            wait_slot(1 - slot)
        wait_slot(slot)


def _dispatch(h_packed, dest, nt, tm):
    t, w = h_packed.shape
    hs_init = jnp.zeros((nt * tm, w), jnp.uint32)
    return pl.pallas_call(
        _dispatch_kernel,
        out_shape=jax.ShapeDtypeStruct((nt * tm, w), jnp.uint32),
        grid=(t // tm,),
        in_specs=[
            pl.BlockSpec((None, 1, tm), lambda i: (i, 0, 0), memory_space=pltpu.SMEM),
            pl.BlockSpec((tm, w), lambda i: (i, 0)),
            pl.BlockSpec(memory_space=pl.ANY),
        ],
        out_specs=pl.BlockSpec(memory_space=pl.ANY),
        scratch_shapes=[pltpu.VMEM((2, tm, w), jnp.uint32), pltpu.SemaphoreType.DMA((2,))],
        input_output_aliases={2: 0},
        compiler_params=_cparams(("arbitrary",)),
        name="moe_dispatch",
    )(dest.reshape(t // tm, 1, tm), h_packed, hs_init)


def _expert_pair_kernel(e1_ref, e2_ref, nv_ref, hs_ref, w1a_ref, w3a_ref, w2a_ref,
                        w1b_ref, w3b_ref, w2b_ref, y_ref):
    del e1_ref, e2_ref
    i = pl.program_id(0)

    @pl.when(i < nv_ref[0])
    def _compute():
        lo, hi = _unpack_bf16_pair(hs_ref[...])
        h = jnp.concatenate([lo, hi], axis=1).astype(BF16)

        def ffn(w1_ref, w3_ref, w2_ref):
            a = _bdot(h, w1_ref[...])
            a = (a * jax.nn.sigmoid(a)) * _bdot(h, w3_ref[...])
            return _bdot(a.astype(BF16), w2_ref[...])

        y_ref[...] = _pack_bf16_pair(ffn(w1a_ref, w3a_ref, w2a_ref), ffn(w1b_ref, w3b_ref, w2b_ref))

    @pl.when(i >= nv_ref[0])
    def _unused_tile():
        y_ref[...] = jnp.zeros(y_ref.shape, jnp.uint32)


def _expert_pairs(hs, tile_e1, tile_e2, n_valid, w1, w3, w2, tm):
    rows, w = hs.shape
    _, d, de = w1.shape
    first = lambda shape: pl.BlockSpec(shape, lambda i, e1, e2, nv: (e1[i], 0, 0))
    second = lambda shape: pl.BlockSpec(shape, lambda i, e1, e2, nv: (e2[i], 0, 0))
    grid_spec = pltpu.PrefetchScalarGridSpec(
        num_scalar_prefetch=3,
        grid=(rows // tm,),
        in_specs=[
            pl.BlockSpec((tm, w), lambda i, e1, e2, nv: (jnp.minimum(i, nv[0] - 1), 0)),
            first((None, d, de)), first((None, d, de)), first((None, de, d)),
            second((None, d, de)), second((None, d, de)), second((None, de, d)),
        ],
        out_specs=pl.BlockSpec((tm, d), lambda i, e1, e2, nv: (i, 0)),
    )
    return pl.pallas_call(
        _expert_pair_kernel,
        out_shape=jax.ShapeDtypeStruct((rows, d), jnp.uint32),
        grid_spec=grid_spec,
        compiler_params=_cparams(("arbitrary",)),
        name="moe_expert_pairs",
    )(tile_e1, tile_e2, n_valid, hs, w1, w3, w2, w1, w3, w2)


def _combine_kernel(dest_ref, dest_next_ref, y2_ref, x_ref, pg_ref, mod_ref, lng_ref, lnb_ref,
                    o_ref, ybuf, sems, *, alpha):
    o_ref[...] = _combine_body(dest_ref, dest_next_ref, y2_ref, x_ref, pg_ref, mod_ref, lng_ref,
                               lnb_ref, ybuf, sems, alpha=alpha)


def _combine_inproj_kernel(dest_ref, dest_next_ref, y2_ref, x_ref, pg_ref, mod_ref, lng_ref, lnb_ref,
                           *refs, alpha, q_scale):
    n_inproj_in = 8
    inproj_in = refs[:n_inproj_in]
    o_ref = refs[n_inproj_in]
    inproj_out = refs[n_inproj_in + 1:n_inproj_in + 7]
    ybuf, sems = refs[n_inproj_in + 7:]
    x_new = _combine_body(dest_ref, dest_next_ref, y2_ref, x_ref, pg_ref, mod_ref, lng_ref, lnb_ref,
                          ybuf, sems, alpha=alpha)
    o_ref[...] = x_new
    _inproj_body(x_new, *inproj_in, *inproj_out, q_scale=q_scale)


def _combine_body(dest_ref, dest_next_ref, y2_ref, x_ref, pg_ref, mod_ref, lng_ref, lnb_ref,
                  ybuf, sems, *, alpha):
    s = pl.program_id(0)
    n = pl.num_programs(0)
    tm, d = x_ref.shape
    slot = s % 2

    def start_gather(idx_ref, k):
        for r in range(tm):
            pltpu.make_async_copy(y2_ref.at[pl.ds(idx_ref[0, r], 1), :],
                                  ybuf.at[k, pl.ds(r, 1), :], sems.at[k]).start()

    @pl.when(s == 0)
    def _first():
        start_gather(dest_ref, 0)

    @pl.when(s + 1 < n)
    def _prefetch():
        start_gather(dest_next_ref, 1 - slot)

    pltpu.make_async_copy(y2_ref.at[pl.ds(0, tm), :], ybuf.at[slot], sems.at[slot]).wait()
    y_lo, y_hi = _unpack_bf16_pair(ybuf[slot])
    pg = pg_ref[...]
    reps = d // LANES
    g_lo = jnp.concatenate([pg[:, :LANES]] * reps, axis=1)
    g_hi = jnp.concatenate([pg[:, LANES:]] * reps, axis=1)
    y = g_lo * y_lo + g_hi * y_hi
    return _layernorm(alpha * x_ref[...] + mod_ref[5:6, :] * y, lng_ref[...], lnb_ref[...])


def _combine(y2, dest, x1, pair_gates, mods_l, lng, lnb, *, n_rows, n_lat, tm, alpha, next_inproj=None):
    b, l, d = x1.shape
    nlat = n_lat // tm
    tiles_per_batch = l // tm
    per_batch = n_rows // tm
    n_steps = b * per_batch
    batch = lambda s: s // per_batch
    tile = lambda s: s % per_batch
    seg = lambda s: jnp.where(tile(s) >= nlat, 1, 0)
    token_tile = lambda s: batch(s) * tiles_per_batch + tile(s)
    row = lambda width: pl.BlockSpec((None, tm, width), lambda s: (batch(s), tile(s), 0))
    const2 = lambda shape: pl.BlockSpec(shape, lambda s: (0, 0))
    mod_spec = pl.BlockSpec((None, None, MOD_ROWS, d), lambda s: (batch(s), seg(s), 0, 0))
    dest3 = dest.reshape(b * tiles_per_batch, 1, tm)
    in_specs = [
        pl.BlockSpec((None, 1, tm), lambda s: (token_tile(s), 0, 0), memory_space=pltpu.SMEM),
        pl.BlockSpec((None, 1, tm), lambda s: (token_tile(jnp.minimum(s + 1, n_steps - 1)), 0, 0),
                     memory_space=pltpu.SMEM),
        pl.BlockSpec(memory_space=pl.ANY),
        row(d), row(2 * LANES), mod_spec, const2((1, d)), const2((1, d)),
    ]
    args = [dest3, dest3, y2, x1, pair_gates, mods_l, lng, lnb]
    out_shape = [jax.ShapeDtypeStruct((b, n_rows, d), F32)]
    out_specs = [row(d)]
    if next_inproj is None:
        body = functools.partial(_combine_kernel, alpha=alpha)
        name = "moe_combine_ln"
    else:
        assert n_rows == l and tm == TOKEN_TILE
        mods_n, w_big, cosf, sinf, slng, slnb, ws, bs_full, q_scale = next_inproj
        in_specs += [
            mod_spec, const2((d, BIG_COLS)),
            pl.BlockSpec((tm, LANES), lambda s: (tile(s), 0)),
            pl.BlockSpec((tm, LANES), lambda s: (tile(s), 0)),
            const2((1, G_WIDTH)), const2((1, G_WIDTH)),
            pl.BlockSpec((G_GROUPS, CHUNK, CHUNK), lambda s: (0, 0, 0)),
            const2((CHUNK, G_WIDTH)),
        ]
        args += [mods_n, w_big, cosf, sinf, slng, slnb, ws, bs_full]
        out_shape += list(_inproj_out_shapes(b, l))
        out_specs += [row(F_WIDTH), row(F_WIDTH), row(A_QK_WIDTH),
                      pl.BlockSpec((None, A_HEADS, A_DV, tm), lambda s: (batch(s), 0, 0, tile(s))),
                      row(2 * A_WIDTH), row(G_WIDTH)]
        body = functools.partial(_combine_inproj_kernel, alpha=alpha, q_scale=q_scale)
        name = "moe_combine_ln_inproj"
    outs = pl.pallas_call(
        body,
        out_shape=tuple(out_shape),
        grid=(n_steps,),
        in_specs=in_specs,
        out_specs=tuple(out_specs),
        scratch_shapes=[pltpu.VMEM((2, tm, d), jnp.uint32), pltpu.SemaphoreType.DMA((2,))],
        compiler_params=_cparams(("arbitrary",)),
        name=name,
    )(*args)
    return outs[0] if next_inproj is None else outs


def _head_lane_fields():
    j = jnp.arange(LANES)
    half = j // (LANES // 2)
    comp = (j % (LANES // 2)) // (A_DH // 2)
    axis = (j % (A_DH // 2)) // (A_DH // 4)
    freq = j % (A_DH // 4)
    return half, comp, axis, freq


def _qk_column_perm():
    half, comp, axis, freq = _head_lane_fields()
    orig = comp * A_DH + axis * (A_DH // 2) + half * (A_DH // 4) + freq
    return (jnp.arange(A_HEADS)[:, None] * LANES + orig[None, :]).reshape(-1)


def _rope_tables(n_lat, n_ctx):
    half, _, axis, freq = _head_lane_fields()
    rows = n_lat // GRID_W
    row = jnp.repeat(jnp.arange(rows, dtype=F32), GRID_W)
    col = jnp.tile(jnp.arange(GRID_W, dtype=F32), rows)
    hd = A_DH // 2
    inv = ROPE_BASE ** (-jnp.arange(0, hd, 2, dtype=F32) / hd)
    pos = jnp.where(axis[None, :] == 0, row[:, None], col[:, None])
    ang = pos * inv[freq][None, :]
    cosf = jnp.cos(ang)
    sinf = jnp.sin(ang) * jnp.where(half == 0, -1.0, 1.0)[None, :]
    cosf = jnp.concatenate([cosf, jnp.ones((n_ctx, LANES), F32)], axis=0)
    sinf = jnp.concatenate([sinf, jnp.zeros((n_ctx, LANES), F32)], axis=0)
    return cosf, sinf


def _dft_mats(n, scale=1.0):
    i = jnp.arange(n, dtype=jnp.int32)
    ang = ((i[:, None] * i[None, :]) % n).astype(F32) * (2.0 * math.pi / n)
    return jnp.cos(ang) * scale, jnp.sin(ang) * scale


def _twiddles(n, bn):
    ka = jnp.arange(DFT_A, dtype=jnp.int32)
    bb = jnp.arange(bn, dtype=jnp.int32)
    ang = ((ka[:, None] * bb[None, :]) % n).astype(F32) * (2.0 * math.pi / n)
    shape = (DFT_A, bn, F_WIDTH)
    twc = jnp.broadcast_to(jnp.cos(ang)[:, :, None], shape).reshape(DFT_A, bn * F_WIDTH)
    tws = jnp.broadcast_to(jnp.sin(ang)[:, :, None], shape).reshape(DFT_A, bn * F_WIDTH)
    return twc, tws


def _block_diag(blocks):
    g, n = blocks.shape[-3], blocks.shape[-1]
    eye = jnp.eye(g, dtype=blocks.dtype)
    out = blocks[..., :, :, None, :] * eye[:, None, :, None]
    return out.reshape(blocks.shape[:-3] + (g * n, g * n))


def kernel(x, c, ctx, c_ctx, w_ada, b_ada, w_in, w_fourier, diff_lambda, diff_subln,
           sgu_ln_g, sgu_ln_b, sgu_w, sgu_b, w_out, ln_g, ln_b, w_router, router_bias,
           moe_w1, moe_w3, moe_w2):
    b, n, d = x.shape
    n_ctx = ctx.shape[1]
    depth = w_ada.shape[0]
    l = n + n_ctx
    alpha = (2 * depth) ** 0.25
    bn = n // DFT_A
    assert n % DFT_A == 0 and bn % SUBLANES == 0 and n % GRID_W == 0
    assert n_ctx % TOKEN_TILE == 0 and n % TOKEN_TILE == 0 and n % n_ctx == 0
    assert b + 1 <= MOD_ROWS

    c_rows = jnp.concatenate([c, c_ctx[None, :], jnp.zeros((MOD_ROWS - b - 1, d), F32)], axis=0)
    mod = _ada(c_rows, w_ada, b_ada).reshape(depth, MOD_ROWS, 6, d)
    lat = mod[:, :b]
    cm = jnp.broadcast_to(mod[:, b:b + 1], lat.shape)
    mods = jnp.stack([lat, cm], axis=2)
    mods = jnp.pad(mods, ((0, 0), (0, 0), (0, 0), (0, MOD_ROWS - 6), (0, 0)))

    perm = _qk_column_perm()
    c64, s64 = _dft_mats(F_DIM)
    eye_g = jnp.eye(F_GROUPS, dtype=F32)
    bdc = jnp.kron(eye_g, c64)
    bds = jnp.kron(eye_g, s64)
    bdw = _block_diag(w_fourier)
    w_fab = _fourier_weights(w_in[:, :, :F_WIDTH], bdc, bds, bdw)
    o_q = F_WIDTH
    o_k = o_q + A_QK_WIDTH
    o_v = o_k + A_QK_WIDTH
    o_g = o_v + A_WIDTH
    w_big = jnp.concatenate([
        w_fab,
        w_in[:, :, o_q:o_k][:, :, perm],
        w_in[:, :, o_k:o_v][:, :, perm],
        w_in[:, :, o_v:],
    ], axis=-1).astype(BF16)
    wo = w_out.astype(BF16)
    w1 = moe_w1.astype(BF16)
    w3 = moe_w3.astype(BF16)
    w2 = moe_w2.astype(BF16)
    ws = sgu_w.astype(BF16)
    bs_full = jnp.repeat(jnp.swapaxes(sgu_b, 1, 2), G_DIM, axis=2)
    wr_f32 = jnp.pad(w_router, ((0, 0), (0, ROUTER_LANES - N_EXPERTS)))
    wr_top = _top_half_bits(wr_f32)
    wr = jnp.concatenate([wr_top.astype(BF16), (wr_f32 - wr_top).astype(BF16)], axis=1)
    rb =jnp.pad(router_bias, (0, ROUTER_LANES - N_EXPERTS)).reshape(1, ROUTER_LANES)

    cosf, sinf = _rope_tables(n, n_ctx)
    ca, sa = _dft_mats(DFT_A)
    l_a = jnp.concatenate([_stack3([ca, -sa]), _stack3([-sa, -ca])], axis=0)
    twc, tws = _twiddles(n, bn)
    l_b = _stack3(list(_dft_mats(bn, scale=(n * F_DIM) ** -0.5)))
    cc, sc = _dft_mats(n_ctx, scale=(n_ctx * F_DIM) ** -0.5)
    l_c = _stack3([cc, -sc])
    q_scale = (A_DH ** -0.5) * math.log2(math.e)

    tk = ATTN_TK
    assert l % tk == 0
    assert n % MOE_TILE == 0 and n_ctx % MOE_TILE == 0

    def inproj_params(li):
        return (mods[li], w_big[li], cosf, sinf, sgu_ln_g[li][None], sgu_ln_b[li][None],
                ws[li], bs_full[li])

    fa, fb, q, kt, v_ext, sg = _inproj(x, ctx, *inproj_params(0), q_scale)
    stream = (x, ctx, 0)
    for li in range(depth):
        last = li == depth - 1
        lam_init = 0.8 - 0.6 * math.exp(-0.3 * li)
        zr, zi = _fft_a(fa, fb, l_a, twc, tws, bn)
        yf_x = _fft_b(zr, zi, l_b, bn)
        yf_c = _dft_ctx(fa, fb, l_c, n, n_ctx)
        gain = diff_subln[li][None]
        att_x = _attention(q, kt, v_ext, diff_lambda[li], gain, q_block0=0, n_q=n, key_block0=0,
                           n_keys=l, tq=ATTN_TQ, tk=tk, group=min(ATTN_GROUP, n), lam_init=lam_init)
        att_c = _attention(q, kt, v_ext, diff_lambda[li], gain, q_block0=n // n_ctx, n_q=n_ctx,
                           key_block0=n // n_ctx, n_keys=n_ctx, tq=n_ctx, tk=n_ctx, group=n_ctx,
                           lam_init=lam_init)
        x1, h_packed, route, pair_gates = _outproj(
            yf_x, yf_c, att_x, att_c, sg, *stream, mods[li], wo[li], ln_g[li, 0][None], ln_b[li, 0][None], wr, rb, n, alpha)
        dest, tile_e1, tile_e2, n_valid, nt = _moe_plan(route, MOE_TILE)
        hs = _dispatch(h_packed.reshape(b * l, d // 2), dest, nt, MOE_TILE)
        y2 = _expert_pairs(hs, tile_e1, tile_e2, n_valid, w1[li], w3[li], w2[li], MOE_TILE)
        combine_args = (y2, dest, x1, pair_gates, mods[li], ln_g[li, 1][None], ln_b[li, 1][None])
        if last:
            xa = _combine(*combine_args, n_rows=n, n_lat=n, tm=MOE_TILE, alpha=alpha)
        else:
            xa, fa, fb, q, kt, v_ext, sg = _combine(
                *combine_args, n_rows=l, n_lat=n, tm=MOE_TILE, alpha=alpha,
                next_inproj=inproj_params(li + 1) + (q_scale,))
            stream = (xa, xa, n)
    return xa
```

```python
import functools
import math

import numpy as np
import jax
import jax.numpy as jnp
from jax import lax
from jax.experimental import pallas as pl
from jax.experimental.pallas import tpu as pltpu

F32 = jnp.float32
BF16 = jnp.bfloat16
HIGHEST = lax.Precision.HIGHEST

GRID_W = 64
F_GROUPS, F_DIM = 4, 64
F_WIDTH = F_GROUPS * F_DIM
A_HEADS, A_DH = 4, 64
A_DV = 2 * A_DH
A_QK_WIDTH = A_HEADS * 2 * A_DH
A_WIDTH = A_HEADS * A_DV
G_GROUPS, G_DIM = 4, 64
G_WIDTH = G_GROUPS * G_DIM
CHUNK = 128
ROPE_BASE = 10000.0
N_EXPERTS = 16
N_EXPERT_GROUPS = 4
EXPERTS_PER_GROUP = N_EXPERTS // N_EXPERT_GROUPS
LN_EPS = 1e-5
RMS_EPS = 1e-5

COL_FA = 0
COL_FB = COL_FA + F_WIDTH
COL_Q = COL_FB + F_WIDTH
COL_K = COL_Q + A_QK_WIDTH
COL_V = COL_K + A_QK_WIDTH
COL_G = COL_V + A_WIDTH
BIG_COLS = COL_G + 2 * G_WIDTH

LANES = 128
SUBLANES = 8
MOD_ROWS = 8
ROUTER_LANES = 128
DFT_A = 128
VMEM_LIMIT = 56 * 1024 * 1024

TOKEN_TILE = 256
ATTN_TQ = 256
ATTN_GROUP = 8192
ATTN_TK = 1280
MOE_TILE = 256


def _cparams(sem):
    return pltpu.CompilerParams(dimension_semantics=sem, vmem_limit_bytes=VMEM_LIMIT)


def _hdot(a, b):
    return jnp.dot(a, b, precision=HIGHEST, preferred_element_type=F32)


def _bdot(a, b):
    return jnp.dot(a, b, preferred_element_type=F32)


def _layernorm(y, g, b):
    mu = jnp.mean(y, axis=-1, keepdims=True)
    d = y - mu
    var = jnp.mean(d * d, axis=-1, keepdims=True)
    return d * lax.rsqrt(var + LN_EPS) * g + b


def _ada_kernel(c_ref, w_ref, b_ref, o_ref):
    c = c_ref[...]
    a = c * jax.nn.sigmoid(c)
    o_ref[...] = _hdot(a, w_ref[...]) + b_ref[...]


def _ada(c_rows, w_ada, b_ada):
    depth, d, cols = w_ada.shape
    tn = 1536
    assert cols % tn == 0
    return pl.pallas_call(
        _ada_kernel,
        out_shape=jax.ShapeDtypeStruct((depth, MOD_ROWS, cols), F32),
        grid=(depth, cols // tn),
        in_specs=[
            pl.BlockSpec((MOD_ROWS, d), lambda l, j: (0, 0)),
            pl.BlockSpec((None, d, tn), lambda l, j: (l, 0, j)),
            pl.BlockSpec((None, 1, tn), lambda l, j: (l, 0, j)),
        ],
        out_specs=pl.BlockSpec((None, MOD_ROWS, tn), lambda l, j: (l, 0, j)),
        compiler_params=_cparams(("parallel", "parallel")),
        name="ada_mod",
    )(c_rows, w_ada, b_ada.reshape(depth, 1, cols))


def _fw_kernel(wf_ref, bdc_ref, bds_ref, bdw_ref, o_ref):
    bdw = bdw_ref[...]
    mc = _hdot(bdc_ref[...], bdw)
    ms = _hdot(bds_ref[...], bdw)
    wf = wf_ref[...]
    o_ref[:, :F_WIDTH] = _hdot(wf, mc)
    o_ref[:, F_WIDTH:] = _hdot(wf, ms)


def _fourier_weights(wf, bdc, bds, bdw):
    depth, d, _ = wf.shape
    return pl.pallas_call(
        _fw_kernel,
        out_shape=jax.ShapeDtypeStruct((depth, d, 2 * F_WIDTH), F32),
        grid=(depth,),
        in_specs=[
            pl.BlockSpec((None, d, F_WIDTH), lambda l: (l, 0, 0)),
            pl.BlockSpec((F_WIDTH, F_WIDTH), lambda l: (0, 0)),
            pl.BlockSpec((F_WIDTH, F_WIDTH), lambda l: (0, 0)),
            pl.BlockSpec((None, F_WIDTH, F_WIDTH), lambda l: (l, 0, 0)),
        ],
        out_specs=pl.BlockSpec((None, d, 2 * F_WIDTH), lambda l: (l, 0, 0)),
        compiler_params=_cparams(("parallel",)),
        name="fourier_weights",
    )(wf, bdc, bds, bdw)


def _gelu_tanh(x):
    c = math.sqrt(2.0 / math.pi)
    return x * (0.5 * (1.0 + jnp.tanh(c * (x + 0.044715 * (x * x * x)))))


def _inproj_kernel(x_lat_ref, x_ctx_ref, *refs, q_scale, nlat):
    x = jnp.where(pl.program_id(1) >= nlat, x_ctx_ref[...], x_lat_ref[...])
    _inproj_body(x, *refs, q_scale=q_scale)


def _inproj_body(x, mod_ref, w_ref, cos_ref, sin_ref, lng_ref, lnb_ref, ws_ref, bs_ref,
                 fa_ref, fb_ref, q_ref, kt_ref, v_ref, sg_ref, *, q_scale):
    sh = mod_ref[0:1, :]
    sc = mod_ref[1:2, :]
    h = (x * (1.0 + sc) + sh).astype(BF16)
    p = _bdot(h, w_ref[...])
    fa_ref[...] = p[:, COL_FA:COL_FA + F_WIDTH]
    fb_ref[...] = p[:, COL_FB:COL_FB + F_WIDTH]
    cosf = cos_ref[...]
    sinf = sin_ref[...]
    for hh in range(A_HEADS):
        t = p[:, COL_Q + LANES * hh:COL_Q + LANES * (hh + 1)]
        r = t * cosf + pltpu.roll(t, LANES // 2, 1) * sinf
        q_ref[:, LANES * hh:LANES * (hh + 1)] = (r * q_scale).astype(BF16)
        t = p[:, COL_K + LANES * hh:COL_K + LANES * (hh + 1)]
        r = t * cosf + pltpu.roll(t, LANES // 2, 1) * sinf
        kt_ref[hh] = r.T.astype(BF16)
        v_ref[:, 2 * LANES * hh:2 * LANES * hh + LANES] = (
            p[:, COL_V + LANES * hh:COL_V + LANES * (hh + 1)].astype(BF16))
        v_ref[:, 2 * LANES * hh + LANES:2 * LANES * (hh + 1)] = jnp.ones((t.shape[0], LANES), BF16)

    g = _gelu_tanh(p[:, COL_G:COL_G + 2 * G_WIDTH])
    u = g[:, :G_WIDTH]
    vn = _layernorm(g[:, G_WIDTH:], lng_ref[...], lnb_ref[...])
    lane_group = lax.broadcasted_iota(jnp.int32, (CHUNK, G_WIDTH), 1) // G_DIM
    tm = x.shape[0]
    for c in range(tm // CHUNK):
        rows = slice(c * CHUNK, (c + 1) * CHUNK)
        vc = vn[rows]
        mixed = bs_ref[...]
        for gi in range(G_GROUPS):
            vm = jnp.where(lane_group == gi, vc, 0.0).astype(BF16)
            mixed = mixed + _bdot(ws_ref[gi], vm)
        sg_ref[rows, :] = (u[rows] * mixed).astype(BF16)


def _inproj_out_shapes(b, l):
    return (
        jax.ShapeDtypeStruct((b, l, F_WIDTH), F32),
        jax.ShapeDtypeStruct((b, l, F_WIDTH), F32),
        jax.ShapeDtypeStruct((b, l, A_QK_WIDTH), BF16),
        jax.ShapeDtypeStruct((b, A_HEADS, A_DV, l), BF16),
        jax.ShapeDtypeStruct((b, l, 2 * A_WIDTH), BF16),
        jax.ShapeDtypeStruct((b, l, G_WIDTH), BF16),
    )


def _inproj(x_lat, x_ctx, mods_l, w_big, cosf, sinf, lng, lnb, ws, bs_full, q_scale):
    b, n_lat, d = x_lat.shape
    l = n_lat + x_ctx.shape[1]
    tm = TOKEN_TILE
    nlat = n_lat // tm
    seg = lambda i: jnp.where(i >= nlat, 1, 0)
    row = lambda width: pl.BlockSpec((None, tm, width), lambda bb, i: (bb, i, 0))
    const2 = lambda shape: pl.BlockSpec(shape, lambda bb, i: (0, 0))
    return pl.pallas_call(
        functools.partial(_inproj_kernel, q_scale=q_scale, nlat=nlat),
        out_shape=_inproj_out_shapes(b, l),
        grid=(b, l // tm),
        in_specs=[
            pl.BlockSpec((None, tm, d), lambda bb, i: (bb, jnp.minimum(i, nlat - 1), 0)),
            pl.BlockSpec((None, tm, d), lambda bb, i: (bb, jnp.maximum(i - nlat, 0), 0)),
            pl.BlockSpec((None, None, MOD_ROWS, d), lambda bb, i: (bb, seg(i), 0, 0)),
            const2((d, BIG_COLS)),
            pl.BlockSpec((tm, LANES), lambda bb, i: (i, 0)),
            pl.BlockSpec((tm, LANES), lambda bb, i: (i, 0)),
            const2((1, G_WIDTH)),
            const2((1, G_WIDTH)),
            pl.BlockSpec((G_GROUPS, CHUNK, CHUNK), lambda bb, i: (0, 0, 0)),
            const2((CHUNK, G_WIDTH)),
        ],
        out_specs=(row(F_WIDTH), row(F_WIDTH), row(A_QK_WIDTH),
                   pl.BlockSpec((None, A_HEADS, A_DV, tm), lambda bb, i: (bb, 0, 0, i)),
                   row(2 * A_WIDTH), row(G_WIDTH)),
        compiler_params=_cparams(("parallel", "parallel")),
        name="inproj",
    )(x_lat, x_ctx, mods_l, w_big, cosf, sinf, lng, lnb, ws, bs_full)


def _split3(x):
    top = _top_half_bits(x)
    hi = top.astype(BF16)
    return [hi, (x - top).astype(BF16), hi]


def _stack3(mats):
    cols = []
    for m in mats:
        top = _top_half_bits(m)
        cols += [top.astype(BF16), top.astype(BF16), (m - top).astype(BF16)]
    return jnp.concatenate(cols, axis=1)


def _fft_a_kernel(fa_ref, fb_ref, l_ref, tc_ref, ts_ref, zr_ref, zi_ref):
    rhs = jnp.concatenate(_split3(fa_ref[...]) + _split3(fb_ref[...]), axis=0)
    z = _bdot(l_ref[...], rhs)
    zr = z[:DFT_A]
    zi = z[DFT_A:]
    tc = tc_ref[...]
    ts = ts_ref[...]
    zr_ref[...] = zr * tc + zi * ts
    zi_ref[...] = zi * tc - zr * ts


def _fft_a(fa, fb, l_a, twc, tws, bn):
    b, l, _ = fa.shape
    cols = bn * F_WIDTH
    fa3 = fa.reshape(b, l // bn, cols)
    fb3 = fb.reshape(b, l // bn, cols)
    tcw = min(cols, 2048)
    assert cols % tcw == 0
    blk = pl.BlockSpec((None, DFT_A, tcw), lambda bb, j: (bb, 0, j))
    tw = pl.BlockSpec((DFT_A, tcw), lambda bb, j: (0, j))
    zr, zi = pl.pallas_call(
        _fft_a_kernel,
        out_shape=(jax.ShapeDtypeStruct((b, DFT_A, cols), F32),) * 2,
        grid=(b, cols // tcw),
        in_specs=[blk, blk, pl.BlockSpec(l_a.shape, lambda bb, j: (0, 0)), tw, tw],
        out_specs=(blk, blk),
        compiler_params=_cparams(("parallel", "parallel")),
        name="fft_stage_a",
    )(fa3, fb3, l_a, twc, tws)
    return zr.reshape(b, DFT_A, bn, F_WIDTH), zi.reshape(b, DFT_A, bn, F_WIDTH)


def _fft_b_kernel(zr_ref, zi_ref, l_ref, o_ref):
    lm = l_ref[...]
    for j in range(zr_ref.shape[0]):
        rhs = jnp.concatenate(_split3(zr_ref[j]) + _split3(zi_ref[j]), axis=0)
        o_ref[:, F_WIDTH * j:F_WIDTH * (j + 1)] = _bdot(lm, rhs).astype(o_ref.dtype)


def _fft_b(zr4, zi4, l_b, bn):
    b = zr4.shape[0]
    kb = 8
    blk = pl.BlockSpec((None, kb, bn, F_WIDTH), lambda bb, i: (bb, i, 0, 0))
    out = pl.pallas_call(
        _fft_b_kernel,
        out_shape=jax.ShapeDtypeStruct((b, bn, DFT_A * F_WIDTH), BF16),
        grid=(b, DFT_A // kb),
        in_specs=[blk, blk, pl.BlockSpec(l_b.shape, lambda bb, i: (0, 0))],
        out_specs=pl.BlockSpec((None, bn, kb * F_WIDTH), lambda bb, i: (bb, 0, i)),
        compiler_params=_cparams(("parallel", "parallel")),
        name="fft_stage_b",
    )(zr4, zi4, l_b)
    return out.reshape(b, bn * DFT_A, F_WIDTH)


def _dft_ctx_kernel(fa_ref, fb_ref, l_ref, o_ref):
    rhs = jnp.concatenate(_split3(fa_ref[...]) + _split3(fb_ref[...]), axis=0)
    o_ref[...] = _bdot(l_ref[...], rhs).astype(o_ref.dtype)


def _dft_ctx(fa, fb, l_c, n_lat, n_ctx):
    b = fa.shape[0]
    blk = pl.BlockSpec((None, n_ctx, F_WIDTH), lambda bb: (bb, n_lat // n_ctx, 0))
    return pl.pallas_call(
        _dft_ctx_kernel,
        out_shape=jax.ShapeDtypeStruct((b, n_ctx, F_WIDTH), BF16),
        grid=(b,),
        in_specs=[blk, blk, pl.BlockSpec(l_c.shape, lambda bb: (0, 0))],
        out_specs=pl.BlockSpec((None, n_ctx, F_WIDTH), lambda bb: (bb, 0, 0)),
        compiler_params=_cparams(("parallel",)),
        name="dft_ctx",
    )(fa, fb, l_c)


def _score_slot(j):
    return 2 if j == 0 else (j - 1) % 2


def _attn_kernel(q_ref, kt_ref, v_ref, lam_ref, gain_ref, o_ref, q2_ref, s_ref, m_ref, acc_ref,
                 *, lam_init, tq, tk):
    nsub = q_ref.shape[0] // tq
    nk = kt_ref.shape[1] // tk

    def keys(j):
        return slice(j * tk, (j + 1) * tk)
    lane = lax.broadcasted_iota(jnp.int32, (tq, A_DV), 1)
    comp0 = (lane % A_DH) < (A_DH // 2)
    lp = lam_ref[...]
    lam = (jnp.exp(jnp.sum(lp[0:1] * lp[1:2], keepdims=True))
           - jnp.exp(jnp.sum(lp[2:3] * lp[3:4], keepdims=True)) + lam_init)
    out_gain = gain_ref[...] * (1.0 - lam_init)

    def rows(i):
        return pl.ds(pl.multiple_of(i * tq, tq), tq)

    def load_q2(i):
        q = q_ref[rows(i), :]
        zero = jnp.zeros_like(q)
        q2_ref[0:tq, :] = jnp.where(comp0, q, zero)
        q2_ref[tq:2 * tq, :] = jnp.where(comp0, zero, q)

    def scores(j):
        s_ref[_score_slot(j)] = _bdot(q2_ref[...], kt_ref[:, keys(j)])

    def softmax_pv(j, k, first):
        s = s_ref[_score_slot(j)]
        s_max = jnp.max(s, axis=1, keepdims=True)
        if first:
            m_new = jnp.broadcast_to(s_max, (2 * tq, LANES))
        else:
            m_old = m_ref[k]
            m_new = jnp.maximum(m_old, s_max)
            alpha = jnp.exp2(m_old - m_new)
        p = jnp.exp2(s - jnp.concatenate([m_new] * (tk // LANES), axis=1)).astype(BF16)
        pv = _bdot(p, v_ref[keys(j), :])
        if first:
            acc_ref[k] = pv
        else:
            acc_ref[k] = acc_ref[k] * jnp.concatenate([alpha, alpha], axis=1) + pv
        m_ref[k] = m_new

    def finalize(i, k):
        acc = acc_ref[k]
        o0 = acc[0:tq, 0:LANES] / acc[0:tq, LANES:]
        o1 = acc[tq:, 0:LANES] / acc[tq:, LANES:]
        o = o0 - lam * o1
        ms = jnp.mean(o * o, axis=-1, keepdims=True)
        o_ref[rows(i), :] = (o * lax.rsqrt(ms + RMS_EPS) * out_gain).astype(o_ref.dtype)

    def next_tile_first_scores(i):
        load_q2(jnp.minimum(i + 1, nsub - 1))
        scores(0)

    def tile(i, k, finish_previous):
        for j in range(nk):
            if j + 1 < nk:
                scores(j + 1)
            elif nk > 1:
                next_tile_first_scores(i)
            if j == 0 and finish_previous:
                finalize(jnp.maximum(i - 1, 0), 1 - k)
            softmax_pv(j, k, first=(j == 0))
            if nk == 1:
                next_tile_first_scores(i)

    load_q2(0)
    scores(0)
    if nsub % 2 == 0:
        acc_ref[1] = jnp.ones(acc_ref.shape[1:], F32)

        def pair(ii, carry):
            i = 2 * ii
            tile(i, 0, True)
            tile(i + 1, 1, True)
            return carry

        lax.fori_loop(0, nsub // 2, pair, 0)
        finalize(nsub - 1, 1)
    else:
        def single(i, carry):
            tile(i, 0, False)
            finalize(i, 0)
            return carry

        lax.fori_loop(0, nsub, single, 0)


def _attention(q, kt, v_ext, lam_p, gain, *, q_block0, n_q, key_block0, n_keys, tq, tk, group,
               lam_init):
    b = q.shape[0]
    assert n_q % group == 0 and group % tq == 0 and n_keys % tk == 0
    return pl.pallas_call(
        functools.partial(_attn_kernel, lam_init=lam_init, tq=tq, tk=tk),
        out_shape=jax.ShapeDtypeStruct((b, n_q, A_WIDTH), BF16),
        grid=(b, A_HEADS, n_q // group),
        in_specs=[
            pl.BlockSpec((None, group, A_DV), lambda bb, h, i: (bb, q_block0 + i, h)),
            pl.BlockSpec((None, None, A_DV, n_keys), lambda bb, h, i: (bb, h, 0, key_block0)),
            pl.BlockSpec((None, n_keys, 2 * LANES), lambda bb, h, i: (bb, key_block0, h)),
            pl.BlockSpec((4, A_DH), lambda bb, h, i: (0, 0)),
            pl.BlockSpec((1, A_DV), lambda bb, h, i: (0, 0)),
        ],
        out_specs=pl.BlockSpec((None, group, A_DV), lambda bb, h, i: (bb, i, h)),
        scratch_shapes=[
            pltpu.VMEM((2 * tq, A_DV), BF16),
            pltpu.VMEM((3, 2 * tq, tk), F32),
            pltpu.VMEM((2, 2 * tq, LANES), F32),
            pltpu.VMEM((2, 2 * tq, 2 * LANES), F32),
        ],
        compiler_params=_cparams(("parallel", "parallel", "arbitrary")),
        name="diff_attn",
    )(q, kt, v_ext, lam_p, gain)


_HI16 = 0xFFFF0000


def _pack_bf16_pair(lo, hi):
    ulo = lax.bitcast_convert_type(lo.astype(BF16).astype(F32), jnp.uint32) >> 16
    uhi = lax.bitcast_convert_type(hi.astype(BF16).astype(F32), jnp.uint32) & jnp.uint32(_HI16)
    return ulo | uhi


def _top_half_bits(x):
    return lax.bitcast_convert_type(lax.bitcast_convert_type(x, jnp.uint32) & jnp.uint32(_HI16), F32)


def _unpack_bf16_pair(u):
    lo = lax.bitcast_convert_type(u << 16, F32)
    hi = lax.bitcast_convert_type(u & jnp.uint32(_HI16), F32)
    return lo, hi


ROUTE_CLASS = 0
N_PAIRS = EXPERTS_PER_GROUP * (EXPERTS_PER_GROUP - 1) // 2
N_PAIR_CLASSES = N_EXPERT_GROUPS * N_PAIRS


def _router_gates(sc_t, sel_t, route_ref, pair_gates_ref):
    s_rows = [sc_t[e:e + 1, :] for e in range(N_EXPERTS)]
    v_rows = [sel_t[e:e + 1, :] for e in range(N_EXPERTS)]
    in_top2 = []
    group_score = []
    for g in range(N_EXPERT_GROUPS):
        vs = v_rows[g * EXPERTS_PER_GROUP:(g + 1) * EXPERTS_PER_GROUP]
        tops = []
        for jj in range(EXPERTS_PER_GROUP):
            rank = jnp.zeros_like(vs[jj])
            for ii in range(EXPERTS_PER_GROUP):
                if ii == jj:
                    continue
                beats = (vs[ii] >= vs[jj]) if ii < jj else (vs[ii] > vs[jj])
                rank = rank + jnp.where(beats, 1.0, 0.0)
            tops.append(rank < 2.0)
        in_top2 += tops
        gs = jnp.zeros_like(vs[0])
        for jj in range(EXPERTS_PER_GROUP):
            gs = gs + jnp.where(tops[jj], vs[jj], 0.0)
        group_score.append(gs)
    best = group_score[0]
    gidx = jnp.zeros_like(best)
    for g in range(1, N_EXPERT_GROUPS):
        upd = group_score[g] > best
        best = jnp.where(upd, group_score[g], best)
        gidx = jnp.where(upd, float(g), gidx)
    chosen = [jnp.logical_and(in_top2[e], gidx == float(e // EXPERTS_PER_GROUP))
              for e in range(N_EXPERTS)]
    denom = jnp.zeros_like(best)
    for e in range(N_EXPERTS):
        denom = denom + jnp.where(chosen[e], s_rows[e], 0.0)
    e_lo = jnp.zeros_like(best)
    g_lo = jnp.zeros_like(best)
    for e in reversed(range(N_EXPERTS)):
        e_lo = jnp.where(chosen[e], float(e), e_lo)
        g_lo = jnp.where(chosen[e], s_rows[e] / denom, g_lo)
    e_hi = jnp.zeros_like(best)
    g_hi = jnp.zeros_like(best)
    for e in range(N_EXPERTS):
        e_hi = jnp.where(chosen[e], float(e), e_hi)
        g_hi = jnp.where(chosen[e], s_rows[e] / denom, g_hi)
    a_lo = e_lo - EXPERTS_PER_GROUP * gidx
    a_hi = e_hi - EXPERTS_PER_GROUP * gidx
    pair_rank = a_lo * (2 * EXPERTS_PER_GROUP - 1 - a_lo) * 0.5 + (a_hi - a_lo - 1.0)
    route_ref[...] = jnp.zeros(route_ref.shape, F32)
    route_ref[ROUTE_CLASS:ROUTE_CLASS + 1, :] = gidx * float(N_PAIRS) + pair_rank
    tm = g_lo.shape[1]
    pair_gates_ref[:, :LANES] = jnp.broadcast_to(g_lo, (LANES, tm)).T
    pair_gates_ref[:, LANES:] = jnp.broadcast_to(g_hi, (LANES, tm)).T


def _outproj_kernel(yf_lat_ref, yf_ctx_ref, att_lat_ref, att_ctx_ref, sg_ref, x_lat_ref, x_ctx_ref,
                    mod_ref, wo_ref, lng_ref, lnb_ref, wr_ref, rb_ref, x1_ref, h_ref, route_ref,
                    pair_gates_ref, *, alpha, nlat):
    is_ctx = pl.program_id(1) >= nlat
    x = jnp.where(is_ctx, x_ctx_ref[...], x_lat_ref[...])
    yf = jnp.where(is_ctx, yf_ctx_ref[...], yf_lat_ref[...])
    att = jnp.where(is_ctx, att_ctx_ref[...], att_lat_ref[...])
    mix = _bdot(yf, wo_ref[0:F_WIDTH, :])
    mix = mix + _bdot(att, wo_ref[F_WIDTH:F_WIDTH + A_WIDTH, :])
    mix = mix + _bdot(sg_ref[...], wo_ref[F_WIDTH + A_WIDTH:, :])
    g1 = mod_ref[2:3, :]
    x1 = _layernorm(alpha * x + g1 * mix, lng_ref[...], lnb_ref[...])
    x1_ref[...] = x1
    h = x1 * (1.0 + mod_ref[4:5, :]) + mod_ref[3:4, :]
    half = h.shape[1] // 2
    h_ref[...] = _pack_bf16_pair(h[:, :half], h[:, half:])
    h_top = _top_half_bits(h)
    h_hi = h_top.astype(BF16)
    h_lo = (h - h_top).astype(BF16)
    hw = _bdot(h_hi, wr_ref[...])
    logits = hw[:, :ROUTER_LANES] + (hw[:, ROUTER_LANES:] + _bdot(h_lo, wr_ref[:, :ROUTER_LANES]))
    scores = jax.nn.sigmoid(logits)
    sel = scores + rb_ref[...]
    _router_gates(scores.T[0:N_EXPERTS, :], sel.T[0:N_EXPERTS, :], route_ref, pair_gates_ref)


def _outproj(yf_lat, yf_ctx, att_lat, att_ctx, sg, x_lat, x_ctx, x_ctx_row0, mods_l, wo, lng, lnb, wr, rb,
             n_lat, alpha):
    b, _, d = x_lat.shape
    l = sg.shape[1]
    tm = TOKEN_TILE
    nlat = n_lat // tm
    ctx_tile0 = x_ctx_row0 // tm
    seg = lambda i: jnp.where(i >= nlat, 1, 0)
    row = lambda width: pl.BlockSpec((None, tm, width), lambda bb, i: (bb, i, 0))
    lat = lambda width: pl.BlockSpec((None, tm, width), lambda bb, i: (bb, jnp.minimum(i, nlat - 1), 0))
    ctx = lambda width: pl.BlockSpec((None, tm, width), lambda bb, i: (bb, jnp.maximum(i - nlat, 0), 0))
    const2 = lambda shape: pl.BlockSpec(shape, lambda bb, i: (0, 0))
    return pl.pallas_call(
        functools.partial(_outproj_kernel, alpha=alpha, nlat=nlat),
        out_shape=(
            jax.ShapeDtypeStruct((b, l, d), F32),
            jax.ShapeDtypeStruct((b, l, d // 2), jnp.uint32),
            jax.ShapeDtypeStruct((b, SUBLANES, l), F32),
            jax.ShapeDtypeStruct((b, l, 2 * LANES), F32),
        ),
        grid=(b, l // tm),
        in_specs=[
            lat(F_WIDTH), ctx(F_WIDTH), lat(A_WIDTH), ctx(A_WIDTH), row(G_WIDTH), lat(d),
            pl.BlockSpec((None, tm, d), lambda bb, i: (bb, ctx_tile0 + jnp.maximum(i - nlat, 0), 0)),
            pl.BlockSpec((None, None, MOD_ROWS, d), lambda bb, i: (bb, seg(i), 0, 0)),
            const2((d, d)), const2((1, d)), const2((1, d)),
            const2((d, 2 * ROUTER_LANES)), const2((1, ROUTER_LANES)),
        ],
        out_specs=(row(d), row(d // 2),
                   pl.BlockSpec((None, SUBLANES, tm), lambda bb, i: (bb, 0, i)), row(2 * LANES)),
        compiler_params=_cparams(("parallel", "parallel")),
        name="outproj_ln_router",
    )(yf_lat, yf_ctx, att_lat, att_ctx, sg, x_lat, x_ctx, mods_l, wo, lng, lnb, wr, rb)


def _moe_plan(route, tm):
    b, _, l = route.shape
    t = b * l
    pairs = [(a, c) for a in range(EXPERTS_PER_GROUP) for c in range(a + 1, EXPERTS_PER_GROUP)]
    cls_e1 = np.zeros((N_PAIR_CLASSES,), np.int32)
    cls_e2 = np.zeros((N_PAIR_CLASSES,), np.int32)
    for k, (a, c) in enumerate(pairs):
        assert k == a * (2 * EXPERTS_PER_GROUP - 1 - a) // 2 + (c - a - 1)
        for g in range(N_EXPERT_GROUPS):
            cls_e1[g * N_PAIRS + k] = g * EXPERTS_PER_GROUP + a
            cls_e2[g * N_PAIRS + k] = g * EXPERTS_PER_GROUP + c
    cls = route[:, ROUTE_CLASS, :].astype(jnp.int32).reshape(t)
    onehot = cls[:, None] == jnp.arange(N_PAIR_CLASSES, dtype=jnp.int32)[None, :]
    oh_tiles = onehot.reshape(t // tm, tm, N_PAIR_CLASSES).astype(BF16)
    tri = jnp.tril(jnp.ones((tm, tm), BF16))
    within = jnp.einsum("ij,njc->nic", tri, oh_tiles, preferred_element_type=F32)
    tile_total = within[:, -1, :]
    n_tiles = t // tm
    earlier = jnp.tril(jnp.ones((n_tiles, n_tiles), BF16), k=-1)
    before = jnp.dot(earlier, tile_total.astype(BF16), preferred_element_type=F32)
    csum = (within + before[:, None, :]).reshape(t, N_PAIR_CLASSES).astype(jnp.int32)
    onehot = onehot.astype(jnp.int32)
    rank = jnp.sum(csum * onehot, axis=1) - 1
    padded = (csum[-1] + tm - 1) // tm * tm
    upto = jnp.tril(jnp.ones((N_PAIR_CLASSES, N_PAIR_CLASSES), jnp.int32))
    off_end = jnp.sum(upto * padded[None, :], axis=1)
    dest = jnp.sum(onehot * (off_end - padded)[None, :], axis=1) + rank
    nt = t // tm + N_PAIR_CLASSES
    n_valid = (off_end[-1] // tm).astype(jnp.int32)
    first_row = jnp.minimum(jnp.arange(nt, dtype=jnp.int32), n_valid - 1) * tm
    tile_cls = jnp.sum((off_end[None, :] <= first_row[:, None]).astype(jnp.int32), axis=1)
    tile_cls = jnp.minimum(tile_cls, N_PAIR_CLASSES - 1)
    return (dest.astype(jnp.int32), jnp.asarray(cls_e1)[tile_cls], jnp.asarray(cls_e2)[tile_cls],
            n_valid.reshape(1), nt)


def _dispatch_kernel(dest_ref, h_ref, hs_init_ref, hs_ref, stage, sems):
    del hs_init_ref
    s = pl.program_id(0)
    n = pl.num_programs(0)
    tm = h_ref.shape[0]
    slot = s % 2

    def wait_slot(k):
        pltpu.make_async_copy(stage.at[k], hs_ref.at[pl.ds(0, tm), :], sems.at[k]).wait()

    @pl.when(s >= 2)
    def _reuse():
        wait_slot(slot)

    stage[slot] = h_ref[...]
    for r in range(tm):
        pltpu.make_async_copy(stage.at[slot, pl.ds(r, 1), :],
                              hs_ref.at[pl.ds(dest_ref[0, r], 1), :], sems.at[slot]).start()

    @pl.when(s == n - 1)
    def _drain():
        @pl.when(n >= 2)
        def _other():
            wait_slot(1 - slot)
        wait_slot(slot)


def _dispatch(h_packed, dest, nt, tm):
    t, w = h_packed.shape
    hs_init = jnp.zeros((nt * tm, w), jnp.uint32)
    return pl.pallas_call(
        _dispatch_kernel,
        out_shape=jax.ShapeDtypeStruct((nt * tm, w), jnp.uint32),
        grid=(t // tm,),
        in_specs=[
            pl.BlockSpec((None, 1, tm), lambda i: (i, 0, 0), memory_space=pltpu.SMEM),
            pl.BlockSpec((tm, w), lambda i: (i, 0)),
            pl.BlockSpec(memory_space=pl.ANY),
        ],
        out_specs=pl.BlockSpec(memory_space=pl.ANY),
        scratch_shapes=[pltpu.VMEM((2, tm, w), jnp.uint32), pltpu.SemaphoreType.DMA((2,))],
        input_output_aliases={2: 0},
        compiler_params=_cparams(("arbitrary",)),
        name="moe_dispatch",
    )(dest.reshape(t // tm, 1, tm), h_packed, hs_init)


def _expert_pair_kernel(e1_ref, e2_ref, nv_ref, hs_ref, w13a_ref, w2a_ref, w13b_ref, w2b_ref, y_ref):
    del e1_ref, e2_ref
    i = pl.program_id(0)

    @pl.when(i < nv_ref[0])
    def _compute():
        lo, hi = _unpack_bf16_pair(hs_ref[...])
        h = jnp.concatenate([lo, hi], axis=1).astype(BF16)

        def ffn(w13_ref, w2_ref):
            gu = _bdot(h, w13_ref[...])
            de = gu.shape[1] // 2
            g = gu[:, :de]
            a = (g * jax.nn.sigmoid(g)) * gu[:, de:]
            return _bdot(a.astype(BF16), w2_ref[...])

        y_ref[...] = _pack_bf16_pair(ffn(w13a_ref, w2a_ref), ffn(w13b_ref, w2b_ref))

    @pl.when(i >= nv_ref[0])
    def _unused_tile():
        y_ref[...] = jnp.zeros(y_ref.shape, jnp.uint32)


def _expert_pairs(hs, tile_e1, tile_e2, n_valid, w13, w2, tm):
    rows, w = hs.shape
    _, de, d = w2.shape
    first = lambda shape: pl.BlockSpec(shape, lambda i, e1, e2, nv: (e1[i], 0, 0))
    second = lambda shape: pl.BlockSpec(shape, lambda i, e1, e2, nv: (e2[i], 0, 0))
    grid_spec = pltpu.PrefetchScalarGridSpec(
        num_scalar_prefetch=3,
        grid=(rows // tm,),
        in_specs=[
            pl.BlockSpec((tm, w), lambda i, e1, e2, nv: (jnp.minimum(i, nv[0] - 1), 0)),
            first((None, d, 2 * de)), first((None, de, d)),
            second((None, d, 2 * de)), second((None, de, d)),
        ],
        out_specs=pl.BlockSpec((tm, d), lambda i, e1, e2, nv: (i, 0)),
    )
    return pl.pallas_call(
        _expert_pair_kernel,
        out_shape=jax.ShapeDtypeStruct((rows, d), jnp.uint32),
        grid_spec=grid_spec,
        compiler_params=_cparams(("arbitrary",)),
        name="moe_expert_pairs",
    )(tile_e1, tile_e2, n_valid, hs, w13, w2, w13, w2)


def _combine_kernel(dest_ref, dest_next_ref, y2_ref, x_ref, pg_ref, mod_ref, lng_ref, lnb_ref,
                    o_ref, ybuf, sems, *, alpha):
    o_ref[...] = _combine_body(dest_ref, dest_next_ref, y2_ref, x_ref, pg_ref, mod_ref, lng_ref,
                               lnb_ref, ybuf, sems, alpha=alpha)


def _combine_inproj_kernel(dest_ref, dest_next_ref, y2_ref, x_ref, pg_ref, mod_ref, lng_ref, lnb_ref,
                           *refs, alpha, q_scale):
    n_inproj_in = 8
    inproj_in = refs[:n_inproj_in]
    o_ref = refs[n_inproj_in]
    inproj_out = refs[n_inproj_in + 1:n_inproj_in + 7]
    ybuf, sems = refs[n_inproj_in + 7:]
    x_new = _combine_body(dest_ref, dest_next_ref, y2_ref, x_ref, pg_ref, mod_ref, lng_ref, lnb_ref,
                          ybuf, sems, alpha=alpha)
    o_ref[...] = x_new
    _inproj_body(x_new, *inproj_in, *inproj_out, q_scale=q_scale)


def _combine_body(dest_ref, dest_next_ref, y2_ref, x_ref, pg_ref, mod_ref, lng_ref, lnb_ref,
                  ybuf, sems, *, alpha):
    s = pl.program_id(0)
    n = pl.num_programs(0)
    tm, d = x_ref.shape
    slot = s % 2

    def start_gather(idx_ref, k):
        for r in range(tm):
            pltpu.make_async_copy(y2_ref.at[pl.ds(idx_ref[0, r], 1), :],
                                  ybuf.at[k, pl.ds(r, 1), :], sems.at[k]).start()

    @pl.when(s == 0)
    def _first():
        start_gather(dest_ref, 0)

    @pl.when(s + 1 < n)
    def _prefetch():
        start_gather(dest_next_ref, 1 - slot)

    pltpu.make_async_copy(y2_ref.at[pl.ds(0, tm), :], ybuf.at[slot], sems.at[slot]).wait()
    y_lo, y_hi = _unpack_bf16_pair(ybuf[slot])
    pg = pg_ref[...]
    reps = d // LANES
    g_lo = jnp.concatenate([pg[:, :LANES]] * reps, axis=1)
    g_hi = jnp.concatenate([pg[:, LANES:]] * reps, axis=1)
    y = g_lo * y_lo + g_hi * y_hi
    return _layernorm(alpha * x_ref[...] + mod_ref[5:6, :] * y, lng_ref[...], lnb_ref[...])


def _combine(y2, dest, x1, pair_gates, mods_l, lng, lnb, *, n_rows, n_lat, tm, alpha, next_inproj=None):
    b, l, d = x1.shape
    nlat = n_lat // tm
    tiles_per_batch = l // tm
    per_batch = n_rows // tm
    n_steps = b * per_batch
    batch = lambda s: s // per_batch
    tile = lambda s: s % per_batch
    seg = lambda s: jnp.where(tile(s) >= nlat, 1, 0)
    token_tile = lambda s: batch(s) * tiles_per_batch + tile(s)
    row = lambda width: pl.BlockSpec((None, tm, width), lambda s: (batch(s), tile(s), 0))
    const2 = lambda shape: pl.BlockSpec(shape, lambda s: (0, 0))
    mod_spec = pl.BlockSpec((None, None, MOD_ROWS, d), lambda s: (batch(s), seg(s), 0, 0))
    dest3 = dest.reshape(b * tiles_per_batch, 1, tm)
    in_specs = [
        pl.BlockSpec((None, 1, tm), lambda s: (token_tile(s), 0, 0), memory_space=pltpu.SMEM),
        pl.BlockSpec((None, 1, tm), lambda s: (token_tile(jnp.minimum(s + 1, n_steps - 1)), 0, 0),
                     memory_space=pltpu.SMEM),
        pl.BlockSpec(memory_space=pl.ANY),
        row(d), row(2 * LANES), mod_spec, const2((1, d)), const2((1, d)),
    ]
    args = [dest3, dest3, y2, x1, pair_gates, mods_l, lng, lnb]
    out_shape = [jax.ShapeDtypeStruct((b, n_rows, d), F32)]
    out_specs = [row(d)]
    if next_inproj is None:
        body = functools.partial(_combine_kernel, alpha=alpha)
        name = "moe_combine_ln"
    else:
        assert n_rows == l and tm == TOKEN_TILE
        mods_n, w_big, cosf, sinf, slng, slnb, ws, bs_full, q_scale = next_inproj
        in_specs += [
            mod_spec, const2((d, BIG_COLS)),
            pl.BlockSpec((tm, LANES), lambda s: (tile(s), 0)),
            pl.BlockSpec((tm, LANES), lambda s: (tile(s), 0)),
            const2((1, G_WIDTH)), const2((1, G_WIDTH)),
            pl.BlockSpec((G_GROUPS, CHUNK, CHUNK), lambda s: (0, 0, 0)),
            const2((CHUNK, G_WIDTH)),
        ]
        args += [mods_n, w_big, cosf, sinf, slng, slnb, ws, bs_full]
        out_shape += list(_inproj_out_shapes(b, l))
        out_specs += [row(F_WIDTH), row(F_WIDTH), row(A_QK_WIDTH),
                      pl.BlockSpec((None, A_HEADS, A_DV, tm), lambda s: (batch(s), 0, 0, tile(s))),
                      row(2 * A_WIDTH), row(G_WIDTH)]
        body = functools.partial(_combine_inproj_kernel, alpha=alpha, q_scale=q_scale)
        name = "moe_combine_ln_inproj"
    outs = pl.pallas_call(
        body,
        out_shape=tuple(out_shape),
        grid=(n_steps,),
        in_specs=in_specs,
        out_specs=tuple(out_specs),
        scratch_shapes=[pltpu.VMEM((2, tm, d), jnp.uint32), pltpu.SemaphoreType.DMA((2,))],
        compiler_params=_cparams(("arbitrary",)),
        name=name,
    )(*args)
    return outs[0] if next_inproj is None else outs


def _head_lane_fields():
    j = jnp.arange(LANES)
    half = j // (LANES // 2)
    comp = (j % (LANES // 2)) // (A_DH // 2)
    axis = (j % (A_DH // 2)) // (A_DH // 4)
    freq = j % (A_DH // 4)
    return half, comp, axis, freq


def _qk_column_perm():
    half, comp, axis, freq = _head_lane_fields()
    orig = comp * A_DH + axis * (A_DH // 2) + half * (A_DH // 4) + freq
    return (jnp.arange(A_HEADS)[:, None] * LANES + orig[None, :]).reshape(-1)


def _rope_tables(n_lat, n_ctx):
    half, _, axis, freq = _head_lane_fields()
    rows = n_lat // GRID_W
    row = jnp.repeat(jnp.arange(rows, dtype=F32), GRID_W)
    col = jnp.tile(jnp.arange(GRID_W, dtype=F32), rows)
    hd = A_DH // 2
    inv = ROPE_BASE ** (-jnp.arange(0, hd, 2, dtype=F32) / hd)
    pos = jnp.where(axis[None, :] == 0, row[:, None], col[:, None])
    ang = pos * inv[freq][None, :]
    cosf = jnp.cos(ang)
    sinf = jnp.sin(ang) * jnp.where(half == 0, -1.0, 1.0)[None, :]
    cosf = jnp.concatenate([cosf, jnp.ones((n_ctx, LANES), F32)], axis=0)
    sinf = jnp.concatenate([sinf, jnp.zeros((n_ctx, LANES), F32)], axis=0)
    return cosf, sinf


def _dft_mats(n, scale=1.0):
    i = jnp.arange(n, dtype=jnp.int32)
    ang = ((i[:, None] * i[None, :]) % n).astype(F32) * (2.0 * math.pi / n)
    return jnp.cos(ang) * scale, jnp.sin(ang) * scale


def _twiddles(n, bn):
    ka = jnp.arange(DFT_A, dtype=jnp.int32)
    bb = jnp.arange(bn, dtype=jnp.int32)
    ang = ((ka[:, None] * bb[None, :]) % n).astype(F32) * (2.0 * math.pi / n)
    shape = (DFT_A, bn, F_WIDTH)
    twc = jnp.broadcast_to(jnp.cos(ang)[:, :, None], shape).reshape(DFT_A, bn * F_WIDTH)
    tws = jnp.broadcast_to(jnp.sin(ang)[:, :, None], shape).reshape(DFT_A, bn * F_WIDTH)
    return twc, tws


def _block_diag(blocks):
    g, n = blocks.shape[-3], blocks.shape[-1]
    eye = jnp.eye(g, dtype=blocks.dtype)
    out = blocks[..., :, :, None, :] * eye[:, None, :, None]
    return out.reshape(blocks.shape[:-3] + (g * n, g * n))


def kernel(x, c, ctx, c_ctx, w_ada, b_ada, w_in, w_fourier, diff_lambda, diff_subln,
           sgu_ln_g, sgu_ln_b, sgu_w, sgu_b, w_out, ln_g, ln_b, w_router, router_bias,
           moe_w1, moe_w3, moe_w2):
    b, n, d = x.shape
    n_ctx = ctx.shape[1]
    depth = w_ada.shape[0]
    l = n + n_ctx
    alpha = (2 * depth) ** 0.25
    bn = n // DFT_A
    assert n % DFT_A == 0 and bn % SUBLANES == 0 and n % GRID_W == 0
    assert n_ctx % TOKEN_TILE == 0 and n % TOKEN_TILE == 0 and n % n_ctx == 0
    assert b + 1 <= MOD_ROWS

    c_rows = jnp.concatenate([c, c_ctx[None, :], jnp.zeros((MOD_ROWS - b - 1, d), F32)], axis=0)
    mod = _ada(c_rows, w_ada, b_ada).reshape(depth, MOD_ROWS, 6, d)
    lat = mod[:, :b]
    cm = jnp.broadcast_to(mod[:, b:b + 1], lat.shape)
    mods = jnp.stack([lat, cm], axis=2)
    mods = jnp.pad(mods, ((0, 0), (0, 0), (0, 0), (0, MOD_ROWS - 6), (0, 0)))

    perm = _qk_column_perm()
    c64, s64 = _dft_mats(F_DIM)
    eye_g = jnp.eye(F_GROUPS, dtype=F32)
    bdc = jnp.kron(eye_g, c64)
    bds = jnp.kron(eye_g, s64)
    bdw = _block_diag(w_fourier)
    w_fab = _fourier_weights(w_in[:, :, :F_WIDTH], bdc, bds, bdw)
    o_q = F_WIDTH
    o_k = o_q + A_QK_WIDTH
    o_v = o_k + A_QK_WIDTH
    o_g = o_v + A_WIDTH
    w_big = jnp.concatenate([
        w_fab,
        w_in[:, :, o_q:o_k][:, :, perm],
        w_in[:, :, o_k:o_v][:, :, perm],
        w_in[:, :, o_v:],
    ], axis=-1).astype(BF16)
    wo = w_out.astype(BF16)
    w13 = jnp.concatenate([moe_w1, moe_w3], axis=-1).astype(BF16)
    w2 = moe_w2.astype(BF16)
    ws = sgu_w.astype(BF16)
    bs_full = jnp.repeat(jnp.swapaxes(sgu_b, 1, 2), G_DIM, axis=2)
    wr_f32 = jnp.pad(w_router, ((0, 0), (0, ROUTER_LANES - N_EXPERTS)))
    wr_top = _top_half_bits(wr_f32)
    wr = jnp.concatenate([wr_top.astype(BF16), (wr_f32 - wr_top).astype(BF16)], axis=1)
    rb =jnp.pad(router_bias, (0, ROUTER_LANES - N_EXPERTS)).reshape(1, ROUTER_LANES)

    cosf, sinf = _rope_tables(n, n_ctx)
    ca, sa = _dft_mats(DFT_A)
    l_a = jnp.concatenate([_stack3([ca, -sa]), _stack3([-sa, -ca])], axis=0)
    twc, tws = _twiddles(n, bn)
    l_b = _stack3(list(_dft_mats(bn, scale=(n * F_DIM) ** -0.5)))
    cc, sc = _dft_mats(n_ctx, scale=(n_ctx * F_DIM) ** -0.5)
    l_c = _stack3([cc, -sc])
    q_scale = (A_DH ** -0.5) * math.log2(math.e)

    tk = ATTN_TK
    assert l % tk == 0
    assert n % MOE_TILE == 0 and n_ctx % MOE_TILE == 0

    def inproj_params(li):
        return (mods[li], w_big[li], cosf, sinf, sgu_ln_g[li][None], sgu_ln_b[li][None],
                ws[li], bs_full[li])

    fa, fb, q, kt, v_ext, sg = _inproj(x, ctx, *inproj_params(0), q_scale)
    stream = (x, ctx, 0)
    for li in range(depth):
        last = li == depth - 1
        lam_init = 0.8 - 0.6 * math.exp(-0.3 * li)
        zr, zi = _fft_a(fa, fb, l_a, twc, tws, bn)
        yf_x = _fft_b(zr, zi, l_b, bn)
        yf_c = _dft_ctx(fa, fb, l_c, n, n_ctx)
        gain = diff_subln[li][None]
        att_x = _attention(q, kt, v_ext, diff_lambda[li], gain, q_block0=0, n_q=n, key_block0=0,
                           n_keys=l, tq=ATTN_TQ, tk=tk, group=min(ATTN_GROUP, n), lam_init=lam_init)
        att_c = _attention(q, kt, v_ext, diff_lambda[li], gain, q_block0=n // n_ctx, n_q=n_ctx,
                           key_block0=n // n_ctx, n_keys=n_ctx, tq=n_ctx, tk=n_ctx, group=n_ctx,
                           lam_init=lam_init)
        x1, h_packed, route, pair_gates = _outproj(
            yf_x, yf_c, att_x, att_c, sg, *stream, mods[li], wo[li], ln_g[li, 0][None], ln_b[li, 0][None], wr, rb, n, alpha)
        dest, tile_e1, tile_e2, n_valid, nt = _moe_plan(route, MOE_TILE)
        hs = _dispatch(h_packed.reshape(b * l, d // 2), dest, nt, MOE_TILE)
        y2 = _expert_pairs(hs, tile_e1, tile_e2, n_valid, w13[li], w2[li], MOE_TILE)
        combine_args = (y2, dest, x1, pair_gates, mods[li], ln_g[li, 1][None], ln_b[li, 1][None])
        if last:
            xa = _combine(*combine_args, n_rows=n, n_lat=n, tm=MOE_TILE, alpha=alpha)
        else:
            xa, fa, fb, q, kt, v_ext, sg = _combine(
                *combine_args, n_rows=l, n_lat=n, tm=MOE_TILE, alpha=alpha,
                next_inproj=inproj_params(li + 1) + (q_scale,))
            stream = (xa, xa, n)
    return xa
```

```python
import functools
import math

import numpy as np
import jax
import jax.numpy as jnp
from jax import lax
from jax.experimental import pallas as pl
from jax.experimental.pallas import tpu as pltpu

F32 = jnp.float32
BF16 = jnp.bfloat16
HIGHEST = lax.Precision.HIGHEST

GRID_W = 64
F_GROUPS, F_DIM = 4, 64
F_WIDTH = F_GROUPS * F_DIM
A_HEADS, A_DH = 4, 64
A_DV = 2 * A_DH
A_QK_WIDTH = A_HEADS * 2 * A_DH
A_WIDTH = A_HEADS * A_DV
G_GROUPS, G_DIM = 4, 64
G_WIDTH = G_GROUPS * G_DIM
CHUNK = 128
ROPE_BASE = 10000.0
N_EXPERTS = 16
N_EXPERT_GROUPS = 4
EXPERTS_PER_GROUP = N_EXPERTS // N_EXPERT_GROUPS
LN_EPS = 1e-5
RMS_EPS = 1e-5

COL_FA = 0
COL_FB = COL_FA + F_WIDTH
COL_Q = COL_FB + F_WIDTH
COL_K = COL_Q + A_QK_WIDTH
COL_V = COL_K + A_QK_WIDTH
COL_G = COL_V + A_WIDTH
BIG_COLS = COL_G + 2 * G_WIDTH

LANES = 128
SUBLANES = 8
MOD_ROWS = 8
ROUTER_LANES = 128
DFT_A = 128
N_DMA_PRIORITIES = 2
VMEM_LIMIT = 56 * 1024 * 1024

TOKEN_TILE = 256
ATTN_TQ = 256
ATTN_GROUP = 8192
ATTN_TK = 1280
MOE_TILE = 256


def _cparams(sem):
    return pltpu.CompilerParams(dimension_semantics=sem, vmem_limit_bytes=VMEM_LIMIT)


def _hdot(a, b):
    return jnp.dot(a, b, precision=HIGHEST, preferred_element_type=F32)


def _bdot(a, b):
    return jnp.dot(a, b, preferred_element_type=F32)


def _layernorm(y, g, b):
    mu = jnp.mean(y, axis=-1, keepdims=True)
    d = y - mu
    var = jnp.mean(d * d, axis=-1, keepdims=True)
    return d * lax.rsqrt(var + LN_EPS) * g + b


def _ada_kernel(c_ref, w_ref, b_ref, o_ref):
    c = c_ref[...]
    a = c * jax.nn.sigmoid(c)
    o_ref[...] = _hdot(a, w_ref[...]) + b_ref[...]


def _ada(c_rows, w_ada, b_ada):
    depth, d, cols = w_ada.shape
    tn = 1536
    assert cols % tn == 0
    return pl.pallas_call(
        _ada_kernel,
        out_shape=jax.ShapeDtypeStruct((depth, MOD_ROWS, cols), F32),
        grid=(depth, cols // tn),
        in_specs=[
            pl.BlockSpec((MOD_ROWS, d), lambda l, j: (0, 0)),
            pl.BlockSpec((None, d, tn), lambda l, j: (l, 0, j)),
            pl.BlockSpec((None, 1, tn), lambda l, j: (l, 0, j)),
        ],
        out_specs=pl.BlockSpec((None, MOD_ROWS, tn), lambda l, j: (l, 0, j)),
        compiler_params=_cparams(("parallel", "parallel")),
        name="ada_mod",
    )(c_rows, w_ada, b_ada.reshape(depth, 1, cols))


def _fw_kernel(wf_ref, bdc_ref, bds_ref, bdw_ref, o_ref):
    bdw = bdw_ref[...]
    mc = _hdot(bdc_ref[...], bdw)
    ms = _hdot(bds_ref[...], bdw)
    wf = wf_ref[...]
    o_ref[:, :F_WIDTH] = _hdot(wf, mc)
    o_ref[:, F_WIDTH:] = _hdot(wf, ms)


def _fourier_weights(wf, bdc, bds, bdw):
    depth, d, _ = wf.shape
    return pl.pallas_call(
        _fw_kernel,
        out_shape=jax.ShapeDtypeStruct((depth, d, 2 * F_WIDTH), F32),
        grid=(depth,),
        in_specs=[
            pl.BlockSpec((None, d, F_WIDTH), lambda l: (l, 0, 0)),
            pl.BlockSpec((F_WIDTH, F_WIDTH), lambda l: (0, 0)),
            pl.BlockSpec((F_WIDTH, F_WIDTH), lambda l: (0, 0)),
            pl.BlockSpec((None, F_WIDTH, F_WIDTH), lambda l: (l, 0, 0)),
        ],
        out_specs=pl.BlockSpec((None, d, 2 * F_WIDTH), lambda l: (l, 0, 0)),
        compiler_params=_cparams(("parallel",)),
        name="fourier_weights",
    )(wf, bdc, bds, bdw)


def _gelu_tanh(x):
    c = math.sqrt(2.0 / math.pi)
    return x * (0.5 * (1.0 + jnp.tanh(c * (x + 0.044715 * (x * x * x)))))


def _inproj_kernel(x_lat_ref, x_ctx_ref, *refs, q_scale, nlat):
    x = jnp.where(pl.program_id(1) >= nlat, x_ctx_ref[...], x_lat_ref[...])
    _inproj_body(x, *refs, q_scale=q_scale)


def _inproj_body(x, mod_ref, w_ref, cos_ref, sin_ref, lng_ref, lnb_ref, ws_ref, bs_ref,
                 fa_ref, fb_ref, q_ref, kt_ref, v_ref, sg_ref, *, q_scale):
    sh = mod_ref[0:1, :]
    sc = mod_ref[1:2, :]
    h = (x * (1.0 + sc) + sh).astype(BF16)
    p = _bdot(h, w_ref[...])
    fa_ref[...] = p[:, COL_FA:COL_FA + F_WIDTH]
    fb_ref[...] = p[:, COL_FB:COL_FB + F_WIDTH]
    cosf = cos_ref[...]
    sinf = sin_ref[...]
    for hh in range(A_HEADS):
        t = p[:, COL_Q + LANES * hh:COL_Q + LANES * (hh + 1)]
        r = t * cosf + pltpu.roll(t, LANES // 2, 1) * sinf
        q_ref[:, LANES * hh:LANES * (hh + 1)] = (r * q_scale).astype(BF16)
        t = p[:, COL_K + LANES * hh:COL_K + LANES * (hh + 1)]
        r = t * cosf + pltpu.roll(t, LANES // 2, 1) * sinf
        kt_ref[hh] = r.T.astype(BF16)
        v_ref[:, 2 * LANES * hh:2 * LANES * hh + LANES] = (
            p[:, COL_V + LANES * hh:COL_V + LANES * (hh + 1)].astype(BF16))
        v_ref[:, 2 * LANES * hh + LANES:2 * LANES * (hh + 1)] = jnp.ones((t.shape[0], LANES), BF16)

    g = _gelu_tanh(p[:, COL_G:COL_G + 2 * G_WIDTH])
    u = g[:, :G_WIDTH]
    vn = _layernorm(g[:, G_WIDTH:], lng_ref[...], lnb_ref[...])
    lane_group = lax.broadcasted_iota(jnp.int32, (CHUNK, G_WIDTH), 1) // G_DIM
    tm = x.shape[0]
    for c in range(tm // CHUNK):
        rows = slice(c * CHUNK, (c + 1) * CHUNK)
        vc = vn[rows]
        mixed = bs_ref[...]
        for gi in range(G_GROUPS):
            vm = jnp.where(lane_group == gi, vc, 0.0).astype(BF16)
            mixed = mixed + _bdot(ws_ref[gi], vm)
        sg_ref[rows, :] = (u[rows] * mixed).astype(BF16)


def _inproj_out_shapes(b, l):
    return (
        jax.ShapeDtypeStruct((b, l, F_WIDTH), F32),
        jax.ShapeDtypeStruct((b, l, F_WIDTH), F32),
        jax.ShapeDtypeStruct((b, l, A_QK_WIDTH), BF16),
        jax.ShapeDtypeStruct((b, A_HEADS, A_DV, l), BF16),
        jax.ShapeDtypeStruct((b, l, 2 * A_WIDTH), BF16),
        jax.ShapeDtypeStruct((b, l, G_WIDTH), BF16),
    )


def _inproj(x_lat, x_ctx, mods_l, w_big, cosf, sinf, lng, lnb, ws, bs_full, q_scale):
    b, n_lat, d = x_lat.shape
    l = n_lat + x_ctx.shape[1]
    tm = TOKEN_TILE
    nlat = n_lat // tm
    seg = lambda i: jnp.where(i >= nlat, 1, 0)
    row = lambda width: pl.BlockSpec((None, tm, width), lambda bb, i: (bb, i, 0))
    const2 = lambda shape: pl.BlockSpec(shape, lambda bb, i: (0, 0))
    return pl.pallas_call(
        functools.partial(_inproj_kernel, q_scale=q_scale, nlat=nlat),
        out_shape=_inproj_out_shapes(b, l),
        grid=(b, l // tm),
        in_specs=[
            pl.BlockSpec((None, tm, d), lambda bb, i: (bb, jnp.minimum(i, nlat - 1), 0)),
            pl.BlockSpec((None, tm, d), lambda bb, i: (bb, jnp.maximum(i - nlat, 0), 0)),
            pl.BlockSpec((None, None, MOD_ROWS, d), lambda bb, i: (bb, seg(i), 0, 0)),
            const2((d, BIG_COLS)),
            pl.BlockSpec((tm, LANES), lambda bb, i: (i, 0)),
            pl.BlockSpec((tm, LANES), lambda bb, i: (i, 0)),
            const2((1, G_WIDTH)),
            const2((1, G_WIDTH)),
            pl.BlockSpec((G_GROUPS, CHUNK, CHUNK), lambda bb, i: (0, 0, 0)),
            const2((CHUNK, G_WIDTH)),
        ],
        out_specs=(row(F_WIDTH), row(F_WIDTH), row(A_QK_WIDTH),
                   pl.BlockSpec((None, A_HEADS, A_DV, tm), lambda bb, i: (bb, 0, 0, i)),
                   row(2 * A_WIDTH), row(G_WIDTH)),
        compiler_params=_cparams(("parallel", "parallel")),
        name="inproj",
    )(x_lat, x_ctx, mods_l, w_big, cosf, sinf, lng, lnb, ws, bs_full)


def _split3(x):
    top = _top_half_bits(x)
    hi = top.astype(BF16)
    return [hi, (x - top).astype(BF16), hi]


def _stack3(mats):
    cols = []
    for m in mats:
        top = _top_half_bits(m)
        cols += [top.astype(BF16), top.astype(BF16), (m - top).astype(BF16)]
    return jnp.concatenate(cols, axis=1)


def _fft_a_kernel(fa_ref, fb_ref, l_ref, tc_ref, ts_ref, zr_ref, zi_ref):
    rhs = jnp.concatenate(_split3(fa_ref[...]) + _split3(fb_ref[...]), axis=0)
    z = _bdot(l_ref[...], rhs)
    zr = z[:DFT_A]
    zi = z[DFT_A:]
    tc = tc_ref[...]
    ts = ts_ref[...]
    zr_ref[...] = zr * tc + zi * ts
    zi_ref[...] = zi * tc - zr * ts


def _fft_a(fa, fb, l_a, twc, tws, bn):
    b, l, _ = fa.shape
    cols = bn * F_WIDTH
    fa3 = fa.reshape(b, l // bn, cols)
    fb3 = fb.reshape(b, l // bn, cols)
    tcw = min(cols, 2048)
    assert cols % tcw == 0
    blk = pl.BlockSpec((None, DFT_A, tcw), lambda bb, j: (bb, 0, j))
    tw = pl.BlockSpec((DFT_A, tcw), lambda bb, j: (0, j))
    zr, zi = pl.pallas_call(
        _fft_a_kernel,
        out_shape=(jax.ShapeDtypeStruct((b, DFT_A, cols), F32),) * 2,
        grid=(b, cols // tcw),
        in_specs=[blk, blk, pl.BlockSpec(l_a.shape, lambda bb, j: (0, 0)), tw, tw],
        out_specs=(blk, blk),
        compiler_params=_cparams(("parallel", "parallel")),
        name="fft_stage_a",
    )(fa3, fb3, l_a, twc, tws)
    return zr.reshape(b, DFT_A, bn, F_WIDTH), zi.reshape(b, DFT_A, bn, F_WIDTH)


def _fft_b_kernel(zr_ref, zi_ref, l_ref, o_ref):
    lm = l_ref[...]
    for j in range(zr_ref.shape[0]):
        rhs = jnp.concatenate(_split3(zr_ref[j]) + _split3(zi_ref[j]), axis=0)
        o_ref[:, F_WIDTH * j:F_WIDTH * (j + 1)] = _bdot(lm, rhs).astype(o_ref.dtype)


def _fft_b(zr4, zi4, l_b, bn):
    b = zr4.shape[0]
    kb = 8
    blk = pl.BlockSpec((None, kb, bn, F_WIDTH), lambda bb, i: (bb, i, 0, 0))
    out = pl.pallas_call(
        _fft_b_kernel,
        out_shape=jax.ShapeDtypeStruct((b, bn, DFT_A * F_WIDTH), BF16),
        grid=(b, DFT_A // kb),
        in_specs=[blk, blk, pl.BlockSpec(l_b.shape, lambda bb, i: (0, 0))],
        out_specs=pl.BlockSpec((None, bn, kb * F_WIDTH), lambda bb, i: (bb, 0, i)),
        compiler_params=_cparams(("parallel", "parallel")),
        name="fft_stage_b",
    )(zr4, zi4, l_b)
    return out.reshape(b, bn * DFT_A, F_WIDTH)


def _dft_ctx_kernel(fa_ref, fb_ref, l_ref, o_ref):
    rhs = jnp.concatenate(_split3(fa_ref[...]) + _split3(fb_ref[...]), axis=0)
    o_ref[...] = _bdot(l_ref[...], rhs).astype(o_ref.dtype)


def _dft_ctx(fa, fb, l_c, n_lat, n_ctx):
    b = fa.shape[0]
    blk = pl.BlockSpec((None, n_ctx, F_WIDTH), lambda bb: (bb, n_lat // n_ctx, 0))
    return pl.pallas_call(
        _dft_ctx_kernel,
        out_shape=jax.ShapeDtypeStruct((b, n_ctx, F_WIDTH), BF16),
        grid=(b,),
        in_specs=[blk, blk, pl.BlockSpec(l_c.shape, lambda bb: (0, 0))],
        out_specs=pl.BlockSpec((None, n_ctx, F_WIDTH), lambda bb: (bb, 0, 0)),
        compiler_params=_cparams(("parallel",)),
        name="dft_ctx",
    )(fa, fb, l_c)


def _score_slot(j):
    return 2 if j == 0 else (j - 1) % 2


def _attn_kernel(q_ref, kt_ref, v_ref, lam_ref, gain_ref, o_ref, q2_ref, s_ref, m_ref, acc_ref,
                 *, lam_init, tq, tk):
    nsub = q_ref.shape[0] // tq
    nk = kt_ref.shape[1] // tk

    def keys(j):
        return slice(j * tk, (j + 1) * tk)
    lane = lax.broadcasted_iota(jnp.int32, (tq, A_DV), 1)
    comp0 = (lane % A_DH) < (A_DH // 2)
    lp = lam_ref[...]
    lam = (jnp.exp(jnp.sum(lp[0:1] * lp[1:2], keepdims=True))
           - jnp.exp(jnp.sum(lp[2:3] * lp[3:4], keepdims=True)) + lam_init)
    out_gain = gain_ref[...] * (1.0 - lam_init)

    def rows(i):
        return pl.ds(pl.multiple_of(i * tq, tq), tq)

    def load_q2(i):
        q = q_ref[rows(i), :]
        zero = jnp.zeros_like(q)
        q2_ref[0:tq, :] = jnp.where(comp0, q, zero)
        q2_ref[tq:2 * tq, :] = jnp.where(comp0, zero, q)

    def scores(j):
        s_ref[_score_slot(j)] = _bdot(q2_ref[...], kt_ref[:, keys(j)])

    def softmax_pv(j, k, first):
        s = s_ref[_score_slot(j)]
        s_max = jnp.max(s, axis=1, keepdims=True)
        if first:
            m_new = jnp.broadcast_to(s_max, (2 * tq, LANES))
        else:
            m_old = m_ref[k]
            m_new = jnp.maximum(m_old, s_max)
            alpha = jnp.exp2(m_old - m_new)
        p = jnp.exp2(s - jnp.concatenate([m_new] * (tk // LANES), axis=1)).astype(BF16)
        pv = _bdot(p, v_ref[keys(j), :])
        if first:
            acc_ref[k] = pv
        else:
            acc_ref[k] = acc_ref[k] * jnp.concatenate([alpha, alpha], axis=1) + pv
        m_ref[k] = m_new

    def finalize(i, k):
        acc = acc_ref[k]
        o0 = acc[0:tq, 0:LANES] / acc[0:tq, LANES:]
        o1 = acc[tq:, 0:LANES] / acc[tq:, LANES:]
        o = o0 - lam * o1
        ms = jnp.mean(o * o, axis=-1, keepdims=True)
        o_ref[rows(i), :] = (o * lax.rsqrt(ms + RMS_EPS) * out_gain).astype(o_ref.dtype)

    def next_tile_first_scores(i):
        load_q2(jnp.minimum(i + 1, nsub - 1))
        scores(0)

    def tile(i, k, finish_previous):
        for j in range(nk):
            if j + 1 < nk:
                scores(j + 1)
            elif nk > 1:
                next_tile_first_scores(i)
            if j == 0 and finish_previous:
                finalize(jnp.maximum(i - 1, 0), 1 - k)
            softmax_pv(j, k, first=(j == 0))
            if nk == 1:
                next_tile_first_scores(i)

    load_q2(0)
    scores(0)
    if nsub % 2 == 0:
        acc_ref[1] = jnp.ones(acc_ref.shape[1:], F32)

        def pair(ii, carry):
            i = 2 * ii
            tile(i, 0, True)
            tile(i + 1, 1, True)
            return carry

        lax.fori_loop(0, nsub // 2, pair, 0)
        finalize(nsub - 1, 1)
    else:
        def single(i, carry):
            tile(i, 0, False)
            finalize(i, 0)
            return carry

        lax.fori_loop(0, nsub, single, 0)


def _attention(q, kt, v_ext, lam_p, gain, *, q_block0, n_q, key_block0, n_keys, tq, tk, group,
               lam_init):
    b = q.shape[0]
    assert n_q % group == 0 and group % tq == 0 and n_keys % tk == 0
    return pl.pallas_call(
        functools.partial(_attn_kernel, lam_init=lam_init, tq=tq, tk=tk),
        out_shape=jax.ShapeDtypeStruct((b, n_q, A_WIDTH), BF16),
        grid=(b, A_HEADS, n_q // group),
        in_specs=[
            pl.BlockSpec((None, group, A_DV), lambda bb, h, i: (bb, q_block0 + i, h)),
            pl.BlockSpec((None, None, A_DV, n_keys), lambda bb, h, i: (bb, h, 0, key_block0)),
            pl.BlockSpec((None, n_keys, 2 * LANES), lambda bb, h, i: (bb, key_block0, h)),
            pl.BlockSpec((4, A_DH), lambda bb, h, i: (0, 0)),
            pl.BlockSpec((1, A_DV), lambda bb, h, i: (0, 0)),
        ],
        out_specs=pl.BlockSpec((None, group, A_DV), lambda bb, h, i: (bb, i, h)),
        scratch_shapes=[
            pltpu.VMEM((2 * tq, A_DV), BF16),
            pltpu.VMEM((3, 2 * tq, tk), F32),
            pltpu.VMEM((2, 2 * tq, LANES), F32),
            pltpu.VMEM((2, 2 * tq, 2 * LANES), F32),
        ],
        compiler_params=_cparams(("parallel", "parallel", "arbitrary")),
        name="diff_attn",
    )(q, kt, v_ext, lam_p, gain)


_HI16 = 0xFFFF0000


def _pack_bf16_pair(lo, hi):
    ulo = lax.bitcast_convert_type(lo.astype(BF16).astype(F32), jnp.uint32) >> 16
    uhi = lax.bitcast_convert_type(hi.astype(BF16).astype(F32), jnp.uint32) & jnp.uint32(_HI16)
    return ulo | uhi


def _top_half_bits(x):
    return lax.bitcast_convert_type(lax.bitcast_convert_type(x, jnp.uint32) & jnp.uint32(_HI16), F32)


def _unpack_bf16_pair(u):
    lo = lax.bitcast_convert_type(u << 16, F32)
    hi = lax.bitcast_convert_type(u & jnp.uint32(_HI16), F32)
    return lo, hi


ROUTE_CLASS = 0
N_PAIRS = EXPERTS_PER_GROUP * (EXPERTS_PER_GROUP - 1) // 2
N_PAIR_CLASSES = N_EXPERT_GROUPS * N_PAIRS


def _router_gates(sc_t, sel_t, route_ref, pair_gates_ref):
    s_rows = [sc_t[e:e + 1, :] for e in range(N_EXPERTS)]
    v_rows = [sel_t[e:e + 1, :] for e in range(N_EXPERTS)]
    in_top2 = []
    group_score = []
    for g in range(N_EXPERT_GROUPS):
        vs = v_rows[g * EXPERTS_PER_GROUP:(g + 1) * EXPERTS_PER_GROUP]
        tops = []
        for jj in range(EXPERTS_PER_GROUP):
            rank = jnp.zeros_like(vs[jj])
            for ii in range(EXPERTS_PER_GROUP):
                if ii == jj:
                    continue
                beats = (vs[ii] >= vs[jj]) if ii < jj else (vs[ii] > vs[jj])
                rank = rank + jnp.where(beats, 1.0, 0.0)
            tops.append(rank < 2.0)
        in_top2 += tops
        gs = jnp.zeros_like(vs[0])
        for jj in range(EXPERTS_PER_GROUP):
            gs = gs + jnp.where(tops[jj], vs[jj], 0.0)
        group_score.append(gs)
    best = group_score[0]
    gidx = jnp.zeros_like(best)
    for g in range(1, N_EXPERT_GROUPS):
        upd = group_score[g] > best
        best = jnp.where(upd, group_score[g], best)
        gidx = jnp.where(upd, float(g), gidx)
    chosen = [jnp.logical_and(in_top2[e], gidx == float(e // EXPERTS_PER_GROUP))
              for e in range(N_EXPERTS)]
    denom = jnp.zeros_like(best)
    for e in range(N_EXPERTS):
        denom = denom + jnp.where(chosen[e], s_rows[e], 0.0)
    e_lo = jnp.zeros_like(best)
    g_lo = jnp.zeros_like(best)
    for e in reversed(range(N_EXPERTS)):
        e_lo = jnp.where(chosen[e], float(e), e_lo)
        g_lo = jnp.where(chosen[e], s_rows[e] / denom, g_lo)
    e_hi = jnp.zeros_like(best)
    g_hi = jnp.zeros_like(best)
    for e in range(N_EXPERTS):
        e_hi = jnp.where(chosen[e], float(e), e_hi)
        g_hi = jnp.where(chosen[e], s_rows[e] / denom, g_hi)
    a_lo = e_lo - EXPERTS_PER_GROUP * gidx
    a_hi = e_hi - EXPERTS_PER_GROUP * gidx
    pair_rank = a_lo * (2 * EXPERTS_PER_GROUP - 1 - a_lo) * 0.5 + (a_hi - a_lo - 1.0)
    route_ref[...] = jnp.zeros(route_ref.shape, F32)
    route_ref[ROUTE_CLASS:ROUTE_CLASS + 1, :] = gidx * float(N_PAIRS) + pair_rank
    tm = g_lo.shape[1]
    pair_gates_ref[:, :LANES] = jnp.broadcast_to(g_lo, (LANES, tm)).T
    pair_gates_ref[:, LANES:] = jnp.broadcast_to(g_hi, (LANES, tm)).T


def _outproj_kernel(yf_lat_ref, yf_ctx_ref, att_lat_ref, att_ctx_ref, sg_ref, x_lat_ref, x_ctx_ref,
                    mod_ref, wo_ref, lng_ref, lnb_ref, wr_ref, rb_ref, x1_ref, h_ref, route_ref,
                    pair_gates_ref, *, alpha, nlat):
    is_ctx = pl.program_id(1) >= nlat
    x = jnp.where(is_ctx, x_ctx_ref[...], x_lat_ref[...])
    yf = jnp.where(is_ctx, yf_ctx_ref[...], yf_lat_ref[...])
    att = jnp.where(is_ctx, att_ctx_ref[...], att_lat_ref[...])
    mix = _bdot(yf, wo_ref[0:F_WIDTH, :])
    mix = mix + _bdot(att, wo_ref[F_WIDTH:F_WIDTH + A_WIDTH, :])
    mix = mix + _bdot(sg_ref[...], wo_ref[F_WIDTH + A_WIDTH:, :])
    g1 = mod_ref[2:3, :]
    x1 = _layernorm(alpha * x + g1 * mix, lng_ref[...], lnb_ref[...])
    x1_ref[...] = x1
    h = x1 * (1.0 + mod_ref[4:5, :]) + mod_ref[3:4, :]
    half = h.shape[1] // 2
    h_ref[...] = _pack_bf16_pair(h[:, :half], h[:, half:])
    h_top = _top_half_bits(h)
    h_hi = h_top.astype(BF16)
    h_lo = (h - h_top).astype(BF16)
    hw = _bdot(h_hi, wr_ref[...])
    logits = hw[:, :ROUTER_LANES] + (hw[:, ROUTER_LANES:] + _bdot(h_lo, wr_ref[:, :ROUTER_LANES]))
    scores = jax.nn.sigmoid(logits)
    sel = scores + rb_ref[...]
    _router_gates(scores.T[0:N_EXPERTS, :], sel.T[0:N_EXPERTS, :], route_ref, pair_gates_ref)


def _outproj(yf_lat, yf_ctx, att_lat, att_ctx, sg, x_lat, x_ctx, x_ctx_row0, mods_l, wo, lng, lnb, wr, rb,
             n_lat, alpha):
    b, _, d = x_lat.shape
    l = sg.shape[1]
    tm = TOKEN_TILE
    nlat = n_lat // tm
    ctx_tile0 = x_ctx_row0 // tm
    seg = lambda i: jnp.where(i >= nlat, 1, 0)
    row = lambda width: pl.BlockSpec((None, tm, width), lambda bb, i: (bb, i, 0))
    lat = lambda width: pl.BlockSpec((None, tm, width), lambda bb, i: (bb, jnp.minimum(i, nlat - 1), 0))
    ctx = lambda width: pl.BlockSpec((None, tm, width), lambda bb, i: (bb, jnp.maximum(i - nlat, 0), 0))
    const2 = lambda shape: pl.BlockSpec(shape, lambda bb, i: (0, 0))
    return pl.pallas_call(
        functools.partial(_outproj_kernel, alpha=alpha, nlat=nlat),
        out_shape=(
            jax.ShapeDtypeStruct((b, l, d), F32),
            jax.ShapeDtypeStruct((b, l, d // 2), jnp.uint32),
            jax.ShapeDtypeStruct((b, SUBLANES, l), F32),
            jax.ShapeDtypeStruct((b, l, 2 * LANES), F32),
        ),
        grid=(b, l // tm),
        in_specs=[
            lat(F_WIDTH), ctx(F_WIDTH), lat(A_WIDTH), ctx(A_WIDTH), row(G_WIDTH), lat(d),
            pl.BlockSpec((None, tm, d), lambda bb, i: (bb, ctx_tile0 + jnp.maximum(i - nlat, 0), 0)),
            pl.BlockSpec((None, None, MOD_ROWS, d), lambda bb, i: (bb, seg(i), 0, 0)),
            const2((d, d)), const2((1, d)), const2((1, d)),
            const2((d, 2 * ROUTER_LANES)), const2((1, ROUTER_LANES)),
        ],
        out_specs=(row(d), row(d // 2),
                   pl.BlockSpec((None, SUBLANES, tm), lambda bb, i: (bb, 0, i)), row(2 * LANES)),
        compiler_params=_cparams(("parallel", "parallel")),
        name="outproj_ln_router",
    )(yf_lat, yf_ctx, att_lat, att_ctx, sg, x_lat, x_ctx, mods_l, wo, lng, lnb, wr, rb)


def _moe_plan(route, tm):
    b, _, l = route.shape
    t = b * l
    pairs = [(a, c) for a in range(EXPERTS_PER_GROUP) for c in range(a + 1, EXPERTS_PER_GROUP)]
    cls_e1 = np.zeros((N_PAIR_CLASSES,), np.int32)
    cls_e2 = np.zeros((N_PAIR_CLASSES,), np.int32)
    for k, (a, c) in enumerate(pairs):
        assert k == a * (2 * EXPERTS_PER_GROUP - 1 - a) // 2 + (c - a - 1)
        for g in range(N_EXPERT_GROUPS):
            cls_e1[g * N_PAIRS + k] = g * EXPERTS_PER_GROUP + a
            cls_e2[g * N_PAIRS + k] = g * EXPERTS_PER_GROUP + c
    cls = route[:, ROUTE_CLASS, :].astype(jnp.int32).reshape(t)
    onehot = cls[:, None] == jnp.arange(N_PAIR_CLASSES, dtype=jnp.int32)[None, :]
    oh_tiles = onehot.reshape(t // tm, tm, N_PAIR_CLASSES).astype(BF16)
    tri = jnp.tril(jnp.ones((tm, tm), BF16))
    within = jnp.einsum("ij,njc->nic", tri, oh_tiles, preferred_element_type=F32)
    tile_total = within[:, -1, :]
    n_tiles = t // tm
    earlier = jnp.tril(jnp.ones((n_tiles, n_tiles), BF16), k=-1)
    before = jnp.dot(earlier, tile_total.astype(BF16), preferred_element_type=F32)
    csum = (within + before[:, None, :]).reshape(t, N_PAIR_CLASSES).astype(jnp.int32)
    onehot = onehot.astype(jnp.int32)
    rank = jnp.sum(csum * onehot, axis=1) - 1
    padded = (csum[-1] + tm - 1) // tm * tm
    upto = jnp.tril(jnp.ones((N_PAIR_CLASSES, N_PAIR_CLASSES), jnp.int32))
    off_end = jnp.sum(upto * padded[None, :], axis=1)
    dest = jnp.sum(onehot * (off_end - padded)[None, :], axis=1) + rank
    nt = t // tm + N_PAIR_CLASSES
    n_valid = (off_end[-1] // tm).astype(jnp.int32)
    first_row = jnp.minimum(jnp.arange(nt, dtype=jnp.int32), n_valid - 1) * tm
    tile_cls = jnp.sum((off_end[None, :] <= first_row[:, None]).astype(jnp.int32), axis=1)
    tile_cls = jnp.minimum(tile_cls, N_PAIR_CLASSES - 1)
    return (dest.astype(jnp.int32), jnp.asarray(cls_e1)[tile_cls], jnp.asarray(cls_e2)[tile_cls],
            n_valid.reshape(1), nt)


def _dispatch_kernel(dest_ref, h_ref, hs_init_ref, hs_ref, stage, sems):
    del hs_init_ref
    s = pl.program_id(0)
    n = pl.num_programs(0)
    tm = h_ref.shape[0]
    slot = s % 2

    def wait_slot(k):
        pltpu.make_async_copy(stage.at[k], hs_ref.at[pl.ds(0, tm), :], sems.at[k]).wait()

    @pl.when(s >= 2)
    def _reuse():
        wait_slot(slot)

    stage[slot] = h_ref[...]
    for r in range(tm):
        pltpu.make_async_copy(stage.at[slot, pl.ds(r, 1), :],
                              hs_ref.at[pl.ds(dest_ref[0, r], 1), :],
                              sems.at[slot]).start(priority=r % N_DMA_PRIORITIES)

    @pl.when(s == n - 1)
    def _drain():
        @pl.when(n >= 2)
        def _other():
            wait_slot(1 - slot)
        wait_slot(slot)


def _dispatch(h_packed, dest, nt, tm):
    t, w = h_packed.shape
    hs_init = jnp.zeros((nt * tm, w), jnp.uint32)
    return pl.pallas_call(
        _dispatch_kernel,
        out_shape=jax.ShapeDtypeStruct((nt * tm, w), jnp.uint32),
        grid=(t // tm,),
        in_specs=[
            pl.BlockSpec((None, 1, tm), lambda i: (i, 0, 0), memory_space=pltpu.SMEM),
            pl.BlockSpec((tm, w), lambda i: (i, 0)),
            pl.BlockSpec(memory_space=pl.ANY),
        ],
        out_specs=pl.BlockSpec(memory_space=pl.ANY),
        scratch_shapes=[pltpu.VMEM((2, tm, w), jnp.uint32), pltpu.SemaphoreType.DMA((2,))],
        input_output_aliases={2: 0},
        compiler_params=_cparams(("arbitrary",)),
        name="moe_dispatch",
    )(dest.reshape(t // tm, 1, tm), h_packed, hs_init)


def _expert_pair_kernel(e1_ref, e2_ref, nv_ref, hs_ref, w13a_ref, w2a_ref, w13b_ref, w2b_ref, y_ref):
    del e1_ref, e2_ref
    i = pl.program_id(0)

    @pl.when(i < nv_ref[0])
    def _compute():
        lo, hi = _unpack_bf16_pair(hs_ref[...])
        h = jnp.concatenate([lo, hi], axis=1).astype(BF16)

        def ffn(w13_ref, w2_ref):
            gu = _bdot(h, w13_ref[...])
            de = gu.shape[1] // 2
            g = gu[:, :de]
            a = (g * jax.nn.sigmoid(g)) * gu[:, de:]
            return _bdot(a.astype(BF16), w2_ref[...])

        y_ref[...] = _pack_bf16_pair(ffn(w13a_ref, w2a_ref), ffn(w13b_ref, w2b_ref))

    @pl.when(i >= nv_ref[0])
    def _unused_tile():
        y_ref[...] = jnp.zeros(y_ref.shape, jnp.uint32)


def _expert_pairs(hs, tile_e1, tile_e2, n_valid, w13, w2, tm):
    rows, w = hs.shape
    _, de, d = w2.shape
    first = lambda shape: pl.BlockSpec(shape, lambda i, e1, e2, nv: (e1[i], 0, 0))
    second = lambda shape: pl.BlockSpec(shape, lambda i, e1, e2, nv: (e2[i], 0, 0))
    grid_spec = pltpu.PrefetchScalarGridSpec(
        num_scalar_prefetch=3,
        grid=(rows // tm,),
        in_specs=[
            pl.BlockSpec((tm, w), lambda i, e1, e2, nv: (jnp.minimum(i, nv[0] - 1), 0)),
            first((None, d, 2 * de)), first((None, de, d)),
            second((None, d, 2 * de)), second((None, de, d)),
        ],
        out_specs=pl.BlockSpec((tm, d), lambda i, e1, e2, nv: (i, 0)),
    )
    return pl.pallas_call(
        _expert_pair_kernel,
        out_shape=jax.ShapeDtypeStruct((rows, d), jnp.uint32),
        grid_spec=grid_spec,
        compiler_params=_cparams(("arbitrary",)),
        name="moe_expert_pairs",
    )(tile_e1, tile_e2, n_valid, hs, w13, w2, w13, w2)


def _combine_kernel(dest_ref, dest_next_ref, y2_ref, x_ref, pg_ref, mod_ref, lng_ref, lnb_ref,
                    o_ref, ybuf, sems, *, alpha):
    o_ref[...] = _combine_body(dest_ref, dest_next_ref, y2_ref, x_ref, pg_ref, mod_ref, lng_ref,
                               lnb_ref, ybuf, sems, alpha=alpha)


def _combine_inproj_kernel(dest_ref, dest_next_ref, y2_ref, x_ref, pg_ref, mod_ref, lng_ref, lnb_ref,
                           *refs, alpha, q_scale):
    n_inproj_in = 8
    inproj_in = refs[:n_inproj_in]
    o_ref = refs[n_inproj_in]
    inproj_out = refs[n_inproj_in + 1:n_inproj_in + 7]
    ybuf, sems = refs[n_inproj_in + 7:]
    x_new = _combine_body(dest_ref, dest_next_ref, y2_ref, x_ref, pg_ref, mod_ref, lng_ref, lnb_ref,
                          ybuf, sems, alpha=alpha)
    o_ref[...] = x_new
    _inproj_body(x_new, *inproj_in, *inproj_out, q_scale=q_scale)


def _combine_body(dest_ref, dest_next_ref, y2_ref, x_ref, pg_ref, mod_ref, lng_ref, lnb_ref,
                  ybuf, sems, *, alpha):
    s = pl.program_id(0)
    n = pl.num_programs(0)
    tm, d = x_ref.shape
    slot = s % 2

    def start_gather(idx_ref, k):
        for r in range(tm):
            pltpu.make_async_copy(y2_ref.at[pl.ds(idx_ref[0, r], 1), :],
                                  ybuf.at[k, pl.ds(r, 1), :],
                                  sems.at[k]).start(priority=r % N_DMA_PRIORITIES)

    @pl.when(s == 0)
    def _first():
        start_gather(dest_ref, 0)

    @pl.when(s + 1 < n)
    def _prefetch():
        start_gather(dest_next_ref, 1 - slot)

    pltpu.make_async_copy(y2_ref.at[pl.ds(0, tm), :], ybuf.at[slot], sems.at[slot]).wait()
    y_lo, y_hi = _unpack_bf16_pair(ybuf[slot])
    pg = pg_ref[...]
    reps = d // LANES
    g_lo = jnp.concatenate([pg[:, :LANES]] * reps, axis=1)
    g_hi = jnp.concatenate([pg[:, LANES:]] * reps, axis=1)
    y = g_lo * y_lo + g_hi * y_hi
    return _layernorm(alpha * x_ref[...] + mod_ref[5:6, :] * y, lng_ref[...], lnb_ref[...])


def _combine(y2, dest, x1, pair_gates, mods_l, lng, lnb, *, n_rows, n_lat, tm, alpha, next_inproj=None):
    b, l, d = x1.shape
    nlat = n_lat // tm
    tiles_per_batch = l // tm
    per_batch = n_rows // tm
    n_steps = b * per_batch
    batch = lambda s: s // per_batch
    tile = lambda s: s % per_batch
    seg = lambda s: jnp.where(tile(s) >= nlat, 1, 0)
    token_tile = lambda s: batch(s) * tiles_per_batch + tile(s)
    row = lambda width: pl.BlockSpec((None, tm, width), lambda s: (batch(s), tile(s), 0))
    const2 = lambda shape: pl.BlockSpec(shape, lambda s: (0, 0))
    mod_spec = pl.BlockSpec((None, None, MOD_ROWS, d), lambda s: (batch(s), seg(s), 0, 0))
    dest3 = dest.reshape(b * tiles_per_batch, 1, tm)
    in_specs = [
        pl.BlockSpec((None, 1, tm), lambda s: (token_tile(s), 0, 0), memory_space=pltpu.SMEM),
        pl.BlockSpec((None, 1, tm), lambda s: (token_tile(jnp.minimum(s + 1, n_steps - 1)), 0, 0),
                     memory_space=pltpu.SMEM),
        pl.BlockSpec(memory_space=pl.ANY),
        row(d), row(2 * LANES), mod_spec, const2((1, d)), const2((1, d)),
    ]
    args = [dest3, dest3, y2, x1, pair_gates, mods_l, lng, lnb]
    out_shape = [jax.ShapeDtypeStruct((b, n_rows, d), F32)]
    out_specs = [row(d)]
    if next_inproj is None:
        body = functools.partial(_combine_kernel, alpha=alpha)
        name = "moe_combine_ln"
    else:
        assert n_rows == l and tm == TOKEN_TILE
        mods_n, w_big, cosf, sinf, slng, slnb, ws, bs_full, q_scale = next_inproj
        in_specs += [
            mod_spec, const2((d, BIG_COLS)),
            pl.BlockSpec((tm, LANES), lambda s: (tile(s), 0)),
            pl.BlockSpec((tm, LANES), lambda s: (tile(s), 0)),
            const2((1, G_WIDTH)), const2((1, G_WIDTH)),
            pl.BlockSpec((G_GROUPS, CHUNK, CHUNK), lambda s: (0, 0, 0)),
            const2((CHUNK, G_WIDTH)),
        ]
        args += [mods_n, w_big, cosf, sinf, slng, slnb, ws, bs_full]
        out_shape += list(_inproj_out_shapes(b, l))
        out_specs += [row(F_WIDTH), row(F_WIDTH), row(A_QK_WIDTH),
                      pl.BlockSpec((None, A_HEADS, A_DV, tm), lambda s: (batch(s), 0, 0, tile(s))),
                      row(2 * A_WIDTH), row(G_WIDTH)]
        body = functools.partial(_combine_inproj_kernel, alpha=alpha, q_scale=q_scale)
        name = "moe_combine_ln_inproj"
    outs = pl.pallas_call(
        body,
        out_shape=tuple(out_shape),
        grid=(n_steps,),
        in_specs=in_specs,
        out_specs=tuple(out_specs),
        scratch_shapes=[pltpu.VMEM((2, tm, d), jnp.uint32), pltpu.SemaphoreType.DMA((2,))],
        compiler_params=_cparams(("arbitrary",)),
        name=name,
    )(*args)
    return outs[0] if next_inproj is None else outs


def _head_lane_fields():
    j = jnp.arange(LANES)
    half = j // (LANES // 2)
    comp = (j % (LANES // 2)) // (A_DH // 2)
    axis = (j % (A_DH // 2)) // (A_DH // 4)
    freq = j % (A_DH // 4)
    return half, comp, axis, freq


def _qk_column_perm():
    half, comp, axis, freq = _head_lane_fields()
    orig = comp * A_DH + axis * (A_DH // 2) + half * (A_DH // 4) + freq
    return (jnp.arange(A_HEADS)[:, None] * LANES + orig[None, :]).reshape(-1)


def _rope_tables(n_lat, n_ctx):
    half, _, axis, freq = _head_lane_fields()
    rows = n_lat // GRID_W
    row = jnp.repeat(jnp.arange(rows, dtype=F32), GRID_W)
    col = jnp.tile(jnp.arange(GRID_W, dtype=F32), rows)
    hd = A_DH // 2
    inv = ROPE_BASE ** (-jnp.arange(0, hd, 2, dtype=F32) / hd)
    pos = jnp.where(axis[None, :] == 0, row[:, None], col[:, None])
    ang = pos * inv[freq][None, :]
    cosf = jnp.cos(ang)
    sinf = jnp.sin(ang) * jnp.where(half == 0, -1.0, 1.0)[None, :]
    cosf = jnp.concatenate([cosf, jnp.ones((n_ctx, LANES), F32)], axis=0)
    sinf = jnp.concatenate([sinf, jnp.zeros((n_ctx, LANES), F32)], axis=0)
    return cosf, sinf


def _dft_mats(n, scale=1.0):
    i = jnp.arange(n, dtype=jnp.int32)
    ang = ((i[:, None] * i[None, :]) % n).astype(F32) * (2.0 * math.pi / n)
    return jnp.cos(ang) * scale, jnp.sin(ang) * scale


def _twiddles(n, bn):
    ka = jnp.arange(DFT_A, dtype=jnp.int32)
    bb = jnp.arange(bn, dtype=jnp.int32)
    ang = ((ka[:, None] * bb[None, :]) % n).astype(F32) * (2.0 * math.pi / n)
    shape = (DFT_A, bn, F_WIDTH)
    twc = jnp.broadcast_to(jnp.cos(ang)[:, :, None], shape).reshape(DFT_A, bn * F_WIDTH)
    tws = jnp.broadcast_to(jnp.sin(ang)[:, :, None], shape).reshape(DFT_A, bn * F_WIDTH)
    return twc, tws


def _block_diag(blocks):
    g, n = blocks.shape[-3], blocks.shape[-1]
    eye = jnp.eye(g, dtype=blocks.dtype)
    out = blocks[..., :, :, None, :] * eye[:, None, :, None]
    return out.reshape(blocks.shape[:-3] + (g * n, g * n))


def kernel(x, c, ctx, c_ctx, w_ada, b_ada, w_in, w_fourier, diff_lambda, diff_subln,
           sgu_ln_g, sgu_ln_b, sgu_w, sgu_b, w_out, ln_g, ln_b, w_router, router_bias,
           moe_w1, moe_w3, moe_w2):
    b, n, d = x.shape
    n_ctx = ctx.shape[1]
    depth = w_ada.shape[0]
    l = n + n_ctx
    alpha = (2 * depth) ** 0.25
    bn = n // DFT_A
    assert n % DFT_A == 0 and bn % SUBLANES == 0 and n % GRID_W == 0
    assert n_ctx % TOKEN_TILE == 0 and n % TOKEN_TILE == 0 and n % n_ctx == 0
    assert b + 1 <= MOD_ROWS

    c_rows = jnp.concatenate([c, c_ctx[None, :], jnp.zeros((MOD_ROWS - b - 1, d), F32)], axis=0)
    mod = _ada(c_rows, w_ada, b_ada).reshape(depth, MOD_ROWS, 6, d)
    lat = mod[:, :b]
    cm = jnp.broadcast_to(mod[:, b:b + 1], lat.shape)
    mods = jnp.stack([lat, cm], axis=2)
    mods = jnp.pad(mods, ((0, 0), (0, 0), (0, 0), (0, MOD_ROWS - 6), (0, 0)))

    perm = _qk_column_perm()
    c64, s64 = _dft_mats(F_DIM)
    eye_g = jnp.eye(F_GROUPS, dtype=F32)
    bdc = jnp.kron(eye_g, c64)
    bds = jnp.kron(eye_g, s64)
    bdw = _block_diag(w_fourier)
    w_fab = _fourier_weights(w_in[:, :, :F_WIDTH], bdc, bds, bdw)
    o_q = F_WIDTH
    o_k = o_q + A_QK_WIDTH
    o_v = o_k + A_QK_WIDTH
    o_g = o_v + A_WIDTH
    w_big = jnp.concatenate([
        w_fab,
        w_in[:, :, o_q:o_k][:, :, perm],
        w_in[:, :, o_k:o_v][:, :, perm],
        w_in[:, :, o_v:],
    ], axis=-1).astype(BF16)
    wo = w_out.astype(BF16)
    w13 = jnp.concatenate([moe_w1, moe_w3], axis=-1).astype(BF16)
    w2 = moe_w2.astype(BF16)
    ws = sgu_w.astype(BF16)
    bs_full = jnp.repeat(jnp.swapaxes(sgu_b, 1, 2), G_DIM, axis=2)
    wr_f32 = jnp.pad(w_router, ((0, 0), (0, ROUTER_LANES - N_EXPERTS)))
    wr_top = _top_half_bits(wr_f32)
    wr = jnp.concatenate([wr_top.astype(BF16), (wr_f32 - wr_top).astype(BF16)], axis=1)
    rb =jnp.pad(router_bias, (0, ROUTER_LANES - N_EXPERTS)).reshape(1, ROUTER_LANES)

    cosf, sinf = _rope_tables(n, n_ctx)
    ca, sa = _dft_mats(DFT_A)
    l_a = jnp.concatenate([_stack3([ca, -sa]), _stack3([-sa, -ca])], axis=0)
    twc, tws = _twiddles(n, bn)
    l_b = _stack3(list(_dft_mats(bn, scale=(n * F_DIM) ** -0.5)))
    cc, sc = _dft_mats(n_ctx, scale=(n_ctx * F_DIM) ** -0.5)
    l_c = _stack3([cc, -sc])
    q_scale = (A_DH ** -0.5) * math.log2(math.e)

    tk = ATTN_TK
    assert l % tk == 0
    assert n % MOE_TILE == 0 and n_ctx % MOE_TILE == 0

    def inproj_params(li):
        return (mods[li], w_big[li], cosf, sinf, sgu_ln_g[li][None], sgu_ln_b[li][None],
                ws[li], bs_full[li])

    fa, fb, q, kt, v_ext, sg = _inproj(x, ctx, *inproj_params(0), q_scale)
    stream = (x, ctx, 0)
    for li in range(depth):
        last = li == depth - 1
        lam_init = 0.8 - 0.6 * math.exp(-0.3 * li)
        zr, zi = _fft_a(fa, fb, l_a, twc, tws, bn)
        yf_x = _fft_b(zr, zi, l_b, bn)
        yf_c = _dft_ctx(fa, fb, l_c, n, n_ctx)
        gain = diff_subln[li][None]
        att_x = _attention(q, kt, v_ext, diff_lambda[li], gain, q_block0=0, n_q=n, key_block0=0,
                           n_keys=l, tq=ATTN_TQ, tk=tk, group=min(ATTN_GROUP, n), lam_init=lam_init)
        att_c = _attention(q, kt, v_ext, diff_lambda[li], gain, q_block0=n // n_ctx, n_q=n_ctx,
                           key_block0=n // n_ctx, n_keys=n_ctx, tq=n_ctx, tk=n_ctx, group=n_ctx,
                           lam_init=lam_init)
        x1, h_packed, route, pair_gates = _outproj(
            yf_x, yf_c, att_x, att_c, sg, *stream, mods[li], wo[li], ln_g[li, 0][None], ln_b[li, 0][None], wr, rb, n, alpha)
        dest, tile_e1, tile_e2, n_valid, nt = _moe_plan(route, MOE_TILE)
        hs = _dispatch(h_packed.reshape(b * l, d // 2), dest, nt, MOE_TILE)
        y2 = _expert_pairs(hs, tile_e1, tile_e2, n_valid, w13[li], w2[li], MOE_TILE)
        combine_args = (y2, dest, x1, pair_gates, mods[li], ln_g[li, 1][None], ln_b[li, 1][None])
        if last:
            xa = _combine(*combine_args, n_rows=n, n_lat=n, tm=MOE_TILE, alpha=alpha)
        else:
            xa, fa, fb, q, kt, v_ext, sg = _combine(
                *combine_args, n_rows=l, n_lat=n, tm=MOE_TILE, alpha=alpha,
                next_inproj=inproj_params(li + 1) + (q_scale,))
            stream = (xa, xa, n)
    return xa
```

```python
import functools
import math

import numpy as np
import jax
import jax.numpy as jnp
from jax import lax
from jax.experimental import pallas as pl
from jax.experimental.pallas import tpu as pltpu

F32 = jnp.float32
BF16 = jnp.bfloat16
HIGHEST = lax.Precision.HIGHEST

GRID_W = 64
F_GROUPS, F_DIM = 4, 64
F_WIDTH = F_GROUPS * F_DIM
A_HEADS, A_DH = 4, 64
A_DV = 2 * A_DH
A_QK_WIDTH = A_HEADS * 2 * A_DH
A_WIDTH = A_HEADS * A_DV
G_GROUPS, G_DIM = 4, 64
G_WIDTH = G_GROUPS * G_DIM
CHUNK = 128
ROPE_BASE = 10000.0
N_EXPERTS = 16
N_EXPERT_GROUPS = 4
EXPERTS_PER_GROUP = N_EXPERTS // N_EXPERT_GROUPS
LN_EPS = 1e-5
RMS_EPS = 1e-5

COL_FA = 0
COL_FB = COL_FA + F_WIDTH
COL_Q = COL_FB + F_WIDTH
COL_K = COL_Q + A_QK_WIDTH
COL_V = COL_K + A_QK_WIDTH
COL_G = COL_V + A_WIDTH
BIG_COLS = COL_G + 2 * G_WIDTH

LANES = 128
SUBLANES = 8
MOD_ROWS = 8
ROUTER_LANES = 128
DFT_A = 128
VMEM_LIMIT = 56 * 1024 * 1024

TOKEN_TILE = 256
ATTN_TQ = 256
ATTN_GROUP = 8192
ATTN_TK = 1280
MOE_TILE = 256


def _cparams(sem):
    return pltpu.CompilerParams(dimension_semantics=sem, vmem_limit_bytes=VMEM_LIMIT)


def _hdot(a, b):
    return jnp.dot(a, b, precision=HIGHEST, preferred_element_type=F32)


def _bdot(a, b):
    return jnp.dot(a, b, preferred_element_type=F32)


def _layernorm(y, g, b):
    mu = jnp.mean(y, axis=-1, keepdims=True)
    d = y - mu
    var = jnp.mean(d * d, axis=-1, keepdims=True)
    return d * lax.rsqrt(var + LN_EPS) * g + b


def _ada_kernel(c_ref, w_ref, b_ref, o_ref):
    c = c_ref[...]
    a = c * jax.nn.sigmoid(c)
    o_ref[...] = _hdot(a, w_ref[...]) + b_ref[...]


def _ada(c_rows, w_ada, b_ada):
    depth, d, cols = w_ada.shape
    tn = 1536
    assert cols % tn == 0
    return pl.pallas_call(
        _ada_kernel,
        out_shape=jax.ShapeDtypeStruct((depth, MOD_ROWS, cols), F32),
        grid=(depth, cols // tn),
        in_specs=[
            pl.BlockSpec((MOD_ROWS, d), lambda l, j: (0, 0)),
            pl.BlockSpec((None, d, tn), lambda l, j: (l, 0, j)),
            pl.BlockSpec((None, 1, tn), lambda l, j: (l, 0, j)),
        ],
        out_specs=pl.BlockSpec((None, MOD_ROWS, tn), lambda l, j: (l, 0, j)),
        compiler_params=_cparams(("parallel", "parallel")),
        name="ada_mod",
    )(c_rows, w_ada, b_ada.reshape(depth, 1, cols))


def _fw_kernel(wf_ref, bdc_ref, bds_ref, bdw_ref, o_ref):
    bdw = bdw_ref[...]
    mc = _hdot(bdc_ref[...], bdw)
    ms = _hdot(bds_ref[...], bdw)
    wf = wf_ref[...]
    o_ref[:, :F_WIDTH] = _hdot(wf, mc)
    o_ref[:, F_WIDTH:] = _hdot(wf, ms)


def _fourier_weights(wf, bdc, bds, bdw):
    depth, d, _ = wf.shape
    return pl.pallas_call(
        _fw_kernel,
        out_shape=jax.ShapeDtypeStruct((depth, d, 2 * F_WIDTH), F32),
        grid=(depth,),
        in_specs=[
            pl.BlockSpec((None, d, F_WIDTH), lambda l: (l, 0, 0)),
            pl.BlockSpec((F_WIDTH, F_WIDTH), lambda l: (0, 0)),
            pl.BlockSpec((F_WIDTH, F_WIDTH), lambda l: (0, 0)),
            pl.BlockSpec((None, F_WIDTH, F_WIDTH), lambda l: (l, 0, 0)),
        ],
        out_specs=pl.BlockSpec((None, d, 2 * F_WIDTH), lambda l: (l, 0, 0)),
        compiler_params=_cparams(("parallel",)),
        name="fourier_weights",
    )(wf, bdc, bds, bdw)


def _gelu_tanh(x):
    c = math.sqrt(2.0 / math.pi)
    return x * (0.5 * (1.0 + jnp.tanh(c * (x + 0.044715 * (x * x * x)))))


def _inproj_kernel(x_lat_ref, x_ctx_ref, *refs, q_scale, nlat):
    x = jnp.where(pl.program_id(1) >= nlat, x_ctx_ref[...], x_lat_ref[...])
    _inproj_body(x, *refs, q_scale=q_scale)


def _inproj_body(x, mod_ref, w_ref, cos_ref, sin_ref, lng_ref, lnb_ref, ws_ref, bs_ref,
                 fa_ref, fb_ref, q_ref, kt_ref, v_ref, sg_ref, *, q_scale):
    sh = mod_ref[0:1, :]
    sc = mod_ref[1:2, :]
    h = (x * (1.0 + sc) + sh).astype(BF16)
    p = _bdot(h, w_ref[...])
    fa_ref[...] = p[:, COL_FA:COL_FA + F_WIDTH]
    fb_ref[...] = p[:, COL_FB:COL_FB + F_WIDTH]
    cosf = cos_ref[...]
    sinf = sin_ref[...]
    for hh in range(A_HEADS):
        t = p[:, COL_Q + LANES * hh:COL_Q + LANES * (hh + 1)]
        r = t * cosf + pltpu.roll(t, LANES // 2, 1) * sinf
        q_ref[:, LANES * hh:LANES * (hh + 1)] = (r * q_scale).astype(BF16)
        t = p[:, COL_K + LANES * hh:COL_K + LANES * (hh + 1)]
        r = t * cosf + pltpu.roll(t, LANES // 2, 1) * sinf
        kt_ref[hh] = r.T.astype(BF16)
        v_ref[:, 2 * LANES * hh:2 * LANES * hh + LANES] = (
            p[:, COL_V + LANES * hh:COL_V + LANES * (hh + 1)].astype(BF16))
        v_ref[:, 2 * LANES * hh + LANES:2 * LANES * (hh + 1)] = jnp.ones((t.shape[0], LANES), BF16)

    g = _gelu_tanh(p[:, COL_G:COL_G + 2 * G_WIDTH])
    u = g[:, :G_WIDTH]
    vn = _layernorm(g[:, G_WIDTH:], lng_ref[...], lnb_ref[...])
    lane_group = lax.broadcasted_iota(jnp.int32, (CHUNK, G_WIDTH), 1) // G_DIM
    tm = x.shape[0]
    for c in range(tm // CHUNK):
        rows = slice(c * CHUNK, (c + 1) * CHUNK)
        vc = vn[rows]
        mixed = bs_ref[...]
        for gi in range(G_GROUPS):
            vm = jnp.where(lane_group == gi, vc, 0.0).astype(BF16)
            mixed = mixed + _bdot(ws_ref[gi], vm)
        sg_ref[rows, :] = (u[rows] * mixed).astype(BF16)


def _inproj_out_shapes(b, l):
    return (
        jax.ShapeDtypeStruct((b, l, F_WIDTH), F32),
        jax.ShapeDtypeStruct((b, l, F_WIDTH), F32),
        jax.ShapeDtypeStruct((b, l, A_QK_WIDTH), BF16),
        jax.ShapeDtypeStruct((b, A_HEADS, A_DV, l), BF16),
        jax.ShapeDtypeStruct((b, l, 2 * A_WIDTH), BF16),
        jax.ShapeDtypeStruct((b, l, G_WIDTH), BF16),
    )


def _inproj(x_lat, x_ctx, mods_l, w_big, cosf, sinf, lng, lnb, ws, bs_full, q_scale):
    b, n_lat, d = x_lat.shape
    l = n_lat + x_ctx.shape[1]
    tm = TOKEN_TILE
    nlat = n_lat // tm
    seg = lambda i: jnp.where(i >= nlat, 1, 0)
    row = lambda width: pl.BlockSpec((None, tm, width), lambda bb, i: (bb, i, 0))
    const2 = lambda shape: pl.BlockSpec(shape, lambda bb, i: (0, 0))
    return pl.pallas_call(
        functools.partial(_inproj_kernel, q_scale=q_scale, nlat=nlat),
        out_shape=_inproj_out_shapes(b, l),
        grid=(b, l // tm),
        in_specs=[
            pl.BlockSpec((None, tm, d), lambda bb, i: (bb, jnp.minimum(i, nlat - 1), 0)),
            pl.BlockSpec((None, tm, d), lambda bb, i: (bb, jnp.maximum(i - nlat, 0), 0)),
            pl.BlockSpec((None, None, MOD_ROWS, d), lambda bb, i: (bb, seg(i), 0, 0)),
            const2((d, BIG_COLS)),
            pl.BlockSpec((tm, LANES), lambda bb, i: (i, 0)),
            pl.BlockSpec((tm, LANES), lambda bb, i: (i, 0)),
            const2((1, G_WIDTH)),
            const2((1, G_WIDTH)),
            pl.BlockSpec((G_GROUPS, CHUNK, CHUNK), lambda bb, i: (0, 0, 0)),
            const2((CHUNK, G_WIDTH)),
        ],
        out_specs=(row(F_WIDTH), row(F_WIDTH), row(A_QK_WIDTH),
                   pl.BlockSpec((None, A_HEADS, A_DV, tm), lambda bb, i: (bb, 0, 0, i)),
                   row(2 * A_WIDTH), row(G_WIDTH)),
        compiler_params=_cparams(("parallel", "parallel")),
        name="inproj",
    )(x_lat, x_ctx, mods_l, w_big, cosf, sinf, lng, lnb, ws, bs_full)


def _split3(x):
    top = _top_half_bits(x)
    hi = top.astype(BF16)
    return [hi, (x - top).astype(BF16), hi]


def _stack3(mats):
    cols = []
    for m in mats:
        top = _top_half_bits(m)
        cols += [top.astype(BF16), top.astype(BF16), (m - top).astype(BF16)]
    return jnp.concatenate(cols, axis=1)


def _fft_a_kernel(fa_ref, fb_ref, l_ref, tc_ref, ts_ref, zr_ref, zi_ref):
    rhs = jnp.concatenate(_split3(fa_ref[...]) + _split3(fb_ref[...]), axis=0)
    z = _bdot(l_ref[...], rhs)
    zr = z[:DFT_A]
    zi = z[DFT_A:]
    tc = tc_ref[...]
    ts = ts_ref[...]
    zr_ref[...] = zr * tc + zi * ts
    zi_ref[...] = zi * tc - zr * ts


def _fft_a(fa, fb, l_a, twc, tws, bn):
    b, l, _ = fa.shape
    cols = bn * F_WIDTH
    fa3 = fa.reshape(b, l // bn, cols)
    fb3 = fb.reshape(b, l // bn, cols)
    tcw = min(cols, 2048)
    assert cols % tcw == 0
    blk = pl.BlockSpec((None, DFT_A, tcw), lambda bb, j: (bb, 0, j))
    tw = pl.BlockSpec((DFT_A, tcw), lambda bb, j: (0, j))
    zr, zi = pl.pallas_call(
        _fft_a_kernel,
        out_shape=(jax.ShapeDtypeStruct((b, DFT_A, cols), F32),) * 2,
        grid=(b, cols // tcw),
        in_specs=[blk, blk, pl.BlockSpec(l_a.shape, lambda bb, j: (0, 0)), tw, tw],
        out_specs=(blk, blk),
        compiler_params=_cparams(("parallel", "parallel")),
        name="fft_stage_a",
    )(fa3, fb3, l_a, twc, tws)
    return zr.reshape(b, DFT_A, bn, F_WIDTH), zi.reshape(b, DFT_A, bn, F_WIDTH)


def _fft_b_kernel(zr_ref, zi_ref, l_ref, o_ref):
    lm = l_ref[...]
    for j in range(zr_ref.shape[0]):
        rhs = jnp.concatenate(_split3(zr_ref[j]) + _split3(zi_ref[j]), axis=0)
        o_ref[:, F_WIDTH * j:F_WIDTH * (j + 1)] = _bdot(lm, rhs).astype(o_ref.dtype)


def _fft_b(zr4, zi4, l_b, bn):
    b = zr4.shape[0]
    kb = 8
    blk = pl.BlockSpec((None, kb, bn, F_WIDTH), lambda bb, i: (bb, i, 0, 0))
    out = pl.pallas_call(
        _fft_b_kernel,
        out_shape=jax.ShapeDtypeStruct((b, bn, DFT_A * F_WIDTH), BF16),
        grid=(b, DFT_A // kb),
        in_specs=[blk, blk, pl.BlockSpec(l_b.shape, lambda bb, i: (0, 0))],
        out_specs=pl.BlockSpec((None, bn, kb * F_WIDTH), lambda bb, i: (bb, 0, i)),
        compiler_params=_cparams(("parallel", "parallel")),
        name="fft_stage_b",
    )(zr4, zi4, l_b)
    return out.reshape(b, bn * DFT_A, F_WIDTH)


def _dft_ctx_kernel(fa_ref, fb_ref, l_ref, o_ref):
    rhs = jnp.concatenate(_split3(fa_ref[...]) + _split3(fb_ref[...]), axis=0)
    o_ref[...] = _bdot(l_ref[...], rhs).astype(o_ref.dtype)


def _dft_ctx(fa, fb, l_c, n_lat, n_ctx):
    b = fa.shape[0]
    blk = pl.BlockSpec((None, n_ctx, F_WIDTH), lambda bb: (bb, n_lat // n_ctx, 0))
    return pl.pallas_call(
        _dft_ctx_kernel,
        out_shape=jax.ShapeDtypeStruct((b, n_ctx, F_WIDTH), BF16),
        grid=(b,),
        in_specs=[blk, blk, pl.BlockSpec(l_c.shape, lambda bb: (0, 0))],
        out_specs=pl.BlockSpec((None, n_ctx, F_WIDTH), lambda bb: (bb, 0, 0)),
        compiler_params=_cparams(("parallel",)),
        name="dft_ctx",
    )(fa, fb, l_c)


def _score_slot(j):
    return 2 if j == 0 else (j - 1) % 2


def _attn_kernel(q_ref, kt_ref, v_ref, lam_ref, gain_ref, o_ref, q2_ref, s_ref, m_ref, acc_ref,
                 *, lam_init, tq, tk):
    nsub = q_ref.shape[0] // tq
    nk = kt_ref.shape[1] // tk

    def keys(j):
        return slice(j * tk, (j + 1) * tk)
    lane = lax.broadcasted_iota(jnp.int32, (tq, A_DV), 1)
    comp0 = (lane % A_DH) < (A_DH // 2)
    lp = lam_ref[...]
    lam = (jnp.exp(jnp.sum(lp[0:1] * lp[1:2], keepdims=True))
           - jnp.exp(jnp.sum(lp[2:3] * lp[3:4], keepdims=True)) + lam_init)
    out_gain = gain_ref[...] * (1.0 - lam_init)

    def rows(i):
        return pl.ds(pl.multiple_of(i * tq, tq), tq)

    def load_q2(i):
        q = q_ref[rows(i), :]
        zero = jnp.zeros_like(q)
        q2_ref[0:tq, :] = jnp.where(comp0, q, zero)
        q2_ref[tq:2 * tq, :] = jnp.where(comp0, zero, q)

    def scores(j):
        s_ref[_score_slot(j)] = _bdot(q2_ref[...], kt_ref[:, keys(j)])

    def softmax_pv(j, k, first):
        for half in range(2):
            rws = slice(half * tq, (half + 1) * tq)
            s = s_ref[_score_slot(j), rws, :]
            s_max = jnp.max(s, axis=1, keepdims=True)
            if first:
                m_new = jnp.broadcast_to(s_max, (tq, LANES))
            else:
                m_old = m_ref[k, rws, :]
                m_new = jnp.maximum(m_old, s_max)
                alpha = jnp.exp2(m_old - m_new)
            p = jnp.exp2(s - jnp.concatenate([m_new] * (tk // LANES), axis=1)).astype(BF16)
            pv = _bdot(p, v_ref[keys(j), :])
            if first:
                acc_ref[k, rws, :] = pv
            else:
                acc_ref[k, rws, :] = acc_ref[k, rws, :] * jnp.concatenate([alpha, alpha], axis=1) + pv
            m_ref[k, rws, :] = m_new

    def finalize(i, k):
        acc = acc_ref[k]
        o0 = acc[0:tq, 0:LANES] / acc[0:tq, LANES:]
        o1 = acc[tq:, 0:LANES] / acc[tq:, LANES:]
        o = o0 - lam * o1
        ms = jnp.mean(o * o, axis=-1, keepdims=True)
        o_ref[rows(i), :] = (o * lax.rsqrt(ms + RMS_EPS) * out_gain).astype(o_ref.dtype)

    def next_tile_first_scores(i):
        load_q2(jnp.minimum(i + 1, nsub - 1))
        scores(0)

    def tile(i, k, finish_previous):
        for j in range(nk):
            if j + 1 < nk:
                scores(j + 1)
            elif nk > 1:
                next_tile_first_scores(i)
            if j == 0 and finish_previous:
                finalize(jnp.maximum(i - 1, 0), 1 - k)
            softmax_pv(j, k, first=(j == 0))
            if nk == 1:
                next_tile_first_scores(i)

    load_q2(0)
    scores(0)
    if nsub % 2 == 0:
        acc_ref[1] = jnp.ones(acc_ref.shape[1:], F32)

        def pair(ii, carry):
            i = 2 * ii
            tile(i, 0, True)
            tile(i + 1, 1, True)
            return carry

        lax.fori_loop(0, nsub // 2, pair, 0)
        finalize(nsub - 1, 1)
    else:
        def single(i, carry):
            tile(i, 0, False)
            finalize(i, 0)
            return carry

        lax.fori_loop(0, nsub, single, 0)


def _attention(q, kt, v_ext, lam_p, gain, *, q_block0, n_q, key_block0, n_keys, tq, tk, group,
               lam_init):
    b = q.shape[0]
    assert n_q % group == 0 and group % tq == 0 and n_keys % tk == 0
    return pl.pallas_call(
        functools.partial(_attn_kernel, lam_init=lam_init, tq=tq, tk=tk),
        out_shape=jax.ShapeDtypeStruct((b, n_q, A_WIDTH), BF16),
        grid=(b, A_HEADS, n_q // group),
        in_specs=[
            pl.BlockSpec((None, group, A_DV), lambda bb, h, i: (bb, q_block0 + i, h)),
            pl.BlockSpec((None, None, A_DV, n_keys), lambda bb, h, i: (bb, h, 0, key_block0)),
            pl.BlockSpec((None, n_keys, 2 * LANES), lambda bb, h, i: (bb, key_block0, h)),
            pl.BlockSpec((4, A_DH), lambda bb, h, i: (0, 0)),
            pl.BlockSpec((1, A_DV), lambda bb, h, i: (0, 0)),
        ],
        out_specs=pl.BlockSpec((None, group, A_DV), lambda bb, h, i: (bb, i, h)),
        scratch_shapes=[
            pltpu.VMEM((2 * tq, A_DV), BF16),
            pltpu.VMEM((3, 2 * tq, tk), F32),
            pltpu.VMEM((2, 2 * tq, LANES), F32),
            pltpu.VMEM((2, 2 * tq, 2 * LANES), F32),
        ],
        compiler_params=_cparams(("parallel", "parallel", "arbitrary")),
        name="diff_attn",
    )(q, kt, v_ext, lam_p, gain)


_HI16 = 0xFFFF0000


def _pack_bf16_pair(lo, hi):
    ulo = lax.bitcast_convert_type(lo.astype(BF16).astype(F32), jnp.uint32) >> 16
    uhi = lax.bitcast_convert_type(hi.astype(BF16).astype(F32), jnp.uint32) & jnp.uint32(_HI16)
    return ulo | uhi


def _top_half_bits(x):
    return lax.bitcast_convert_type(lax.bitcast_convert_type(x, jnp.uint32) & jnp.uint32(_HI16), F32)


def _unpack_bf16_pair(u):
    lo = lax.bitcast_convert_type(u << 16, F32)
    hi = lax.bitcast_convert_type(u & jnp.uint32(_HI16), F32)
    return lo, hi


ROUTE_CLASS = 0
N_PAIRS = EXPERTS_PER_GROUP * (EXPERTS_PER_GROUP - 1) // 2
N_PAIR_CLASSES = N_EXPERT_GROUPS * N_PAIRS


def _router_gates(sc_t, sel_t, route_ref, pair_gates_ref):
    s_rows = [sc_t[e:e + 1, :] for e in range(N_EXPERTS)]
    v_rows = [sel_t[e:e + 1, :] for e in range(N_EXPERTS)]
    in_top2 = []
    group_score = []
    for g in range(N_EXPERT_GROUPS):
        vs = v_rows[g * EXPERTS_PER_GROUP:(g + 1) * EXPERTS_PER_GROUP]
        tops = []
        for jj in range(EXPERTS_PER_GROUP):
            rank = jnp.zeros_like(vs[jj])
            for ii in range(EXPERTS_PER_GROUP):
                if ii == jj:
                    continue
                beats = (vs[ii] >= vs[jj]) if ii < jj else (vs[ii] > vs[jj])
                rank = rank + jnp.where(beats, 1.0, 0.0)
            tops.append(rank < 2.0)
        in_top2 += tops
        gs = jnp.zeros_like(vs[0])
        for jj in range(EXPERTS_PER_GROUP):
            gs = gs + jnp.where(tops[jj], vs[jj], 0.0)
        group_score.append(gs)
    best = group_score[0]
    gidx = jnp.zeros_like(best)
    for g in range(1, N_EXPERT_GROUPS):
        upd = group_score[g] > best
        best = jnp.where(upd, group_score[g], best)
        gidx = jnp.where(upd, float(g), gidx)
    chosen = [jnp.logical_and(in_top2[e], gidx == float(e // EXPERTS_PER_GROUP))
              for e in range(N_EXPERTS)]
    denom = jnp.zeros_like(best)
    for e in range(N_EXPERTS):
        denom = denom + jnp.where(chosen[e], s_rows[e], 0.0)
    e_lo = jnp.zeros_like(best)
    g_lo = jnp.zeros_like(best)
    for e in reversed(range(N_EXPERTS)):
        e_lo = jnp.where(chosen[e], float(e), e_lo)
        g_lo = jnp.where(chosen[e], s_rows[e] / denom, g_lo)
    e_hi = jnp.zeros_like(best)
    g_hi = jnp.zeros_like(best)
    for e in range(N_EXPERTS):
        e_hi = jnp.where(chosen[e], float(e), e_hi)
        g_hi = jnp.where(chosen[e], s_rows[e] / denom, g_hi)
    a_lo = e_lo - EXPERTS_PER_GROUP * gidx
    a_hi = e_hi - EXPERTS_PER_GROUP * gidx
    pair_rank = a_lo * (2 * EXPERTS_PER_GROUP - 1 - a_lo) * 0.5 + (a_hi - a_lo - 1.0)
    route_ref[...] = jnp.zeros(route_ref.shape, F32)
    route_ref[ROUTE_CLASS:ROUTE_CLASS + 1, :] = gidx * float(N_PAIRS) + pair_rank
    tm = g_lo.shape[1]
    pair_gates_ref[:, :LANES] = jnp.broadcast_to(g_lo, (LANES, tm)).T
    pair_gates_ref[:, LANES:] = jnp.broadcast_to(g_hi, (LANES, tm)).T


def _outproj_kernel(yf_lat_ref, yf_ctx_ref, att_lat_ref, att_ctx_ref, sg_ref, x_lat_ref, x_ctx_ref,
                    mod_ref, wo_ref, lng_ref, lnb_ref, wr_ref, rb_ref, x1_ref, h_ref, route_ref,
                    pair_gates_ref, *, alpha, nlat):
    is_ctx = pl.program_id(1) >= nlat
    x = jnp.where(is_ctx, x_ctx_ref[...], x_lat_ref[...])
    yf = jnp.where(is_ctx, yf_ctx_ref[...], yf_lat_ref[...])
    att = jnp.where(is_ctx, att_ctx_ref[...], att_lat_ref[...])
    mix = _bdot(yf, wo_ref[0:F_WIDTH, :])
    mix = mix + _bdot(att, wo_ref[F_WIDTH:F_WIDTH + A_WIDTH, :])
    mix = mix + _bdot(sg_ref[...], wo_ref[F_WIDTH + A_WIDTH:, :])
    g1 = mod_ref[2:3, :]
    x1 = _layernorm(alpha * x + g1 * mix, lng_ref[...], lnb_ref[...])
    x1_ref[...] = x1
    h = x1 * (1.0 + mod_ref[4:5, :]) + mod_ref[3:4, :]
    half = h.shape[1] // 2
    h_ref[...] = _pack_bf16_pair(h[:, :half], h[:, half:])
    h_top = _top_half_bits(h)
    h_hi = h_top.astype(BF16)
    h_lo = (h - h_top).astype(BF16)
    hw = _bdot(h_hi, wr_ref[...])
    logits = hw[:, :ROUTER_LANES] + (hw[:, ROUTER_LANES:] + _bdot(h_lo, wr_ref[:, :ROUTER_LANES]))
    scores = jax.nn.sigmoid(logits)
    sel = scores + rb_ref[...]
    _router_gates(scores.T[0:N_EXPERTS, :], sel.T[0:N_EXPERTS, :], route_ref, pair_gates_ref)


def _outproj(yf_lat, yf_ctx, att_lat, att_ctx, sg, x_lat, x_ctx, x_ctx_row0, mods_l, wo, lng, lnb, wr, rb,
             n_lat, alpha):
    b, _, d = x_lat.shape
    l = sg.shape[1]
    tm = TOKEN_TILE
    nlat = n_lat // tm
    ctx_tile0 = x_ctx_row0 // tm
    seg = lambda i: jnp.where(i >= nlat, 1, 0)
    row = lambda width: pl.BlockSpec((None, tm, width), lambda bb, i: (bb, i, 0))
    lat = lambda width: pl.BlockSpec((None, tm, width), lambda bb, i: (bb, jnp.minimum(i, nlat - 1), 0))
    ctx = lambda width: pl.BlockSpec((None, tm, width), lambda bb, i: (bb, jnp.maximum(i - nlat, 0), 0))
    const2 = lambda shape: pl.BlockSpec(shape, lambda bb, i: (0, 0))
    return pl.pallas_call(
        functools.partial(_outproj_kernel, alpha=alpha, nlat=nlat),
        out_shape=(
            jax.ShapeDtypeStruct((b, l, d), F32),
            jax.ShapeDtypeStruct((b, l, d // 2), jnp.uint32),
            jax.ShapeDtypeStruct((b, SUBLANES, l), F32),
            jax.ShapeDtypeStruct((b, l, 2 * LANES), F32),
        ),
        grid=(b, l // tm),
        in_specs=[
            lat(F_WIDTH), ctx(F_WIDTH), lat(A_WIDTH), ctx(A_WIDTH), row(G_WIDTH), lat(d),
            pl.BlockSpec((None, tm, d), lambda bb, i: (bb, ctx_tile0 + jnp.maximum(i - nlat, 0), 0)),
            pl.BlockSpec((None, None, MOD_ROWS, d), lambda bb, i: (bb, seg(i), 0, 0)),
            const2((d, d)), const2((1, d)), const2((1, d)),
            const2((d, 2 * ROUTER_LANES)), const2((1, ROUTER_LANES)),
        ],
        out_specs=(row(d), row(d // 2),
                   pl.BlockSpec((None, SUBLANES, tm), lambda bb, i: (bb, 0, i)), row(2 * LANES)),
        compiler_params=_cparams(("parallel", "parallel")),
        name="outproj_ln_router",
    )(yf_lat, yf_ctx, att_lat, att_ctx, sg, x_lat, x_ctx, mods_l, wo, lng, lnb, wr, rb)


def _moe_plan(route, tm):
    b, _, l = route.shape
    t = b * l
    pairs = [(a, c) for a in range(EXPERTS_PER_GROUP) for c in range(a + 1, EXPERTS_PER_GROUP)]
    cls_e1 = np.zeros((N_PAIR_CLASSES,), np.int32)
    cls_e2 = np.zeros((N_PAIR_CLASSES,), np.int32)
    for k, (a, c) in enumerate(pairs):
        assert k == a * (2 * EXPERTS_PER_GROUP - 1 - a) // 2 + (c - a - 1)
        for g in range(N_EXPERT_GROUPS):
            cls_e1[g * N_PAIRS + k] = g * EXPERTS_PER_GROUP + a
            cls_e2[g * N_PAIRS + k] = g * EXPERTS_PER_GROUP + c
    cls = route[:, ROUTE_CLASS, :].astype(jnp.int32).reshape(t)
    onehot = cls[:, None] == jnp.arange(N_PAIR_CLASSES, dtype=jnp.int32)[None, :]
    oh_tiles = onehot.reshape(t // tm, tm, N_PAIR_CLASSES).astype(BF16)
    tri = jnp.tril(jnp.ones((tm, tm), BF16))
    within = jnp.einsum("ij,njc->nic", tri, oh_tiles, preferred_element_type=F32)
    tile_total = within[:, -1, :]
    n_tiles = t // tm
    earlier = jnp.tril(jnp.ones((n_tiles, n_tiles), BF16), k=-1)
    before = jnp.dot(earlier, tile_total.astype(BF16), preferred_element_type=F32)
    csum = (within + before[:, None, :]).reshape(t, N_PAIR_CLASSES).astype(jnp.int32)
    onehot = onehot.astype(jnp.int32)
    rank = jnp.sum(csum * onehot, axis=1) - 1
    padded = (csum[-1] + tm - 1) // tm * tm
    upto = jnp.tril(jnp.ones((N_PAIR_CLASSES, N_PAIR_CLASSES), jnp.int32))
    off_end = jnp.sum(upto * padded[None, :], axis=1)
    dest = jnp.sum(onehot * (off_end - padded)[None, :], axis=1) + rank
    nt = t // tm + N_PAIR_CLASSES
    n_valid = (off_end[-1] // tm).astype(jnp.int32)
    first_row = jnp.minimum(jnp.arange(nt, dtype=jnp.int32), n_valid - 1) * tm
    tile_cls = jnp.sum((off_end[None, :] <= first_row[:, None]).astype(jnp.int32), axis=1)
    tile_cls = jnp.minimum(tile_cls, N_PAIR_CLASSES - 1)
    return (dest.astype(jnp.int32), jnp.asarray(cls_e1)[tile_cls], jnp.asarray(cls_e2)[tile_cls],
            n_valid.reshape(1), nt)


def _dispatch_kernel(dest_ref, h_ref, hs_init_ref, hs_ref, stage, sems):
    del hs_init_ref
    s = pl.program_id(0)
    n = pl.num_programs(0)
    tm = h_ref.shape[0]
    slot = s % 2

    def wait_slot(k):
        pltpu.make_async_copy(stage.at[k], hs_ref.at[pl.ds(0, tm), :], sems.at[k]).wait()

    @pl.when(s >= 2)
    def _reuse():
        wait_slot(slot)

    stage[slot] = h_ref[...]
    for r in range(tm):
        pltpu.make_async_copy(stage.at[slot, pl.ds(r, 1), :],
                              hs_ref.at[pl.ds(dest_ref[0, r], 1), :], sems.at[slot]).start()

    @pl.when(s == n - 1)
    def _drain():
        @pl.when(n >= 2)
        def _other():
            wait_slot(1 - slot)
        wait_slot(slot)


def _dispatch(h_packed, dest, nt, tm):
    t, w = h_packed.shape
    hs_init = jnp.zeros((nt * tm, w), jnp.uint32)
    return pl.pallas_call(
        _dispatch_kernel,
        out_shape=jax.ShapeDtypeStruct((nt * tm, w), jnp.uint32),
        grid=(t // tm,),
        in_specs=[
            pl.BlockSpec((None, 1, tm), lambda i: (i, 0, 0), memory_space=pltpu.SMEM),
            pl.BlockSpec((tm, w), lambda i: (i, 0)),
            pl.BlockSpec(memory_space=pl.ANY),
        ],
        out_specs=pl.BlockSpec(memory_space=pl.ANY),
        scratch_shapes=[pltpu.VMEM((2, tm, w), jnp.uint32), pltpu.SemaphoreType.DMA((2,))],
        input_output_aliases={2: 0},
        compiler_params=_cparams(("arbitrary",)),
        name="moe_dispatch",
    )(dest.reshape(t // tm, 1, tm), h_packed, hs_init)


def _expert_pair_kernel(e1_ref, e2_ref, nv_ref, hs_ref, w13a_ref, w2a_ref, w13b_ref, w2b_ref, y_ref):
    del e1_ref, e2_ref
    i = pl.program_id(0)

    @pl.when(i < nv_ref[0])
    def _compute():
        lo, hi = _unpack_bf16_pair(hs_ref[...])
        h = jnp.concatenate([lo, hi], axis=1).astype(BF16)

        def ffn(w13_ref, w2_ref):
            gu = _bdot(h, w13_ref[...])
            de = gu.shape[1] // 2
            g = gu[:, :de]
            a = (g * jax.nn.sigmoid(g)) * gu[:, de:]
            return _bdot(a.astype(BF16), w2_ref[...])

        y_ref[...] = _pack_bf16_pair(ffn(w13a_ref, w2a_ref), ffn(w13b_ref, w2b_ref))

    @pl.when(i >= nv_ref[0])
    def _unused_tile():
        y_ref[...] = jnp.zeros(y_ref.shape, jnp.uint32)


def _expert_pairs(hs, tile_e1, tile_e2, n_valid, w13, w2, tm):
    rows, w = hs.shape
    _, de, d = w2.shape
    first = lambda shape: pl.BlockSpec(shape, lambda i, e1, e2, nv: (e1[i], 0, 0))
    second = lambda shape: pl.BlockSpec(shape, lambda i, e1, e2, nv: (e2[i], 0, 0))
    grid_spec = pltpu.PrefetchScalarGridSpec(
        num_scalar_prefetch=3,
        grid=(rows // tm,),
        in_specs=[
            pl.BlockSpec((tm, w), lambda i, e1, e2, nv: (jnp.minimum(i, nv[0] - 1), 0)),
            first((None, d, 2 * de)), first((None, de, d)),
            second((None, d, 2 * de)), second((None, de, d)),
        ],
        out_specs=pl.BlockSpec((tm, d), lambda i, e1, e2, nv: (i, 0)),
    )
    return pl.pallas_call(
        _expert_pair_kernel,
        out_shape=jax.ShapeDtypeStruct((rows, d), jnp.uint32),
        grid_spec=grid_spec,
        compiler_params=_cparams(("arbitrary",)),
        name="moe_expert_pairs",
    )(tile_e1, tile_e2, n_valid, hs, w13, w2, w13, w2)


def _combine_kernel(dest_ref, dest_next_ref, y2_ref, x_ref, pg_ref, mod_ref, lng_ref, lnb_ref,
                    o_ref, ybuf, sems, *, alpha):
    o_ref[...] = _combine_body(dest_ref, dest_next_ref, y2_ref, x_ref, pg_ref, mod_ref, lng_ref,
                               lnb_ref, ybuf, sems, alpha=alpha)


def _combine_inproj_kernel(dest_ref, dest_next_ref, y2_ref, x_ref, pg_ref, mod_ref, lng_ref, lnb_ref,
                           *refs, alpha, q_scale):
    n_inproj_in = 8
    inproj_in = refs[:n_inproj_in]
    o_ref = refs[n_inproj_in]
    inproj_out = refs[n_inproj_in + 1:n_inproj_in + 7]
    ybuf, sems = refs[n_inproj_in + 7:]
    x_new = _combine_body(dest_ref, dest_next_ref, y2_ref, x_ref, pg_ref, mod_ref, lng_ref, lnb_ref,
                          ybuf, sems, alpha=alpha)
    o_ref[...] = x_new
    _inproj_body(x_new, *inproj_in, *inproj_out, q_scale=q_scale)


def _combine_body(dest_ref, dest_next_ref, y2_ref, x_ref, pg_ref, mod_ref, lng_ref, lnb_ref,
                  ybuf, sems, *, alpha):
    s = pl.program_id(0)
    n = pl.num_programs(0)
    tm, d = x_ref.shape
    slot = s % 2

    def start_gather(idx_ref, k):
        for r in range(tm):
            pltpu.make_async_copy(y2_ref.at[pl.ds(idx_ref[0, r], 1), :],
                                  ybuf.at[k, pl.ds(r, 1), :], sems.at[k]).start()

    @pl.when(s == 0)
    def _first():
        start_gather(dest_ref, 0)

    @pl.when(s + 1 < n)
    def _prefetch():
        start_gather(dest_next_ref, 1 - slot)

    pltpu.make_async_copy(y2_ref.at[pl.ds(0, tm), :], ybuf.at[slot], sems.at[slot]).wait()
    y_lo, y_hi = _unpack_bf16_pair(ybuf[slot])
    pg = pg_ref[...]
    reps = d // LANES
    g_lo = jnp.concatenate([pg[:, :LANES]] * reps, axis=1)
    g_hi = jnp.concatenate([pg[:, LANES:]] * reps, axis=1)
    y = g_lo * y_lo + g_hi * y_hi
    return _layernorm(alpha * x_ref[...] + mod_ref[5:6, :] * y, lng_ref[...], lnb_ref[...])


def _combine(y2, dest, x1, pair_gates, mods_l, lng, lnb, *, n_rows, n_lat, tm, alpha, next_inproj=None):
    b, l, d = x1.shape
    nlat = n_lat // tm
    tiles_per_batch = l // tm
    per_batch = n_rows // tm
    n_steps = b * per_batch
    batch = lambda s: s // per_batch
    tile = lambda s: s % per_batch
    seg = lambda s: jnp.where(tile(s) >= nlat, 1, 0)
    token_tile = lambda s: batch(s) * tiles_per_batch + tile(s)
    row = lambda width: pl.BlockSpec((None, tm, width), lambda s: (batch(s), tile(s), 0))
    const2 = lambda shape: pl.BlockSpec(shape, lambda s: (0, 0))
    mod_spec = pl.BlockSpec((None, None, MOD_ROWS, d), lambda s: (batch(s), seg(s), 0, 0))
    dest3 = dest.reshape(b * tiles_per_batch, 1, tm)
    in_specs = [
        pl.BlockSpec((None, 1, tm), lambda s: (token_tile(s), 0, 0), memory_space=pltpu.SMEM),
        pl.BlockSpec((None, 1, tm), lambda s: (token_tile(jnp.minimum(s + 1, n_steps - 1)), 0, 0),
                     memory_space=pltpu.SMEM),
        pl.BlockSpec(memory_space=pl.ANY),
        row(d), row(2 * LANES), mod_spec, const2((1, d)), const2((1, d)),
    ]
    args = [dest3, dest3, y2, x1, pair_gates, mods_l, lng, lnb]
    out_shape = [jax.ShapeDtypeStruct((b, n_rows, d), F32)]
    out_specs = [row(d)]
    if next_inproj is None:
        body = functools.partial(_combine_kernel, alpha=alpha)
        name = "moe_combine_ln"
    else:
        assert n_rows == l and tm == TOKEN_TILE
        mods_n, w_big, cosf, sinf, slng, slnb, ws, bs_full, q_scale = next_inproj
        in_specs += [
            mod_spec, const2((d, BIG_COLS)),
            pl.BlockSpec((tm, LANES), lambda s: (tile(s), 0)),
            pl.BlockSpec((tm, LANES), lambda s: (tile(s), 0)),
            const2((1, G_WIDTH)), const2((1, G_WIDTH)),
            pl.BlockSpec((G_GROUPS, CHUNK, CHUNK), lambda s: (0, 0, 0)),
            const2((CHUNK, G_WIDTH)),
        ]
        args += [mods_n, w_big, cosf, sinf, slng, slnb, ws, bs_full]
        out_shape += list(_inproj_out_shapes(b, l))
        out_specs += [row(F_WIDTH), row(F_WIDTH), row(A_QK_WIDTH),
                      pl.BlockSpec((None, A_HEADS, A_DV, tm), lambda s: (batch(s), 0, 0, tile(s))),
                      row(2 * A_WIDTH), row(G_WIDTH)]
        body = functools.partial(_combine_inproj_kernel, alpha=alpha, q_scale=q_scale)
        name = "moe_combine_ln_inproj"
    outs = pl.pallas_call(
        body,
        out_shape=tuple(out_shape),
        grid=(n_steps,),
        in_specs=in_specs,
        out_specs=tuple(out_specs),
        scratch_shapes=[pltpu.VMEM((2, tm, d), jnp.uint32), pltpu.SemaphoreType.DMA((2,))],
        compiler_params=_cparams(("arbitrary",)),
        name=name,
    )(*args)
    return outs[0] if next_inproj is None else outs


def _head_lane_fields():
    j = jnp.arange(LANES)
    half = j // (LANES // 2)
    comp = (j % (LANES // 2)) // (A_DH // 2)
    axis = (j % (A_DH // 2)) // (A_DH // 4)
    freq = j % (A_DH // 4)
    return half, comp, axis, freq


def _qk_column_perm():
    half, comp, axis, freq = _head_lane_fields()
    orig = comp * A_DH + axis * (A_DH // 2) + half * (A_DH // 4) + freq
    return (jnp.arange(A_HEADS)[:, None] * LANES + orig[None, :]).reshape(-1)


def _rope_tables(n_lat, n_ctx):
    half, _, axis, freq = _head_lane_fields()
    rows = n_lat // GRID_W
    row = jnp.repeat(jnp.arange(rows, dtype=F32), GRID_W)
    col = jnp.tile(jnp.arange(GRID_W, dtype=F32), rows)
    hd = A_DH // 2
    inv = ROPE_BASE ** (-jnp.arange(0, hd, 2, dtype=F32) / hd)
    pos = jnp.where(axis[None, :] == 0, row[:, None], col[:, None])
    ang = pos * inv[freq][None, :]
    cosf = jnp.cos(ang)
    sinf = jnp.sin(ang) * jnp.where(half == 0, -1.0, 1.0)[None, :]
    cosf = jnp.concatenate([cosf, jnp.ones((n_ctx, LANES), F32)], axis=0)
    sinf = jnp.concatenate([sinf, jnp.zeros((n_ctx, LANES), F32)], axis=0)
    return cosf, sinf


def _dft_mats(n, scale=1.0):
    i = jnp.arange(n, dtype=jnp.int32)
    ang = ((i[:, None] * i[None, :]) % n).astype(F32) * (2.0 * math.pi / n)
    return jnp.cos(ang) * scale, jnp.sin(ang) * scale


def _twiddles(n, bn):
    ka = jnp.arange(DFT_A, dtype=jnp.int32)
    bb = jnp.arange(bn, dtype=jnp.int32)
    ang = ((ka[:, None] * bb[None, :]) % n).astype(F32) * (2.0 * math.pi / n)
    shape = (DFT_A, bn, F_WIDTH)
    twc = jnp.broadcast_to(jnp.cos(ang)[:, :, None], shape).reshape(DFT_A, bn * F_WIDTH)
    tws = jnp.broadcast_to(jnp.sin(ang)[:, :, None], shape).reshape(DFT_A, bn * F_WIDTH)
    return twc, tws


def _block_diag(blocks):
    g, n = blocks.shape[-3], blocks.shape[-1]
    eye = jnp.eye(g, dtype=blocks.dtype)
    out = blocks[..., :, :, None, :] * eye[:, None, :, None]
    return out.reshape(blocks.shape[:-3] + (g * n, g * n))


def kernel(x, c, ctx, c_ctx, w_ada, b_ada, w_in, w_fourier, diff_lambda, diff_subln,
           sgu_ln_g, sgu_ln_b, sgu_w, sgu_b, w_out, ln_g, ln_b, w_router, router_bias,
           moe_w1, moe_w3, moe_w2):
    b, n, d = x.shape
    n_ctx = ctx.shape[1]
    depth = w_ada.shape[0]
    l = n + n_ctx
    alpha = (2 * depth) ** 0.25
    bn = n // DFT_A
    assert n % DFT_A == 0 and bn % SUBLANES == 0 and n % GRID_W == 0
    assert n_ctx % TOKEN_TILE == 0 and n % TOKEN_TILE == 0 and n % n_ctx == 0
    assert b + 1 <= MOD_ROWS

    c_rows = jnp.concatenate([c, c_ctx[None, :], jnp.zeros((MOD_ROWS - b - 1, d), F32)], axis=0)
    mod = _ada(c_rows, w_ada, b_ada).reshape(depth, MOD_ROWS, 6, d)
    lat = mod[:, :b]
    cm = jnp.broadcast_to(mod[:, b:b + 1], lat.shape)
    mods = jnp.stack([lat, cm], axis=2)
    mods = jnp.pad(mods, ((0, 0), (0, 0), (0, 0), (0, MOD_ROWS - 6), (0, 0)))

    perm = _qk_column_perm()
    c64, s64 = _dft_mats(F_DIM)
    eye_g = jnp.eye(F_GROUPS, dtype=F32)
    bdc = jnp.kron(eye_g, c64)
    bds = jnp.kron(eye_g, s64)
    bdw = _block_diag(w_fourier)
    w_fab = _fourier_weights(w_in[:, :, :F_WIDTH], bdc, bds, bdw)
    o_q = F_WIDTH
    o_k = o_q + A_QK_WIDTH
    o_v = o_k + A_QK_WIDTH
    o_g = o_v + A_WIDTH
    w_big = jnp.concatenate([
        w_fab,
        w_in[:, :, o_q:o_k][:, :, perm],
        w_in[:, :, o_k:o_v][:, :, perm],
        w_in[:, :, o_v:],
    ], axis=-1).astype(BF16)
    wo = w_out.astype(BF16)
    w13 = jnp.concatenate([moe_w1, moe_w3], axis=-1).astype(BF16)
    w2 = moe_w2.astype(BF16)
    ws = sgu_w.astype(BF16)
    bs_full = jnp.repeat(jnp.swapaxes(sgu_b, 1, 2), G_DIM, axis=2)
    wr_f32 = jnp.pad(w_router, ((0, 0), (0, ROUTER_LANES - N_EXPERTS)))
    wr_top = _top_half_bits(wr_f32)
    wr = jnp.concatenate([wr_top.astype(BF16), (wr_f32 - wr_top).astype(BF16)], axis=1)
    rb =jnp.pad(router_bias, (0, ROUTER_LANES - N_EXPERTS)).reshape(1, ROUTER_LANES)

    cosf, sinf = _rope_tables(n, n_ctx)
    ca, sa = _dft_mats(DFT_A)
    l_a = jnp.concatenate([_stack3([ca, -sa]), _stack3([-sa, -ca])], axis=0)
    twc, tws = _twiddles(n, bn)
    l_b = _stack3(list(_dft_mats(bn, scale=(n * F_DIM) ** -0.5)))
    cc, sc = _dft_mats(n_ctx, scale=(n_ctx * F_DIM) ** -0.5)
    l_c = _stack3([cc, -sc])
    q_scale = (A_DH ** -0.5) * math.log2(math.e)

    tk = ATTN_TK
    assert l % tk == 0
    assert n % MOE_TILE == 0 and n_ctx % MOE_TILE == 0

    def inproj_params(li):
        return (mods[li], w_big[li], cosf, sinf, sgu_ln_g[li][None], sgu_ln_b[li][None],
                ws[li], bs_full[li])

    fa, fb, q, kt, v_ext, sg = _inproj(x, ctx, *inproj_params(0), q_scale)
    stream = (x, ctx, 0)
    for li in range(depth):
        last = li == depth - 1
        lam_init = 0.8 - 0.6 * math.exp(-0.3 * li)
        zr, zi = _fft_a(fa, fb, l_a, twc, tws, bn)
        yf_x = _fft_b(zr, zi, l_b, bn)
        yf_c = _dft_ctx(fa, fb, l_c, n, n_ctx)
        gain = diff_subln[li][None]
        att_x = _attention(q, kt, v_ext, diff_lambda[li], gain, q_block0=0, n_q=n, key_block0=0,
                           n_keys=l, tq=ATTN_TQ, tk=tk, group=min(ATTN_GROUP, n), lam_init=lam_init)
        att_c = _attention(q, kt, v_ext, diff_lambda[li], gain, q_block0=n // n_ctx, n_q=n_ctx,
                           key_block0=n // n_ctx, n_keys=n_ctx, tq=n_ctx, tk=n_ctx, group=n_ctx,
                           lam_init=lam_init)
        x1, h_packed, route, pair_gates = _outproj(
            yf_x, yf_c, att_x, att_c, sg, *stream, mods[li], wo[li], ln_g[li, 0][None], ln_b[li, 0][None], wr, rb, n, alpha)
        dest, tile_e1, tile_e2, n_valid, nt = _moe_plan(route, MOE_TILE)
        hs = _dispatch(h_packed.reshape(b * l, d // 2), dest, nt, MOE_TILE)
        y2 = _expert_pairs(hs, tile_e1, tile_e2, n_valid, w13[li], w2[li], MOE_TILE)
        combine_args = (y2, dest, x1, pair_gates, mods[li], ln_g[li, 1][None], ln_b[li, 1][None])
        if last:
            xa = _combine(*combine_args, n_rows=n, n_lat=n, tm=MOE_TILE, alpha=alpha)
        else:
            xa, fa, fb, q, kt, v_ext, sg = _combine(
                *combine_args, n_rows=l, n_lat=n, tm=MOE_TILE, alpha=alpha,
                next_inproj=inproj_params(li + 1) + (q_scale,))
            stream = (xa, xa, n)
    return xa
```
